```python
import math
import jax, jax.numpy as jnp
from jax import lax
import numpy as np

D_MODEL = 1024
BATCH = 16
SEQ = 4096
DEPTH = 2
DEC_BATCH = 8
DEC_SEQ = 2048
PAST_LEN = 128

RMS_EPS = 1e-6
HEAD_DIM = 64
NEG_INF = -1e30
A_PATTERNS = ((128, 1), (512, 4), (2048, 16))
A_GROUPS = 3
A_HEADS_PER_GROUP = 4
A_HEADS = A_GROUPS * A_HEADS_PER_GROUP
A_BLOCK = 64
A_IN = 3 * A_HEADS * HEAD_DIM
A_OUT = A_HEADS_PER_GROUP * HEAD_DIM
T5_BUCKETS = 32
T5_MAX_DISTANCE = 1024
B_HEADS = 12
B_Q_RANK = 384
B_KV_RANK = 256
B_NOPE = 64
B_ROPE = 32
B_V = 64
B_IN = B_Q_RANK + B_KV_RANK + B_ROPE
B_OUT = B_HEADS * B_V
B_QBLOCK = 128
ROPE_THETA = 10000.0
EVEN_IN = A_IN + B_IN
EVEN_OUT = A_OUT + B_OUT
GRID_W = 64
C_HEADS = 16
C_WIN_ROWS = 8
C_WIN_COLS = 16
C_COL_BLOCK = 16
C_KEY_COLS = C_COL_BLOCK + C_WIN_COLS
C_IN = 3 * C_HEADS * HEAD_DIM
C_OUT = C_HEADS * HEAD_DIM
N_EXPERTS = 16
EXPERT_FF = 1024
CAPACITY_FACTOR = 2
N_EVEN = (DEPTH + 1) // 2
N_ODD = DEPTH // 2

kernel_name = "hybrid_dilated_mla_natten_ec_moe_encoder"


def rms_norm(x, g):
    xf = x.astype(jnp.float32)
    y = xf * lax.rsqrt(jnp.mean(xf * xf, axis=-1, keepdims=True) + RMS_EPS)
    return (y * g.astype(jnp.float32)).astype(x.dtype)


def t5_bucket(rel):
    half = T5_BUCKETS // 2
    max_exact = half // 2
    n = np.abs(rel)
    large = max_exact + (np.log(np.maximum(n, 1) / max_exact) / np.log(T5_MAX_DISTANCE / max_exact) * (half - max_exact)).astype(np.int32)
    large = np.minimum(large, half - 1)
    return (np.where(rel > 0, half, 0) + np.where(n < max_exact, n, large)).astype(np.int32)


def dilated_group_attention(q, k, v, bias_table, window, dilation):
    Bn, S, H, dh = q.shape
    radius = window // (2 * dilation)
    L = S // dilation
    nb = -(-L // A_BLOCK)
    Lp = nb * A_BLOCK

    def to_blocks(t):
        t = t.reshape(Bn, L, dilation, H, dh).transpose(0, 2, 1, 3, 4)
        t = jnp.pad(t, ((0, 0), (0, 0), (0, Lp - L), (0, 0), (0, 0)))
        return t.reshape(Bn, dilation, nb, A_BLOCK, H, dh)

    def band(t):
        tp = jnp.pad(t, ((0, 0), (0, 0), (1, 1), (0, 0), (0, 0), (0, 0)))
        return jnp.concatenate([tp[:, :, :-2], tp[:, :, 1:-1], tp[:, :, 2:]], axis=3)

    qb = to_blocks(q)
    kband = band(to_blocks(k))
    vband = band(to_blocks(v))
    qi = np.arange(A_BLOCK)[:, None]
    ci = np.arange(3 * A_BLOCK)[None, :]
    off = ci - A_BLOCK - qi
    key_m = np.arange(nb)[:, None, None] * A_BLOCK + ci[None] - A_BLOCK
    valid = (np.abs(off) <= radius)[None] & (key_m >= 0) & (key_m < L)
    bias = jnp.transpose(bias_table[t5_bucket(off * dilation)], (2, 0, 1)).astype(jnp.float32)
    s = jnp.einsum('brnqhd,brnkhd->brnhqk', qb, kband).astype(jnp.float32) * (dh ** -0.5)
    s = jnp.where(valid[None, None, :, None], s + bias[None, None, None], NEG_INF)
    m = jnp.max(s, axis=-1, keepdims=True)
    p = jnp.exp(s - m)
    den = jnp.sum(p, axis=-1, keepdims=True)
    o = jnp.einsum('brnhqk,brnkhd->brnqhd', (p / den).astype(v.dtype), vband)
    lse = (m + jnp.log(den))[..., 0]
    o = o.reshape(Bn, dilation, Lp, H, dh)[:, :, :L].transpose(0, 2, 1, 3, 4).reshape(Bn, S, H, dh)
    lse = lse.transpose(0, 1, 2, 4, 3).reshape(Bn, dilation, Lp, H)[:, :, :L].transpose(0, 2, 1, 3).reshape(Bn, S, H)
    return o, lse


def mixer_a(u, t5_table):
    Bn, S, _ = u.shape
    u = u.reshape(Bn, S, 3, A_GROUPS, A_HEADS_PER_GROUP, HEAD_DIM)
    outs, lses = [], []
    for g, (window, dilation) in enumerate(A_PATTERNS):
        tab = t5_table[:, g * A_HEADS_PER_GROUP:(g + 1) * A_HEADS_PER_GROUP]
        o, l = dilated_group_attention(u[:, :, 0, g], u[:, :, 1, g], u[:, :, 2, g], tab, window, dilation)
        outs.append(o)
        lses.append(l)
    wts = jax.nn.softmax(jnp.stack(lses), axis=0)
    o = jnp.einsum('gbsh,gbshd->bshd', wts, jnp.stack(outs).astype(jnp.float32))
    return o.reshape(Bn, S, A_OUT).astype(u.dtype)


def rope_tables(S):
    inv = 1.0 / (ROPE_THETA ** (np.arange(0, B_ROPE, 2, dtype=np.float32) / B_ROPE))
    ang = np.arange(S, dtype=np.float32)[:, None] * inv[None]
    return jnp.asarray(np.cos(ang), dtype=jnp.float32), jnp.asarray(np.sin(ang), dtype=jnp.float32)


def apply_rope(x, cos, sin):
    xf = x.astype(jnp.float32)
    x1, x2 = jnp.split(xf, 2, axis=-1)
    return jnp.concatenate([x1 * cos - x2 * sin, x2 * cos + x1 * sin], axis=-1).astype(x.dtype)


def mixer_b(u, g_qa, w_qb, g_kva, w_kvb):
    Bn, S, _ = u.shape
    c_q, c_kv, k_r = jnp.split(u, [B_Q_RANK, B_Q_RANK + B_KV_RANK], axis=-1)
    q = (rms_norm(c_q, g_qa) @ w_qb).reshape(Bn, S, B_HEADS, B_NOPE + B_ROPE)
    kv = (rms_norm(c_kv, g_kva) @ w_kvb).reshape(Bn, S, B_HEADS, B_NOPE + B_V)
    q_nope, q_rope = q[..., :B_NOPE], q[..., B_NOPE:]
    k_nope, v = kv[..., :B_NOPE], kv[..., B_NOPE:]
    cos, sin = rope_tables(S)
    q_rope = apply_rope(q_rope, cos[:, None, :], sin[:, None, :])
    k_rope = apply_rope(k_r, cos, sin)
    scale = (B_NOPE + B_ROPE) ** -0.5
    nq = S // B_QBLOCK

    def to_qblocks(t):
        return t.reshape(Bn, nq, B_QBLOCK, *t.shape[2:]).swapaxes(0, 1)

    def attend(args):
        qn, qr = args
        s = (jnp.einsum('bqhd,bkhd->bhqk', qn, k_nope).astype(jnp.float32)
             + jnp.einsum('bqhr,bkr->bhqk', qr, k_rope).astype(jnp.float32)) * scale
        p = jax.nn.softmax(s, axis=-1).astype(v.dtype)
        return jnp.einsum('bhqk,bkhd->bqhd', p, v)

    o = lax.map(attend, (to_qblocks(q_nope), to_qblocks(q_rope)))
    return o.swapaxes(0, 1).reshape(Bn, S, B_OUT)


def mixer_c(u, rpb):
    Bn, S, _ = u.shape
    rows = S // GRID_W
    wr = min(C_WIN_ROWS, rows)
    u = u.reshape(Bn, rows, GRID_W, 3, C_HEADS, HEAD_DIM)
    q, k, v = u[:, :, :, 0], u[:, :, :, 1], u[:, :, :, 2]
    n_cb = GRID_W // C_COL_BLOCK
    qcol = np.arange(GRID_W).reshape(n_cb, C_COL_BLOCK)
    kstart = np.clip(np.arange(n_cb) * C_COL_BLOCK - C_WIN_COLS // 2, 0, GRID_W - C_KEY_COLS)
    kcol = kstart[:, None] + np.arange(C_KEY_COLS)[None]
    wstart = np.clip(qcol - C_WIN_COLS // 2, 0, GRID_W - C_WIN_COLS)
    col_valid = (kcol[:, None, :] >= wstart[..., None]) & (kcol[:, None, :] < wstart[..., None] + C_WIN_COLS)
    dcol_idx = np.clip(kcol[:, None, :] - qcol[..., None] + C_WIN_COLS - 1, 0, 2 * C_WIN_COLS - 2)
    rpb_cols = rpb.astype(jnp.float32)[:, :, dcol_idx]
    k_cols = k[:, :, kcol]
    v_cols = v[:, :, kcol]
    mask = col_valid[None, None, :, :, None, :]
    scale = HEAD_DIM ** -0.5

    def attend_row(r):
        r0 = jnp.clip(r - wr // 2, 0, rows - wr)
        kr = lax.dynamic_slice_in_dim(k_cols, r0, wr, axis=1)
        vr = lax.dynamic_slice_in_dim(v_cols, r0, wr, axis=1)
        qr = lax.dynamic_index_in_dim(q, r, axis=1, keepdims=False).reshape(Bn, n_cb, C_COL_BLOCK, C_HEADS, HEAD_DIM)
        s = jnp.einsum('bjqhd,bwjchd->bhjqwc', qr, kr).astype(jnp.float32) * scale
        drow = r0 + jnp.arange(wr) - r + C_WIN_ROWS - 1
        bias = jnp.take(rpb_cols, drow, axis=1).transpose(0, 2, 3, 1, 4)
        s = jnp.where(mask, s + bias[None], NEG_INF)
        shp = s.shape
        p = jax.nn.softmax(s.reshape(Bn, C_HEADS, n_cb, C_COL_BLOCK, wr * C_KEY_COLS), axis=-1).reshape(shp)
        o = jnp.einsum('bhjqwc,bwjchd->bjqhd', p.astype(v.dtype), vr)
        return o.reshape(Bn, GRID_W, C_OUT)

    o = lax.map(attend_row, jnp.arange(rows))
    return o.transpose(1, 0, 2, 3).reshape(Bn, S, C_OUT)


def expert_choice_ffn(h, router, w_gate, w_up, w_down):
    Bn, S, D = h.shape
    T = Bn * S
    cap = CAPACITY_FACTOR * T // N_EXPERTS
    xt = h.reshape(T, D)
    aff = jax.nn.softmax((xt @ router).astype(jnp.float32), axis=-1)
    gate, idx = lax.top_k(aff.T, cap)
    xe = xt[idx]
    hid = jax.nn.silu(jnp.einsum('ecd,edf->ecf', xe, w_gate)) * jnp.einsum('ecd,edf->ecf', xe, w_up)
    ye = jnp.einsum('ecf,efd->ecd', hid, w_down) * gate[..., None].astype(h.dtype)
    out = jnp.zeros_like(xt).at[idx.reshape(-1)].add(ye.reshape(-1, D))
    return out.reshape(Bn, S, D)


def trunk(x, ln_mix_g, w_in_even, g_qa, w_qb, g_kva, w_kvb, t5_table, w_out_even,
          w_in_odd, rpb, w_out_odd, ln_ffn_g, router, w_gate, w_up, w_down, final_g):
    for layer in range(DEPTH):
        i = layer // 2
        h = rms_norm(x, ln_mix_g[layer])
        if layer % 2 == 0:
            u = h @ w_in_even[i]
            a = mixer_a(u[..., :A_IN], t5_table)
            b = mixer_b(u[..., A_IN:], g_qa[i], w_qb[i], g_kva[i], w_kvb[i])
            mix = jnp.concatenate([a, b], axis=-1) @ w_out_even[i]
        else:
            mix = mixer_c(h @ w_in_odd[i], rpb[i]) @ w_out_odd[i]
        x = x + mix
        x = x + expert_choice_ffn(rms_norm(x, ln_ffn_g[layer]), router[layer], w_gate[layer], w_up[layer], w_down[layer])
    return rms_norm(x, final_g)


def setup_inputs(seed: int = 0) -> dict:
    key = jax.random.key(seed)
    ks = jax.random.split(key, 24)

    def nrm(k, shape, scale):
        return jax.random.normal(k, shape, jnp.float32) * scale

    def gain(k, shape):
        return 1.0 + 0.02 * jax.random.normal(k, shape, jnp.float32)

    return {
        "x_prompt": jax.random.normal(ks[0], (BATCH, SEQ, D_MODEL), jnp.float32),
        "x_sample": jax.random.normal(ks[1], (DEC_BATCH, DEC_SEQ, D_MODEL), jnp.float32),
        "ln_mix_g": gain(ks[2], (DEPTH, D_MODEL)),
        "w_in_even": nrm(ks[3], (N_EVEN, D_MODEL, EVEN_IN), D_MODEL ** -0.5),
        "g_qa": gain(ks[4], (N_EVEN, B_Q_RANK)),
        "w_qb": nrm(ks[5], (N_EVEN, B_Q_RANK, B_HEADS * (B_NOPE + B_ROPE)), B_Q_RANK ** -0.5),
        "g_kva": gain(ks[6], (N_EVEN, B_KV_RANK)),
        "w_kvb": nrm(ks[7], (N_EVEN, B_KV_RANK, B_HEADS * (B_NOPE + B_V)), B_KV_RANK ** -0.5),
        "t5_table": nrm(ks[8], (T5_BUCKETS, A_HEADS), 0.2),
        "w_out_even": nrm(ks[9], (N_EVEN, EVEN_OUT, D_MODEL), EVEN_OUT ** -0.5),
        "w_in_odd": nrm(ks[10], (N_ODD, D_MODEL, C_IN), D_MODEL ** -0.5),
        "rpb": nrm(ks[11], (N_ODD, C_HEADS, 2 * C_WIN_ROWS - 1, 2 * C_WIN_COLS - 1), 0.2),
        "w_out_odd": nrm(ks[12], (N_ODD, C_OUT, D_MODEL), C_OUT ** -0.5),
        "ln_ffn_g": gain(ks[13], (DEPTH, D_MODEL)),
        "router": nrm(ks[14], (DEPTH, D_MODEL, N_EXPERTS), D_MODEL ** -0.5),
        "w_gate": nrm(ks[15], (DEPTH, N_EXPERTS, D_MODEL, EXPERT_FF), D_MODEL ** -0.5),
        "w_up": nrm(ks[16], (DEPTH, N_EXPERTS, D_MODEL, EXPERT_FF), D_MODEL ** -0.5),
        "w_down": nrm(ks[17], (DEPTH, N_EXPERTS, EXPERT_FF, D_MODEL), EXPERT_FF ** -0.5),
        "final_g": gain(ks[18], (D_MODEL,)),
    }


def reference(x_prompt, x_sample, ln_mix_g, w_in_even, g_qa, w_qb, g_kva, w_kvb, t5_table, w_out_even,
              w_in_odd, rpb, w_out_odd, ln_ffn_g, router, w_gate, w_up, w_down, final_g):
    y_prompt = trunk(x_prompt, ln_mix_g, w_in_even, g_qa, w_qb, g_kva, w_kvb, t5_table, w_out_even,
                     w_in_odd, rpb, w_out_odd, ln_ffn_g, router, w_gate, w_up, w_down, final_g)
    y_sample = trunk(x_sample, ln_mix_g, w_in_even, g_qa, w_qb, g_kva, w_kvb, t5_table, w_out_even,
                     w_in_odd, rpb, w_out_odd, ln_ffn_g, router, w_gate, w_up, w_down, final_g)
    return (y_prompt, y_sample)
```

```python
import functools
import math

import jax
import jax.numpy as jnp
import numpy as np
from jax import lax
from jax.experimental import pallas as pl
from jax.experimental.pallas import tpu as pltpu

D_MODEL = 1024
RMS_EPS = 1e-6
HEAD_DIM = 64
NEG_INF = -1e30
A_PATTERNS = ((128, 1), (512, 4), (2048, 16))
A_GROUPS = 3
A_HEADS_PER_GROUP = 4
A_HEADS = A_GROUPS * A_HEADS_PER_GROUP
A_BLOCK = 64
A_IN = 3 * A_HEADS * HEAD_DIM
A_OUT = A_HEADS_PER_GROUP * HEAD_DIM
T5_BUCKETS = 32
T5_MAX_DISTANCE = 1024
B_HEADS = 12
B_Q_RANK = 384
B_KV_RANK = 256
B_NOPE = 64
B_ROPE = 32
B_V = 64
B_IN = B_Q_RANK + B_KV_RANK + B_ROPE
B_OUT = B_HEADS * B_V
B_QBLOCK = 128
ROPE_THETA = 10000.0
GRID_W = 64
C_HEADS = 16
C_WIN_ROWS = 8
C_WIN_COLS = 16
C_COL_BLOCK = 16
C_KEY_COLS = C_COL_BLOCK + C_WIN_COLS
C_IN = 3 * C_HEADS * HEAD_DIM
C_OUT = C_HEADS * HEAD_DIM
N_EXPERTS = 16
EXPERT_FF = 1024
CAPACITY_FACTOR = 2
DEPTH = 2

VMEM_LIMIT_BYTES = 48 * 1024 * 1024

F32 = jnp.float32
BF16 = jnp.bfloat16


def _mm_kernel(*refs, has_gain, has_res, precise):
    x_ref, w_ref = refs[0], refs[1]
    pos = 2
    g_ref = res_ref = None
    if has_gain:
        g_ref = refs[pos]
        pos += 1
    if has_res:
        res_ref = refs[pos]
        pos += 1
    o_ref = refs[pos]
    x = x_ref[...]
    if has_gain:
        xf = x.astype(F32)
        ms = jnp.mean(xf * xf, axis=-1, keepdims=True)
        x = xf * lax.rsqrt(ms + RMS_EPS) * g_ref[...]
    if precise:
        acc = jnp.dot(x.astype(F32), w_ref[...], preferred_element_type=F32,
                      precision=lax.Precision.HIGHEST)
    else:
        acc = jnp.dot(x.astype(BF16), w_ref[...], preferred_element_type=F32)
    if has_res:
        acc = acc + res_ref[...]
    o_ref[...] = acc.astype(o_ref.dtype)


def _matmul(x, w, gain=None, res=None, out_dtype=F32, tm=512, precise=False):
    M, K = x.shape
    N = w.shape[1]
    assert M % tm == 0
    in_specs = [pl.BlockSpec((tm, K), lambda i: (i, 0)),
                pl.BlockSpec((K, N), lambda i: (0, 0))]
    args = [x, w]
    if gain is not None:
        in_specs.append(pl.BlockSpec((1, K), lambda i: (0, 0)))
        args.append(gain.reshape(1, K).astype(F32))
    if res is not None:
        in_specs.append(pl.BlockSpec((tm, N), lambda i: (i, 0)))
        args.append(res)
    return pl.pallas_call(
        functools.partial(_mm_kernel, has_gain=gain is not None, has_res=res is not None, precise=precise),
        grid=(M // tm,),
        in_specs=in_specs,
        out_specs=pl.BlockSpec((tm, N), lambda i: (i, 0)),
        out_shape=jax.ShapeDtypeStruct((M, N), out_dtype),
        compiler_params=pltpu.CompilerParams(
            dimension_semantics=("parallel",), vmem_limit_bytes=VMEM_LIMIT_BYTES),
        name="fused_matmul",
    )(*args)


def _ffn_kernel(x_ref, wg_ref, wu_ref, wd_ref, gate_ref, o_ref):
    x = x_ref[0]
    g = jnp.dot(x, wg_ref[0], preferred_element_type=F32)
    u = jnp.dot(x, wu_ref[0], preferred_element_type=F32)
    h = (g * jax.nn.sigmoid(g)) * u
    y = jnp.dot(h.astype(BF16), wd_ref[0], preferred_element_type=F32)
    o_ref[0] = y * gate_ref[0]


def _expert_ffn(xe, gate, wg, wu, wd, tm=512):
    E, cap, D = xe.shape
    FF = wg.shape[2]
    assert cap % tm == 0
    return pl.pallas_call(
        _ffn_kernel,
        grid=(E, cap // tm),
        in_specs=[pl.BlockSpec((1, tm, D), lambda e, i: (e, i, 0)),
                  pl.BlockSpec((1, D, FF), lambda e, i: (e, 0, 0)),
                  pl.BlockSpec((1, D, FF), lambda e, i: (e, 0, 0)),
                  pl.BlockSpec((1, FF, D), lambda e, i: (e, 0, 0)),
                  pl.BlockSpec((1, tm, 1), lambda e, i: (e, i, 0))],
        out_specs=pl.BlockSpec((1, tm, D), lambda e, i: (e, i, 0)),
        out_shape=jax.ShapeDtypeStruct((E, cap, D), F32),
        compiler_params=pltpu.CompilerParams(
            dimension_semantics=("parallel", "parallel"), vmem_limit_bytes=VMEM_LIMIT_BYTES),
        name="expert_ffn",
    )(xe, wg, wu, wd, gate.reshape(E, cap, 1))


def _rms_norm(x, g):
    xf = x.astype(F32)
    y = xf * lax.rsqrt(jnp.mean(xf * xf, axis=-1, keepdims=True) + RMS_EPS)
    return y * g.astype(F32)


def _t5_bucket(rel):
    half = T5_BUCKETS // 2
    max_exact = half // 2
    n = np.abs(rel)
    large = max_exact + (np.log(np.maximum(n, 1) / max_exact) / np.log(T5_MAX_DISTANCE / max_exact)
                         * (half - max_exact)).astype(np.int32)
    large = np.minimum(large, half - 1)
    return (np.where(rel > 0, half, 0) + np.where(n < max_exact, n, large)).astype(np.int32)


def _dilated_group_attention(q, k, v, bias_table, window, dilation):
    Bn, S, H, dh = q.shape
    radius = window // (2 * dilation)
    L = S // dilation
    nb = -(-L // A_BLOCK)
    Lp = nb * A_BLOCK

    def to_blocks(t):
        t = t.reshape(Bn, L, dilation, H, dh).transpose(0, 2, 1, 3, 4)
        t = jnp.pad(t, ((0, 0), (0, 0), (0, Lp - L), (0, 0), (0, 0)))
        return t.reshape(Bn, dilation, nb, A_BLOCK, H, dh)

    def band(t):
        tp = jnp.pad(t, ((0, 0), (0, 0), (1, 1), (0, 0), (0, 0), (0, 0)))
        return jnp.concatenate([tp[:, :, :-2], tp[:, :, 1:-1], tp[:, :, 2:]], axis=3)

    qb = to_blocks(q)
    kband = band(to_blocks(k))
    vband = band(to_blocks(v))
    qi = np.arange(A_BLOCK)[:, None]
    ci = np.arange(3 * A_BLOCK)[None, :]
    off = ci - A_BLOCK - qi
    key_m = np.arange(nb)[:, None, None] * A_BLOCK + ci[None] - A_BLOCK
    valid = (np.abs(off) <= radius)[None] & (key_m >= 0) & (key_m < L)
    bias = jnp.transpose(bias_table[_t5_bucket(off * dilation)], (2, 0, 1)).astype(F32)
    s = jnp.einsum('brnqhd,brnkhd->brnhqk', qb, kband, preferred_element_type=F32) * (dh ** -0.5)
    s = jnp.where(valid[None, None, :, None], s + bias[None, None, None], NEG_INF)
    m = jnp.max(s, axis=-1, keepdims=True)
    p = jnp.exp(s - m)
    den = jnp.sum(p, axis=-1, keepdims=True)
    o = jnp.einsum('brnhqk,brnkhd->brnqhd', (p / den).astype(v.dtype), vband, preferred_element_type=F32)
    lse = (m + jnp.log(den))[..., 0]
    o = o.reshape(Bn, dilation, Lp, H, dh)[:, :, :L].transpose(0, 2, 1, 3, 4).reshape(Bn, S, H, dh)
    lse = lse.transpose(0, 1, 2, 4, 3).reshape(Bn, dilation, Lp, H)[:, :, :L].transpose(0, 2, 1, 3).reshape(Bn, S, H)
    return o, lse


def _mixer_a(u, t5_table):
    Bn, S, _ = u.shape
    u = u.reshape(Bn, S, 3, A_GROUPS, A_HEADS_PER_GROUP, HEAD_DIM)
    outs, lses = [], []
    for g, (window, dilation) in enumerate(A_PATTERNS):
        tab = t5_table[:, g * A_HEADS_PER_GROUP:(g + 1) * A_HEADS_PER_GROUP]
        o, l = _dilated_group_attention(u[:, :, 0, g], u[:, :, 1, g], u[:, :, 2, g], tab, window, dilation)
        outs.append(o)
        lses.append(l)
    wts = jax.nn.softmax(jnp.stack(lses), axis=0)
    o = jnp.einsum('gbsh,gbshd->bshd', wts, jnp.stack(outs).astype(F32))
    return o.reshape(Bn, S, A_OUT)


def _rope_tables(S):
    inv = 1.0 / (ROPE_THETA ** (np.arange(0, B_ROPE, 2, dtype=np.float32) / B_ROPE))
    ang = np.arange(S, dtype=np.float32)[:, None] * inv[None]
    return jnp.asarray(np.cos(ang), dtype=F32), jnp.asarray(np.sin(ang), dtype=F32)


def _apply_rope(x, cos, sin):
    xf = x.astype(F32)
    x1, x2 = jnp.split(xf, 2, axis=-1)
    return jnp.concatenate([x1 * cos - x2 * sin, x2 * cos + x1 * sin], axis=-1)


def _mixer_b(u, g_qa, w_qb, g_kva, w_kvb):
    Bn, S, _ = u.shape
    T = Bn * S
    c_q, c_kv, k_r = jnp.split(u, [B_Q_RANK, B_Q_RANK + B_KV_RANK], axis=-1)
    q = _matmul(c_q.reshape(T, B_Q_RANK), w_qb, gain=g_qa, out_dtype=F32).reshape(Bn, S, B_HEADS, B_NOPE + B_ROPE)
    kv = _matmul(c_kv.reshape(T, B_KV_RANK), w_kvb, gain=g_kva, out_dtype=F32).reshape(Bn, S, B_HEADS, B_NOPE + B_V)
    q_nope, q_rope = q[..., :B_NOPE], q[..., B_NOPE:]
    k_nope, v = kv[..., :B_NOPE], kv[..., B_NOPE:]
    cos, sin = _rope_tables(S)
    q_rope = _apply_rope(q_rope, cos[:, None, :], sin[:, None, :])
    k_rope = _apply_rope(k_r, cos, sin)
    scale = (B_NOPE + B_ROPE) ** -0.5
    nq = S // B_QBLOCK

    def to_qblocks(t):
        return t.reshape(Bn, nq, B_QBLOCK, *t.shape[2:]).swapaxes(0, 1)

    k_nope_b = k_nope.astype(BF16)
    k_rope_b = k_rope.astype(BF16)
    v_b = v.astype(BF16)

    def attend(args):
        qn, qr = args
        s = (jnp.einsum('bqhd,bkhd->bhqk', qn.astype(BF16), k_nope_b, preferred_element_type=F32)
             + jnp.einsum('bqhr,bkr->bhqk', qr.astype(BF16), k_rope_b, preferred_element_type=F32)) * scale
        p = jax.nn.softmax(s, axis=-1).astype(BF16)
        return jnp.einsum('bhqk,bkhd->bqhd', p, v_b, preferred_element_type=F32)

    o = lax.map(attend, (to_qblocks(q_nope), to_qblocks(q_rope)))
    return o.swapaxes(0, 1).reshape(Bn, S, B_OUT)


def _mixer_c(u, rpb):
    Bn, S, _ = u.shape
    rows = S // GRID_W
    wr = min(C_WIN_ROWS, rows)
    u = u.reshape(Bn, rows, GRID_W, 3, C_HEADS, HEAD_DIM)
    q, k, v = u[:, :, :, 0], u[:, :, :, 1], u[:, :, :, 2]
    n_cb = GRID_W // C_COL_BLOCK
    qcol = np.arange(GRID_W).reshape(n_cb, C_COL_BLOCK)
    kstart = np.clip(np.arange(n_cb) * C_COL_BLOCK - C_WIN_COLS // 2, 0, GRID_W - C_KEY_COLS)
    kcol = kstart[:, None] + np.arange(C_KEY_COLS)[None]
    wstart = np.clip(qcol - C_WIN_COLS // 2, 0, GRID_W - C_WIN_COLS)
    col_valid = (kcol[:, None, :] >= wstart[..., None]) & (kcol[:, None, :] < wstart[..., None] + C_WIN_COLS)
    dcol_idx = np.clip(kcol[:, None, :] - qcol[..., None] + C_WIN_COLS - 1, 0, 2 * C_WIN_COLS - 2)
    rpb_cols = rpb.astype(F32)[:, :, dcol_idx]
    k_cols = k[:, :, kcol]
    v_cols = v[:, :, kcol]
    mask = col_valid[None, None, :, :, None, :]
    scale = HEAD_DIM ** -0.5

    def attend_row(r):
        r0 = jnp.clip(r - wr // 2, 0, rows - wr)
        kr = lax.dynamic_slice_in_dim(k_cols, r0, wr, axis=1)
        vr = lax.dynamic_slice_in_dim(v_cols, r0, wr, axis=1)
        qr = lax.dynamic_index_in_dim(q, r, axis=1, keepdims=False).reshape(Bn, n_cb, C_COL_BLOCK, C_HEADS, HEAD_DIM)
        s = jnp.einsum('bjqhd,bwjchd->bhjqwc', qr, kr, preferred_element_type=F32) * scale
        drow = r0 + jnp.arange(wr) - r + C_WIN_ROWS - 1
        bias = jnp.take(rpb_cols, drow, axis=1).transpose(0, 2, 3, 1, 4)
        s = jnp.where(mask, s + bias[None], NEG_INF)
        shp = s.shape
        p = jax.nn.softmax(s.reshape(Bn, C_HEADS, n_cb, C_COL_BLOCK, wr * C_KEY_COLS), axis=-1).reshape(shp)
        o = jnp.einsum('bhjqwc,bwjchd->bjqhd', p.astype(vr.dtype), vr, preferred_element_type=F32)
        return o.reshape(Bn, GRID_W, C_OUT)

    o = lax.map(attend_row, jnp.arange(rows))
    return o.transpose(1, 0, 2, 3).reshape(Bn, S, C_OUT)


def _moe(x2, ln_g, router, wg, wu, wd):
    T, D = x2.shape
    cap = CAPACITY_FACTOR * T // N_EXPERTS
    logits = _matmul(x2, router, gain=ln_g, out_dtype=F32, precise=True)
    aff = jax.nn.softmax(logits, axis=-1)
    gate, idx = lax.top_k(aff.T, cap)
    h = _rms_norm(x2, ln_g).astype(BF16)
    xe = h[idx]
    ye = _expert_ffn(xe, gate, wg, wu, wd)
    return x2.at[idx.reshape(-1)].add(ye.reshape(-1, D))


def _trunk(x, p):
    Bn, S, D = x.shape
    T = Bn * S
    x2 = x.reshape(T, D)
    u = _matmul(x2, p["w_in_even"], gain=p["ln_mix_g"][0], out_dtype=BF16)
    u3 = u.reshape(Bn, S, -1)
    a = _mixer_a(u3[..., :A_IN], p["t5_table"])
    b = _mixer_b(u3[..., A_IN:], p["g_qa"], p["w_qb"], p["g_kva"], p["w_kvb"])
    ab = jnp.concatenate([a.astype(BF16), b.astype(BF16)], axis=-1).reshape(T, -1)
    x2 = _matmul(ab, p["w_out_even"], res=x2, out_dtype=F32)
    x2 = _moe(x2, p["ln_ffn_g"][0], p["router"][0], p["w_gate"][0], p["w_up"][0], p["w_down"][0])
    u = _matmul(x2, p["w_in_odd"], gain=p["ln_mix_g"][1], out_dtype=BF16)
    c = _mixer_c(u.reshape(Bn, S, -1), p["rpb"])
    x2 = _matmul(c.astype(BF16).reshape(T, -1), p["w_out_odd"], res=x2, out_dtype=F32)
    x2 = _moe(x2, p["ln_ffn_g"][1], p["router"][1], p["w_gate"][1], p["w_up"][1], p["w_down"][1])
    return _rms_norm(x2, p["final_g"]).reshape(Bn, S, D)


def kernel(x_prompt, x_sample, ln_mix_g, w_in_even, g_qa, w_qb, g_kva, w_kvb, t5_table, w_out_even,
           w_in_odd, rpb, w_out_odd, ln_ffn_g, router, w_gate, w_up, w_down, final_g):
    p = dict(
        ln_mix_g=ln_mix_g, ln_ffn_g=ln_ffn_g, final_g=final_g,
        w_in_even=w_in_even[0].astype(BF16), g_qa=g_qa[0], w_qb=w_qb[0].astype(BF16),
        g_kva=g_kva[0], w_kvb=w_kvb[0].astype(BF16), t5_table=t5_table,
        w_out_even=w_out_even[0].astype(BF16), w_in_odd=w_in_odd[0].astype(BF16), rpb=rpb[0],
        w_out_odd=w_out_odd[0].astype(BF16), router=router,
        w_gate=w_gate.astype(BF16), w_up=w_up.astype(BF16), w_down=w_down.astype(BF16),
    )
    return _trunk(x_prompt, p), _trunk(x_sample, p)
```

```python
import functools

import jax
import jax.numpy as jnp
import numpy as np
from jax import lax
from jax.experimental import pallas as pl
from jax.experimental.pallas import tpu as pltpu

D_MODEL = 1024
RMS_EPS = 1e-6
HEAD_DIM = 64
NEG_INF = -1e30
A_PATTERNS = ((128, 1), (512, 4), (2048, 16))
A_GROUPS = 3
A_HEADS_PER_GROUP = 4
A_HEADS = A_GROUPS * A_HEADS_PER_GROUP
A_IN = 3 * A_HEADS * HEAD_DIM
A_OUT = A_HEADS_PER_GROUP * HEAD_DIM
T5_BUCKETS = 32
T5_MAX_DISTANCE = 1024
B_HEADS = 12
B_Q_RANK = 384
B_KV_RANK = 256
B_NOPE = 64
B_ROPE = 32
B_V = 64
B_OUT = B_HEADS * B_V
ROPE_THETA = 10000.0
GRID_W = 64
C_HEADS = 16
C_WIN_ROWS = 8
C_WIN_COLS = 16
C_IN = 3 * C_HEADS * HEAD_DIM
C_OUT = C_HEADS * HEAD_DIM
N_EXPERTS = 16
CAPACITY_FACTOR = 2

LANES = 128
EVEN_COLS = A_IN + B_KV_RANK + LANES + B_Q_RANK
VMEM_LIMIT_BYTES = 48 * 1024 * 1024

F32 = jnp.float32
BF16 = jnp.bfloat16
_NT = (((1,), (1,)), ((), ()))


def _params(*sem):
    return pltpu.CompilerParams(dimension_semantics=sem, vmem_limit_bytes=VMEM_LIMIT_BYTES)


def _rms(x, g):
    xf = x.astype(F32)
    return xf * lax.rsqrt(jnp.mean(xf * xf, axis=-1, keepdims=True) + RMS_EPS) * g


def _mm_kernel(*refs, n_in, has_gain, has_res, precise):
    xs, ws = refs[:n_in], refs[n_in:2 * n_in]
    pos = 2 * n_in
    g_ref = res_ref = None
    if has_gain:
        g_ref = refs[pos]
        pos += 1
    if has_res:
        res_ref = refs[pos]
        pos += 1
    o_ref = refs[pos]
    acc = None
    for x_ref, w_ref in zip(xs, ws):
        x = x_ref[...]
        if has_gain:
            x = _rms(x, g_ref[...])
        if precise:
            part = jnp.dot(x.astype(F32), w_ref[...], preferred_element_type=F32,
                           precision=lax.Precision.HIGHEST)
        else:
            part = jnp.dot(x.astype(BF16), w_ref[...], preferred_element_type=F32)
        acc = part if acc is None else acc + part
    if has_res:
        acc = acc + res_ref[...]
    o_ref[...] = acc.astype(o_ref.dtype)


def _matmul(xs, ws, gain=None, res=None, out_dtype=F32, tm=512, precise=False):
    if not isinstance(xs, (list, tuple)):
        xs, ws = [xs], [ws]
    M = xs[0].shape[0]
    N = ws[0].shape[1]
    assert M % tm == 0
    in_specs = [pl.BlockSpec((tm, x.shape[1]), lambda i: (i, 0)) for x in xs]
    in_specs += [pl.BlockSpec(w.shape, lambda i: (0, 0)) for w in ws]
    args = list(xs) + list(ws)
    if gain is not None:
        assert len(xs) == 1
        in_specs.append(pl.BlockSpec((1, gain.shape[-1]), lambda i: (0, 0)))
        args.append(gain.reshape(1, -1).astype(F32))
    if res is not None:
        in_specs.append(pl.BlockSpec((tm, N), lambda i: (i, 0)))
        args.append(res)
    return pl.pallas_call(
        functools.partial(_mm_kernel, n_in=len(xs), has_gain=gain is not None,
                          has_res=res is not None, precise=precise),
        grid=(M // tm,),
        in_specs=in_specs,
        out_specs=pl.BlockSpec((tm, N), lambda i: (i, 0)),
        out_shape=jax.ShapeDtypeStruct((M, N), out_dtype),
        compiler_params=_params("parallel"),
        name="fused_matmul",
    )(*args)


def _ffn_kernel(x_ref, wg_ref, wu_ref, wd_ref, gate_ref, o_ref):
    x = x_ref[0]
    g = jnp.dot(x, wg_ref[0], preferred_element_type=F32)
    u = jnp.dot(x, wu_ref[0], preferred_element_type=F32)
    h = (g * jax.nn.sigmoid(g)) * u
    y = jnp.dot(h.astype(BF16), wd_ref[0], preferred_element_type=F32)
    o_ref[0] = y * gate_ref[0]


def _expert_ffn(xe, gate, wg, wu, wd, tm=512):
    E, cap, D = xe.shape
    FF = wg.shape[2]
    assert cap % tm == 0
    return pl.pallas_call(
        _ffn_kernel,
        grid=(E, cap // tm),
        in_specs=[pl.BlockSpec((1, tm, D), lambda e, i: (e, i, 0)),
                  pl.BlockSpec((1, D, FF), lambda e, i: (e, 0, 0)),
                  pl.BlockSpec((1, D, FF), lambda e, i: (e, 0, 0)),
                  pl.BlockSpec((1, FF, D), lambda e, i: (e, 0, 0)),
                  pl.BlockSpec((1, tm, 1), lambda e, i: (e, i, 0))],
        out_specs=pl.BlockSpec((1, tm, D), lambda e, i: (e, i, 0)),
        out_shape=jax.ShapeDtypeStruct((E, cap, D), F32),
        compiler_params=_params("parallel", "parallel"),
        name="expert_ffn",
    )(xe, wg, wu, wd, gate.reshape(E, cap, 1))


def _t5_bucket(rel):
    half = T5_BUCKETS // 2
    max_exact = half // 2
    n = np.abs(rel)
    large = max_exact + (np.log(np.maximum(n, 1) / max_exact) / np.log(T5_MAX_DISTANCE / max_exact)
                         * (half - max_exact)).astype(np.int32)
    large = np.minimum(large, half - 1)
    return (np.where(rel > 0, half, 0) + np.where(n < max_exact, n, large)).astype(np.int32)


def _dilated_tiles(L, radius):
    tq = min(LANES, L)
    kw = min(tq + 2 * radius, L)
    return tq, kw


def _dilated_bias(t5_table, g, L, window, dilation):
    radius = window // (2 * dilation)
    tq, kw = _dilated_tiles(L, radius)
    deltas = (0, -((kw - tq) // 2), tq - kw)
    row = np.arange(tq)[:, None]
    col = np.arange(kw)[None, :]
    rel = np.stack([col - row + d for d in deltas])
    valid = np.abs(rel) <= radius
    tab = t5_table[:, g * A_HEADS_PER_GROUP:(g + 1) * A_HEADS_PER_GROUP].astype(F32)
    bias = jnp.transpose(tab[_t5_bucket(rel * dilation)], (0, 3, 1, 2))
    return jnp.where(valid[:, None], bias, NEG_INF)


def _dilated_kernel(q_ref, k_ref, v_ref, bias_ref, o_ref, lse_ref, *, L, tq, kw):
    nq = L // tq
    lane = lax.broadcasted_iota(jnp.int32, (1, LANES), 1)
    lo = lane < HEAD_DIM

    def body(i, carry):
        q0 = pl.multiple_of(i * tq, tq)
        start = jnp.clip(q0 - (kw - tq) // 2, 0, L - kw)
        start = pl.multiple_of(start, 16)
        var = jnp.where(i == 0, 0, jnp.where(i == nq - 1, 2, 1))
        for pair in range(A_HEADS_PER_GROUP // 2):
            cs = slice(pair * LANES, (pair + 1) * LANES)
            q2 = q_ref[0, pl.ds(q0, tq), cs]
            k2 = k_ref[0, pl.ds(start, kw), cs]
            v2 = v_ref[0, pl.ds(start, kw), cs]
            outs, lses = [], []
            for sub in range(2):
                qm = jnp.where(lo if sub == 0 else ~lo, q2, jnp.zeros_like(q2))
                s = lax.dot_general(qm, k2, _NT, preferred_element_type=F32) * (HEAD_DIM ** -0.5)
                s = s + bias_ref[var, 2 * pair + sub]
                m = jnp.max(s, axis=-1, keepdims=True)
                p = jnp.exp(s - m)
                den = jnp.sum(p, axis=-1, keepdims=True)
                outs.append(jnp.dot((p / den).astype(BF16), v2, preferred_element_type=F32))
                lses.append(m + jnp.log(den))
            o_ref[0, pl.ds(q0, tq), cs] = jnp.where(lo, outs[0], outs[1])
            lse_ref[0, pl.ds(q0, tq), cs] = jnp.where(lo, lses[0], lses[1])
        return carry

    lax.fori_loop(0, nq, body, 0)


def _dilated_attention(u3, bias, g, window, dilation):
    Bn, S, C = u3.shape
    L = S // dilation
    radius = window // (2 * dilation)
    tq, kw = _dilated_tiles(L, radius)
    assert L % tq == 0 and kw % 16 == 0 and tq % 16 == 0
    uv = u3.reshape(Bn, L, dilation * C)
    cb = C // A_OUT
    n_col = A_HEADS * HEAD_DIM // A_OUT

    def spec(which):
        return pl.BlockSpec((1, L, A_OUT), lambda b, r: (b, 0, r * cb + which * n_col + g))

    out_spec = pl.BlockSpec((1, L, A_OUT), lambda b, r: (b, 0, r))
    o, lse = pl.pallas_call(
        functools.partial(_dilated_kernel, L=L, tq=tq, kw=kw),
        grid=(Bn, dilation),
        in_specs=[spec(0), spec(1), spec(2), pl.BlockSpec(bias.shape, lambda b, r: (0, 0, 0, 0))],
        out_specs=[out_spec, out_spec],
        out_shape=[jax.ShapeDtypeStruct((Bn, L, dilation * A_OUT), F32)] * 2,
        compiler_params=_params("parallel", "parallel"),
        name="dilated_attention",
    )(uv, uv, uv, bias)
    return o.reshape(Bn * S, A_OUT), lse.reshape(Bn * S, A_OUT)


def _merge_kernel(o0, o1, o2, l0, l1, l2, out_ref):
    ls = [l0[...], l1[...], l2[...]]
    m = jnp.maximum(jnp.maximum(ls[0], ls[1]), ls[2])
    es = [jnp.exp(l - m) for l in ls]
    den = es[0] + es[1] + es[2]
    out = (es[0] / den) * o0[...] + (es[1] / den) * o1[...] + (es[2] / den) * o2[...]
    out_ref[...] = out.astype(out_ref.dtype)


def _merge_groups(os_, ls_, tm=1024):
    T = os_[0].shape[0]
    spec = pl.BlockSpec((tm, A_OUT), lambda i: (i, 0))
    return pl.pallas_call(
        _merge_kernel, grid=(T // tm,), in_specs=[spec] * 6, out_specs=spec,
        out_shape=jax.ShapeDtypeStruct((T, A_OUT), BF16),
        compiler_params=_params("parallel"), name="merge_groups",
    )(*os_, *ls_)


def _rope_tables(S):
    inv = 1.0 / (ROPE_THETA ** (np.arange(0, B_ROPE, 2, dtype=np.float32) / B_ROPE))
    ang = np.arange(S, dtype=np.float32)[:, None] * inv[None]
    cos, sin = np.cos(ang), np.sin(ang)
    half = B_ROPE // 2
    c = np.ones((S, LANES), np.float32)
    c[:, B_NOPE:B_NOPE + half] = cos
    c[:, B_NOPE + half:B_NOPE + B_ROPE] = cos
    s_lo = np.zeros((S, LANES), np.float32)
    s_lo[:, B_NOPE:B_NOPE + half] = -sin
    s_hi = np.zeros((S, LANES), np.float32)
    s_hi[:, B_NOPE + half:B_NOPE + B_ROPE] = sin
    return jnp.asarray(c), jnp.asarray(s_lo), jnp.asarray(s_hi)


def _rope(x, c, s_lo, s_hi):
    half = B_ROPE // 2
    return x * c + pltpu.roll(x, LANES - half, 1) * s_lo + pltpu.roll(x, half, 1) * s_hi


def _mla_prep_kernel(cq_ref, ckv_ref, kr_ref, gq_ref, gkv_ref, wq_ref, wk_ref, wv_ref,
                     c_ref, slo_ref, shi_ref, q_ref, k_ref, v_ref):
    c, s_lo, s_hi = c_ref[...], slo_ref[...], shi_ref[...]
    scale = (B_NOPE + B_ROPE) ** -0.5
    xq = _rms(cq_ref[...], gq_ref[...]).astype(BF16)
    q = jnp.dot(xq, wq_ref[...], preferred_element_type=F32)
    xkv = _rms(ckv_ref[...], gkv_ref[...]).astype(BF16)
    k = jnp.dot(xkv, wk_ref[...], preferred_element_type=F32)
    v_ref[...] = jnp.dot(xkv, wv_ref[...], preferred_element_type=F32).astype(v_ref.dtype)
    k_rope = _rope(kr_ref[...].astype(F32), c, s_lo, s_hi)
    for h in range(B_HEADS):
        cs = slice(h * LANES, (h + 1) * LANES)
        q_ref[:, cs] = (_rope(q[:, cs], c, s_lo, s_hi) * scale).astype(q_ref.dtype)
        k_ref[:, cs] = (k[:, cs] + k_rope).astype(k_ref.dtype)


def _mla_prep(u3, g_qa, g_kva, wq, wk, wv, ts=512):
    Bn, S, C = u3.shape
    ts = min(ts, S)
    tabs = _rope_tables(S)
    ns = S // ts
    u2 = u3.reshape(Bn * S, C)
    row = lambda b, i: b * ns + i
    off_kv, off_kr, off_q = A_IN // B_KV_RANK, (A_IN + B_KV_RANK) // LANES, (A_IN + B_KV_RANK + LANES) // B_Q_RANK
    full = lambda a: pl.BlockSpec(a.shape, lambda b, i: (0, 0))
    tab_spec = pl.BlockSpec((ts, LANES), lambda b, i: (i, 0))
    hw = B_HEADS * LANES
    return pl.pallas_call(
        _mla_prep_kernel,
        grid=(Bn, ns),
        in_specs=[pl.BlockSpec((ts, B_Q_RANK), lambda b, i: (row(b, i), off_q)),
                  pl.BlockSpec((ts, B_KV_RANK), lambda b, i: (row(b, i), off_kv)),
                  pl.BlockSpec((ts, LANES), lambda b, i: (row(b, i), off_kr)),
                  full(g_qa), full(g_kva), full(wq), full(wk), full(wv),
                  tab_spec, tab_spec, tab_spec],
        out_specs=[pl.BlockSpec((ts, hw), lambda b, i: (row(b, i), 0)),
                   pl.BlockSpec((ts, hw), lambda b, i: (row(b, i), 0)),
                   pl.BlockSpec((ts, B_OUT), lambda b, i: (row(b, i), 0))],
        out_shape=[jax.ShapeDtypeStruct((Bn * S, hw), BF16),
                   jax.ShapeDtypeStruct((Bn * S, hw), BF16),
                   jax.ShapeDtypeStruct((Bn * S, B_OUT), BF16)],
        compiler_params=_params("parallel", "parallel"),
        name="mla_prep",
    )(u2, u2, u2, g_qa, g_kva, wq, wk, wv, *tabs)


def _mla_kernel(q_ref, k_ref, v_ref, o_ref, *, tk):
    S = k_ref.shape[1]
    tq = q_ref.shape[1]
    nk = S // tk
    lane = lax.broadcasted_iota(jnp.int32, (1, LANES), 1)
    qs = [q_ref[0, :, sub * LANES:(sub + 1) * LANES] for sub in range(2)]

    def body(j, carry):
        k0 = pl.multiple_of(j * tk, tk)
        vt = v_ref[0, pl.ds(k0, tk), :]
        new = []
        for sub in range(2):
            m, l, acc = carry[sub]
            kt = k_ref[0, pl.ds(k0, tk), sub * LANES:(sub + 1) * LANES]
            s = lax.dot_general(qs[sub], kt, _NT, preferred_element_type=F32)
            m_new = jnp.maximum(m, jnp.max(s, axis=-1, keepdims=True))
            alpha = jnp.exp(m - m_new)
            p = jnp.exp(s - m_new)
            l = alpha * l + jnp.sum(p, axis=-1, keepdims=True)
            acc = alpha * acc + jnp.dot(p.astype(BF16), vt, preferred_element_type=F32)
            new.append((m_new, l, acc))
        return tuple(new)

    init = tuple((jnp.full((tq, 1), NEG_INF, F32), jnp.zeros((tq, 1), F32), jnp.zeros((tq, LANES), F32))
                 for _ in range(2))
    res = lax.fori_loop(0, nk, body, init)
    outs = [acc / l for (_, l, acc) in res]
    o_ref[0] = jnp.where(lane < B_V, outs[0], outs[1]).astype(o_ref.dtype)


def _mla_attention(q, k, v, Bn, S, tq=512, tk=512):
    tq, tk = min(tq, S), min(tk, S)
    hw = B_HEADS * LANES
    q3, k3, v3 = q.reshape(Bn, S, hw), k.reshape(Bn, S, hw), v.reshape(Bn, S, B_OUT)
    o = pl.pallas_call(
        functools.partial(_mla_kernel, tk=tk),
        grid=(Bn, B_HEADS // 2, S // tq),
        in_specs=[pl.BlockSpec((1, tq, 2 * LANES), lambda b, h, i: (b, i, h)),
                  pl.BlockSpec((1, S, 2 * LANES), lambda b, h, i: (b, 0, h)),
                  pl.BlockSpec((1, S, LANES), lambda b, h, i: (b, 0, h))],
        out_specs=pl.BlockSpec((1, tq, LANES), lambda b, h, i: (b, i, h)),
        out_shape=jax.ShapeDtypeStruct((Bn, S, B_OUT), BF16),
        compiler_params=_params("parallel", "parallel", "parallel"),
        name="mla_attention",
    )(q3, k3, v3)
    return o.reshape(Bn * S, B_OUT)


def _natten_bias(rpb):
    W = GRID_W
    c = np.arange(W)[:, None]
    kc = np.arange(W)[None, :]
    wstart = np.clip(c - C_WIN_COLS // 2, 0, W - C_WIN_COLS)
    col_valid = (kc >= wstart) & (kc < wstart + C_WIN_COLS)
    dcol = np.clip(kc - c + C_WIN_COLS - 1, 0, 2 * C_WIN_COLS - 2)
    off = np.arange(C_WIN_ROWS)[:, None]
    j = np.arange(C_WIN_ROWS)[None, :]
    drow = j - off + C_WIN_ROWS - 1
    b = rpb.astype(F32)[:, drow][:, :, :, dcol]
    b = jnp.where(col_valid[None, None, None], b, NEG_INF)
    b = jnp.transpose(b, (1, 0, 3, 2, 4))
    return b.reshape(C_WIN_ROWS, C_HEADS, W, C_WIN_ROWS * W)


def _natten_kernel(q_ref, k_ref, v_ref, bias_ref, o_ref, *, rows):
    r = pl.program_id(1)
    r0 = jnp.clip(r - C_WIN_ROWS // 2, 0, rows - C_WIN_ROWS)
    k0 = pl.multiple_of(r0 * GRID_W, GRID_W)
    nk = C_WIN_ROWS * GRID_W
    lane = lax.broadcasted_iota(jnp.int32, (1, LANES), 1)
    lo = lane < HEAD_DIM
    for pair in range(C_HEADS // 2):
        cs = slice(pair * LANES, (pair + 1) * LANES)
        q2 = q_ref[0, :, cs]
        k2 = k_ref[0, pl.ds(k0, nk), cs]
        v2 = v_ref[0, pl.ds(k0, nk), cs]
        outs = []
        for sub in range(2):
            qm = jnp.where(lo if sub == 0 else ~lo, q2, jnp.zeros_like(q2))
            s = lax.dot_general(qm, k2, _NT, preferred_element_type=F32) * (HEAD_DIM ** -0.5)
            s = s + bias_ref[0, 2 * pair + sub]
            m = jnp.max(s, axis=-1, keepdims=True)
            p = jnp.exp(s - m)
            den = jnp.sum(p, axis=-1, keepdims=True)
            outs.append(jnp.dot((p / den).astype(BF16), v2, preferred_element_type=F32))
        o_ref[0, :, cs] = jnp.where(lo, outs[0], outs[1]).astype(o_ref.dtype)


def _natten(u3, bias):
    Bn, S, _ = u3.shape
    rows = S // GRID_W
    assert rows >= C_WIN_ROWS

    def variant(b, r):
        return (r - jnp.clip(r - C_WIN_ROWS // 2, 0, rows - C_WIN_ROWS), 0, 0, 0)

    o = pl.pallas_call(
        functools.partial(_natten_kernel, rows=rows),
        grid=(Bn, rows),
        in_specs=[pl.BlockSpec((1, GRID_W, C_OUT), lambda b, r: (b, r, 0)),
                  pl.BlockSpec((1, S, C_OUT), lambda b, r: (b, 0, 1), pipeline_mode=pl.Buffered(1)),
                  pl.BlockSpec((1, S, C_OUT), lambda b, r: (b, 0, 2), pipeline_mode=pl.Buffered(1)),
                  pl.BlockSpec((1, C_HEADS, GRID_W, C_WIN_ROWS * GRID_W), variant)],
        out_specs=pl.BlockSpec((1, GRID_W, C_OUT), lambda b, r: (b, r, 0)),
        out_shape=jax.ShapeDtypeStruct((Bn, S, C_OUT), BF16),
        compiler_params=_params("parallel", "arbitrary"),
        name="natten",
    )(u3, u3, u3, bias)
    return o.reshape(Bn * S, C_OUT)


def _moe(x2, ln_g, router, wg, wu, wd):
    T, D = x2.shape
    cap = CAPACITY_FACTOR * T // N_EXPERTS
    logits = _matmul(x2, router, gain=ln_g, out_dtype=F32, precise=True)
    aff = jax.nn.softmax(logits, axis=-1)
    gate, idx = lax.top_k(aff.T, cap)
    h = _rms(x2, ln_g.astype(F32)).astype(BF16)
    xe = h[idx]
    ye = _expert_ffn(xe, gate, wg, wu, wd)
    return x2.at[idx.reshape(-1)].add(ye.reshape(-1, D))


def _trunk(x, p):
    Bn, S, D = x.shape
    T = Bn * S
    x2 = x.reshape(T, D)
    u = _matmul(x2, p["w_in_even"], gain=p["ln_mix_g"][0], out_dtype=BF16)
    u3 = u.reshape(Bn, S, EVEN_COLS)
    os_, ls_ = [], []
    for g, (window, dilation) in enumerate(A_PATTERNS):
        bias = _dilated_bias(p["t5_table"], g, S // dilation, window, dilation)
        o, l = _dilated_attention(u3, bias, g, window, dilation)
        os_.append(o)
        ls_.append(l)
    a = _merge_groups(os_, ls_)
    q, k, v = _mla_prep(u3, p["g_qa"], p["g_kva"], p["wq"], p["wk"], p["wv"])
    b = _mla_attention(q, k, v, Bn, S)
    x2 = _matmul([a, b], [p["w_out_even"][:A_OUT], p["w_out_even"][A_OUT:]], res=x2, out_dtype=F32)
    x2 = _moe(x2, p["ln_ffn_g"][0], p["router"][0], p["w_gate"][0], p["w_up"][0], p["w_down"][0])
    u = _matmul(x2, p["w_in_odd"], gain=p["ln_mix_g"][1], out_dtype=BF16)
    c = _natten(u.reshape(Bn, S, C_IN), p["natten_bias"])
    x2 = _matmul(c, p["w_out_odd"], res=x2, out_dtype=F32)
    x2 = _moe(x2, p["ln_ffn_g"][1], p["router"][1], p["w_gate"][1], p["w_up"][1], p["w_down"][1])
    return _rms(x2, p["final_g"].astype(F32)).reshape(Bn, S, D)


def _head_tiles(w, n_heads, per_head, keep):
    K = w.shape[0]
    w = w.reshape(K, n_heads, per_head)[:, :, :keep]
    return jnp.pad(w, ((0, 0), (0, 0), (0, LANES - keep))).reshape(K, n_heads * LANES)


def _prepare(ln_mix_g, w_in_even, g_qa, w_qb, g_kva, w_kvb, t5_table, w_out_even,
             w_in_odd, rpb, w_out_odd, ln_ffn_g, router, w_gate, w_up, w_down, final_g):
    w_in = w_in_even[0]
    w_a, w_cq = w_in[:, :A_IN], w_in[:, A_IN:A_IN + B_Q_RANK]
    w_ckv = w_in[:, A_IN + B_Q_RANK:A_IN + B_Q_RANK + B_KV_RANK]
    w_kr = w_in[:, A_IN + B_Q_RANK + B_KV_RANK:]
    w_kr_tile = jnp.pad(w_kr, ((0, 0), (B_NOPE, LANES - B_NOPE - B_ROPE)))
    w_even = jnp.concatenate([w_a, w_ckv, w_kr_tile, w_cq], axis=1)
    w_kv = w_kvb[0].reshape(B_KV_RANK, B_HEADS, B_NOPE + B_V)
    return dict(
        ln_mix_g=ln_mix_g, ln_ffn_g=ln_ffn_g, final_g=final_g, t5_table=t5_table,
        w_in_even=w_even.astype(BF16),
        g_qa=g_qa[0].reshape(1, -1).astype(F32), g_kva=g_kva[0].reshape(1, -1).astype(F32),
        wq=_head_tiles(w_qb[0], B_HEADS, B_NOPE + B_ROPE, B_NOPE + B_ROPE).astype(BF16),
        wk=_head_tiles(w_kvb[0], B_HEADS, B_NOPE + B_V, B_NOPE).astype(BF16),
        wv=w_kv[:, :, B_NOPE:].reshape(B_KV_RANK, B_OUT).astype(BF16),
        w_out_even=w_out_even[0].astype(BF16), w_in_odd=w_in_odd[0].astype(BF16),
        natten_bias=_natten_bias(rpb[0]), w_out_odd=w_out_odd[0].astype(BF16), router=router,
        w_gate=w_gate.astype(BF16), w_up=w_up.astype(BF16), w_down=w_down.astype(BF16),
    )


def kernel(x_prompt, x_sample, ln_mix_g, w_in_even, g_qa, w_qb, g_kva, w_kvb, t5_table, w_out_even,
           w_in_odd, rpb, w_out_odd, ln_ffn_g, router, w_gate, w_up, w_down, final_g):
    p = _prepare(ln_mix_g, w_in_even, g_qa, w_qb, g_kva, w_kvb, t5_table, w_out_even,
                 w_in_odd, rpb, w_out_odd, ln_ffn_g, router, w_gate, w_up, w_down, final_g)
    return _trunk(x_prompt, p), _trunk(x_sample, p)
```

```python
import functools

import jax
import jax.numpy as jnp
import numpy as np
from jax import lax
from jax.experimental import pallas as pl
from jax.experimental.pallas import tpu as pltpu

D_MODEL = 1024
RMS_EPS = 1e-6
HEAD_DIM = 64
NEG_INF = -1e30
A_PATTERNS = ((128, 1), (512, 4), (2048, 16))
A_GROUPS = 3
A_HEADS_PER_GROUP = 4
A_HEADS = A_GROUPS * A_HEADS_PER_GROUP
A_IN = 3 * A_HEADS * HEAD_DIM
A_OUT = A_HEADS_PER_GROUP * HEAD_DIM
T5_BUCKETS = 32
T5_MAX_DISTANCE = 1024
B_HEADS = 12
B_Q_RANK = 384
B_KV_RANK = 256
B_NOPE = 64
B_ROPE = 32
B_V = 64
B_OUT = B_HEADS * B_V
ROPE_THETA = 10000.0
GRID_W = 64
C_HEADS = 16
C_WIN_ROWS = 8
C_WIN_COLS = 16
C_IN = 3 * C_HEADS * HEAD_DIM
C_OUT = C_HEADS * HEAD_DIM
N_EXPERTS = 16
CAPACITY_FACTOR = 2

LANES = 128
EVEN_COLS = A_IN + B_KV_RANK + LANES + B_Q_RANK
VMEM_LIMIT_BYTES = 48 * 1024 * 1024

F32 = jnp.float32
BF16 = jnp.bfloat16
_NT = (((1,), (1,)), ((), ()))


def _params(*sem):
    return pltpu.CompilerParams(dimension_semantics=sem, vmem_limit_bytes=VMEM_LIMIT_BYTES)


def _rms(x, g):
    xf = x.astype(F32)
    return xf * lax.rsqrt(jnp.mean(xf * xf, axis=-1, keepdims=True) + RMS_EPS) * g


def _mm_kernel(*refs, n_in, has_gain, has_res, precise):
    xs, ws = refs[:n_in], refs[n_in:2 * n_in]
    pos = 2 * n_in
    g_ref = res_ref = None
    if has_gain:
        g_ref = refs[pos]
        pos += 1
    if has_res:
        res_ref = refs[pos]
        pos += 1
    o_ref = refs[pos]
    acc = None
    for x_ref, w_ref in zip(xs, ws):
        x = x_ref[...]
        if has_gain:
            x = _rms(x, g_ref[...])
        if precise:
            part = jnp.dot(x.astype(F32), w_ref[...], preferred_element_type=F32,
                           precision=lax.Precision.HIGHEST)
        else:
            part = jnp.dot(x.astype(BF16), w_ref[...], preferred_element_type=F32)
        acc = part if acc is None else acc + part
    if has_res:
        acc = acc + res_ref[...]
    o_ref[...] = acc.astype(o_ref.dtype)


def _matmul(xs, ws, gain=None, res=None, out_dtype=F32, tm=512, precise=False):
    if not isinstance(xs, (list, tuple)):
        xs, ws = [xs], [ws]
    M = xs[0].shape[0]
    N = ws[0].shape[1]
    assert M % tm == 0
    in_specs = [pl.BlockSpec((tm, x.shape[1]), lambda i: (i, 0)) for x in xs]
    in_specs += [pl.BlockSpec(w.shape, lambda i: (0, 0)) for w in ws]
    args = list(xs) + list(ws)
    if gain is not None:
        assert len(xs) == 1
        in_specs.append(pl.BlockSpec((1, gain.shape[-1]), lambda i: (0, 0)))
        args.append(gain.reshape(1, -1).astype(F32))
    if res is not None:
        in_specs.append(pl.BlockSpec((tm, N), lambda i: (i, 0)))
        args.append(res)
    return pl.pallas_call(
        functools.partial(_mm_kernel, n_in=len(xs), has_gain=gain is not None,
                          has_res=res is not None, precise=precise),
        grid=(M // tm,),
        in_specs=in_specs,
        out_specs=pl.BlockSpec((tm, N), lambda i: (i, 0)),
        out_shape=jax.ShapeDtypeStruct((M, N), out_dtype),
        compiler_params=_params("parallel"),
        name="fused_matmul",
    )(*args)


def _ffn_kernel(x_ref, wg_ref, wu_ref, wd_ref, gate_ref, o_ref):
    x = x_ref[0]
    g = jnp.dot(x, wg_ref[0], preferred_element_type=F32)
    u = jnp.dot(x, wu_ref[0], preferred_element_type=F32)
    h = (g * jax.nn.sigmoid(g)) * u
    y = jnp.dot(h.astype(BF16), wd_ref[0], preferred_element_type=F32)
    o_ref[0] = (y * gate_ref[0]).astype(o_ref.dtype)


def _expert_ffn(xe, gate, wg, wu, wd, tm=512):
    E, cap, D = xe.shape
    FF = wg.shape[2]
    assert cap % tm == 0
    return pl.pallas_call(
        _ffn_kernel,
        grid=(E, cap // tm),
        in_specs=[pl.BlockSpec((1, tm, D), lambda e, i: (e, i, 0)),
                  pl.BlockSpec((1, D, FF), lambda e, i: (e, 0, 0)),
                  pl.BlockSpec((1, D, FF), lambda e, i: (e, 0, 0)),
                  pl.BlockSpec((1, FF, D), lambda e, i: (e, 0, 0)),
                  pl.BlockSpec((1, tm, 1), lambda e, i: (e, i, 0))],
        out_specs=pl.BlockSpec((1, tm, D), lambda e, i: (e, i, 0)),
        out_shape=jax.ShapeDtypeStruct((E, cap, D), BF16),
        compiler_params=_params("parallel", "parallel"),
        name="expert_ffn",
    )(xe, wg, wu, wd, gate.reshape(E, cap, 1))


CHUNK = 256
COMBINE_TB = 512
COMBINE_W = 128


def _router_kernel(x_ref, g_ref, rt_ref, aff_ref, h_ref):
    xn = _rms(x_ref[...], g_ref[...])
    h_ref[...] = xn.astype(h_ref.dtype)
    logits = lax.dot_general(rt_ref[...], xn, _NT, preferred_element_type=F32,
                             precision=lax.Precision.HIGHEST)
    e = jnp.exp(logits - jnp.max(logits, axis=0, keepdims=True))
    aff_ref[...] = e / jnp.sum(e, axis=0, keepdims=True)


def _router(x2, ln_g, router, tm=512):
    T, D = x2.shape
    E = router.shape[1]
    return pl.pallas_call(
        _router_kernel,
        grid=(T // tm,),
        in_specs=[pl.BlockSpec((tm, D), lambda i: (i, 0)),
                  pl.BlockSpec((1, D), lambda i: (0, 0)),
                  pl.BlockSpec((E, D), lambda i: (0, 0))],
        out_specs=[pl.BlockSpec((E, tm), lambda i: (0, i)),
                   pl.BlockSpec((tm, D), lambda i: (i, 0))],
        out_shape=[jax.ShapeDtypeStruct((E, T), F32), jax.ShapeDtypeStruct((T, D), BF16)],
        compiler_params=_params("parallel"),
        name="router",
    )(x2, ln_g.reshape(1, D).astype(F32), router.T.astype(F32))


def _cumsum_excl(mask, upper, lower_strict):
    incl = jnp.dot(mask.astype(BF16), upper, preferred_element_type=F32)
    before = jnp.dot(lower_strict, incl.astype(BF16), preferred_element_type=F32)
    return incl + before[:, CHUNK - 1:CHUNK] - mask


def _select_kernel(aff_ref, slot_ref, pos_ref, *, cap):
    aff = aff_ref[0]
    nc = aff.shape[0]
    bits = pltpu.bitcast(aff, jnp.int32)

    def count(cond):
        c = jnp.sum(jnp.where(cond, 1.0, 0.0), axis=0, keepdims=True)
        return jnp.sum(c, axis=1, keepdims=True)

    def bit_step(i, prefix):
        cand = prefix | (jnp.int32(1) << (30 - i))
        return jnp.where(count(bits >= cand) >= cap, cand, prefix)

    thr = lax.fori_loop(0, 31, bit_step, jnp.zeros((1, 1), jnp.int32))
    r = lax.broadcasted_iota(jnp.int32, (CHUNK, CHUNK), 0)
    c = lax.broadcasted_iota(jnp.int32, (CHUNK, CHUNK), 1)
    upper = (r <= c).astype(BF16)
    r = lax.broadcasted_iota(jnp.int32, (nc, nc), 0)
    c = lax.broadcasted_iota(jnp.int32, (nc, nc), 1)
    lower_strict = (c < r).astype(BF16)
    gt = bits > thr
    eq = jnp.where(bits == thr, 1.0, 0.0)
    need = cap - count(gt)
    sel = jnp.where(gt | ((eq > 0) & (_cumsum_excl(eq, upper, lower_strict) < need)), 1.0, 0.0)
    pos = _cumsum_excl(sel, upper, lower_strict).astype(jnp.int32)
    pos_ref[0] = pos
    slot_ref[0] = jnp.where(sel > 0, pos, -1)


def _select(aff_t, cap):
    E, T = aff_t.shape
    nc = T // CHUNK
    assert nc <= CHUNK and nc % 8 == 0
    spec = pl.BlockSpec((1, nc, CHUNK), lambda e: (e, 0, 0))
    slot, pos = pl.pallas_call(
        functools.partial(_select_kernel, cap=cap),
        grid=(E,), in_specs=[spec], out_specs=[spec, spec],
        out_shape=[jax.ShapeDtypeStruct((E, nc, CHUNK), jnp.int32)] * 2,
        compiler_params=_params("parallel"),
        name="select_topk",
    )(aff_t.reshape(E, nc, CHUNK))
    return slot.reshape(E, T), pos.reshape(E, T)


def _combine_kernel(starts_ref, x_ref, slot_ref, first_ref, ye_ref, o_ref, win_ref, extra_ref, sem, extra_sem,
                    *, cap):
    b = pl.program_id(0)
    nb = pl.num_programs(0)
    E = ye_ref.shape[0]
    W = COMBINE_W

    def first_row(e, blk, k):
        return (starts_ref[e, blk] // 16) * 16 + k * W

    def window_start(e, blk, k):
        return pl.multiple_of(jnp.minimum(first_row(e, blk, k), cap - W), 16)

    def copy(e, blk):
        buf = blk % 2
        return pltpu.make_async_copy(ye_ref.at[e, pl.ds(window_start(e, blk, 0), W)],
                                     win_ref.at[buf, pl.ds(e * W, W)], sem.at[buf, e])

    @pl.when(b == 0)
    def _():
        for e in range(E):
            copy(e, b).start()

    @pl.when(b + 1 < nb)
    def _():
        for e in range(E):
            copy(e, b + 1).start()

    slots = slot_ref[...]
    first = first_ref[0]
    rel = slots - jnp.minimum(first, cap - W)
    rel = jnp.where((slots >= first) & (rel >= 0) & (rel < W), rel, -1).astype(F32).astype(BF16)
    lane = lax.broadcasted_iota(jnp.int32, (E, E * W), 1)
    expert = lax.broadcasted_iota(jnp.int32, (E, E * W), 0)
    expand = (lane // W == expert).astype(BF16)
    wanted = (lax.broadcasted_iota(jnp.int32, (1, E * W), 1) % W).astype(F32)
    onehot = (jnp.dot(rel, expand, preferred_element_type=F32) == wanted).astype(BF16)

    for e in range(E):
        copy(e, b).wait()
    o_ref[...] = x_ref[...] + jnp.dot(onehot, win_ref[b % 2], preferred_element_type=F32)

    col = lax.broadcasted_iota(jnp.int32, (1, W), 1)
    expert_lane = lax.broadcasted_iota(jnp.int32, (1, E), 1)

    def per_expert(e, carry):
        n_win = (starts_ref[e, b + 1] - first_row(e, b, 0) + W - 1) // W

        def extra(k, inner):
            cp = pltpu.make_async_copy(ye_ref.at[e, pl.ds(window_start(e, b, k), W)], extra_ref, extra_sem.at[0])
            cp.start()
            cp.wait()
            slot = jnp.max(jnp.where(expert_lane == e, slots, -1), axis=1, keepdims=True)
            oh = ((slot == window_start(e, b, k) + col) & (slot >= first_row(e, b, k))).astype(BF16)
            o_ref[...] += jnp.dot(oh, extra_ref[...], preferred_element_type=F32)
            return inner

        return lax.fori_loop(1, n_win, extra, carry)

    lax.fori_loop(0, E, per_expert, 0)


def _combine(x2, slot_t, starts, ye):
    T, D = x2.shape
    E, cap, _ = ye.shape
    tb = min(COMBINE_TB, T)
    nb = T // tb
    assert cap >= COMBINE_W and cap % 16 == 0
    first = ((starts[:, :nb] // 16) * 16).T.reshape(nb, 1, E)
    return pl.pallas_call(
        functools.partial(_combine_kernel, cap=cap),
        grid_spec=pltpu.PrefetchScalarGridSpec(
            num_scalar_prefetch=1,
            grid=(nb,),
            in_specs=[pl.BlockSpec((tb, D), lambda i, st: (i, 0)),
                      pl.BlockSpec((tb, E), lambda i, st: (i, 0)),
                      pl.BlockSpec((1, 1, E), lambda i, st: (i, 0, 0)),
                      pl.BlockSpec(memory_space=pl.ANY)],
            out_specs=pl.BlockSpec((tb, D), lambda i, st: (i, 0)),
            scratch_shapes=[pltpu.VMEM((2, E * COMBINE_W, D), BF16),
                            pltpu.VMEM((COMBINE_W, D), BF16),
                            pltpu.SemaphoreType.DMA((2, E)),
                            pltpu.SemaphoreType.DMA((1,))]),
        out_shape=jax.ShapeDtypeStruct((T, D), F32),
        compiler_params=_params("arbitrary"),
        name="moe_combine",
    )(starts, x2, slot_t, first, ye)


def _t5_bucket(rel):
    half = T5_BUCKETS // 2
    max_exact = half // 2
    n = np.abs(rel)
    large = max_exact + (np.log(np.maximum(n, 1) / max_exact) / np.log(T5_MAX_DISTANCE / max_exact)
                         * (half - max_exact)).astype(np.int32)
    large = np.minimum(large, half - 1)
    return (np.where(rel > 0, half, 0) + np.where(n < max_exact, n, large)).astype(np.int32)


def _dilated_tiles(L, radius):
    tq = min(LANES, L)
    kw = min(tq + 2 * radius, L)
    return tq, kw


def _dilated_bias(t5_table, g, L, window, dilation):
    radius = window // (2 * dilation)
    tq, kw = _dilated_tiles(L, radius)
    deltas = (0, -((kw - tq) // 2), tq - kw)
    row = np.arange(tq)[:, None]
    col = np.arange(kw)[None, :]
    rel = np.stack([col - row + d for d in deltas])
    valid = np.abs(rel) <= radius
    tab = t5_table[:, g * A_HEADS_PER_GROUP:(g + 1) * A_HEADS_PER_GROUP].astype(F32)
    bias = jnp.transpose(tab[_t5_bucket(rel * dilation)], (0, 3, 1, 2))
    bias = jnp.where(valid[:, None], bias, NEG_INF)
    return bias.reshape(3, A_HEADS_PER_GROUP // 2, 2 * tq, kw)


def _dilated_kernel(q_ref, k_ref, v_ref, bias_ref, o_ref, lse_ref, *, L, tq, kw):
    nq = L // tq
    lane = lax.broadcasted_iota(jnp.int32, (1, LANES), 1)
    lo = lane < HEAD_DIM

    def body(i, carry):
        q0 = pl.multiple_of(i * tq, tq)
        start = jnp.clip(q0 - (kw - tq) // 2, 0, L - kw)
        start = pl.multiple_of(start, 16)
        var = jnp.where(i == 0, 0, jnp.where(i == nq - 1, 2, 1))
        n_pairs = A_HEADS_PER_GROUP // 2
        cols = [slice(pair * LANES, (pair + 1) * LANES) for pair in range(n_pairs)]
        scores = []
        for pair in range(n_pairs):
            q2 = q_ref[0, pl.ds(q0, tq), cols[pair]] * (HEAD_DIM ** -0.5)
            zero = jnp.zeros_like(q2)
            qs = jnp.concatenate([jnp.where(lo, q2, zero), jnp.where(lo, zero, q2)], axis=0)
            k2 = k_ref[0, pl.ds(start, kw), cols[pair]]
            scores.append(lax.dot_general(qs, k2, _NT, preferred_element_type=F32) + bias_ref[var, pair])
        probs, dens, maxs = [], [], []
        for s in scores:
            m = jnp.max(s, axis=-1, keepdims=True)
            p = jnp.exp(s - m)
            maxs.append(m)
            dens.append(jnp.sum(p, axis=-1, keepdims=True))
            probs.append(p.astype(BF16))
        for pair in range(n_pairs):
            v2 = v_ref[0, pl.ds(start, kw), cols[pair]]
            o = jnp.dot(probs[pair], v2, preferred_element_type=F32) * (1.0 / dens[pair])
            lse = maxs[pair] + jnp.log(dens[pair])
            o_ref[0, pl.ds(q0, tq), cols[pair]] = jnp.where(lo, o[:tq], o[tq:])
            lse_ref[0, pl.ds(q0, tq), cols[pair]] = jnp.where(lo, lse[:tq], lse[tq:])
        return carry

    lax.fori_loop(0, nq, body, 0)


def _dilated_attention(u3, bias, g, window, dilation):
    Bn, S, C = u3.shape
    L = S // dilation
    radius = window // (2 * dilation)
    tq, kw = _dilated_tiles(L, radius)
    assert L % tq == 0 and kw % 16 == 0 and tq % 16 == 0
    uv = u3.reshape(Bn, L, dilation * C)
    cb = C // A_OUT
    n_col = A_HEADS * HEAD_DIM // A_OUT

    def spec(which):
        return pl.BlockSpec((1, L, A_OUT), lambda b, r: (b, 0, r * cb + which * n_col + g))

    out_spec = pl.BlockSpec((1, L, A_OUT), lambda b, r: (b, 0, r))
    o, lse = pl.pallas_call(
        functools.partial(_dilated_kernel, L=L, tq=tq, kw=kw),
        grid=(Bn, dilation),
        in_specs=[spec(0), spec(1), spec(2), pl.BlockSpec(bias.shape, lambda b, r: (0, 0, 0, 0))],
        out_specs=[out_spec, out_spec],
        out_shape=[jax.ShapeDtypeStruct((Bn, L, dilation * A_OUT), F32)] * 2,
        compiler_params=_params("parallel", "parallel"),
        name="dilated_attention",
    )(uv, uv, uv, bias)
    return o.reshape(Bn * S, A_OUT), lse.reshape(Bn * S, A_OUT)


def _merge_kernel(o0, o1, o2, l0, l1, l2, out_ref):
    ls = [l0[...], l1[...], l2[...]]
    m = jnp.maximum(jnp.maximum(ls[0], ls[1]), ls[2])
    es = [jnp.exp(l - m) for l in ls]
    den = es[0] + es[1] + es[2]
    out = (es[0] / den) * o0[...] + (es[1] / den) * o1[...] + (es[2] / den) * o2[...]
    out_ref[...] = out.astype(out_ref.dtype)


def _merge_groups(os_, ls_, tm=1024):
    T = os_[0].shape[0]
    spec = pl.BlockSpec((tm, A_OUT), lambda i: (i, 0))
    return pl.pallas_call(
        _merge_kernel, grid=(T // tm,), in_specs=[spec] * 6, out_specs=spec,
        out_shape=jax.ShapeDtypeStruct((T, A_OUT), BF16),
        compiler_params=_params("parallel"), name="merge_groups",
    )(*os_, *ls_)


def _rope_tables(S):
    inv = 1.0 / (ROPE_THETA ** (np.arange(0, B_ROPE, 2, dtype=np.float32) / B_ROPE))
    ang = np.arange(S, dtype=np.float32)[:, None] * inv[None]
    cos, sin = np.cos(ang), np.sin(ang)
    half = B_ROPE // 2
    c = np.ones((S, LANES), np.float32)
    c[:, B_NOPE:B_NOPE + half] = cos
    c[:, B_NOPE + half:B_NOPE + B_ROPE] = cos
    s_lo = np.zeros((S, LANES), np.float32)
    s_lo[:, B_NOPE:B_NOPE + half] = -sin
    s_hi = np.zeros((S, LANES), np.float32)
    s_hi[:, B_NOPE + half:B_NOPE + B_ROPE] = sin
    return jnp.asarray(c), jnp.asarray(s_lo), jnp.asarray(s_hi)


def _rope(x, c, s_lo, s_hi):
    half = B_ROPE // 2
    return x * c + pltpu.roll(x, LANES - half, 1) * s_lo + pltpu.roll(x, half, 1) * s_hi


def _mla_prep_kernel(cq_ref, ckv_ref, kr_ref, gq_ref, gkv_ref, wq_ref, wk_ref, wv_ref,
                     c_ref, slo_ref, shi_ref, q_ref, k_ref, v_ref):
    c, s_lo, s_hi = c_ref[...], slo_ref[...], shi_ref[...]
    scale = (B_NOPE + B_ROPE) ** -0.5
    xq = _rms(cq_ref[...], gq_ref[...]).astype(BF16)
    q = jnp.dot(xq, wq_ref[...], preferred_element_type=F32)
    xkv = _rms(ckv_ref[...], gkv_ref[...]).astype(BF16)
    k = jnp.dot(xkv, wk_ref[...], preferred_element_type=F32)
    v_ref[...] = jnp.dot(xkv, wv_ref[...], preferred_element_type=F32).astype(v_ref.dtype)
    k_rope = _rope(kr_ref[...].astype(F32), c, s_lo, s_hi)
    for h in range(B_HEADS):
        cs = slice(h * LANES, (h + 1) * LANES)
        q_ref[:, cs] = (_rope(q[:, cs], c, s_lo, s_hi) * scale).astype(q_ref.dtype)
        k_ref[:, cs] = (k[:, cs] + k_rope).astype(k_ref.dtype)


def _mla_prep(u3, g_qa, g_kva, wq, wk, wv, ts=512):
    Bn, S, C = u3.shape
    ts = min(ts, S)
    tabs = _rope_tables(S)
    ns = S // ts
    u2 = u3.reshape(Bn * S, C)
    row = lambda b, i: b * ns + i
    off_kv, off_kr, off_q = A_IN // B_KV_RANK, (A_IN + B_KV_RANK) // LANES, (A_IN + B_KV_RANK + LANES) // B_Q_RANK
    full = lambda a: pl.BlockSpec(a.shape, lambda b, i: (0, 0))
    tab_spec = pl.BlockSpec((ts, LANES), lambda b, i: (i, 0))
    hw = B_HEADS * LANES
    return pl.pallas_call(
        _mla_prep_kernel,
        grid=(Bn, ns),
        in_specs=[pl.BlockSpec((ts, B_Q_RANK), lambda b, i: (row(b, i), off_q)),
                  pl.BlockSpec((ts, B_KV_RANK), lambda b, i: (row(b, i), off_kv)),
                  pl.BlockSpec((ts, LANES), lambda b, i: (row(b, i), off_kr)),
                  full(g_qa), full(g_kva), full(wq), full(wk), full(wv),
                  tab_spec, tab_spec, tab_spec],
        out_specs=[pl.BlockSpec((ts, hw), lambda b, i: (row(b, i), 0)),
                   pl.BlockSpec((ts, hw), lambda b, i: (row(b, i), 0)),
                   pl.BlockSpec((ts, B_OUT), lambda b, i: (row(b, i), 0))],
        out_shape=[jax.ShapeDtypeStruct((Bn * S, hw), BF16),
                   jax.ShapeDtypeStruct((Bn * S, hw), BF16),
                   jax.ShapeDtypeStruct((Bn * S, B_OUT), BF16)],
        compiler_params=_params("parallel", "parallel"),
        name="mla_prep",
    )(u2, u2, u2, g_qa, g_kva, wq, wk, wv, *tabs)


def _mla_kernel(q_ref, k_ref, v_ref, o_ref, *, tk):
    S = k_ref.shape[1]
    tq = q_ref.shape[1]
    nk = S // tk
    lane = lax.broadcasted_iota(jnp.int32, (1, LANES), 1)
    qs = [q_ref[0, :, sub * LANES:(sub + 1) * LANES] for sub in range(2)]

    def body(j, carry):
        k0 = pl.multiple_of(j * tk, tk)
        vt = v_ref[0, pl.ds(k0, tk), :]
        new = []
        for sub in range(2):
            m, l, acc = carry[sub]
            kt = k_ref[0, pl.ds(k0, tk), sub * LANES:(sub + 1) * LANES]
            s = lax.dot_general(qs[sub], kt, _NT, preferred_element_type=F32)
            m_new = jnp.maximum(m, jnp.max(s, axis=-1, keepdims=True))
            alpha = jnp.exp(m - m_new)
            p = jnp.exp(s - m_new)
            l = alpha * l + jnp.sum(p, axis=-1, keepdims=True)
            acc = alpha * acc + jnp.dot(p.astype(BF16), vt, preferred_element_type=F32)
            new.append((m_new, l, acc))
        return tuple(new)

    init = tuple((jnp.full((tq, 1), NEG_INF, F32), jnp.zeros((tq, 1), F32), jnp.zeros((tq, LANES), F32))
                 for _ in range(2))
    res = lax.fori_loop(0, nk, body, init)
    outs = [acc / l for (_, l, acc) in res]
    o_ref[0] = jnp.where(lane < B_V, outs[0], outs[1]).astype(o_ref.dtype)


def _mla_attention(q, k, v, Bn, S, tq=512, tk=512):
    tq, tk = min(tq, S), min(tk, S)
    hw = B_HEADS * LANES
    q3, k3, v3 = q.reshape(Bn, S, hw), k.reshape(Bn, S, hw), v.reshape(Bn, S, B_OUT)
    o = pl.pallas_call(
        functools.partial(_mla_kernel, tk=tk),
        grid=(Bn, B_HEADS // 2, S // tq),
        in_specs=[pl.BlockSpec((1, tq, 2 * LANES), lambda b, h, i: (b, i, h)),
                  pl.BlockSpec((1, S, 2 * LANES), lambda b, h, i: (b, 0, h)),
                  pl.BlockSpec((1, S, LANES), lambda b, h, i: (b, 0, h))],
        out_specs=pl.BlockSpec((1, tq, LANES), lambda b, h, i: (b, i, h)),
        out_shape=jax.ShapeDtypeStruct((Bn, S, B_OUT), BF16),
        compiler_params=_params("parallel", "parallel", "parallel"),
        name="mla_attention",
    )(q3, k3, v3)
    return o.reshape(Bn * S, B_OUT)


def _natten_bias(rpb):
    W = GRID_W
    c = np.arange(W)[:, None]
    kc = np.arange(W)[None, :]
    wstart = np.clip(c - C_WIN_COLS // 2, 0, W - C_WIN_COLS)
    col_valid = (kc >= wstart) & (kc < wstart + C_WIN_COLS)
    dcol = np.clip(kc - c + C_WIN_COLS - 1, 0, 2 * C_WIN_COLS - 2)
    off = np.arange(C_WIN_ROWS)[:, None]
    j = np.arange(C_WIN_ROWS)[None, :]
    drow = j - off + C_WIN_ROWS - 1
    b = rpb.astype(F32)[:, drow][:, :, :, dcol]
    b = jnp.where(col_valid[None, None, None], b, NEG_INF)
    b = jnp.transpose(b, (1, 0, 3, 2, 4))
    return b.reshape(C_WIN_ROWS, C_HEADS // 2, 2 * W, C_WIN_ROWS * W)


def _natten_kernel(q_ref, k_ref, v_ref, bias_ref, o_ref, *, rows):
    r = pl.program_id(1)
    r0 = jnp.clip(r - C_WIN_ROWS // 2, 0, rows - C_WIN_ROWS)
    k0 = pl.multiple_of(r0 * GRID_W, GRID_W)
    nk = C_WIN_ROWS * GRID_W
    lane = lax.broadcasted_iota(jnp.int32, (1, LANES), 1)
    lo = lane < HEAD_DIM
    n_pairs = C_HEADS // 2
    cols = [slice(pair * LANES, (pair + 1) * LANES) for pair in range(n_pairs)]
    scores = []
    for pair in range(n_pairs):
        q2 = q_ref[0, :, cols[pair]] * (HEAD_DIM ** -0.5)
        zero = jnp.zeros_like(q2)
        qs = jnp.concatenate([jnp.where(lo, q2, zero), jnp.where(lo, zero, q2)], axis=0)
        k2 = k_ref[0, pl.ds(k0, nk), cols[pair]]
        scores.append(lax.dot_general(qs, k2, _NT, preferred_element_type=F32) + bias_ref[0, pair])
    probs, dens = [], []
    for s in scores:
        p = jnp.exp(s - jnp.max(s, axis=-1, keepdims=True))
        dens.append(jnp.sum(p, axis=-1, keepdims=True))
        probs.append(p.astype(BF16))
    for pair in range(n_pairs):
        v2 = v_ref[0, pl.ds(k0, nk), cols[pair]]
        o = jnp.dot(probs[pair], v2, preferred_element_type=F32) * (1.0 / dens[pair])
        o_ref[0, :, cols[pair]] = jnp.where(lo, o[:GRID_W], o[GRID_W:]).astype(o_ref.dtype)


def _natten(u3, bias):
    Bn, S, _ = u3.shape
    rows = S // GRID_W
    assert rows >= C_WIN_ROWS

    def variant(b, r):
        return (r - jnp.clip(r - C_WIN_ROWS // 2, 0, rows - C_WIN_ROWS), 0, 0, 0)

    o = pl.pallas_call(
        functools.partial(_natten_kernel, rows=rows),
        grid=(Bn, rows),
        in_specs=[pl.BlockSpec((1, GRID_W, C_OUT), lambda b, r: (b, r, 0)),
                  pl.BlockSpec((1, S, C_OUT), lambda b, r: (b, 0, 1), pipeline_mode=pl.Buffered(1)),
                  pl.BlockSpec((1, S, C_OUT), lambda b, r: (b, 0, 2), pipeline_mode=pl.Buffered(1)),
                  pl.BlockSpec((1, C_HEADS // 2, 2 * GRID_W, C_WIN_ROWS * GRID_W), variant)],
        out_specs=pl.BlockSpec((1, GRID_W, C_OUT), lambda b, r: (b, r, 0)),
        out_shape=jax.ShapeDtypeStruct((Bn, S, C_OUT), BF16),
        compiler_params=_params("parallel", "arbitrary"),
        name="natten",
    )(u3, u3, u3, bias)
    return o.reshape(Bn * S, C_OUT)


def _moe(x2, ln_g, router, wg, wu, wd):
    T, D = x2.shape
    cap = CAPACITY_FACTOR * T // N_EXPERTS
    aff_t, h = _router(x2, ln_g, router)
    slot, pos = _select(aff_t, cap)
    tb = min(COMBINE_TB, T)
    starts = jnp.concatenate([pos[:, ::tb], jnp.full((N_EXPERTS, 1), cap, jnp.int32)], axis=1)
    incl = pos + (slot >= 0)
    wanted = jnp.arange(1, cap + 1, dtype=jnp.int32)
    idx = jax.vmap(lambda row: jnp.searchsorted(row, wanted, side="left"))(incl)
    gate = jnp.take_along_axis(aff_t, idx, axis=1)
    xe = h[idx]
    ye = _expert_ffn(xe, gate, wg, wu, wd)
    return _combine(x2, slot.T, starts, ye)


def _trunk(x, p):
    Bn, S, D = x.shape
    T = Bn * S
    x2 = x.reshape(T, D)
    u = _matmul(x2, p["w_in_even"], gain=p["ln_mix_g"][0], out_dtype=BF16)
    u3 = u.reshape(Bn, S, EVEN_COLS)
    os_, ls_ = [], []
    for g, (window, dilation) in enumerate(A_PATTERNS):
        bias = _dilated_bias(p["t5_table"], g, S // dilation, window, dilation)
        o, l = _dilated_attention(u3, bias, g, window, dilation)
        os_.append(o)
        ls_.append(l)
    a = _merge_groups(os_, ls_)
    q, k, v = _mla_prep(u3, p["g_qa"], p["g_kva"], p["wq"], p["wk"], p["wv"])
    b = _mla_attention(q, k, v, Bn, S)
    x2 = _matmul([a, b], [p["w_out_even"][:A_OUT], p["w_out_even"][A_OUT:]], res=x2, out_dtype=F32)
    x2 = _moe(x2, p["ln_ffn_g"][0], p["router"][0], p["w_gate"][0], p["w_up"][0], p["w_down"][0])
    u = _matmul(x2, p["w_in_odd"], gain=p["ln_mix_g"][1], out_dtype=BF16)
    c = _natten(u.reshape(Bn, S, C_IN), p["natten_bias"])
    x2 = _matmul(c, p["w_out_odd"], res=x2, out_dtype=F32)
    x2 = _moe(x2, p["ln_ffn_g"][1], p["router"][1], p["w_gate"][1], p["w_up"][1], p["w_down"][1])
    return _rms(x2, p["final_g"].astype(F32)).reshape(Bn, S, D)


def _head_tiles(w, n_heads, per_head, keep):
    K = w.shape[0]
    w = w.reshape(K, n_heads, per_head)[:, :, :keep]
    return jnp.pad(w, ((0, 0), (0, 0), (0, LANES - keep))).reshape(K, n_heads * LANES)


def _prepare(ln_mix_g, w_in_even, g_qa, w_qb, g_kva, w_kvb, t5_table, w_out_even,
             w_in_odd, rpb, w_out_odd, ln_ffn_g, router, w_gate, w_up, w_down, final_g):
    w_in = w_in_even[0]
    w_a, w_cq = w_in[:, :A_IN], w_in[:, A_IN:A_IN + B_Q_RANK]
    w_ckv = w_in[:, A_IN + B_Q_RANK:A_IN + B_Q_RANK + B_KV_RANK]
    w_kr = w_in[:, A_IN + B_Q_RANK + B_KV_RANK:]
    w_kr_tile = jnp.pad(w_kr, ((0, 0), (B_NOPE, LANES - B_NOPE - B_ROPE)))
    w_even = jnp.concatenate([w_a, w_ckv, w_kr_tile, w_cq], axis=1)
    w_kv = w_kvb[0].reshape(B_KV_RANK, B_HEADS, B_NOPE + B_V)
    return dict(
        ln_mix_g=ln_mix_g, ln_ffn_g=ln_ffn_g, final_g=final_g, t5_table=t5_table,
        w_in_even=w_even.astype(BF16),
        g_qa=g_qa[0].reshape(1, -1).astype(F32), g_kva=g_kva[0].reshape(1, -1).astype(F32),
        wq=_head_tiles(w_qb[0], B_HEADS, B_NOPE + B_ROPE, B_NOPE + B_ROPE).astype(BF16),
        wk=_head_tiles(w_kvb[0], B_HEADS, B_NOPE + B_V, B_NOPE).astype(BF16),
        wv=w_kv[:, :, B_NOPE:].reshape(B_KV_RANK, B_OUT).astype(BF16),
        w_out_even=w_out_even[0].astype(BF16), w_in_odd=w_in_odd[0].astype(BF16),
        natten_bias=_natten_bias(rpb[0]), w_out_odd=w_out_odd[0].astype(BF16), router=router,
        w_gate=w_gate.astype(BF16), w_up=w_up.astype(BF16), w_down=w_down.astype(BF16),
    )


def kernel(x_prompt, x_sample, ln_mix_g, w_in_even, g_qa, w_qb, g_kva, w_kvb, t5_table, w_out_even,
           w_in_odd, rpb, w_out_odd, ln_ffn_g, router, w_gate, w_up, w_down, final_g):
    p = _prepare(ln_mix_g, w_in_even, g_qa, w_qb, g_kva, w_kvb, t5_table, w_out_even,
                 w_in_odd, rpb, w_out_odd, ln_ffn_g, router, w_gate, w_up, w_down, final_g)
    return _trunk(x_prompt, p), _trunk(x_sample, p)
```

```python
import functools

import jax
import jax.numpy as jnp
import numpy as np
from jax import lax
from jax.experimental import pallas as pl
from jax.experimental.pallas import tpu as pltpu

D_MODEL = 1024
RMS_EPS = 1e-6
HEAD_DIM = 64
NEG_INF = -1e30
A_PATTERNS = ((128, 1), (512, 4), (2048, 16))
A_GROUPS = 3
A_HEADS_PER_GROUP = 4
A_HEADS = A_GROUPS * A_HEADS_PER_GROUP
A_IN = 3 * A_HEADS * HEAD_DIM
A_OUT = A_HEADS_PER_GROUP * HEAD_DIM
T5_BUCKETS = 32
T5_MAX_DISTANCE = 1024
B_HEADS = 12
B_Q_RANK = 384
B_KV_RANK = 256
B_NOPE = 64
B_ROPE = 32
B_V = 64
B_OUT = B_HEADS * B_V
ROPE_THETA = 10000.0
GRID_W = 64
C_HEADS = 16
C_WIN_ROWS = 8
C_WIN_COLS = 16
C_IN = 3 * C_HEADS * HEAD_DIM
C_OUT = C_HEADS * HEAD_DIM
N_EXPERTS = 16
CAPACITY_FACTOR = 2

LANES = 128
EVEN_COLS = A_IN + B_KV_RANK + LANES + B_Q_RANK
VMEM_LIMIT_BYTES = 48 * 1024 * 1024

F32 = jnp.float32
BF16 = jnp.bfloat16
_NT = (((1,), (1,)), ((), ()))


def _params(*sem):
    return pltpu.CompilerParams(dimension_semantics=sem, vmem_limit_bytes=VMEM_LIMIT_BYTES)


def _rms(x, g):
    xf = x.astype(F32)
    return xf * lax.rsqrt(jnp.mean(xf * xf, axis=-1, keepdims=True) + RMS_EPS) * g


def _mm_kernel(*refs, n_in, has_gain, has_res, precise):
    xs, ws = refs[:n_in], refs[n_in:2 * n_in]
    pos = 2 * n_in
    g_ref = res_ref = None
    if has_gain:
        g_ref = refs[pos]
        pos += 1
    if has_res:
        res_ref = refs[pos]
        pos += 1
    o_ref = refs[pos]
    acc = None
    for x_ref, w_ref in zip(xs, ws):
        x = x_ref[...]
        if has_gain:
            x = _rms(x, g_ref[...])
        if precise:
            part = jnp.dot(x.astype(F32), w_ref[...], preferred_element_type=F32,
                           precision=lax.Precision.HIGHEST)
        else:
            part = jnp.dot(x.astype(BF16), w_ref[...], preferred_element_type=F32)
        acc = part if acc is None else acc + part
    if has_res:
        acc = acc + res_ref[...]
    o_ref[...] = acc.astype(o_ref.dtype)


def _matmul(xs, ws, gain=None, res=None, out_dtype=F32, tm=512, precise=False):
    if not isinstance(xs, (list, tuple)):
        xs, ws = [xs], [ws]
    M = xs[0].shape[0]
    N = ws[0].shape[1]
    assert M % tm == 0
    in_specs = [pl.BlockSpec((tm, x.shape[1]), lambda i: (i, 0)) for x in xs]
    in_specs += [pl.BlockSpec(w.shape, lambda i: (0, 0)) for w in ws]
    args = list(xs) + list(ws)
    if gain is not None:
        assert len(xs) == 1
        in_specs.append(pl.BlockSpec((1, gain.shape[-1]), lambda i: (0, 0)))
        args.append(gain.reshape(1, -1).astype(F32))
    if res is not None:
        in_specs.append(pl.BlockSpec((tm, N), lambda i: (i, 0)))
        args.append(res)
    return pl.pallas_call(
        functools.partial(_mm_kernel, n_in=len(xs), has_gain=gain is not None,
                          has_res=res is not None, precise=precise),
        grid=(M // tm,),
        in_specs=in_specs,
        out_specs=pl.BlockSpec((tm, N), lambda i: (i, 0)),
        out_shape=jax.ShapeDtypeStruct((M, N), out_dtype),
        compiler_params=_params("parallel"),
        name="fused_matmul",
    )(*args)


def _ffn_kernel(x_ref, wg_ref, wu_ref, wd_ref, o_ref):
    D = wg_ref.shape[1]
    x = x_ref[0, :, :D]
    lane = lax.broadcasted_iota(jnp.int32, (1, LANES), 1)
    mine = lane // GATE_TERMS == pl.program_id(0)
    gate = jnp.sum(jnp.where(mine, x_ref[0, :, D:].astype(F32), 0.0), axis=1, keepdims=True)
    g = jnp.dot(x, wg_ref[0], preferred_element_type=F32)
    u = jnp.dot(x, wu_ref[0], preferred_element_type=F32)
    h = (g * jax.nn.sigmoid(g)) * u
    y = jnp.dot(h.astype(BF16), wd_ref[0], preferred_element_type=F32)
    o_ref[0] = (y * gate).astype(o_ref.dtype)


def _expert_ffn(xe, cap, wg, wu, wd, tm=512):
    E = xe.shape[0]
    D, FF = wg.shape[1], wg.shape[2]
    assert cap % tm == 0
    return pl.pallas_call(
        _ffn_kernel,
        grid=(E, cap // tm),
        in_specs=[pl.BlockSpec((1, tm, D + LANES), lambda e, i: (e, i, 0)),
                  pl.BlockSpec((1, D, FF), lambda e, i: (e, 0, 0)),
                  pl.BlockSpec((1, D, FF), lambda e, i: (e, 0, 0)),
                  pl.BlockSpec((1, FF, D), lambda e, i: (e, 0, 0))],
        out_specs=pl.BlockSpec((1, tm, D), lambda e, i: (e, i, 0)),
        out_shape=jax.ShapeDtypeStruct((E, cap, D), BF16),
        compiler_params=_params("parallel", "parallel"),
        name="expert_ffn",
    )(xe, wg, wu, wd)


CHUNK = 256
COMBINE_TB = 512
COMBINE_W = 128


GATE_TERMS = 3


def _router_kernel(x_ref, g_ref, rt_ref, aff_ref, h_ref):
    D = x_ref.shape[1]
    E = rt_ref.shape[0]
    xn = _rms(x_ref[...], g_ref[...])
    logits = lax.dot_general(rt_ref[...], xn, _NT, preferred_element_type=F32,
                             precision=lax.Precision.HIGHEST)
    e = jnp.exp(logits - jnp.max(logits, axis=0, keepdims=True))
    aff = e / jnp.sum(e, axis=0, keepdims=True)
    aff_ref[...] = aff
    lane = lax.broadcasted_iota(jnp.int32, (E, LANES), 1)
    expert = lax.broadcasted_iota(jnp.int32, (E, LANES), 0)
    rest = aff
    gates = jnp.zeros((x_ref.shape[0], LANES), F32)
    for j in range(GATE_TERMS):
        term = rest.astype(BF16)
        rest = rest - term.astype(F32)
        place = (lane == GATE_TERMS * expert + j).astype(BF16)
        gates = gates + lax.dot_general(term, place, (((0,), (0,)), ((), ())), preferred_element_type=F32)
    h_ref[:, :D] = xn.astype(h_ref.dtype)
    h_ref[:, D:] = gates.astype(h_ref.dtype)


def _router(x2, ln_g, router, tm=512):
    T, D = x2.shape
    E = router.shape[1]
    assert GATE_TERMS * E <= LANES
    return pl.pallas_call(
        _router_kernel,
        grid=(T // tm,),
        in_specs=[pl.BlockSpec((tm, D), lambda i: (i, 0)),
                  pl.BlockSpec((1, D), lambda i: (0, 0)),
                  pl.BlockSpec((E, D), lambda i: (0, 0))],
        out_specs=[pl.BlockSpec((E, tm), lambda i: (0, i)),
                   pl.BlockSpec((tm, D + LANES), lambda i: (i, 0))],
        out_shape=[jax.ShapeDtypeStruct((E, T), F32), jax.ShapeDtypeStruct((T, D + LANES), BF16)],
        compiler_params=_params("parallel"),
        name="router",
    )(x2, ln_g.reshape(1, D).astype(F32), router.T.astype(F32))


def _cumsum_excl(mask, upper, lower_strict):
    incl = jnp.dot(mask.astype(BF16), upper, preferred_element_type=F32)
    before = jnp.dot(lower_strict, incl.astype(BF16), preferred_element_type=F32)
    return incl + before[:, CHUNK - 1:CHUNK] - mask


def _select_kernel(aff_ref, slot_ref, pos_ref, *, cap):
    aff = aff_ref[0]
    nc = aff.shape[0]
    bits = pltpu.bitcast(aff, jnp.int32)

    def count(cond):
        c = jnp.sum(jnp.where(cond, 1.0, 0.0), axis=0, keepdims=True)
        return jnp.sum(c, axis=1, keepdims=True)

    def bit_step(i, prefix):
        cand = prefix | (jnp.int32(1) << (30 - i))
        return jnp.where(count(bits >= cand) >= cap, cand, prefix)

    thr = lax.fori_loop(0, 31, bit_step, jnp.zeros((1, 1), jnp.int32))
    r = lax.broadcasted_iota(jnp.int32, (CHUNK, CHUNK), 0)
    c = lax.broadcasted_iota(jnp.int32, (CHUNK, CHUNK), 1)
    upper = (r <= c).astype(BF16)
    r = lax.broadcasted_iota(jnp.int32, (nc, nc), 0)
    c = lax.broadcasted_iota(jnp.int32, (nc, nc), 1)
    lower_strict = (c < r).astype(BF16)
    gt = bits > thr
    eq = jnp.where(bits == thr, 1.0, 0.0)
    need = cap - count(gt)
    sel = jnp.where(gt | ((eq > 0) & (_cumsum_excl(eq, upper, lower_strict) < need)), 1.0, 0.0)
    pos = _cumsum_excl(sel, upper, lower_strict).astype(jnp.int32)
    pos_ref[0] = pos
    slot_ref[0] = jnp.where(sel > 0, pos, -1)


def _select(aff_t, cap):
    E, T = aff_t.shape
    nc = T // CHUNK
    assert nc <= CHUNK and nc % 8 == 0
    spec = pl.BlockSpec((1, nc, CHUNK), lambda e: (e, 0, 0))
    slot, pos = pl.pallas_call(
        functools.partial(_select_kernel, cap=cap),
        grid=(E,), in_specs=[spec], out_specs=[spec, spec],
        out_shape=[jax.ShapeDtypeStruct((E, nc, CHUNK), jnp.int32)] * 2,
        compiler_params=_params("parallel"),
        name="select_topk",
    )(aff_t.reshape(E, nc, CHUNK))
    return slot.reshape(E, T), pos.reshape(E, T)


GATHER_STEP = 128
GATHER_W = GATHER_STEP + 16


def _gather_kernel(starts_ref, h_ref, slot_ref, xe_ref, stage_ref, extra_ref, tail_ref, sem, extra_sem, *, cap):
    b = pl.program_id(0)
    nb = pl.num_programs(0)
    E = slot_ref.shape[0]
    buf = b % 2
    h = h_ref[...]
    row = lax.broadcasted_iota(jnp.int32, (GATHER_W, 1), 0)

    @pl.when(b == 0)
    def _():
        tail_ref[...] = jnp.zeros_like(tail_ref)

    def first_row(e, blk):
        return pl.multiple_of((starts_ref[e, blk] // 16) * 16, 16)

    def copy(e, blk):
        return pltpu.make_async_copy(stage_ref.at[blk % 2, e], xe_ref.at[e, pl.ds(first_row(e, blk), GATHER_W)],
                                     sem.at[blk % 2, e])

    def per_expert(e, carry):
        base = first_row(e, b)
        slots = slot_ref[pl.ds(e, 1), :]
        end = starts_ref[e, b + 1] - base
        group = pl.multiple_of((end // 16) * 16, 16)
        last = jnp.maximum(group - 1, 0) // GATHER_STEP

        def window(k):
            onehot = (slots - (base + k * GATHER_STEP) == row).astype(BF16)
            return jnp.dot(onehot, h, preferred_element_type=F32).astype(BF16)

        stage_ref[buf, e] = window(0)
        stage_ref[buf, e, pl.ds(0, 16), :] += tail_ref[e]

        @pl.when(last == 0)
        def _():
            tail_ref[e] = stage_ref[buf, e, pl.ds(group, 16), :]

        @pl.when(b > 0)
        def _():
            copy(e, b - 1).wait()

        copy(e, b).start()

        def extra(k, inner):
            extra_ref[...] = window(k)

            @pl.when(k == last)
            def _():
                tail_ref[e] = extra_ref[pl.ds(pl.multiple_of(group - k * GATHER_STEP, 16), 16), :]

            cp = pltpu.make_async_copy(extra_ref, xe_ref.at[e, pl.ds(base + k * GATHER_STEP, GATHER_W)],
                                       extra_sem.at[0])
            cp.start()
            cp.wait()
            return inner

        return lax.fori_loop(1, last + 1, extra, carry)

    lax.fori_loop(0, E, per_expert, 0)

    @pl.when(b == nb - 1)
    def _():
        extra_ref[...] = jnp.zeros_like(extra_ref)

        def drain(e, carry):
            copy(e, b).wait()
            cp = pltpu.make_async_copy(extra_ref, xe_ref.at[e, pl.ds(cap, GATHER_W)], extra_sem.at[0])
            cp.start()
            cp.wait()
            return carry
        lax.fori_loop(0, E, drain, 0)


def _gather(h, slot, starts, cap):
    T, D = h.shape
    E = slot.shape[0]
    tb = min(COMBINE_TB, T)
    rows = cap + GATHER_W
    return pl.pallas_call(
        functools.partial(_gather_kernel, cap=cap),
        grid_spec=pltpu.PrefetchScalarGridSpec(
            num_scalar_prefetch=1,
            grid=(T // tb,),
            in_specs=[pl.BlockSpec((tb, D), lambda i, st: (i, 0)),
                      pl.BlockSpec((E, tb), lambda i, st: (0, i))],
            out_specs=pl.BlockSpec(memory_space=pl.ANY),
            scratch_shapes=[pltpu.VMEM((2, E, GATHER_W, D), BF16),
                            pltpu.VMEM((GATHER_W, D), BF16),
                            pltpu.VMEM((E, 16, D), BF16),
                            pltpu.SemaphoreType.DMA((2, E)),
                            pltpu.SemaphoreType.DMA((1,))]),
        out_shape=jax.ShapeDtypeStruct((E, rows, D), BF16),
        compiler_params=_params("arbitrary"),
        name="moe_gather",
    )(starts, h, slot)


def _combine_kernel(starts_ref, x_ref, slot_ref, first_ref, ye_ref, o_ref, win_ref, extra_ref, sem, extra_sem,
                    *, cap):
    b = pl.program_id(0)
    nb = pl.num_programs(0)
    E = ye_ref.shape[0]
    W = COMBINE_W

    def first_row(e, blk, k):
        return (starts_ref[e, blk] // 16) * 16 + k * W

    def window_start(e, blk, k):
        return pl.multiple_of(jnp.minimum(first_row(e, blk, k), cap - W), 16)

    def copy(e, blk):
        buf = blk % 2
        return pltpu.make_async_copy(ye_ref.at[e, pl.ds(window_start(e, blk, 0), W)],
                                     win_ref.at[buf, pl.ds(e * W, W)], sem.at[buf, e])

    @pl.when(b == 0)
    def _():
        for e in range(E):
            copy(e, b).start()

    @pl.when(b + 1 < nb)
    def _():
        for e in range(E):
            copy(e, b + 1).start()

    slots = slot_ref[...]
    first = first_ref[0]
    rel = slots - jnp.minimum(first, cap - W)
    rel = jnp.where((slots >= first) & (rel >= 0) & (rel < W), rel, -1).astype(F32).astype(BF16)
    lane = lax.broadcasted_iota(jnp.int32, (E, E * W), 1)
    expert = lax.broadcasted_iota(jnp.int32, (E, E * W), 0)
    expand = (lane // W == expert).astype(BF16)
    wanted = (lax.broadcasted_iota(jnp.int32, (1, E * W), 1) % W).astype(F32)
    onehot = (jnp.dot(rel, expand, preferred_element_type=F32) == wanted).astype(BF16)

    for e in range(E):
        copy(e, b).wait()
    o_ref[...] = x_ref[...] + jnp.dot(onehot, win_ref[b % 2], preferred_element_type=F32)

    col = lax.broadcasted_iota(jnp.int32, (1, W), 1)
    expert_lane = lax.broadcasted_iota(jnp.int32, (1, E), 1)

    def per_expert(e, carry):
        n_win = (starts_ref[e, b + 1] - first_row(e, b, 0) + W - 1) // W

        def extra(k, inner):
            cp = pltpu.make_async_copy(ye_ref.at[e, pl.ds(window_start(e, b, k), W)], extra_ref, extra_sem.at[0])
            cp.start()
            cp.wait()
            slot = jnp.max(jnp.where(expert_lane == e, slots, -1), axis=1, keepdims=True)
            oh = ((slot == window_start(e, b, k) + col) & (slot >= first_row(e, b, k))).astype(BF16)
            o_ref[...] += jnp.dot(oh, extra_ref[...], preferred_element_type=F32)
            return inner

        return lax.fori_loop(1, n_win, extra, carry)

    lax.fori_loop(0, E, per_expert, 0)


def _combine(x2, slot_t, starts, ye):
    T, D = x2.shape
    E, cap, _ = ye.shape
    tb = min(COMBINE_TB, T)
    nb = T // tb
    assert cap >= COMBINE_W and cap % 16 == 0
    first = ((starts[:, :nb] // 16) * 16).T.reshape(nb, 1, E)
    return pl.pallas_call(
        functools.partial(_combine_kernel, cap=cap),
        grid_spec=pltpu.PrefetchScalarGridSpec(
            num_scalar_prefetch=1,
            grid=(nb,),
            in_specs=[pl.BlockSpec((tb, D), lambda i, st: (i, 0)),
                      pl.BlockSpec((tb, E), lambda i, st: (i, 0)),
                      pl.BlockSpec((1, 1, E), lambda i, st: (i, 0, 0)),
                      pl.BlockSpec(memory_space=pl.ANY)],
            out_specs=pl.BlockSpec((tb, D), lambda i, st: (i, 0)),
            scratch_shapes=[pltpu.VMEM((2, E * COMBINE_W, D), BF16),
                            pltpu.VMEM((COMBINE_W, D), BF16),
                            pltpu.SemaphoreType.DMA((2, E)),
                            pltpu.SemaphoreType.DMA((1,))]),
        out_shape=jax.ShapeDtypeStruct((T, D), F32),
        compiler_params=_params("arbitrary"),
        name="moe_combine",
    )(starts, x2, slot_t, first, ye)


def _t5_bucket(rel):
    half = T5_BUCKETS // 2
    max_exact = half // 2
    n = np.abs(rel)
    large = max_exact + (np.log(np.maximum(n, 1) / max_exact) / np.log(T5_MAX_DISTANCE / max_exact)
                         * (half - max_exact)).astype(np.int32)
    large = np.minimum(large, half - 1)
    return (np.where(rel > 0, half, 0) + np.where(n < max_exact, n, large)).astype(np.int32)


def _dilated_tiles(L, radius):
    tq = min(LANES, L)
    kw = min(tq + 2 * radius, L)
    return tq, kw


def _dilated_bias(t5_table, g, L, window, dilation):
    radius = window // (2 * dilation)
    tq, kw = _dilated_tiles(L, radius)
    deltas = (0, -((kw - tq) // 2), tq - kw)
    row = np.arange(tq)[:, None]
    col = np.arange(kw)[None, :]
    rel = np.stack([col - row + d for d in deltas])
    valid = np.abs(rel) <= radius
    tab = t5_table[:, g * A_HEADS_PER_GROUP:(g + 1) * A_HEADS_PER_GROUP].astype(F32)
    bias = jnp.transpose(tab[_t5_bucket(rel * dilation)], (0, 3, 1, 2))
    bias = jnp.where(valid[:, None], bias, NEG_INF)
    return bias.reshape(3, A_HEADS_PER_GROUP // 2, 2 * tq, kw)


def _dilated_kernel(q_ref, k_ref, v_ref, bias_ref, o_ref, lse_ref, *, L, tq, kw):
    nq = L // tq
    lane = lax.broadcasted_iota(jnp.int32, (1, LANES), 1)
    lo = lane < HEAD_DIM

    def body(i, carry):
        q0 = pl.multiple_of(i * tq, tq)
        start = jnp.clip(q0 - (kw - tq) // 2, 0, L - kw)
        start = pl.multiple_of(start, 16)
        var = jnp.where(i == 0, 0, jnp.where(i == nq - 1, 2, 1))
        n_pairs = A_HEADS_PER_GROUP // 2
        cols = [slice(pair * LANES, (pair + 1) * LANES) for pair in range(n_pairs)]
        scores = []
        for pair in range(n_pairs):
            q2 = q_ref[0, pl.ds(q0, tq), cols[pair]] * (HEAD_DIM ** -0.5)
            zero = jnp.zeros_like(q2)
            qs = jnp.concatenate([jnp.where(lo, q2, zero), jnp.where(lo, zero, q2)], axis=0)
            k2 = k_ref[0, pl.ds(start, kw), cols[pair]]
            scores.append(lax.dot_general(qs, k2, _NT, preferred_element_type=F32) + bias_ref[var, pair])
        probs, dens, maxs = [], [], []
        for s in scores:
            m = jnp.max(s, axis=-1, keepdims=True)
            p = jnp.exp(s - m)
            maxs.append(m)
            dens.append(jnp.sum(p, axis=-1, keepdims=True))
            probs.append(p.astype(BF16))
        for pair in range(n_pairs):
            v2 = v_ref[0, pl.ds(start, kw), cols[pair]]
            o = jnp.dot(probs[pair], v2, preferred_element_type=F32) * (1.0 / dens[pair])
            lse = maxs[pair] + jnp.log(dens[pair])
            o_ref[0, pl.ds(q0, tq), cols[pair]] = jnp.where(lo, o[:tq], o[tq:])
            lse_ref[0, pl.ds(q0, tq), cols[pair]] = jnp.where(lo, lse[:tq], lse[tq:])
        return carry

    lax.fori_loop(0, nq, body, 0)


def _dilated_attention(u3, bias, g, window, dilation):
    Bn, S, C = u3.shape
    L = S // dilation
    radius = window // (2 * dilation)
    tq, kw = _dilated_tiles(L, radius)
    assert L % tq == 0 and kw % 16 == 0 and tq % 16 == 0
    uv = u3.reshape(Bn, L, dilation * C)
    cb = C // A_OUT
    n_col = A_HEADS * HEAD_DIM // A_OUT

    def spec(which):
        return pl.BlockSpec((1, L, A_OUT), lambda b, r: (b, 0, r * cb + which * n_col + g))

    out_spec = pl.BlockSpec((1, L, A_OUT), lambda b, r: (b, 0, r))
    o, lse = pl.pallas_call(
        functools.partial(_dilated_kernel, L=L, tq=tq, kw=kw),
        grid=(Bn, dilation),
        in_specs=[spec(0), spec(1), spec(2), pl.BlockSpec(bias.shape, lambda b, r: (0, 0, 0, 0))],
        out_specs=[out_spec, out_spec],
        out_shape=[jax.ShapeDtypeStruct((Bn, L, dilation * A_OUT), F32)] * 2,
        compiler_params=_params("parallel", "parallel"),
        name="dilated_attention",
    )(uv, uv, uv, bias)
    return o.reshape(Bn * S, A_OUT), lse.reshape(Bn * S, A_OUT)


def _merge_kernel(o0, o1, o2, l0, l1, l2, out_ref):
    ls = [l0[...], l1[...], l2[...]]
    m = jnp.maximum(jnp.maximum(ls[0], ls[1]), ls[2])
    es = [jnp.exp(l - m) for l in ls]
    den = es[0] + es[1] + es[2]
    out = (es[0] / den) * o0[...] + (es[1] / den) * o1[...] + (es[2] / den) * o2[...]
    out_ref[...] = out.astype(out_ref.dtype)


def _merge_groups(os_, ls_, tm=1024):
    T = os_[0].shape[0]
    spec = pl.BlockSpec((tm, A_OUT), lambda i: (i, 0))
    return pl.pallas_call(
        _merge_kernel, grid=(T // tm,), in_specs=[spec] * 6, out_specs=spec,
        out_shape=jax.ShapeDtypeStruct((T, A_OUT), BF16),
        compiler_params=_params("parallel"), name="merge_groups",
    )(*os_, *ls_)


def _rope_tables(S):
    inv = 1.0 / (ROPE_THETA ** (np.arange(0, B_ROPE, 2, dtype=np.float32) / B_ROPE))
    ang = np.arange(S, dtype=np.float32)[:, None] * inv[None]
    cos, sin = np.cos(ang), np.sin(ang)
    half = B_ROPE // 2
    c = np.ones((S, LANES), np.float32)
    c[:, B_NOPE:B_NOPE + half] = cos
    c[:, B_NOPE + half:B_NOPE + B_ROPE] = cos
    s_lo = np.zeros((S, LANES), np.float32)
    s_lo[:, B_NOPE:B_NOPE + half] = -sin
    s_hi = np.zeros((S, LANES), np.float32)
    s_hi[:, B_NOPE + half:B_NOPE + B_ROPE] = sin
    return jnp.asarray(c), jnp.asarray(s_lo), jnp.asarray(s_hi)


def _rope(x, c, s_lo, s_hi):
    half = B_ROPE // 2
    return x * c + pltpu.roll(x, LANES - half, 1) * s_lo + pltpu.roll(x, half, 1) * s_hi


def _mla_prep_kernel(cq_ref, ckv_ref, kr_ref, gq_ref, gkv_ref, wq_ref, wk_ref, wv_ref,
                     c_ref, slo_ref, shi_ref, q_ref, k_ref, v_ref):
    c, s_lo, s_hi = c_ref[...], slo_ref[...], shi_ref[...]
    scale = (B_NOPE + B_ROPE) ** -0.5
    xq = _rms(cq_ref[...], gq_ref[...]).astype(BF16)
    q = jnp.dot(xq, wq_ref[...], preferred_element_type=F32)
    xkv = _rms(ckv_ref[...], gkv_ref[...]).astype(BF16)
    k = jnp.dot(xkv, wk_ref[...], preferred_element_type=F32)
    v_ref[...] = jnp.dot(xkv, wv_ref[...], preferred_element_type=F32).astype(v_ref.dtype)
    k_rope = _rope(kr_ref[...].astype(F32), c, s_lo, s_hi)
    for h in range(B_HEADS):
        cs = slice(h * LANES, (h + 1) * LANES)
        q_ref[:, cs] = (_rope(q[:, cs], c, s_lo, s_hi) * scale).astype(q_ref.dtype)
        k_ref[:, cs] = (k[:, cs] + k_rope).astype(k_ref.dtype)


def _mla_prep(u3, g_qa, g_kva, wq, wk, wv, ts=512):
    Bn, S, C = u3.shape
    ts = min(ts, S)
    tabs = _rope_tables(S)
    ns = S // ts
    u2 = u3.reshape(Bn * S, C)
    row = lambda b, i: b * ns + i
    off_kv, off_kr, off_q = A_IN // B_KV_RANK, (A_IN + B_KV_RANK) // LANES, (A_IN + B_KV_RANK + LANES) // B_Q_RANK
    full = lambda a: pl.BlockSpec(a.shape, lambda b, i: (0, 0))
    tab_spec = pl.BlockSpec((ts, LANES), lambda b, i: (i, 0))
    hw = B_HEADS * LANES
    return pl.pallas_call(
        _mla_prep_kernel,
        grid=(Bn, ns),
        in_specs=[pl.BlockSpec((ts, B_Q_RANK), lambda b, i: (row(b, i), off_q)),
                  pl.BlockSpec((ts, B_KV_RANK), lambda b, i: (row(b, i), off_kv)),
                  pl.BlockSpec((ts, LANES), lambda b, i: (row(b, i), off_kr)),
                  full(g_qa), full(g_kva), full(wq), full(wk), full(wv),
                  tab_spec, tab_spec, tab_spec],
        out_specs=[pl.BlockSpec((ts, hw), lambda b, i: (row(b, i), 0)),
                   pl.BlockSpec((ts, hw), lambda b, i: (row(b, i), 0)),
                   pl.BlockSpec((ts, B_OUT), lambda b, i: (row(b, i), 0))],
        out_shape=[jax.ShapeDtypeStruct((Bn * S, hw), BF16),
                   jax.ShapeDtypeStruct((Bn * S, hw), BF16),
                   jax.ShapeDtypeStruct((Bn * S, B_OUT), BF16)],
        compiler_params=_params("parallel", "parallel"),
        name="mla_prep",
    )(u2, u2, u2, g_qa, g_kva, wq, wk, wv, *tabs)


def _mla_kernel(q_ref, k_ref, v_ref, o_ref, *, tk):
    S = k_ref.shape[1]
    tq = q_ref.shape[1]
    nk = S // tk
    lane = lax.broadcasted_iota(jnp.int32, (1, LANES), 1)
    qs = [q_ref[0, :, sub * LANES:(sub + 1) * LANES] for sub in range(2)]

    def body(j, carry):
        k0 = pl.multiple_of(j * tk, tk)
        vt = v_ref[0, pl.ds(k0, tk), :]
        new = []
        for sub in range(2):
            m, l, acc = carry[sub]
            kt = k_ref[0, pl.ds(k0, tk), sub * LANES:(sub + 1) * LANES]
            s = lax.dot_general(qs[sub], kt, _NT, preferred_element_type=F32)
            m_new = jnp.maximum(m, jnp.max(s, axis=-1, keepdims=True))
            alpha = jnp.exp(m - m_new)
            p = jnp.exp(s - m_new)
            l = alpha * l + jnp.sum(p, axis=-1, keepdims=True)
            acc = alpha * acc + jnp.dot(p.astype(BF16), vt, preferred_element_type=F32)
            new.append((m_new, l, acc))
        return tuple(new)

    init = tuple((jnp.full((tq, 1), NEG_INF, F32), jnp.zeros((tq, 1), F32), jnp.zeros((tq, LANES), F32))
                 for _ in range(2))
    res = lax.fori_loop(0, nk, body, init)
    outs = [acc / l for (_, l, acc) in res]
    o_ref[0] = jnp.where(lane < B_V, outs[0], outs[1]).astype(o_ref.dtype)


def _mla_attention(q, k, v, Bn, S, tq=512, tk=512):
    tq, tk = min(tq, S), min(tk, S)
    hw = B_HEADS * LANES
    q3, k3, v3 = q.reshape(Bn, S, hw), k.reshape(Bn, S, hw), v.reshape(Bn, S, B_OUT)
    o = pl.pallas_call(
        functools.partial(_mla_kernel, tk=tk),
        grid=(Bn, B_HEADS // 2, S // tq),
        in_specs=[pl.BlockSpec((1, tq, 2 * LANES), lambda b, h, i: (b, i, h)),
                  pl.BlockSpec((1, S, 2 * LANES), lambda b, h, i: (b, 0, h)),
                  pl.BlockSpec((1, S, LANES), lambda b, h, i: (b, 0, h))],
        out_specs=pl.BlockSpec((1, tq, LANES), lambda b, h, i: (b, i, h)),
        out_shape=jax.ShapeDtypeStruct((Bn, S, B_OUT), BF16),
        compiler_params=_params("parallel", "parallel", "parallel"),
        name="mla_attention",
    )(q3, k3, v3)
    return o.reshape(Bn * S, B_OUT)


def _natten_bias(rpb):
    W = GRID_W
    c = np.arange(W)[:, None]
    kc = np.arange(W)[None, :]
    wstart = np.clip(c - C_WIN_COLS // 2, 0, W - C_WIN_COLS)
    col_valid = (kc >= wstart) & (kc < wstart + C_WIN_COLS)
    dcol = np.clip(kc - c + C_WIN_COLS - 1, 0, 2 * C_WIN_COLS - 2)
    off = np.arange(C_WIN_ROWS)[:, None]
    j = np.arange(C_WIN_ROWS)[None, :]
    drow = j - off + C_WIN_ROWS - 1
    b = rpb.astype(F32)[:, drow][:, :, :, dcol]
    b = jnp.where(col_valid[None, None, None], b, NEG_INF)
    b = jnp.transpose(b, (1, 0, 3, 2, 4))
    return b.reshape(C_WIN_ROWS, C_HEADS // 2, 2 * W, C_WIN_ROWS * W)


def _natten_kernel(q_ref, k_ref, v_ref, bias_ref, o_ref, *, rows):
    r = pl.program_id(1)
    r0 = jnp.clip(r - C_WIN_ROWS // 2, 0, rows - C_WIN_ROWS)
    k0 = pl.multiple_of(r0 * GRID_W, GRID_W)
    nk = C_WIN_ROWS * GRID_W
    lane = lax.broadcasted_iota(jnp.int32, (1, LANES), 1)
    lo = lane < HEAD_DIM
    n_pairs = C_HEADS // 2
    cols = [slice(pair * LANES, (pair + 1) * LANES) for pair in range(n_pairs)]
    scores = []
    for pair in range(n_pairs):
        q2 = q_ref[0, :, cols[pair]] * (HEAD_DIM ** -0.5)
        zero = jnp.zeros_like(q2)
        qs = jnp.concatenate([jnp.where(lo, q2, zero), jnp.where(lo, zero, q2)], axis=0)
        k2 = k_ref[0, pl.ds(k0, nk), cols[pair]]
        scores.append(lax.dot_general(qs, k2, _NT, preferred_element_type=F32) + bias_ref[0, pair])
    probs, dens = [], []
    for s in scores:
        p = jnp.exp(s - jnp.max(s, axis=-1, keepdims=True))
        dens.append(jnp.sum(p, axis=-1, keepdims=True))
        probs.append(p.astype(BF16))
    for pair in range(n_pairs):
        v2 = v_ref[0, pl.ds(k0, nk), cols[pair]]
        o = jnp.dot(probs[pair], v2, preferred_element_type=F32) * (1.0 / dens[pair])
        o_ref[0, :, cols[pair]] = jnp.where(lo, o[:GRID_W], o[GRID_W:]).astype(o_ref.dtype)


def _natten(u3, bias):
    Bn, S, _ = u3.shape
    rows = S // GRID_W
    assert rows >= C_WIN_ROWS

    def variant(b, r):
        return (r - jnp.clip(r - C_WIN_ROWS // 2, 0, rows - C_WIN_ROWS), 0, 0, 0)

    o = pl.pallas_call(
        functools.partial(_natten_kernel, rows=rows),
        grid=(Bn, rows),
        in_specs=[pl.BlockSpec((1, GRID_W, C_OUT), lambda b, r: (b, r, 0)),
                  pl.BlockSpec((1, S, C_OUT), lambda b, r: (b, 0, 1), pipeline_mode=pl.Buffered(1)),
                  pl.BlockSpec((1, S, C_OUT), lambda b, r: (b, 0, 2), pipeline_mode=pl.Buffered(1)),
                  pl.BlockSpec((1, C_HEADS // 2, 2 * GRID_W, C_WIN_ROWS * GRID_W), variant)],
        out_specs=pl.BlockSpec((1, GRID_W, C_OUT), lambda b, r: (b, r, 0)),
        out_shape=jax.ShapeDtypeStruct((Bn, S, C_OUT), BF16),
        compiler_params=_params("parallel", "arbitrary"),
        name="natten",
    )(u3, u3, u3, bias)
    return o.reshape(Bn * S, C_OUT)


def _moe(x2, ln_g, router, wg, wu, wd):
    T, D = x2.shape
    cap = CAPACITY_FACTOR * T // N_EXPERTS
    aff_t, h = _router(x2, ln_g, router)
    slot, pos = _select(aff_t, cap)
    tb = min(COMBINE_TB, T)
    starts = jnp.concatenate([pos[:, ::tb], jnp.full((N_EXPERTS, 1), cap, jnp.int32)], axis=1)
    xe = _gather(h, slot, starts, cap)
    ye = _expert_ffn(xe, cap, wg, wu, wd)
    return _combine(x2, slot.T, starts, ye)


def _trunk(x, p):
    Bn, S, D = x.shape
    T = Bn * S
    x2 = x.reshape(T, D)
    u = _matmul(x2, p["w_in_even"], gain=p["ln_mix_g"][0], out_dtype=BF16)
    u3 = u.reshape(Bn, S, EVEN_COLS)
    os_, ls_ = [], []
    for g, (window, dilation) in enumerate(A_PATTERNS):
        bias = _dilated_bias(p["t5_table"], g, S // dilation, window, dilation)
        o, l = _dilated_attention(u3, bias, g, window, dilation)
        os_.append(o)
        ls_.append(l)
    a = _merge_groups(os_, ls_)
    q, k, v = _mla_prep(u3, p["g_qa"], p["g_kva"], p["wq"], p["wk"], p["wv"])
    b = _mla_attention(q, k, v, Bn, S)
    x2 = _matmul([a, b], [p["w_out_even"][:A_OUT], p["w_out_even"][A_OUT:]], res=x2, out_dtype=F32)
    x2 = _moe(x2, p["ln_ffn_g"][0], p["router"][0], p["w_gate"][0], p["w_up"][0], p["w_down"][0])
    u = _matmul(x2, p["w_in_odd"], gain=p["ln_mix_g"][1], out_dtype=BF16)
    c = _natten(u.reshape(Bn, S, C_IN), p["natten_bias"])
    x2 = _matmul(c, p["w_out_odd"], res=x2, out_dtype=F32)
    x2 = _moe(x2, p["ln_ffn_g"][1], p["router"][1], p["w_gate"][1], p["w_up"][1], p["w_down"][1])
    return _rms(x2, p["final_g"].astype(F32)).reshape(Bn, S, D)


def _head_tiles(w, n_heads, per_head, keep):
    K = w.shape[0]
    w = w.reshape(K, n_heads, per_head)[:, :, :keep]
    return jnp.pad(w, ((0, 0), (0, 0), (0, LANES - keep))).reshape(K, n_heads * LANES)


def _prepare(ln_mix_g, w_in_even, g_qa, w_qb, g_kva, w_kvb, t5_table, w_out_even,
             w_in_odd, rpb, w_out_odd, ln_ffn_g, router, w_gate, w_up, w_down, final_g):
    w_in = w_in_even[0]
    w_a, w_cq = w_in[:, :A_IN], w_in[:, A_IN:A_IN + B_Q_RANK]
    w_ckv = w_in[:, A_IN + B_Q_RANK:A_IN + B_Q_RANK + B_KV_RANK]
    w_kr = w_in[:, A_IN + B_Q_RANK + B_KV_RANK:]
    w_kr_tile = jnp.pad(w_kr, ((0, 0), (B_NOPE, LANES - B_NOPE - B_ROPE)))
    w_even = jnp.concatenate([w_a, w_ckv, w_kr_tile, w_cq], axis=1)
    w_kv = w_kvb[0].reshape(B_KV_RANK, B_HEADS, B_NOPE + B_V)
    return dict(
        ln_mix_g=ln_mix_g, ln_ffn_g=ln_ffn_g, final_g=final_g, t5_table=t5_table,
        w_in_even=w_even.astype(BF16),
        g_qa=g_qa[0].reshape(1, -1).astype(F32), g_kva=g_kva[0].reshape(1, -1).astype(F32),
        wq=_head_tiles(w_qb[0], B_HEADS, B_NOPE + B_ROPE, B_NOPE + B_ROPE).astype(BF16),
        wk=_head_tiles(w_kvb[0], B_HEADS, B_NOPE + B_V, B_NOPE).astype(BF16),
        wv=w_kv[:, :, B_NOPE:].reshape(B_KV_RANK, B_OUT).astype(BF16),
        w_out_even=w_out_even[0].astype(BF16), w_in_odd=w_in_odd[0].astype(BF16),
        natten_bias=_natten_bias(rpb[0]), w_out_odd=w_out_odd[0].astype(BF16), router=router,
        w_gate=w_gate.astype(BF16), w_up=w_up.astype(BF16), w_down=w_down.astype(BF16),
    )


def kernel(x_prompt, x_sample, ln_mix_g, w_in_even, g_qa, w_qb, g_kva, w_kvb, t5_table, w_out_even,
           w_in_odd, rpb, w_out_odd, ln_ffn_g, router, w_gate, w_up, w_down, final_g):
    p = _prepare(ln_mix_g, w_in_even, g_qa, w_qb, g_kva, w_kvb, t5_table, w_out_even,
                 w_in_odd, rpb, w_out_odd, ln_ffn_g, router, w_gate, w_up, w_down, final_g)
    return _trunk(x_prompt, p), _trunk(x_sample, p)
```

```python
import functools

import jax
import jax.numpy as jnp
import numpy as np
from jax import lax
from jax.experimental import pallas as pl
from jax.experimental.pallas import tpu as pltpu

D_MODEL = 1024
RMS_EPS = 1e-6
HEAD_DIM = 64
NEG_INF = -1e30
A_PATTERNS = ((128, 1), (512, 4), (2048, 16))
A_GROUPS = 3
A_HEADS_PER_GROUP = 4
A_HEADS = A_GROUPS * A_HEADS_PER_GROUP
A_IN = 3 * A_HEADS * HEAD_DIM
A_OUT = A_HEADS_PER_GROUP * HEAD_DIM
T5_BUCKETS = 32
T5_MAX_DISTANCE = 1024
B_HEADS = 12
B_Q_RANK = 384
B_KV_RANK = 256
B_NOPE = 64
B_ROPE = 32
B_V = 64
B_OUT = B_HEADS * B_V
ROPE_THETA = 10000.0
GRID_W = 64
C_HEADS = 16
C_WIN_ROWS = 8
C_WIN_COLS = 16
C_IN = 3 * C_HEADS * HEAD_DIM
C_OUT = C_HEADS * HEAD_DIM
N_EXPERTS = 16
CAPACITY_FACTOR = 2

LANES = 128
EVEN_COLS = A_IN + B_KV_RANK + LANES + B_Q_RANK
VMEM_LIMIT_BYTES = 48 * 1024 * 1024

F32 = jnp.float32
BF16 = jnp.bfloat16
_NT = (((1,), (1,)), ((), ()))
LOG2_E = 1.4426950408889634


def _params(*sem):
    return pltpu.CompilerParams(dimension_semantics=sem, vmem_limit_bytes=VMEM_LIMIT_BYTES)


def _rms(x, g):
    xf = x.astype(F32)
    return xf * lax.rsqrt(jnp.mean(xf * xf, axis=-1, keepdims=True) + RMS_EPS) * g


def _mm_kernel(*refs, n_in, has_gain, has_res, precise):
    xs, ws = refs[:n_in], refs[n_in:2 * n_in]
    pos = 2 * n_in
    g_ref = res_ref = None
    if has_gain:
        g_ref = refs[pos]
        pos += 1
    if has_res:
        res_ref = refs[pos]
        pos += 1
    o_ref = refs[pos]
    acc = None
    for x_ref, w_ref in zip(xs, ws):
        x = x_ref[...]
        if has_gain:
            x = _rms(x, g_ref[...])
        if precise:
            part = jnp.dot(x.astype(F32), w_ref[...], preferred_element_type=F32,
                           precision=lax.Precision.HIGHEST)
        else:
            part = jnp.dot(x.astype(BF16), w_ref[...], preferred_element_type=F32)
        acc = part if acc is None else acc + part
    if has_res:
        acc = acc + res_ref[...]
    o_ref[...] = acc.astype(o_ref.dtype)


def _matmul(xs, ws, gain=None, res=None, out_dtype=F32, tm=512, precise=False):
    if not isinstance(xs, (list, tuple)):
        xs, ws = [xs], [ws]
    M = xs[0].shape[0]
    N = ws[0].shape[1]
    assert M % tm == 0
    in_specs = [pl.BlockSpec((tm, x.shape[1]), lambda i: (i, 0)) for x in xs]
    in_specs += [pl.BlockSpec(w.shape, lambda i: (0, 0)) for w in ws]
    args = list(xs) + list(ws)
    if gain is not None:
        assert len(xs) == 1
        in_specs.append(pl.BlockSpec((1, gain.shape[-1]), lambda i: (0, 0)))
        args.append(gain.reshape(1, -1).astype(F32))
    if res is not None:
        in_specs.append(pl.BlockSpec((tm, N), lambda i: (i, 0)))
        args.append(res)
    return pl.pallas_call(
        functools.partial(_mm_kernel, n_in=len(xs), has_gain=gain is not None,
                          has_res=res is not None, precise=precise),
        grid=(M // tm,),
        in_specs=in_specs,
        out_specs=pl.BlockSpec((tm, N), lambda i: (i, 0)),
        out_shape=jax.ShapeDtypeStruct((M, N), out_dtype),
        compiler_params=_params("parallel"),
        name="fused_matmul",
    )(*args)


def _ffn_kernel(x_ref, wg_ref, wu_ref, wd_ref, o_ref):
    D = wg_ref.shape[1]
    x = x_ref[0, :, :D]
    lane = lax.broadcasted_iota(jnp.int32, (1, LANES), 1)
    mine = lane // GATE_TERMS == pl.program_id(0)
    gate = jnp.sum(jnp.where(mine, x_ref[0, :, D:].astype(F32), 0.0), axis=1, keepdims=True)
    g = jnp.dot(x, wg_ref[0], preferred_element_type=F32)
    u = jnp.dot(x, wu_ref[0], preferred_element_type=F32)
    h = (g * jax.nn.sigmoid(g)) * u
    y = jnp.dot(h.astype(BF16), wd_ref[0], preferred_element_type=F32)
    o_ref[0] = (y * gate).astype(o_ref.dtype)


def _expert_ffn(xe, cap, wg, wu, wd, tm=512):
    E = xe.shape[0]
    D, FF = wg.shape[1], wg.shape[2]
    assert cap % tm == 0
    return pl.pallas_call(
        _ffn_kernel,
        grid=(E, cap // tm),
        in_specs=[pl.BlockSpec((1, tm, D + LANES), lambda e, i: (e, i, 0)),
                  pl.BlockSpec((1, D, FF), lambda e, i: (e, 0, 0)),
                  pl.BlockSpec((1, D, FF), lambda e, i: (e, 0, 0)),
                  pl.BlockSpec((1, FF, D), lambda e, i: (e, 0, 0))],
        out_specs=pl.BlockSpec((1, tm, D), lambda e, i: (e, i, 0)),
        out_shape=jax.ShapeDtypeStruct((E, cap, D), BF16),
        compiler_params=_params("parallel", "parallel"),
        name="expert_ffn",
    )(xe, wg, wu, wd)


CHUNK = 256
COMBINE_TB = 512
COMBINE_W = 128


GATE_TERMS = 3


def _router_kernel(x_ref, g_ref, rt_ref, aff_ref, h_ref):
    D = x_ref.shape[1]
    E = rt_ref.shape[0]
    xn = _rms(x_ref[...], g_ref[...])
    logits = lax.dot_general(rt_ref[...], xn, _NT, preferred_element_type=F32,
                             precision=lax.Precision.HIGHEST)
    e = jnp.exp(logits - jnp.max(logits, axis=0, keepdims=True))
    aff = e / jnp.sum(e, axis=0, keepdims=True)
    aff_ref[...] = aff
    lane = lax.broadcasted_iota(jnp.int32, (E, LANES), 1)
    expert = lax.broadcasted_iota(jnp.int32, (E, LANES), 0)
    rest = aff
    gates = jnp.zeros((x_ref.shape[0], LANES), F32)
    for j in range(GATE_TERMS):
        term = rest.astype(BF16)
        rest = rest - term.astype(F32)
        place = (lane == GATE_TERMS * expert + j).astype(BF16)
        gates = gates + lax.dot_general(term, place, (((0,), (0,)), ((), ())), preferred_element_type=F32)
    h_ref[:, :D] = xn.astype(h_ref.dtype)
    h_ref[:, D:] = gates.astype(h_ref.dtype)


def _router(x2, ln_g, router, tm=512):
    T, D = x2.shape
    E = router.shape[1]
    assert GATE_TERMS * E <= LANES
    return pl.pallas_call(
        _router_kernel,
        grid=(T // tm,),
        in_specs=[pl.BlockSpec((tm, D), lambda i: (i, 0)),
                  pl.BlockSpec((1, D), lambda i: (0, 0)),
                  pl.BlockSpec((E, D), lambda i: (0, 0))],
        out_specs=[pl.BlockSpec((E, tm), lambda i: (0, i)),
                   pl.BlockSpec((tm, D + LANES), lambda i: (i, 0))],
        out_shape=[jax.ShapeDtypeStruct((E, T), F32), jax.ShapeDtypeStruct((T, D + LANES), BF16)],
        compiler_params=_params("parallel"),
        name="router",
    )(x2, ln_g.reshape(1, D).astype(F32), router.T.astype(F32))


def _cumsum_excl(mask, upper, lower_strict):
    incl = jnp.dot(mask.astype(BF16), upper, preferred_element_type=F32)
    before = jnp.dot(lower_strict, incl.astype(BF16), preferred_element_type=F32)
    return incl + before[:, CHUNK - 1:CHUNK] - mask


def _select_kernel(aff_ref, slot_ref, pos_ref, *, cap):
    aff = aff_ref[0]
    nc = aff.shape[0]
    bits = pltpu.bitcast(aff, jnp.int32)

    def count(cond):
        c = jnp.sum(jnp.where(cond, 1.0, 0.0), axis=0, keepdims=True)
        return jnp.sum(c, axis=1, keepdims=True)

    def bit_step(i, prefix):
        cand = prefix | (jnp.int32(1) << (30 - i))
        return jnp.where(count(bits >= cand) >= cap, cand, prefix)

    thr = lax.fori_loop(0, 31, bit_step, jnp.zeros((1, 1), jnp.int32))
    r = lax.broadcasted_iota(jnp.int32, (CHUNK, CHUNK), 0)
    c = lax.broadcasted_iota(jnp.int32, (CHUNK, CHUNK), 1)
    upper = (r <= c).astype(BF16)
    r = lax.broadcasted_iota(jnp.int32, (nc, nc), 0)
    c = lax.broadcasted_iota(jnp.int32, (nc, nc), 1)
    lower_strict = (c < r).astype(BF16)
    gt = bits > thr
    eq = jnp.where(bits == thr, 1.0, 0.0)
    need = cap - count(gt)
    sel = jnp.where(gt | ((eq > 0) & (_cumsum_excl(eq, upper, lower_strict) < need)), 1.0, 0.0)
    pos = _cumsum_excl(sel, upper, lower_strict).astype(jnp.int32)
    pos_ref[0] = pos
    slot_ref[0] = jnp.where(sel > 0, pos, -1)


def _select(aff_t, cap):
    E, T = aff_t.shape
    nc = T // CHUNK
    assert nc <= CHUNK and nc % 8 == 0
    spec = pl.BlockSpec((1, nc, CHUNK), lambda e: (e, 0, 0))
    slot, pos = pl.pallas_call(
        functools.partial(_select_kernel, cap=cap),
        grid=(E,), in_specs=[spec], out_specs=[spec, spec],
        out_shape=[jax.ShapeDtypeStruct((E, nc, CHUNK), jnp.int32)] * 2,
        compiler_params=_params("parallel"),
        name="select_topk",
    )(aff_t.reshape(E, nc, CHUNK))
    return slot.reshape(E, T), pos.reshape(E, T)


GATHER_STEP = 128
GATHER_W = GATHER_STEP + 16


def _gather_kernel(starts_ref, h_ref, slot_ref, xe_ref, stage_ref, extra_ref, tail_ref, sem, extra_sem, *, cap):
    b = pl.program_id(0)
    nb = pl.num_programs(0)
    E = slot_ref.shape[0]
    buf = b % 2
    h = h_ref[...]
    row = lax.broadcasted_iota(jnp.int32, (GATHER_W, 1), 0)

    @pl.when(b == 0)
    def _():
        tail_ref[...] = jnp.zeros_like(tail_ref)

    def first_row(e, blk):
        return pl.multiple_of((starts_ref[e, blk] // 16) * 16, 16)

    def copy(e, blk):
        return pltpu.make_async_copy(stage_ref.at[blk % 2, e], xe_ref.at[e, pl.ds(first_row(e, blk), GATHER_W)],
                                     sem.at[blk % 2, e])

    def per_expert(e, carry):
        base = first_row(e, b)
        slots = slot_ref[pl.ds(e, 1), :]
        end = starts_ref[e, b + 1] - base
        group = pl.multiple_of((end // 16) * 16, 16)
        last = jnp.maximum(group - 1, 0) // GATHER_STEP

        def window(k):
            onehot = (slots - (base + k * GATHER_STEP) == row).astype(BF16)
            return jnp.dot(onehot, h, preferred_element_type=F32).astype(BF16)

        stage_ref[buf, e] = window(0)
        stage_ref[buf, e, pl.ds(0, 16), :] += tail_ref[e]

        @pl.when(last == 0)
        def _():
            tail_ref[e] = stage_ref[buf, e, pl.ds(group, 16), :]

        @pl.when(b > 0)
        def _():
            copy(e, b - 1).wait()

        copy(e, b).start()

        def extra(k, inner):
            extra_ref[...] = window(k)

            @pl.when(k == last)
            def _():
                tail_ref[e] = extra_ref[pl.ds(pl.multiple_of(group - k * GATHER_STEP, 16), 16), :]

            cp = pltpu.make_async_copy(extra_ref, xe_ref.at[e, pl.ds(base + k * GATHER_STEP, GATHER_W)],
                                       extra_sem.at[0])
            cp.start()
            cp.wait()
            return inner

        return lax.fori_loop(1, last + 1, extra, carry)

    lax.fori_loop(0, E, per_expert, 0)

    @pl.when(b == nb - 1)
    def _():
        extra_ref[...] = jnp.zeros_like(extra_ref)

        def drain(e, carry):
            copy(e, b).wait()
            cp = pltpu.make_async_copy(extra_ref, xe_ref.at[e, pl.ds(cap, GATHER_W)], extra_sem.at[0])
            cp.start()
            cp.wait()
            return carry
        lax.fori_loop(0, E, drain, 0)


def _gather(h, slot, starts, cap):
    T, D = h.shape
    E = slot.shape[0]
    tb = min(COMBINE_TB, T)
    rows = cap + GATHER_W
    return pl.pallas_call(
        functools.partial(_gather_kernel, cap=cap),
        grid_spec=pltpu.PrefetchScalarGridSpec(
            num_scalar_prefetch=1,
            grid=(T // tb,),
            in_specs=[pl.BlockSpec((tb, D), lambda i, st: (i, 0)),
                      pl.BlockSpec((E, tb), lambda i, st: (0, i))],
            out_specs=pl.BlockSpec(memory_space=pl.ANY),
            scratch_shapes=[pltpu.VMEM((2, E, GATHER_W, D), BF16),
                            pltpu.VMEM((GATHER_W, D), BF16),
                            pltpu.VMEM((E, 16, D), BF16),
                            pltpu.SemaphoreType.DMA((2, E)),
                            pltpu.SemaphoreType.DMA((1,))]),
        out_shape=jax.ShapeDtypeStruct((E, rows, D), BF16),
        compiler_params=_params("arbitrary"),
        name="moe_gather",
    )(starts, h, slot)


def _combine_kernel(starts_ref, x_ref, slot_ref, first_ref, g_ref, ye_ref, o_ref, win_ref, extra_ref, sem,
                    extra_sem, *, cap, final_norm):
    b = pl.program_id(0)
    nb = pl.num_programs(0)
    E = ye_ref.shape[0]
    W = COMBINE_W

    def first_row(e, blk, k):
        return (starts_ref[e, blk] // 16) * 16 + k * W

    def window_start(e, blk, k):
        return pl.multiple_of(jnp.minimum(first_row(e, blk, k), cap - W), 16)

    def copy(e, blk):
        buf = blk % 2
        return pltpu.make_async_copy(ye_ref.at[e, pl.ds(window_start(e, blk, 0), W)],
                                     win_ref.at[buf, pl.ds(e * W, W)], sem.at[buf, e])

    @pl.when(b == 0)
    def _():
        for e in range(E):
            copy(e, b).start()

    @pl.when(b + 1 < nb)
    def _():
        for e in range(E):
            copy(e, b + 1).start()

    slots = slot_ref[...]
    first = first_ref[0]
    rel = slots - jnp.minimum(first, cap - W)
    rel = jnp.where((slots >= first) & (rel >= 0) & (rel < W), rel, -1).astype(F32).astype(BF16)
    lane = lax.broadcasted_iota(jnp.int32, (E, E * W), 1)
    expert = lax.broadcasted_iota(jnp.int32, (E, E * W), 0)
    expand = (lane // W == expert).astype(BF16)
    wanted = (lax.broadcasted_iota(jnp.int32, (1, E * W), 1) % W).astype(F32)
    onehot = (jnp.dot(rel, expand, preferred_element_type=F32) == wanted).astype(BF16)

    for e in range(E):
        copy(e, b).wait()
    o_ref[...] = x_ref[...] + jnp.dot(onehot, win_ref[b % 2], preferred_element_type=F32)

    col = lax.broadcasted_iota(jnp.int32, (1, W), 1)
    expert_lane = lax.broadcasted_iota(jnp.int32, (1, E), 1)

    def per_expert(e, carry):
        n_win = (starts_ref[e, b + 1] - first_row(e, b, 0) + W - 1) // W

        def extra(k, inner):
            cp = pltpu.make_async_copy(ye_ref.at[e, pl.ds(window_start(e, b, k), W)], extra_ref, extra_sem.at[0])
            cp.start()
            cp.wait()
            slot = jnp.max(jnp.where(expert_lane == e, slots, -1), axis=1, keepdims=True)
            oh = ((slot == window_start(e, b, k) + col) & (slot >= first_row(e, b, k))).astype(BF16)
            o_ref[...] += jnp.dot(oh, extra_ref[...], preferred_element_type=F32)
            return inner

        return lax.fori_loop(1, n_win, extra, carry)

    lax.fori_loop(0, E, per_expert, 0)
    if final_norm:
        o_ref[...] = _rms(o_ref[...], g_ref[...])


def _combine(x2, slot_t, starts, ye, final_g=None):
    T, D = x2.shape
    gain = jnp.ones((1, D), F32) if final_g is None else final_g.reshape(1, D).astype(F32)
    E, cap, _ = ye.shape
    tb = min(COMBINE_TB, T)
    nb = T // tb
    assert cap >= COMBINE_W and cap % 16 == 0
    first = ((starts[:, :nb] // 16) * 16).T.reshape(nb, 1, E)
    return pl.pallas_call(
        functools.partial(_combine_kernel, cap=cap, final_norm=final_g is not None),
        grid_spec=pltpu.PrefetchScalarGridSpec(
            num_scalar_prefetch=1,
            grid=(nb,),
            in_specs=[pl.BlockSpec((tb, D), lambda i, st: (i, 0)),
                      pl.BlockSpec((tb, E), lambda i, st: (i, 0)),
                      pl.BlockSpec((1, 1, E), lambda i, st: (i, 0, 0)),
                      pl.BlockSpec((1, D), lambda i, st: (0, 0)),
                      pl.BlockSpec(memory_space=pl.ANY)],
            out_specs=pl.BlockSpec((tb, D), lambda i, st: (i, 0)),
            scratch_shapes=[pltpu.VMEM((2, E * COMBINE_W, D), BF16),
                            pltpu.VMEM((COMBINE_W, D), BF16),
                            pltpu.SemaphoreType.DMA((2, E)),
                            pltpu.SemaphoreType.DMA((1,))]),
        out_shape=jax.ShapeDtypeStruct((T, D), F32),
        compiler_params=_params("arbitrary"),
        name="moe_combine",
    )(starts, x2, slot_t, first, gain, ye)


def _t5_bucket(rel):
    half = T5_BUCKETS // 2
    max_exact = half // 2
    n = np.abs(rel)
    large = max_exact + (np.log(np.maximum(n, 1) / max_exact) / np.log(T5_MAX_DISTANCE / max_exact)
                         * (half - max_exact)).astype(np.int32)
    large = np.minimum(large, half - 1)
    return (np.where(rel > 0, half, 0) + np.where(n < max_exact, n, large)).astype(np.int32)


def _dilated_tiles(L, radius):
    tq = min(LANES, L)
    kw = min(tq + 2 * radius, L)
    return tq, kw


def _dilated_bias(t5_table, g, L, window, dilation):
    radius = window // (2 * dilation)
    tq, kw = _dilated_tiles(L, radius)
    deltas = (0, -((kw - tq) // 2), tq - kw)
    period = 2 * kw - 1
    rel = np.arange(period + 1) - (kw - 1)
    tab = t5_table[:, g * A_HEADS_PER_GROUP:(g + 1) * A_HEADS_PER_GROUP].astype(F32)
    by_rel = jnp.where((np.abs(rel) <= radius)[None], tab[_t5_bucket(rel * dilation)].T, NEG_INF)
    skew = jnp.tile(by_rel, (1, tq))[:, :tq * period].reshape(A_HEADS_PER_GROUP, tq, period)
    bias = jnp.stack([skew[:, :, d + kw - 1:d + 2 * kw - 1] for d in deltas])
    return bias.reshape(3, A_HEADS_PER_GROUP // 2, 2 * tq, kw)


def _dilated_kernel(q_ref, k_ref, v_ref, bias_ref, o_ref, lse_ref, *, L, tq, kw):
    nq = L // tq
    lane = lax.broadcasted_iota(jnp.int32, (1, LANES), 1)
    lo = lane < HEAD_DIM

    def body(i, carry):
        q0 = pl.multiple_of(i * tq, tq)
        start = jnp.clip(q0 - (kw - tq) // 2, 0, L - kw)
        start = pl.multiple_of(start, 16)
        var = jnp.where(i == 0, 0, jnp.where(i == nq - 1, 2, 1))
        n_pairs = A_HEADS_PER_GROUP // 2
        cols = [slice(pair * LANES, (pair + 1) * LANES) for pair in range(n_pairs)]
        scores = []
        for pair in range(n_pairs):
            q2 = q_ref[0, pl.ds(q0, tq), cols[pair]] * (HEAD_DIM ** -0.5)
            zero = jnp.zeros_like(q2)
            qs = jnp.concatenate([jnp.where(lo, q2, zero), jnp.where(lo, zero, q2)], axis=0)
            k2 = k_ref[0, pl.ds(start, kw), cols[pair]]
            scores.append(lax.dot_general(qs, k2, _NT, preferred_element_type=F32) + bias_ref[var, pair])
        probs, dens, maxs = [], [], []
        for s in scores:
            m = jnp.max(s, axis=-1, keepdims=True)
            p = jnp.exp(s - m)
            maxs.append(m)
            dens.append(jnp.sum(p, axis=-1, keepdims=True))
            probs.append(p.astype(BF16))
        for pair in range(n_pairs):
            v2 = v_ref[0, pl.ds(start, kw), cols[pair]]
            o = jnp.dot(probs[pair], v2, preferred_element_type=F32) * (1.0 / dens[pair])
            lse = maxs[pair] + jnp.log(dens[pair])
            o_ref[0, pl.ds(q0, tq), cols[pair]] = jnp.where(lo, o[:tq], o[tq:])
            lse_ref[0, pl.ds(q0, tq), cols[pair]] = jnp.where(lo, lse[:tq], lse[tq:])
        return carry

    lax.fori_loop(0, nq, body, 0)


def _dilated_attention(u3, bias, g, window, dilation):
    Bn, S, C = u3.shape
    L = S // dilation
    radius = window // (2 * dilation)
    tq, kw = _dilated_tiles(L, radius)
    assert L % tq == 0 and kw % 16 == 0 and tq % 16 == 0
    uv = u3.reshape(Bn, L, dilation * C)
    cb = C // A_OUT
    n_col = A_HEADS * HEAD_DIM // A_OUT

    def spec(which):
        return pl.BlockSpec((1, L, A_OUT), lambda b, r: (b, 0, r * cb + which * n_col + g))

    out_spec = pl.BlockSpec((1, L, A_OUT), lambda b, r: (b, 0, r))
    o, lse = pl.pallas_call(
        functools.partial(_dilated_kernel, L=L, tq=tq, kw=kw),
        grid=(Bn, dilation),
        in_specs=[spec(0), spec(1), spec(2), pl.BlockSpec(bias.shape, lambda b, r: (0, 0, 0, 0))],
        out_specs=[out_spec, out_spec],
        out_shape=[jax.ShapeDtypeStruct((Bn, L, dilation * A_OUT), F32)] * 2,
        compiler_params=_params("parallel", "parallel"),
        name="dilated_attention",
    )(uv, uv, uv, bias)
    return o.reshape(Bn * S, A_OUT), lse.reshape(Bn * S, A_OUT)


def _merge_kernel(o0, o1, o2, l0, l1, l2, out_ref):
    ls = [l0[...], l1[...], l2[...]]
    m = jnp.maximum(jnp.maximum(ls[0], ls[1]), ls[2])
    es = [jnp.exp(l - m) for l in ls]
    den = es[0] + es[1] + es[2]
    out = (es[0] / den) * o0[...] + (es[1] / den) * o1[...] + (es[2] / den) * o2[...]
    out_ref[...] = out.astype(out_ref.dtype)


def _merge_groups(os_, ls_, tm=1024):
    T = os_[0].shape[0]
    spec = pl.BlockSpec((tm, A_OUT), lambda i: (i, 0))
    return pl.pallas_call(
        _merge_kernel, grid=(T // tm,), in_specs=[spec] * 6, out_specs=spec,
        out_shape=jax.ShapeDtypeStruct((T, A_OUT), BF16),
        compiler_params=_params("parallel"), name="merge_groups",
    )(*os_, *ls_)


def _rope_tables(S):
    inv = 1.0 / (ROPE_THETA ** (np.arange(0, B_ROPE, 2, dtype=np.float32) / B_ROPE))
    ang = np.arange(S, dtype=np.float32)[:, None] * inv[None]
    cos, sin = np.cos(ang), np.sin(ang)
    half = B_ROPE // 2
    c = np.ones((S, LANES), np.float32)
    c[:, B_NOPE:B_NOPE + half] = cos
    c[:, B_NOPE + half:B_NOPE + B_ROPE] = cos
    s_lo = np.zeros((S, LANES), np.float32)
    s_lo[:, B_NOPE:B_NOPE + half] = -sin
    s_hi = np.zeros((S, LANES), np.float32)
    s_hi[:, B_NOPE + half:B_NOPE + B_ROPE] = sin
    return jnp.asarray(c), jnp.asarray(s_lo), jnp.asarray(s_hi)


def _rope(x, c, s_lo, s_hi):
    half = B_ROPE // 2
    return x * c + pltpu.roll(x, LANES - half, 1) * s_lo + pltpu.roll(x, half, 1) * s_hi


def _mla_prep_kernel(cq_ref, ckv_ref, kr_ref, gq_ref, gkv_ref, wq_ref, wk_ref, wv_ref,
                     c_ref, slo_ref, shi_ref, q_ref, k_ref, v_ref):
    c, s_lo, s_hi = c_ref[...], slo_ref[...], shi_ref[...]
    scale = (B_NOPE + B_ROPE) ** -0.5 * LOG2_E
    xq = _rms(cq_ref[...], gq_ref[...]).astype(BF16)
    q = jnp.dot(xq, wq_ref[...], preferred_element_type=F32)
    xkv = _rms(ckv_ref[...], gkv_ref[...]).astype(BF16)
    k = jnp.dot(xkv, wk_ref[...], preferred_element_type=F32)
    vt = lax.dot_general(wv_ref[...], xkv, _NT, preferred_element_type=F32)
    ones_row = lax.broadcasted_iota(jnp.int32, vt.shape, 0) % LANES == B_V
    v_ref[0, 0] = jnp.where(ones_row, 1.0, vt).astype(v_ref.dtype)
    k_rope = _rope(kr_ref[...].astype(F32), c, s_lo, s_hi)
    for h in range(B_HEADS):
        cs = slice(h * LANES, (h + 1) * LANES)
        q_ref[:, cs] = (_rope(q[:, cs], c, s_lo, s_hi) * scale).astype(q_ref.dtype)
        k_ref[:, cs] = (k[:, cs] + k_rope).astype(k_ref.dtype)


MLA_KEY_CHUNK = 1024


def _mla_prep(u3, g_qa, g_kva, wq, wk, wv):
    Bn, S, C = u3.shape
    ts = min(MLA_KEY_CHUNK, S)
    tabs = _rope_tables(S)
    ns = S // ts
    u2 = u3.reshape(Bn * S, C)
    row = lambda b, i: b * ns + i
    off_kv, off_kr, off_q = A_IN // B_KV_RANK, (A_IN + B_KV_RANK) // LANES, (A_IN + B_KV_RANK + LANES) // B_Q_RANK
    full = lambda a: pl.BlockSpec(a.shape, lambda b, i: (0, 0))
    tab_spec = pl.BlockSpec((ts, LANES), lambda b, i: (i, 0))
    hw = B_HEADS * LANES
    return pl.pallas_call(
        _mla_prep_kernel,
        grid=(Bn, ns),
        in_specs=[pl.BlockSpec((ts, B_Q_RANK), lambda b, i: (row(b, i), off_q)),
                  pl.BlockSpec((ts, B_KV_RANK), lambda b, i: (row(b, i), off_kv)),
                  pl.BlockSpec((ts, LANES), lambda b, i: (row(b, i), off_kr)),
                  full(g_qa), full(g_kva), full(wq), full(wk), full(wv),
                  tab_spec, tab_spec, tab_spec],
        out_specs=[pl.BlockSpec((ts, hw), lambda b, i: (row(b, i), 0)),
                   pl.BlockSpec((ts, hw), lambda b, i: (row(b, i), 0)),
                   pl.BlockSpec((1, 1, hw, ts), lambda b, i: (b, i, 0, 0))],
        out_shape=[jax.ShapeDtypeStruct((Bn * S, hw), BF16),
                   jax.ShapeDtypeStruct((Bn * S, hw), BF16),
                   jax.ShapeDtypeStruct((Bn, ns, hw, ts), BF16)],
        compiler_params=_params("parallel", "parallel"),
        name="mla_prep",
    )(u2, u2, u2, g_qa, g_kva, wq, wk, wv, *tabs)


def _mla_kernel(q_ref, k_ref, v_ref, o_ref):
    nk, _, tk = v_ref.shape[1:]
    tq = q_ref.shape[1]
    lane = lax.broadcasted_iota(jnp.int32, (1, LANES), 1)
    qs = [q_ref[0, :, sub * LANES:(sub + 1) * LANES] for sub in range(2)]

    def body(j, carry):
        k0 = pl.multiple_of(j * tk, tk)
        scores = [lax.dot_general(k_ref[0, pl.ds(k0, tk), sub * LANES:(sub + 1) * LANES], qs[sub], _NT,
                                  preferred_element_type=F32) for sub in range(2)]
        new = []
        for sub in range(2):
            m, acc = carry[sub]
            s = scores[sub]
            m_new = jnp.maximum(m, jnp.max(s, axis=0, keepdims=True))
            p = jnp.exp2(s - m_new).astype(BF16)
            vt = v_ref[0, j, sub * LANES:(sub + 1) * LANES, :]
            acc = jnp.exp2(m - m_new) * acc + jnp.dot(vt, p, preferred_element_type=F32)
            new.append((m_new, acc))
        return tuple(new)

    init = tuple((jnp.full((1, tq), NEG_INF, F32), jnp.zeros((LANES, tq), F32)) for _ in range(2))
    res = lax.fori_loop(0, nk, body, init, unroll=2 if nk % 2 == 0 else 1)
    outs = [(acc * (1.0 / acc[B_V:B_V + 1])).T for (_, acc) in res]
    o_ref[0] = jnp.where(lane < B_V, outs[0], pltpu.roll(outs[1], B_V, 1)).astype(o_ref.dtype)


def _mla_attention(q, k, vt, Bn, S, tq=1024):
    tq = min(tq, S)
    nk, _, tk = vt.shape[1:]
    hw = B_HEADS * LANES
    q3, k3 = q.reshape(Bn, S, hw), k.reshape(Bn, S, hw)
    o = pl.pallas_call(
        _mla_kernel,
        grid=(Bn, B_HEADS // 2, S // tq),
        in_specs=[pl.BlockSpec((1, tq, 2 * LANES), lambda b, h, i: (b, i, h)),
                  pl.BlockSpec((1, S, 2 * LANES), lambda b, h, i: (b, 0, h)),
                  pl.BlockSpec((1, nk, 2 * LANES, tk), lambda b, h, i: (b, 0, h, 0))],
        out_specs=pl.BlockSpec((1, tq, LANES), lambda b, h, i: (b, i, h)),
        out_shape=jax.ShapeDtypeStruct((Bn, S, B_OUT), BF16),
        compiler_params=_params("parallel", "parallel", "parallel"),
        name="mla_attention",
    )(q3, k3, vt)
    return o.reshape(Bn * S, B_OUT)


def _natten_bias(rpb):
    W = GRID_W
    c = np.arange(W)[:, None]
    kc = np.arange(W)[None, :]
    wstart = np.clip(c - C_WIN_COLS // 2, 0, W - C_WIN_COLS)
    col_valid = (kc >= wstart) & (kc < wstart + C_WIN_COLS)
    dcol = np.clip(kc - c + C_WIN_COLS - 1, 0, 2 * C_WIN_COLS - 2)
    off = np.arange(C_WIN_ROWS)[:, None]
    j = np.arange(C_WIN_ROWS)[None, :]
    drow = j - off + C_WIN_ROWS - 1
    b = rpb.astype(F32)[:, drow][:, :, :, dcol]
    b = jnp.where(col_valid[None, None, None], b, NEG_INF)
    b = jnp.transpose(b, (1, 0, 3, 2, 4))
    return b.reshape(C_WIN_ROWS, C_HEADS // 2, 2 * W, C_WIN_ROWS * W)


def _natten_kernel(q_ref, k_ref, v_ref, bias_ref, o_ref, *, rows):
    r = pl.program_id(1)
    r0 = jnp.clip(r - C_WIN_ROWS // 2, 0, rows - C_WIN_ROWS)
    k0 = pl.multiple_of(r0 * GRID_W, GRID_W)
    nk = C_WIN_ROWS * GRID_W
    lane = lax.broadcasted_iota(jnp.int32, (1, LANES), 1)
    lo = lane < HEAD_DIM
    n_pairs = C_HEADS // 2
    cols = [slice(pair * LANES, (pair + 1) * LANES) for pair in range(n_pairs)]
    scores = []
    for pair in range(n_pairs):
        q2 = q_ref[0, :, cols[pair]] * (HEAD_DIM ** -0.5)
        zero = jnp.zeros_like(q2)
        qs = jnp.concatenate([jnp.where(lo, q2, zero), jnp.where(lo, zero, q2)], axis=0)
        k2 = k_ref[0, pl.ds(k0, nk), cols[pair]]
        scores.append(lax.dot_general(qs, k2, _NT, preferred_element_type=F32) + bias_ref[0, pair])
    probs, dens = [], []
    for s in scores:
        p = jnp.exp(s - jnp.max(s, axis=-1, keepdims=True))
        dens.append(jnp.sum(p, axis=-1, keepdims=True))
        probs.append(p.astype(BF16))
    for pair in range(n_pairs):
        v2 = v_ref[0, pl.ds(k0, nk), cols[pair]]
        o = jnp.dot(probs[pair], v2, preferred_element_type=F32) * (1.0 / dens[pair])
        o_ref[0, :, cols[pair]] = jnp.where(lo, o[:GRID_W], o[GRID_W:]).astype(o_ref.dtype)


def _natten(u3, bias):
    Bn, S, _ = u3.shape
    rows = S // GRID_W
    assert rows >= C_WIN_ROWS

    def variant(b, r):
        return (r - jnp.clip(r - C_WIN_ROWS // 2, 0, rows - C_WIN_ROWS), 0, 0, 0)

    o = pl.pallas_call(
        functools.partial(_natten_kernel, rows=rows),
        grid=(Bn, rows),
        in_specs=[pl.BlockSpec((1, GRID_W, C_OUT), lambda b, r: (b, r, 0)),
                  pl.BlockSpec((1, S, C_OUT), lambda b, r: (b, 0, 1), pipeline_mode=pl.Buffered(1)),
                  pl.BlockSpec((1, S, C_OUT), lambda b, r: (b, 0, 2), pipeline_mode=pl.Buffered(1)),
                  pl.BlockSpec((1, C_HEADS // 2, 2 * GRID_W, C_WIN_ROWS * GRID_W), variant)],
        out_specs=pl.BlockSpec((1, GRID_W, C_OUT), lambda b, r: (b, r, 0)),
        out_shape=jax.ShapeDtypeStruct((Bn, S, C_OUT), BF16),
        compiler_params=_params("parallel", "arbitrary"),
        name="natten",
    )(u3, u3, u3, bias)
    return o.reshape(Bn * S, C_OUT)


def _moe(x2, ln_g, router, wg, wu, wd, final_g=None):
    T, D = x2.shape
    cap = CAPACITY_FACTOR * T // N_EXPERTS
    aff_t, h = _router(x2, ln_g, router)
    slot, pos = _select(aff_t, cap)
    tb = min(COMBINE_TB, T)
    starts = jnp.concatenate([pos[:, ::tb], jnp.full((N_EXPERTS, 1), cap, jnp.int32)], axis=1)
    xe = _gather(h, slot, starts, cap)
    ye = _expert_ffn(xe, cap, wg, wu, wd)
    return _combine(x2, slot.T, starts, ye, final_g)


def _trunk(x, p):
    Bn, S, D = x.shape
    T = Bn * S
    x2 = x.reshape(T, D)
    u = _matmul(x2, p["w_in_even"], gain=p["ln_mix_g"][0], out_dtype=BF16)
    u3 = u.reshape(Bn, S, EVEN_COLS)
    os_, ls_ = [], []
    for g, (window, dilation) in enumerate(A_PATTERNS):
        bias = _dilated_bias(p["t5_table"], g, S // dilation, window, dilation)
        o, l = _dilated_attention(u3, bias, g, window, dilation)
        os_.append(o)
        ls_.append(l)
    a = _merge_groups(os_, ls_)
    q, k, v = _mla_prep(u3, p["g_qa"], p["g_kva"], p["wq"], p["wk"], p["wv"])
    b = _mla_attention(q, k, v, Bn, S)
    x2 = _matmul([a, b], [p["w_out_even"][:A_OUT], p["w_out_even"][A_OUT:]], res=x2, out_dtype=F32)
    x2 = _moe(x2, p["ln_ffn_g"][0], p["router"][0], p["w_gate"][0], p["w_up"][0], p["w_down"][0])
    u = _matmul(x2, p["w_in_odd"], gain=p["ln_mix_g"][1], out_dtype=BF16)
    c = _natten(u.reshape(Bn, S, C_IN), p["natten_bias"])
    x2 = _matmul(c, p["w_out_odd"], res=x2, out_dtype=F32)
    x2 = _moe(x2, p["ln_ffn_g"][1], p["router"][1], p["w_gate"][1], p["w_up"][1], p["w_down"][1],
              final_g=p["final_g"])
    return x2.reshape(Bn, S, D)


def _head_tiles(w, n_heads, per_head, keep):
    K = w.shape[0]
    w = w.reshape(K, n_heads, per_head)[:, :, :keep]
    return jnp.pad(w, ((0, 0), (0, 0), (0, LANES - keep))).reshape(K, n_heads * LANES)


def _prepare(ln_mix_g, w_in_even, g_qa, w_qb, g_kva, w_kvb, t5_table, w_out_even,
             w_in_odd, rpb, w_out_odd, ln_ffn_g, router, w_gate, w_up, w_down, final_g):
    w_in = w_in_even[0]
    w_a, w_cq = w_in[:, :A_IN], w_in[:, A_IN:A_IN + B_Q_RANK]
    w_ckv = w_in[:, A_IN + B_Q_RANK:A_IN + B_Q_RANK + B_KV_RANK]
    w_kr = w_in[:, A_IN + B_Q_RANK + B_KV_RANK:]
    w_kr_tile = jnp.pad(w_kr, ((0, 0), (B_NOPE, LANES - B_NOPE - B_ROPE)))
    w_even = jnp.concatenate([w_a, w_ckv, w_kr_tile, w_cq], axis=1)
    w_kv = w_kvb[0].reshape(B_KV_RANK, B_HEADS, B_NOPE + B_V)
    return dict(
        ln_mix_g=ln_mix_g, ln_ffn_g=ln_ffn_g, final_g=final_g, t5_table=t5_table,
        w_in_even=w_even.astype(BF16),
        g_qa=g_qa[0].reshape(1, -1).astype(F32), g_kva=g_kva[0].reshape(1, -1).astype(F32),
        wq=_head_tiles(w_qb[0], B_HEADS, B_NOPE + B_ROPE, B_NOPE + B_ROPE).astype(BF16),
        wk=_head_tiles(w_kvb[0], B_HEADS, B_NOPE + B_V, B_NOPE).astype(BF16),
        wv=_head_tiles(w_kv[:, :, B_NOPE:].reshape(B_KV_RANK, B_OUT), B_HEADS, B_V, B_V).T.astype(BF16),
        w_out_even=w_out_even[0].astype(BF16), w_in_odd=w_in_odd[0].astype(BF16),
        natten_bias=_natten_bias(rpb[0]), w_out_odd=w_out_odd[0].astype(BF16), router=router,
        w_gate=w_gate.astype(BF16), w_up=w_up.astype(BF16), w_down=w_down.astype(BF16),
    )


def kernel(x_prompt, x_sample, ln_mix_g, w_in_even, g_qa, w_qb, g_kva, w_kvb, t5_table, w_out_even,
           w_in_odd, rpb, w_out_odd, ln_ffn_g, router, w_gate, w_up, w_down, final_g):
    p = _prepare(ln_mix_g, w_in_even, g_qa, w_qb, g_kva, w_kvb, t5_table, w_out_even,
                 w_in_odd, rpb, w_out_odd, ln_ffn_g, router, w_gate, w_up, w_down, final_g)
    return _trunk(x_prompt, p), _trunk(x_sample, p)
```

```python
import functools

import jax
import jax.numpy as jnp
import numpy as np
from jax import lax
from jax.experimental import pallas as pl
from jax.experimental.pallas import tpu as pltpu

D_MODEL = 1024
RMS_EPS = 1e-6
HEAD_DIM = 64
NEG_INF = -1e30
A_PATTERNS = ((128, 1), (512, 4), (2048, 16))
A_GROUPS = 3
A_HEADS_PER_GROUP = 4
A_HEADS = A_GROUPS * A_HEADS_PER_GROUP
A_IN = 3 * A_HEADS * HEAD_DIM
A_OUT = A_HEADS_PER_GROUP * HEAD_DIM
T5_BUCKETS = 32
T5_MAX_DISTANCE = 1024
B_HEADS = 12
B_Q_RANK = 384
B_KV_RANK = 256
B_NOPE = 64
B_ROPE = 32
B_V = 64
B_OUT = B_HEADS * B_V
ROPE_THETA = 10000.0
GRID_W = 64
C_HEADS = 16
C_WIN_ROWS = 8
C_WIN_COLS = 16
C_IN = 3 * C_HEADS * HEAD_DIM
C_OUT = C_HEADS * HEAD_DIM
N_EXPERTS = 16
CAPACITY_FACTOR = 2

LANES = 128
VMEM_LIMIT_BYTES = 48 * 1024 * 1024

F32 = jnp.float32
BF16 = jnp.bfloat16
_NT = (((1,), (1,)), ((), ()))
LOG2_E = 1.4426950408889634


def _params(*sem):
    return pltpu.CompilerParams(dimension_semantics=sem, vmem_limit_bytes=VMEM_LIMIT_BYTES)


def _rms(x, g):
    xf = x.astype(F32)
    return xf * lax.rsqrt(jnp.mean(xf * xf, axis=-1, keepdims=True) + RMS_EPS) * g


def _mm_kernel(*refs, n_in, has_gain, has_res, precise):
    xs, ws = refs[:n_in], refs[n_in:2 * n_in]
    pos = 2 * n_in
    g_ref = res_ref = None
    if has_gain:
        g_ref = refs[pos]
        pos += 1
    if has_res:
        res_ref = refs[pos]
        pos += 1
    o_ref = refs[pos]
    acc = None
    for x_ref, w_ref in zip(xs, ws):
        x = x_ref[...]
        if has_gain:
            x = _rms(x, g_ref[...])
        if precise:
            part = jnp.dot(x.astype(F32), w_ref[...], preferred_element_type=F32,
                           precision=lax.Precision.HIGHEST)
        else:
            part = jnp.dot(x.astype(BF16), w_ref[...], preferred_element_type=F32)
        acc = part if acc is None else acc + part
    if has_res:
        acc = acc + res_ref[...]
    o_ref[...] = acc.astype(o_ref.dtype)


def _matmul(xs, ws, gain=None, res=None, out_dtype=F32, tm=512, precise=False):
    if not isinstance(xs, (list, tuple)):
        xs, ws = [xs], [ws]
    M = xs[0].shape[0]
    N = ws[0].shape[1]
    assert M % tm == 0
    in_specs = [pl.BlockSpec((tm, x.shape[1]), lambda i: (i, 0)) for x in xs]
    in_specs += [pl.BlockSpec(w.shape, lambda i: (0, 0)) for w in ws]
    args = list(xs) + list(ws)
    if gain is not None:
        assert len(xs) == 1
        in_specs.append(pl.BlockSpec((1, gain.shape[-1]), lambda i: (0, 0)))
        args.append(gain.reshape(1, -1).astype(F32))
    if res is not None:
        in_specs.append(pl.BlockSpec((tm, N), lambda i: (i, 0)))
        args.append(res)
    return pl.pallas_call(
        functools.partial(_mm_kernel, n_in=len(xs), has_gain=gain is not None,
                          has_res=res is not None, precise=precise),
        grid=(M // tm,),
        in_specs=in_specs,
        out_specs=pl.BlockSpec((tm, N), lambda i: (i, 0)),
        out_shape=jax.ShapeDtypeStruct((M, N), out_dtype),
        compiler_params=_params("parallel"),
        name="fused_matmul",
    )(*args)


def _ffn_kernel(x_ref, wg_ref, wu_ref, wd_ref, o_ref):
    D = wg_ref.shape[1]
    x = x_ref[0, :, :D]
    lane = lax.broadcasted_iota(jnp.int32, (1, LANES), 1)
    mine = lane // GATE_TERMS == pl.program_id(0)
    gate = jnp.sum(jnp.where(mine, x_ref[0, :, D:].astype(F32), 0.0), axis=1, keepdims=True)
    g = jnp.dot(x, wg_ref[0], preferred_element_type=F32)
    u = jnp.dot(x, wu_ref[0], preferred_element_type=F32)
    h = (g * jax.nn.sigmoid(g)) * u
    y = jnp.dot(h.astype(BF16), wd_ref[0], preferred_element_type=F32)
    o_ref[0] = (y * gate).astype(o_ref.dtype)


def _expert_ffn(xe, cap, wg, wu, wd, tm=512):
    E = xe.shape[0]
    D, FF = wg.shape[1], wg.shape[2]
    assert cap % tm == 0
    return pl.pallas_call(
        _ffn_kernel,
        grid=(E, cap // tm),
        in_specs=[pl.BlockSpec((1, tm, D + LANES), lambda e, i: (e, i, 0)),
                  pl.BlockSpec((1, D, FF), lambda e, i: (e, 0, 0)),
                  pl.BlockSpec((1, D, FF), lambda e, i: (e, 0, 0)),
                  pl.BlockSpec((1, FF, D), lambda e, i: (e, 0, 0))],
        out_specs=pl.BlockSpec((1, tm, D), lambda e, i: (e, i, 0)),
        out_shape=jax.ShapeDtypeStruct((E, cap, D), BF16),
        compiler_params=_params("parallel", "parallel"),
        name="expert_ffn",
    )(xe, wg, wu, wd)


CHUNK = 256
COMBINE_TB = 512
COMBINE_W = 128


GATE_TERMS = 3


def _router_kernel(x_ref, g_ref, rt_ref, aff_ref, h_ref):
    D = x_ref.shape[1]
    E = rt_ref.shape[0]
    xn = _rms(x_ref[...], g_ref[...])
    logits = lax.dot_general(rt_ref[...], xn, _NT, preferred_element_type=F32,
                             precision=lax.Precision.HIGHEST)
    e = jnp.exp(logits - jnp.max(logits, axis=0, keepdims=True))
    aff = e / jnp.sum(e, axis=0, keepdims=True)
    aff_ref[...] = aff
    lane = lax.broadcasted_iota(jnp.int32, (E, LANES), 1)
    expert = lax.broadcasted_iota(jnp.int32, (E, LANES), 0)
    rest = aff
    gates = jnp.zeros((x_ref.shape[0], LANES), F32)
    for j in range(GATE_TERMS):
        term = rest.astype(BF16)
        rest = rest - term.astype(F32)
        place = (lane == GATE_TERMS * expert + j).astype(BF16)
        gates = gates + lax.dot_general(term, place, (((0,), (0,)), ((), ())), preferred_element_type=F32)
    h_ref[:, :D] = xn.astype(h_ref.dtype)
    h_ref[:, D:] = gates.astype(h_ref.dtype)


def _router(x2, ln_g, router, tm=512):
    T, D = x2.shape
    E = router.shape[1]
    assert GATE_TERMS * E <= LANES
    return pl.pallas_call(
        _router_kernel,
        grid=(T // tm,),
        in_specs=[pl.BlockSpec((tm, D), lambda i: (i, 0)),
                  pl.BlockSpec((1, D), lambda i: (0, 0)),
                  pl.BlockSpec((E, D), lambda i: (0, 0))],
        out_specs=[pl.BlockSpec((E, tm), lambda i: (0, i)),
                   pl.BlockSpec((tm, D + LANES), lambda i: (i, 0))],
        out_shape=[jax.ShapeDtypeStruct((E, T), F32), jax.ShapeDtypeStruct((T, D + LANES), BF16)],
        compiler_params=_params("parallel"),
        name="router",
    )(x2, ln_g.reshape(1, D).astype(F32), router.T.astype(F32))


def _cumsum_excl(mask, upper, lower_strict):
    incl = jnp.dot(mask.astype(BF16), upper, preferred_element_type=F32)
    before = jnp.dot(lower_strict, incl.astype(BF16), preferred_element_type=F32)
    return incl + before[:, CHUNK - 1:CHUNK] - mask


def _select_kernel(aff_ref, slot_ref, pos_ref, *, cap):
    aff = aff_ref[0]
    nc = aff.shape[0]
    bits = pltpu.bitcast(aff, jnp.int32)

    def count(cond):
        c = jnp.sum(jnp.where(cond, 1.0, 0.0), axis=0, keepdims=True)
        return jnp.sum(c, axis=1, keepdims=True)

    def bit_step(i, prefix):
        cand = prefix | (jnp.int32(1) << (30 - i))
        return jnp.where(count(bits >= cand) >= cap, cand, prefix)

    thr = lax.fori_loop(0, 31, bit_step, jnp.zeros((1, 1), jnp.int32))
    r = lax.broadcasted_iota(jnp.int32, (CHUNK, CHUNK), 0)
    c = lax.broadcasted_iota(jnp.int32, (CHUNK, CHUNK), 1)
    upper = (r <= c).astype(BF16)
    r = lax.broadcasted_iota(jnp.int32, (nc, nc), 0)
    c = lax.broadcasted_iota(jnp.int32, (nc, nc), 1)
    lower_strict = (c < r).astype(BF16)
    gt = bits > thr
    eq = jnp.where(bits == thr, 1.0, 0.0)
    need = cap - count(gt)
    sel = jnp.where(gt | ((eq > 0) & (_cumsum_excl(eq, upper, lower_strict) < need)), 1.0, 0.0)
    pos = _cumsum_excl(sel, upper, lower_strict).astype(jnp.int32)
    pos_ref[0] = pos
    slot_ref[0] = jnp.where(sel > 0, pos, -1)


def _select(aff_t, cap):
    E, T = aff_t.shape
    nc = T // CHUNK
    assert nc <= CHUNK and nc % 8 == 0
    spec = pl.BlockSpec((1, nc, CHUNK), lambda e: (e, 0, 0))
    slot, pos = pl.pallas_call(
        functools.partial(_select_kernel, cap=cap),
        grid=(E,), in_specs=[spec], out_specs=[spec, spec],
        out_shape=[jax.ShapeDtypeStruct((E, nc, CHUNK), jnp.int32)] * 2,
        compiler_params=_params("parallel"),
        name="select_topk",
    )(aff_t.reshape(E, nc, CHUNK))
    return slot.reshape(E, T), pos.reshape(E, T)


GATHER_STEP = 128
GATHER_W = GATHER_STEP + 16
GATHER_STACK = 8


def _gather_kernel(starts_ref, h_ref, slot_ref, xe_ref, stage_ref, extra_ref, tail_ref, sem, extra_sem, *, cap):
    b = pl.program_id(0)
    nb = pl.num_programs(0)
    E = slot_ref.shape[0]
    buf = b % 2
    h = h_ref[...]
    row = lax.broadcasted_iota(jnp.int32, (GATHER_W, 1), 0)

    @pl.when(b == 0)
    def _():
        tail_ref[...] = jnp.zeros_like(tail_ref)

    def first_row(e, blk):
        return pl.multiple_of((starts_ref[e, blk] // 16) * 16, 16)

    def copy(e, blk):
        return pltpu.make_async_copy(stage_ref.at[blk % 2, e], xe_ref.at[e, pl.ds(first_row(e, blk), GATHER_W)],
                                     sem.at[blk % 2, e])

    def onehot(e, k):
        return (slot_ref[pl.ds(e, 1), :] - (first_row(e, b) + k * GATHER_STEP) == row).astype(BF16)

    for e0 in range(0, E, GATHER_STACK):
        stacked = jnp.concatenate([onehot(e, 0) for e in range(e0, e0 + GATHER_STACK)], axis=0)
        rows = jnp.dot(stacked, h, preferred_element_type=F32).astype(BF16)
        for i in range(GATHER_STACK):
            stage_ref[buf, e0 + i] = rows[i * GATHER_W:(i + 1) * GATHER_W]

    def per_expert(e, carry):
        base = first_row(e, b)
        end = starts_ref[e, b + 1] - base
        group = pl.multiple_of((end // 16) * 16, 16)
        last = jnp.maximum(group - 1, 0) // GATHER_STEP

        def window(k):
            return jnp.dot(onehot(e, k), h, preferred_element_type=F32).astype(BF16)

        stage_ref[buf, e, pl.ds(0, 16), :] += tail_ref[e]

        @pl.when(last == 0)
        def _():
            tail_ref[e] = stage_ref[buf, e, pl.ds(group, 16), :]

        @pl.when(b > 0)
        def _():
            copy(e, b - 1).wait()

        copy(e, b).start()

        def extra(k, inner):
            extra_ref[...] = window(k)

            @pl.when(k == last)
            def _():
                tail_ref[e] = extra_ref[pl.ds(pl.multiple_of(group - k * GATHER_STEP, 16), 16), :]

            cp = pltpu.make_async_copy(extra_ref, xe_ref.at[e, pl.ds(base + k * GATHER_STEP, GATHER_W)],
                                       extra_sem.at[0])
            cp.start()
            cp.wait()
            return inner

        return lax.fori_loop(1, last + 1, extra, carry)

    lax.fori_loop(0, E, per_expert, 0)

    @pl.when(b == nb - 1)
    def _():
        extra_ref[...] = jnp.zeros_like(extra_ref)

        def drain(e, carry):
            copy(e, b).wait()
            cp = pltpu.make_async_copy(extra_ref, xe_ref.at[e, pl.ds(cap, GATHER_W)], extra_sem.at[0])
            cp.start()
            cp.wait()
            return carry
        lax.fori_loop(0, E, drain, 0)


def _gather(h, slot, starts, cap):
    T, D = h.shape
    E = slot.shape[0]
    tb = min(COMBINE_TB, T)
    rows = cap + GATHER_W
    return pl.pallas_call(
        functools.partial(_gather_kernel, cap=cap),
        grid_spec=pltpu.PrefetchScalarGridSpec(
            num_scalar_prefetch=1,
            grid=(T // tb,),
            in_specs=[pl.BlockSpec((tb, D), lambda i, st: (i, 0)),
                      pl.BlockSpec((E, tb), lambda i, st: (0, i))],
            out_specs=pl.BlockSpec(memory_space=pl.ANY),
            scratch_shapes=[pltpu.VMEM((2, E, GATHER_W, D), BF16),
                            pltpu.VMEM((GATHER_W, D), BF16),
                            pltpu.VMEM((E, 16, D), BF16),
                            pltpu.SemaphoreType.DMA((2, E)),
                            pltpu.SemaphoreType.DMA((1,))]),
        out_shape=jax.ShapeDtypeStruct((E, rows, D), BF16),
        compiler_params=_params("arbitrary"),
        name="moe_gather",
    )(starts, h, slot)


def _combine_kernel(starts_ref, x_ref, slot_ref, first_ref, g_ref, ye_ref, o_ref, win_ref, extra_ref, sem,
                    extra_sem, *, cap, final_norm):
    b = pl.program_id(0)
    nb = pl.num_programs(0)
    E = ye_ref.shape[0]
    W = COMBINE_W

    def first_row(e, blk, k):
        return (starts_ref[e, blk] // 16) * 16 + k * W

    def window_start(e, blk, k):
        return pl.multiple_of(jnp.minimum(first_row(e, blk, k), cap - W), 16)

    def copy(e, blk):
        buf = blk % 2
        return pltpu.make_async_copy(ye_ref.at[e, pl.ds(window_start(e, blk, 0), W)],
                                     win_ref.at[buf, pl.ds(e * W, W)], sem.at[buf, e])

    @pl.when(b == 0)
    def _():
        for e in range(E):
            copy(e, b).start()

    @pl.when(b + 1 < nb)
    def _():
        for e in range(E):
            copy(e, b + 1).start()

    slots = slot_ref[...]
    first = first_ref[0]
    rel = slots - jnp.minimum(first, cap - W)
    rel = jnp.where((slots >= first) & (rel >= 0) & (rel < W), rel, -1).astype(F32).astype(BF16)
    lane = lax.broadcasted_iota(jnp.int32, (E, E * W), 1)
    expert = lax.broadcasted_iota(jnp.int32, (E, E * W), 0)
    expand = (lane // W == expert).astype(BF16)
    wanted = (lax.broadcasted_iota(jnp.int32, (1, E * W), 1) % W).astype(F32)
    onehot = (jnp.dot(rel, expand, preferred_element_type=F32) == wanted).astype(BF16)

    for e in range(E):
        copy(e, b).wait()
    o_ref[...] = x_ref[...] + jnp.dot(onehot, win_ref[b % 2], preferred_element_type=F32)

    col = lax.broadcasted_iota(jnp.int32, (1, W), 1)
    expert_lane = lax.broadcasted_iota(jnp.int32, (1, E), 1)

    def per_expert(e, carry):
        n_win = (starts_ref[e, b + 1] - first_row(e, b, 0) + W - 1) // W

        def extra(k, inner):
            cp = pltpu.make_async_copy(ye_ref.at[e, pl.ds(window_start(e, b, k), W)], extra_ref, extra_sem.at[0])
            cp.start()
            cp.wait()
            slot = jnp.max(jnp.where(expert_lane == e, slots, -1), axis=1, keepdims=True)
            oh = ((slot == window_start(e, b, k) + col) & (slot >= first_row(e, b, k))).astype(BF16)
            o_ref[...] += jnp.dot(oh, extra_ref[...], preferred_element_type=F32)
            return inner

        return lax.fori_loop(1, n_win, extra, carry)

    lax.fori_loop(0, E, per_expert, 0)
    if final_norm:
        o_ref[...] = _rms(o_ref[...], g_ref[...])


def _combine(x2, slot_t, starts, ye, final_g=None):
    T, D = x2.shape
    gain = jnp.ones((1, D), F32) if final_g is None else final_g.reshape(1, D).astype(F32)
    E, cap, _ = ye.shape
    tb = min(COMBINE_TB, T)
    nb = T // tb
    assert cap >= COMBINE_W and cap % 16 == 0
    first = ((starts[:, :nb] // 16) * 16).T.reshape(nb, 1, E)
    return pl.pallas_call(
        functools.partial(_combine_kernel, cap=cap, final_norm=final_g is not None),
        grid_spec=pltpu.PrefetchScalarGridSpec(
            num_scalar_prefetch=1,
            grid=(nb,),
            in_specs=[pl.BlockSpec((tb, D), lambda i, st: (i, 0)),
                      pl.BlockSpec((tb, E), lambda i, st: (i, 0)),
                      pl.BlockSpec((1, 1, E), lambda i, st: (i, 0, 0)),
                      pl.BlockSpec((1, D), lambda i, st: (0, 0)),
                      pl.BlockSpec(memory_space=pl.ANY)],
            out_specs=pl.BlockSpec((tb, D), lambda i, st: (i, 0)),
            scratch_shapes=[pltpu.VMEM((2, E * COMBINE_W, D), BF16),
                            pltpu.VMEM((COMBINE_W, D), BF16),
                            pltpu.SemaphoreType.DMA((2, E)),
                            pltpu.SemaphoreType.DMA((1,))]),
        out_shape=jax.ShapeDtypeStruct((T, D), F32),
        compiler_params=_params("arbitrary"),
        name="moe_combine",
    )(starts, x2, slot_t, first, gain, ye)


def _t5_bucket(rel):
    half = T5_BUCKETS // 2
    max_exact = half // 2
    n = np.abs(rel)
    large = max_exact + (np.log(np.maximum(n, 1) / max_exact) / np.log(T5_MAX_DISTANCE / max_exact)
                         * (half - max_exact)).astype(np.int32)
    large = np.minimum(large, half - 1)
    return (np.where(rel > 0, half, 0) + np.where(n < max_exact, n, large)).astype(np.int32)


def _dilated_tiles(L, radius):
    tq = min(LANES, L)
    kw = min(tq + 2 * radius, L)
    return tq, kw


def _dilated_bias(t5_table, g, L, window, dilation):
    radius = window // (2 * dilation)
    tq, kw = _dilated_tiles(L, radius)
    deltas = (0, -((kw - tq) // 2), tq - kw)
    period = 2 * kw - 1
    rel = np.arange(period + 1) - (kw - 1)
    tab = t5_table[:, g * A_HEADS_PER_GROUP:(g + 1) * A_HEADS_PER_GROUP].astype(F32)
    by_rel = jnp.where((np.abs(rel) <= radius)[None], tab[_t5_bucket(rel * dilation)].T, NEG_INF)
    skew = jnp.tile(by_rel, (1, tq))[:, :tq * period].reshape(A_HEADS_PER_GROUP, tq, period)
    bias = jnp.stack([skew[:, :, d + kw - 1:d + 2 * kw - 1] for d in deltas])
    return bias.reshape(3, A_HEADS_PER_GROUP // 2, 2 * tq, kw)


def _dilated_kernel(q_ref, k_ref, v_ref, bias_ref, o_ref, lse_ref, *, L, tq, kw):
    nq = L // tq
    lane = lax.broadcasted_iota(jnp.int32, (1, LANES), 1)
    lo = lane < HEAD_DIM

    def body(i, carry):
        q0 = pl.multiple_of(i * tq, tq)
        start = jnp.clip(q0 - (kw - tq) // 2, 0, L - kw)
        start = pl.multiple_of(start, 16)
        var = jnp.where(i == 0, 0, jnp.where(i == nq - 1, 2, 1))
        n_pairs = A_HEADS_PER_GROUP // 2
        cols = [slice(pair * LANES, (pair + 1) * LANES) for pair in range(n_pairs)]
        scores = []
        for pair in range(n_pairs):
            q2 = q_ref[0, 0, pl.ds(q0, tq), cols[pair]] * (HEAD_DIM ** -0.5)
            zero = jnp.zeros_like(q2)
            qs = jnp.concatenate([jnp.where(lo, q2, zero), jnp.where(lo, zero, q2)], axis=0)
            k2 = k_ref[0, 0, pl.ds(start, kw), cols[pair]]
            scores.append(lax.dot_general(qs, k2, _NT, preferred_element_type=F32) + bias_ref[var, pair])
        probs, dens, maxs = [], [], []
        for s in scores:
            m = jnp.max(s, axis=-1, keepdims=True)
            p = jnp.exp(s - m)
            maxs.append(m)
            dens.append(jnp.sum(p, axis=-1, keepdims=True))
            probs.append(p.astype(BF16))
        for pair in range(n_pairs):
            v2 = v_ref[0, 0, pl.ds(start, kw), cols[pair]]
            o = jnp.dot(probs[pair], v2, preferred_element_type=F32) * (1.0 / dens[pair])
            lse = maxs[pair] + jnp.log(dens[pair])
            o_ref[0, 0, pl.ds(q0, tq), cols[pair]] = jnp.where(lo, o[:tq], o[tq:])
            lse_ref[0, 0, pl.ds(q0, tq), cols[pair]] = jnp.where(lo, lse[:tq], lse[tq:])
        return carry

    lax.fori_loop(0, nq, body, 0)


def _dilated_attention(qkv, bias, window):
    Bn, dilation, L, _ = qkv.shape
    radius = window // (2 * dilation)
    tq, kw = _dilated_tiles(L, radius)
    assert L % tq == 0 and kw % 16 == 0 and tq % 16 == 0

    def spec(which):
        return pl.BlockSpec((1, 1, L, A_OUT), lambda b, r: (b, r, 0, which))

    o, lse = pl.pallas_call(
        functools.partial(_dilated_kernel, L=L, tq=tq, kw=kw),
        grid=(Bn, dilation),
        in_specs=[spec(0), spec(1), spec(2), pl.BlockSpec(bias.shape, lambda b, r: (0, 0, 0, 0))],
        out_specs=[spec(0), spec(0)],
        out_shape=[jax.ShapeDtypeStruct((Bn, dilation, L, A_OUT), F32)] * 2,
        compiler_params=_params("parallel", "parallel"),
        name="dilated_attention",
    )(qkv, qkv, qkv, bias)
    return o, lse


def _merge_kernel(*refs):
    n = len(A_PATTERNS)
    o_refs, l_refs, out_ref = refs[:n], refs[n:2 * n], refs[2 * n]
    scratch = refs[2 * n + 1:]
    tm = out_ref.shape[0]

    def token_order(ref, scr, dilation):
        if dilation == 1:
            return ref[0, 0]
        tiles = A_OUT // LANES
        for r in range(dilation):
            for c in range(tiles):
                scr[c, pl.ds(r, tm // dilation, stride=dilation), :] = ref[0, r, :, c * LANES:(c + 1) * LANES]
        return jnp.concatenate([scr[c] for c in range(tiles)], axis=1)

    os_ = [token_order(o_refs[g], scratch[2 * g], d) for g, (_, d) in enumerate(A_PATTERNS)]
    ls = [token_order(l_refs[g], scratch[2 * g + 1], d) for g, (_, d) in enumerate(A_PATTERNS)]
    m = jnp.maximum(jnp.maximum(ls[0], ls[1]), ls[2])
    es = [jnp.exp(l - m) for l in ls]
    den = es[0] + es[1] + es[2]
    out = (es[0] / den) * os_[0] + (es[1] / den) * os_[1] + (es[2] / den) * os_[2]
    out_ref[...] = out.astype(out_ref.dtype)


def _merge_groups(os_, ls_, S, tm=512):
    Bn = os_[0].shape[0]
    ns = S // tm

    def spec(d):
        return pl.BlockSpec((1, d, tm // d, A_OUT), lambda i: (i // ns, 0, i % ns, 0))

    specs = [spec(d) for _, d in A_PATTERNS]
    return pl.pallas_call(
        _merge_kernel, grid=(Bn * ns,), in_specs=specs * 2,
        out_specs=pl.BlockSpec((tm, A_OUT), lambda i: (i, 0)),
        out_shape=jax.ShapeDtypeStruct((Bn * S, A_OUT), BF16),
        scratch_shapes=[pltpu.VMEM((A_OUT // LANES, tm, LANES), F32)] * (2 * len(A_PATTERNS)),
        compiler_params=_params("parallel"), name="merge_groups",
    )(*os_, *ls_)


EVEN_TOKEN_COLS = 3 * A_OUT + B_KV_RANK + LANES + B_Q_RANK


def _in_proj_even_kernel(x_ref, w_ref, g_ref, u0_ref, *rest):
    n_dil = len(A_PATTERNS) - 1
    dil_refs, scr = rest[:n_dil], rest[n_dil]
    tm = x_ref.shape[0]
    acc = jnp.dot(_rms(x_ref[...], g_ref[...]).astype(BF16), w_ref[...], preferred_element_type=F32)
    u0_ref[...] = acc[:, :EVEN_TOKEN_COLS].astype(u0_ref.dtype)
    for i, (_, d) in enumerate(A_PATTERNS[1:]):
        c0 = EVEN_TOKEN_COLS + i * 3 * A_OUT
        for c in range(3 * A_OUT // LANES):
            scr[c] = acc[:, c0 + c * LANES:c0 + (c + 1) * LANES]
        for r in range(d):
            for c in range(3 * A_OUT // LANES):
                piece = scr[c, pl.ds(r, tm // d, stride=d), :]
                dil_refs[i][0, r, :, c * LANES:(c + 1) * LANES] = piece.astype(dil_refs[i].dtype)


def _in_proj_even(x2, w, gain, Bn, S, tm=512):
    T, K = x2.shape
    ns = S // tm
    dils = [d for _, d in A_PATTERNS[1:]]
    assert S % tm == 0 and all(tm % (16 * d) == 0 for d in dils)
    return pl.pallas_call(
        _in_proj_even_kernel,
        grid=(T // tm,),
        in_specs=[pl.BlockSpec((tm, K), lambda i: (i, 0)),
                  pl.BlockSpec(w.shape, lambda i: (0, 0)),
                  pl.BlockSpec((1, K), lambda i: (0, 0))],
        out_specs=[pl.BlockSpec((tm, EVEN_TOKEN_COLS), lambda i: (i, 0))]
        + [pl.BlockSpec((1, d, tm // d, 3 * A_OUT), lambda i: (i // ns, 0, i % ns, 0)) for d in dils],
        out_shape=[jax.ShapeDtypeStruct((T, EVEN_TOKEN_COLS), BF16)]
        + [jax.ShapeDtypeStruct((Bn, d, S // d, 3 * A_OUT), BF16) for d in dils],
        scratch_shapes=[pltpu.VMEM((3 * A_OUT // LANES, tm, LANES), F32)],
        compiler_params=_params("parallel"),
        name="in_proj_even",
    )(x2, w, gain.reshape(1, K).astype(F32))


def _rope_tables(S):
    inv = 1.0 / (ROPE_THETA ** (np.arange(0, B_ROPE, 2, dtype=np.float32) / B_ROPE))
    ang = np.arange(S, dtype=np.float32)[:, None] * inv[None]
    cos, sin = np.cos(ang), np.sin(ang)
    half = B_ROPE // 2
    c = np.ones((S, LANES), np.float32)
    c[:, B_NOPE:B_NOPE + half] = cos
    c[:, B_NOPE + half:B_NOPE + B_ROPE] = cos
    s_lo = np.zeros((S, LANES), np.float32)
    s_lo[:, B_NOPE:B_NOPE + half] = -sin
    s_hi = np.zeros((S, LANES), np.float32)
    s_hi[:, B_NOPE + half:B_NOPE + B_ROPE] = sin
    return jnp.asarray(c), jnp.asarray(s_lo), jnp.asarray(s_hi)


def _rope(x, c, s_lo, s_hi):
    half = B_ROPE // 2
    return x * c + pltpu.roll(x, LANES - half, 1) * s_lo + pltpu.roll(x, half, 1) * s_hi


def _mla_prep_kernel(cq_ref, ckv_ref, kr_ref, gq_ref, gkv_ref, wq_ref, wk_ref, wv_ref,
                     c_ref, slo_ref, shi_ref, q_ref, k_ref, v_ref):
    c, s_lo, s_hi = c_ref[...], slo_ref[...], shi_ref[...]
    scale = (B_NOPE + B_ROPE) ** -0.5 * LOG2_E
    xq = _rms(cq_ref[...], gq_ref[...]).astype(BF16)
    q = jnp.dot(xq, wq_ref[...], preferred_element_type=F32)
    xkv = _rms(ckv_ref[...], gkv_ref[...]).astype(BF16)
    k = jnp.dot(xkv, wk_ref[...], preferred_element_type=F32)
    vt = lax.dot_general(wv_ref[...], xkv, _NT, preferred_element_type=F32)
    ones_row = lax.broadcasted_iota(jnp.int32, vt.shape, 0) % LANES == B_V
    v_ref[0, 0] = jnp.where(ones_row, 1.0, vt).astype(v_ref.dtype)
    k_rope = _rope(kr_ref[...].astype(F32), c, s_lo, s_hi)
    for h in range(B_HEADS):
        cs = slice(h * LANES, (h + 1) * LANES)
        q_ref[:, cs] = (_rope(q[:, cs], c, s_lo, s_hi) * scale).astype(q_ref.dtype)
        k_ref[:, cs] = (k[:, cs] + k_rope).astype(k_ref.dtype)


MLA_KEY_CHUNK = 1024


def _mla_prep(u3, g_qa, g_kva, wq, wk, wv):
    Bn, S, C = u3.shape
    ts = min(MLA_KEY_CHUNK, S)
    tabs = _rope_tables(S)
    ns = S // ts
    u2 = u3.reshape(Bn * S, C)
    row = lambda b, i: b * ns + i
    a_cols = 3 * A_OUT
    off_kv, off_kr, off_q = a_cols // B_KV_RANK, (a_cols + B_KV_RANK) // LANES, (a_cols + B_KV_RANK + LANES) // B_Q_RANK
    full = lambda a: pl.BlockSpec(a.shape, lambda b, i: (0, 0))
    tab_spec = pl.BlockSpec((ts, LANES), lambda b, i: (i, 0))
    hw = B_HEADS * LANES
    return pl.pallas_call(
        _mla_prep_kernel,
        grid=(Bn, ns),
        in_specs=[pl.BlockSpec((ts, B_Q_RANK), lambda b, i: (row(b, i), off_q)),
                  pl.BlockSpec((ts, B_KV_RANK), lambda b, i: (row(b, i), off_kv)),
                  pl.BlockSpec((ts, LANES), lambda b, i: (row(b, i), off_kr)),
                  full(g_qa), full(g_kva), full(wq), full(wk), full(wv),
                  tab_spec, tab_spec, tab_spec],
        out_specs=[pl.BlockSpec((ts, hw), lambda b, i: (row(b, i), 0)),
                   pl.BlockSpec((ts, hw), lambda b, i: (row(b, i), 0)),
                   pl.BlockSpec((1, 1, hw, ts), lambda b, i: (b, i, 0, 0))],
        out_shape=[jax.ShapeDtypeStruct((Bn * S, hw), BF16),
                   jax.ShapeDtypeStruct((Bn * S, hw), BF16),
                   jax.ShapeDtypeStruct((Bn, ns, hw, ts), BF16)],
        compiler_params=_params("parallel", "parallel"),
        name="mla_prep",
    )(u2, u2, u2, g_qa, g_kva, wq, wk, wv, *tabs)


def _mla_kernel(q_ref, k_ref, v_ref, o_ref):
    nk, _, tk = v_ref.shape[1:]
    tq = q_ref.shape[1]
    lane = lax.broadcasted_iota(jnp.int32, (1, LANES), 1)
    qs = [q_ref[0, :, sub * LANES:(sub + 1) * LANES] for sub in range(2)]

    def body(j, carry):
        k0 = pl.multiple_of(j * tk, tk)
        scores = [lax.dot_general(k_ref[0, pl.ds(k0, tk), sub * LANES:(sub + 1) * LANES], qs[sub], _NT,
                                  preferred_element_type=F32) for sub in range(2)]
        new = []
        for sub in range(2):
            m, acc = carry[sub]
            s = scores[sub]
            m_new = jnp.maximum(m, jnp.max(s, axis=0, keepdims=True))
            p = jnp.exp2(s - m_new).astype(BF16)
            vt = v_ref[0, j, sub * LANES:(sub + 1) * LANES, :]
            acc = jnp.exp2(m - m_new) * acc + jnp.dot(vt, p, preferred_element_type=F32)
            new.append((m_new, acc))
        return tuple(new)

    init = tuple((jnp.full((1, tq), NEG_INF, F32), jnp.zeros((LANES, tq), F32)) for _ in range(2))
    res = lax.fori_loop(0, nk, body, init, unroll=2 if nk % 2 == 0 else 1)
    outs = [(acc * (1.0 / acc[B_V:B_V + 1])).T for (_, acc) in res]
    o_ref[0] = jnp.where(lane < B_V, outs[0], pltpu.roll(outs[1], B_V, 1)).astype(o_ref.dtype)


def _mla_attention(q, k, vt, Bn, S, tq=1024):
    tq = min(tq, S)
    nk, _, tk = vt.shape[1:]
    hw = B_HEADS * LANES
    q3, k3 = q.reshape(Bn, S, hw), k.reshape(Bn, S, hw)
    o = pl.pallas_call(
        _mla_kernel,
        grid=(Bn, B_HEADS // 2, S // tq),
        in_specs=[pl.BlockSpec((1, tq, 2 * LANES), lambda b, h, i: (b, i, h)),
                  pl.BlockSpec((1, S, 2 * LANES), lambda b, h, i: (b, 0, h)),
                  pl.BlockSpec((1, nk, 2 * LANES, tk), lambda b, h, i: (b, 0, h, 0))],
        out_specs=pl.BlockSpec((1, tq, LANES), lambda b, h, i: (b, i, h)),
        out_shape=jax.ShapeDtypeStruct((Bn, S, B_OUT), BF16),
        compiler_params=_params("parallel", "parallel", "parallel"),
        name="mla_attention",
    )(q3, k3, vt)
    return o.reshape(Bn * S, B_OUT)


def _natten_bias(rpb):
    W = GRID_W
    c = np.arange(W)[:, None]
    kc = np.arange(W)[None, :]
    wstart = np.clip(c - C_WIN_COLS // 2, 0, W - C_WIN_COLS)
    col_valid = (kc >= wstart) & (kc < wstart + C_WIN_COLS)
    dcol = np.clip(kc - c + C_WIN_COLS - 1, 0, 2 * C_WIN_COLS - 2)
    off = np.arange(C_WIN_ROWS)[:, None]
    j = np.arange(C_WIN_ROWS)[None, :]
    drow = j - off + C_WIN_ROWS - 1
    b = rpb.astype(F32)[:, drow][:, :, :, dcol]
    b = jnp.where(col_valid[None, None, None], b, NEG_INF)
    b = jnp.transpose(b, (1, 0, 3, 2, 4))
    return b.reshape(C_WIN_ROWS, C_HEADS // 2, 2 * W, C_WIN_ROWS * W)


def _natten_kernel(q_ref, k_ref, v_ref, bias_ref, o_ref, *, rows):
    r = pl.program_id(1)
    r0 = jnp.clip(r - C_WIN_ROWS // 2, 0, rows - C_WIN_ROWS)
    k0 = pl.multiple_of(r0 * GRID_W, GRID_W)
    nk = C_WIN_ROWS * GRID_W
    lane = lax.broadcasted_iota(jnp.int32, (1, LANES), 1)
    lo = lane < HEAD_DIM
    n_pairs = C_HEADS // 2
    cols = [slice(pair * LANES, (pair + 1) * LANES) for pair in range(n_pairs)]
    scores = []
    for pair in range(n_pairs):
        q2 = q_ref[0, :, cols[pair]] * (HEAD_DIM ** -0.5)
        zero = jnp.zeros_like(q2)
        qs = jnp.concatenate([jnp.where(lo, q2, zero), jnp.where(lo, zero, q2)], axis=0)
        k2 = k_ref[0, pl.ds(k0, nk), cols[pair]]
        scores.append(lax.dot_general(qs, k2, _NT, preferred_element_type=F32) + bias_ref[0, pair])
    probs, dens = [], []
    for s in scores:
        p = jnp.exp(s - jnp.max(s, axis=-1, keepdims=True))
        dens.append(jnp.sum(p, axis=-1, keepdims=True))
        probs.append(p.astype(BF16))
    for pair in range(n_pairs):
        v2 = v_ref[0, pl.ds(k0, nk), cols[pair]]
        o = jnp.dot(probs[pair], v2, preferred_element_type=F32) * (1.0 / dens[pair])
        o_ref[0, :, cols[pair]] = jnp.where(lo, o[:GRID_W], o[GRID_W:]).astype(o_ref.dtype)


def _natten(u3, bias):
    Bn, S, _ = u3.shape
    rows = S // GRID_W
    assert rows >= C_WIN_ROWS

    def variant(b, r):
        return (r - jnp.clip(r - C_WIN_ROWS // 2, 0, rows - C_WIN_ROWS), 0, 0, 0)

    o = pl.pallas_call(
        functools.partial(_natten_kernel, rows=rows),
        grid=(Bn, rows),
        in_specs=[pl.BlockSpec((1, GRID_W, C_OUT), lambda b, r: (b, r, 0)),
                  pl.BlockSpec((1, S, C_OUT), lambda b, r: (b, 0, 1), pipeline_mode=pl.Buffered(1)),
                  pl.BlockSpec((1, S, C_OUT), lambda b, r: (b, 0, 2), pipeline_mode=pl.Buffered(1)),
                  pl.BlockSpec((1, C_HEADS // 2, 2 * GRID_W, C_WIN_ROWS * GRID_W), variant)],
        out_specs=pl.BlockSpec((1, GRID_W, C_OUT), lambda b, r: (b, r, 0)),
        out_shape=jax.ShapeDtypeStruct((Bn, S, C_OUT), BF16),
        compiler_params=_params("parallel", "arbitrary"),
        name="natten",
    )(u3, u3, u3, bias)
    return o.reshape(Bn * S, C_OUT)


def _moe(x2, ln_g, router, wg, wu, wd, final_g=None):
    T, D = x2.shape
    cap = CAPACITY_FACTOR * T // N_EXPERTS
    aff_t, h = _router(x2, ln_g, router)
    slot, pos = _select(aff_t, cap)
    tb = min(COMBINE_TB, T)
    starts = jnp.concatenate([pos[:, ::tb], jnp.full((N_EXPERTS, 1), cap, jnp.int32)], axis=1)
    xe = _gather(h, slot, starts, cap)
    ye = _expert_ffn(xe, cap, wg, wu, wd)
    return _combine(x2, slot.T, starts, ye, final_g)


def _trunk(x, p):
    Bn, S, D = x.shape
    T = Bn * S
    x2 = x.reshape(T, D)
    u0, *u_dil = _in_proj_even(x2, p["w_in_even"], p["ln_mix_g"][0], Bn, S)
    u3 = u0.reshape(Bn, S, EVEN_TOKEN_COLS)
    os_, ls_ = [], []
    for g, (window, dilation) in enumerate(A_PATTERNS):
        bias = _dilated_bias(p["t5_table"], g, S // dilation, window, dilation)
        qkv = u3.reshape(Bn, 1, S, EVEN_TOKEN_COLS) if g == 0 else u_dil[g - 1]
        o, l = _dilated_attention(qkv, bias, window)
        os_.append(o)
        ls_.append(l)
    a = _merge_groups(os_, ls_, S)
    q, k, v = _mla_prep(u3, p["g_qa"], p["g_kva"], p["wq"], p["wk"], p["wv"])
    b = _mla_attention(q, k, v, Bn, S)
    x2 = _matmul([a, b], [p["w_out_even"][:A_OUT], p["w_out_even"][A_OUT:]], res=x2, out_dtype=F32)
    x2 = _moe(x2, p["ln_ffn_g"][0], p["router"][0], p["w_gate"][0], p["w_up"][0], p["w_down"][0])
    u = _matmul(x2, p["w_in_odd"], gain=p["ln_mix_g"][1], out_dtype=BF16)
    c = _natten(u.reshape(Bn, S, C_IN), p["natten_bias"])
    x2 = _matmul(c, p["w_out_odd"], res=x2, out_dtype=F32)
    x2 = _moe(x2, p["ln_ffn_g"][1], p["router"][1], p["w_gate"][1], p["w_up"][1], p["w_down"][1],
              final_g=p["final_g"])
    return x2.reshape(Bn, S, D)


def _head_tiles(w, n_heads, per_head, keep):
    K = w.shape[0]
    w = w.reshape(K, n_heads, per_head)[:, :, :keep]
    return jnp.pad(w, ((0, 0), (0, 0), (0, LANES - keep))).reshape(K, n_heads * LANES)


def _prepare(ln_mix_g, w_in_even, g_qa, w_qb, g_kva, w_kvb, t5_table, w_out_even,
             w_in_odd, rpb, w_out_odd, ln_ffn_g, router, w_gate, w_up, w_down, final_g):
    w_in = w_in_even[0]
    w_a, w_cq = w_in[:, :A_IN], w_in[:, A_IN:A_IN + B_Q_RANK]
    w_ckv = w_in[:, A_IN + B_Q_RANK:A_IN + B_Q_RANK + B_KV_RANK]
    w_kr = w_in[:, A_IN + B_Q_RANK + B_KV_RANK:]
    w_kr_tile = jnp.pad(w_kr, ((0, 0), (B_NOPE, LANES - B_NOPE - B_ROPE)))
    w_a = w_a.reshape(-1, 3, A_GROUPS, A_OUT)
    groups = [w_a[:, :, g].reshape(-1, 3 * A_OUT) for g in range(A_GROUPS)]
    w_even = jnp.concatenate([groups[0], w_ckv, w_kr_tile, w_cq] + groups[1:], axis=1)
    w_kv = w_kvb[0].reshape(B_KV_RANK, B_HEADS, B_NOPE + B_V)
    return dict(
        ln_mix_g=ln_mix_g, ln_ffn_g=ln_ffn_g, final_g=final_g, t5_table=t5_table,
        w_in_even=w_even.astype(BF16),
        g_qa=g_qa[0].reshape(1, -1).astype(F32), g_kva=g_kva[0].reshape(1, -1).astype(F32),
        wq=_head_tiles(w_qb[0], B_HEADS, B_NOPE + B_ROPE, B_NOPE + B_ROPE).astype(BF16),
        wk=_head_tiles(w_kvb[0], B_HEADS, B_NOPE + B_V, B_NOPE).astype(BF16),
        wv=_head_tiles(w_kv[:, :, B_NOPE:].reshape(B_KV_RANK, B_OUT), B_HEADS, B_V, B_V).T.astype(BF16),
        w_out_even=w_out_even[0].astype(BF16), w_in_odd=w_in_odd[0].astype(BF16),
        natten_bias=_natten_bias(rpb[0]), w_out_odd=w_out_odd[0].astype(BF16), router=router,
        w_gate=w_gate.astype(BF16), w_up=w_up.astype(BF16), w_down=w_down.astype(BF16),
    )


def kernel(x_prompt, x_sample, ln_mix_g, w_in_even, g_qa, w_qb, g_kva, w_kvb, t5_table, w_out_even,
           w_in_odd, rpb, w_out_odd, ln_ffn_g, router, w_gate, w_up, w_down, final_g):
    p = _prepare(ln_mix_g, w_in_even, g_qa, w_qb, g_kva, w_kvb, t5_table, w_out_even,
                 w_in_odd, rpb, w_out_odd, ln_ffn_g, router, w_gate, w_up, w_down, final_g)
    return _trunk(x_prompt, p), _trunk(x_sample, p)
```

```python
import functools

import jax
import jax.numpy as jnp
import numpy as np
from jax import lax
from jax.experimental import pallas as pl
from jax.experimental.pallas import tpu as pltpu

D_MODEL = 1024
RMS_EPS = 1e-6
HEAD_DIM = 64
NEG_INF = -1e30
A_PATTERNS = ((128, 1), (512, 4), (2048, 16))
A_GROUPS = 3
A_HEADS_PER_GROUP = 4
A_HEADS = A_GROUPS * A_HEADS_PER_GROUP
A_IN = 3 * A_HEADS * HEAD_DIM
A_OUT = A_HEADS_PER_GROUP * HEAD_DIM
T5_BUCKETS = 32
T5_MAX_DISTANCE = 1024
B_HEADS = 12
B_Q_RANK = 384
B_KV_RANK = 256
B_NOPE = 64
B_ROPE = 32
B_V = 64
B_OUT = B_HEADS * B_V
ROPE_THETA = 10000.0
GRID_W = 64
C_HEADS = 16
C_WIN_ROWS = 8
C_WIN_COLS = 16
C_IN = 3 * C_HEADS * HEAD_DIM
C_OUT = C_HEADS * HEAD_DIM
N_EXPERTS = 16
CAPACITY_FACTOR = 2

LANES = 128
VMEM_LIMIT_BYTES = 48 * 1024 * 1024

F32 = jnp.float32
BF16 = jnp.bfloat16
_NT = (((1,), (1,)), ((), ()))
LOG2_E = 1.4426950408889634


def _params(*sem):
    return pltpu.CompilerParams(dimension_semantics=sem, vmem_limit_bytes=VMEM_LIMIT_BYTES)


def _rms(x, g):
    xf = x.astype(F32)
    return xf * lax.rsqrt(jnp.mean(xf * xf, axis=-1, keepdims=True) + RMS_EPS) * g


def _mm_kernel(*refs, n_in, has_gain, has_res, precise):
    xs, ws = refs[:n_in], refs[n_in:2 * n_in]
    pos = 2 * n_in
    g_ref = res_ref = None
    if has_gain:
        g_ref = refs[pos]
        pos += 1
    if has_res:
        res_ref = refs[pos]
        pos += 1
    o_ref = refs[pos]
    acc = None
    for x_ref, w_ref in zip(xs, ws):
        x = x_ref[...]
        if has_gain:
            x = _rms(x, g_ref[...])
        if precise:
            part = jnp.dot(x.astype(F32), w_ref[...], preferred_element_type=F32,
                           precision=lax.Precision.HIGHEST)
        else:
            part = jnp.dot(x.astype(BF16), w_ref[...], preferred_element_type=F32)
        acc = part if acc is None else acc + part
    if has_res:
        acc = acc + res_ref[...]
    o_ref[...] = acc.astype(o_ref.dtype)


def _matmul(xs, ws, gain=None, res=None, out_dtype=F32, tm=512, precise=False):
    if not isinstance(xs, (list, tuple)):
        xs, ws = [xs], [ws]
    M = xs[0].shape[0]
    N = ws[0].shape[1]
    assert M % tm == 0
    in_specs = [pl.BlockSpec((tm, x.shape[1]), lambda i: (i, 0)) for x in xs]
    in_specs += [pl.BlockSpec(w.shape, lambda i: (0, 0)) for w in ws]
    args = list(xs) + list(ws)
    if gain is not None:
        assert len(xs) == 1
        in_specs.append(pl.BlockSpec((1, gain.shape[-1]), lambda i: (0, 0)))
        args.append(gain.reshape(1, -1).astype(F32))
    if res is not None:
        in_specs.append(pl.BlockSpec((tm, N), lambda i: (i, 0)))
        args.append(res)
    return pl.pallas_call(
        functools.partial(_mm_kernel, n_in=len(xs), has_gain=gain is not None,
                          has_res=res is not None, precise=precise),
        grid=(M // tm,),
        in_specs=in_specs,
        out_specs=pl.BlockSpec((tm, N), lambda i: (i, 0)),
        out_shape=jax.ShapeDtypeStruct((M, N), out_dtype),
        compiler_params=_params("parallel"),
        name="fused_matmul",
    )(*args)


def _ffn_kernel(x_ref, wg_ref, wu_ref, wd_ref, o_ref):
    D = wg_ref.shape[1]
    x = x_ref[0, :, :D]
    lane = lax.broadcasted_iota(jnp.int32, (1, LANES), 1)
    mine = lane // GATE_TERMS == pl.program_id(0)
    gate = jnp.sum(jnp.where(mine, x_ref[0, :, D:].astype(F32), 0.0), axis=1, keepdims=True)
    g = jnp.dot(x, wg_ref[0], preferred_element_type=F32)
    u = jnp.dot(x, wu_ref[0], preferred_element_type=F32)
    h = (g * jax.nn.sigmoid(g)) * u
    y = jnp.dot(h.astype(BF16), wd_ref[0], preferred_element_type=F32)
    o_ref[0] = (y * gate).astype(o_ref.dtype)


def _expert_ffn(xe, cap, wg, wu, wd, tm=512):
    E = xe.shape[0]
    D, FF = wg.shape[1], wg.shape[2]
    assert cap % tm == 0
    return pl.pallas_call(
        _ffn_kernel,
        grid=(E, cap // tm),
        in_specs=[pl.BlockSpec((1, tm, D + LANES), lambda e, i: (e, i, 0)),
                  pl.BlockSpec((1, D, FF), lambda e, i: (e, 0, 0)),
                  pl.BlockSpec((1, D, FF), lambda e, i: (e, 0, 0)),
                  pl.BlockSpec((1, FF, D), lambda e, i: (e, 0, 0))],
        out_specs=pl.BlockSpec((1, tm, D), lambda e, i: (e, i, 0)),
        out_shape=jax.ShapeDtypeStruct((E, cap, D), BF16),
        compiler_params=_params("parallel", "parallel"),
        name="expert_ffn",
    )(xe, wg, wu, wd)


CHUNK = 256
COMBINE_TB = 512
COMBINE_W = 128


GATE_TERMS = 3


def _router_kernel(x_ref, g_ref, rt_ref, aff_ref, h_ref):
    D = x_ref.shape[1]
    E = rt_ref.shape[0]
    xn = _rms(x_ref[...], g_ref[...])
    logits = lax.dot_general(rt_ref[...], xn, _NT, preferred_element_type=F32,
                             precision=lax.Precision.HIGHEST)
    e = jnp.exp(logits - jnp.max(logits, axis=0, keepdims=True))
    aff = e / jnp.sum(e, axis=0, keepdims=True)
    aff_ref[...] = aff
    lane = lax.broadcasted_iota(jnp.int32, (E, LANES), 1)
    expert = lax.broadcasted_iota(jnp.int32, (E, LANES), 0)
    rest = aff
    gates = jnp.zeros((x_ref.shape[0], LANES), F32)
    for j in range(GATE_TERMS):
        term = rest.astype(BF16)
        rest = rest - term.astype(F32)
        place = (lane == GATE_TERMS * expert + j).astype(BF16)
        gates = gates + lax.dot_general(term, place, (((0,), (0,)), ((), ())), preferred_element_type=F32)
    h_ref[:, :D] = xn.astype(h_ref.dtype)
    h_ref[:, D:] = gates.astype(h_ref.dtype)


def _router(x2, ln_g, router, tm=512):
    T, D = x2.shape
    E = router.shape[1]
    assert GATE_TERMS * E <= LANES
    return pl.pallas_call(
        _router_kernel,
        grid=(T // tm,),
        in_specs=[pl.BlockSpec((tm, D), lambda i: (i, 0)),
                  pl.BlockSpec((1, D), lambda i: (0, 0)),
                  pl.BlockSpec((E, D), lambda i: (0, 0))],
        out_specs=[pl.BlockSpec((E, tm), lambda i: (0, i)),
                   pl.BlockSpec((tm, D + LANES), lambda i: (i, 0))],
        out_shape=[jax.ShapeDtypeStruct((E, T), F32), jax.ShapeDtypeStruct((T, D + LANES), BF16)],
        compiler_params=_params("parallel"),
        name="router",
    )(x2, ln_g.reshape(1, D).astype(F32), router.T.astype(F32))


def _cumsum_excl(mask, upper, lower_strict):
    incl = jnp.dot(mask.astype(BF16), upper, preferred_element_type=F32)
    before = jnp.dot(lower_strict, incl.astype(BF16), preferred_element_type=F32)
    return incl + before[:, CHUNK - 1:CHUNK] - mask


def _select_kernel(aff_ref, slot_ref, pos_ref, *, cap):
    aff = aff_ref[0]
    nc = aff.shape[0]
    bits = pltpu.bitcast(aff, jnp.int32)

    def count(cond):
        c = jnp.sum(jnp.where(cond, 1.0, 0.0), axis=0, keepdims=True)
        return jnp.sum(c, axis=1, keepdims=True)

    def bit_step(i, prefix):
        cand = prefix | (jnp.int32(1) << (30 - i))
        return jnp.where(count(bits >= cand) >= cap, cand, prefix)

    thr = lax.fori_loop(0, 31, bit_step, jnp.zeros((1, 1), jnp.int32))
    r = lax.broadcasted_iota(jnp.int32, (CHUNK, CHUNK), 0)
    c = lax.broadcasted_iota(jnp.int32, (CHUNK, CHUNK), 1)
    upper = (r <= c).astype(BF16)
    r = lax.broadcasted_iota(jnp.int32, (nc, nc), 0)
    c = lax.broadcasted_iota(jnp.int32, (nc, nc), 1)
    lower_strict = (c < r).astype(BF16)
    gt = bits > thr
    eq = jnp.where(bits == thr, 1.0, 0.0)
    need = cap - count(gt)
    sel = jnp.where(gt | ((eq > 0) & (_cumsum_excl(eq, upper, lower_strict) < need)), 1.0, 0.0)
    pos = _cumsum_excl(sel, upper, lower_strict).astype(jnp.int32)
    pos_ref[0] = pos
    slot_ref[0] = jnp.where(sel > 0, pos, -1)


def _select(aff_t, cap):
    E, T = aff_t.shape
    nc = T // CHUNK
    assert nc <= CHUNK and nc % 8 == 0
    spec = pl.BlockSpec((1, nc, CHUNK), lambda e: (e, 0, 0))
    slot, pos = pl.pallas_call(
        functools.partial(_select_kernel, cap=cap),
        grid=(E,), in_specs=[spec], out_specs=[spec, spec],
        out_shape=[jax.ShapeDtypeStruct((E, nc, CHUNK), jnp.int32)] * 2,
        compiler_params=_params("parallel"),
        name="select_topk",
    )(aff_t.reshape(E, nc, CHUNK))
    return slot.reshape(E, T), pos.reshape(E, T)


GATHER_STEP = 128
GATHER_W = GATHER_STEP + 16
GATHER_STACK = 8


def _gather_kernel(starts_ref, h_ref, slot_ref, xe_ref, stage_ref, extra_ref, tail_ref, sem, extra_sem, *, cap):
    b = pl.program_id(0)
    nb = pl.num_programs(0)
    E = slot_ref.shape[0]
    buf = b % 2
    h = h_ref[...]
    row = lax.broadcasted_iota(jnp.int32, (GATHER_W, 1), 0)

    @pl.when(b == 0)
    def _():
        tail_ref[...] = jnp.zeros_like(tail_ref)

    def first_row(e, blk):
        return pl.multiple_of((starts_ref[e, blk] // 16) * 16, 16)

    def copy(e, blk):
        return pltpu.make_async_copy(stage_ref.at[blk % 2, e], xe_ref.at[e, pl.ds(first_row(e, blk), GATHER_W)],
                                     sem.at[blk % 2, e])

    def onehot(e, k):
        return (slot_ref[pl.ds(e, 1), :] - (first_row(e, b) + k * GATHER_STEP) == row).astype(BF16)

    for e0 in range(0, E, GATHER_STACK):
        stacked = jnp.concatenate([onehot(e, 0) for e in range(e0, e0 + GATHER_STACK)], axis=0)
        rows = jnp.dot(stacked, h, preferred_element_type=F32).astype(BF16)
        for i in range(GATHER_STACK):
            stage_ref[buf, e0 + i] = rows[i * GATHER_W:(i + 1) * GATHER_W]

    def per_expert(e, carry):
        base = first_row(e, b)
        end = starts_ref[e, b + 1] - base
        group = pl.multiple_of((end // 16) * 16, 16)
        last = jnp.maximum(group - 1, 0) // GATHER_STEP

        def window(k):
            return jnp.dot(onehot(e, k), h, preferred_element_type=F32).astype(BF16)

        stage_ref[buf, e, pl.ds(0, 16), :] += tail_ref[e]

        @pl.when(last == 0)
        def _():
            tail_ref[e] = stage_ref[buf, e, pl.ds(group, 16), :]

        @pl.when(b > 0)
        def _():
            copy(e, b - 1).wait()

        copy(e, b).start()

        def extra(k, inner):
            extra_ref[...] = window(k)

            @pl.when(k == last)
            def _():
                tail_ref[e] = extra_ref[pl.ds(pl.multiple_of(group - k * GATHER_STEP, 16), 16), :]

            cp = pltpu.make_async_copy(extra_ref, xe_ref.at[e, pl.ds(base + k * GATHER_STEP, GATHER_W)],
                                       extra_sem.at[0])
            cp.start()
            cp.wait()
            return inner

        return lax.fori_loop(1, last + 1, extra, carry)

    lax.fori_loop(0, E, per_expert, 0)

    @pl.when(b == nb - 1)
    def _():
        extra_ref[...] = jnp.zeros_like(extra_ref)

        def drain(e, carry):
            copy(e, b).wait()
            cp = pltpu.make_async_copy(extra_ref, xe_ref.at[e, pl.ds(cap, GATHER_W)], extra_sem.at[0])
            cp.start()
            cp.wait()
            return carry
        lax.fori_loop(0, E, drain, 0)


def _gather(h, slot, starts, cap):
    T, D = h.shape
    E = slot.shape[0]
    tb = min(COMBINE_TB, T)
    rows = cap + GATHER_W
    return pl.pallas_call(
        functools.partial(_gather_kernel, cap=cap),
        grid_spec=pltpu.PrefetchScalarGridSpec(
            num_scalar_prefetch=1,
            grid=(T // tb,),
            in_specs=[pl.BlockSpec((tb, D), lambda i, st: (i, 0)),
                      pl.BlockSpec((E, tb), lambda i, st: (0, i))],
            out_specs=pl.BlockSpec(memory_space=pl.ANY),
            scratch_shapes=[pltpu.VMEM((2, E, GATHER_W, D), BF16),
                            pltpu.VMEM((GATHER_W, D), BF16),
                            pltpu.VMEM((E, 16, D), BF16),
                            pltpu.SemaphoreType.DMA((2, E)),
                            pltpu.SemaphoreType.DMA((1,))]),
        out_shape=jax.ShapeDtypeStruct((E, rows, D), BF16),
        compiler_params=_params("arbitrary"),
        name="moe_gather",
    )(starts, h, slot)


def _combine_kernel(starts_ref, x_ref, slot_ref, first_ref, g_ref, ye_ref, o_ref, win_ref, extra_ref, sem,
                    extra_sem, *, cap, final_norm):
    b = pl.program_id(0)
    nb = pl.num_programs(0)
    E = ye_ref.shape[0]
    W = COMBINE_W

    def first_row(e, blk, k):
        return (starts_ref[e, blk] // 16) * 16 + k * W

    def window_start(e, blk, k):
        return pl.multiple_of(jnp.minimum(first_row(e, blk, k), cap - W), 16)

    def copy(e, blk):
        buf = blk % 2
        return pltpu.make_async_copy(ye_ref.at[e, pl.ds(window_start(e, blk, 0), W)],
                                     win_ref.at[buf, pl.ds(e * W, W)], sem.at[buf, e])

    @pl.when(b == 0)
    def _():
        for e in range(E):
            copy(e, b).start()

    @pl.when(b + 1 < nb)
    def _():
        for e in range(E):
            copy(e, b + 1).start()

    slots = slot_ref[...]
    first = first_ref[0]
    rel = slots - jnp.minimum(first, cap - W)
    rel = jnp.where((slots >= first) & (rel >= 0) & (rel < W), rel, -1).astype(F32).astype(BF16)
    lane = lax.broadcasted_iota(jnp.int32, (E, E * W), 1)
    expert = lax.broadcasted_iota(jnp.int32, (E, E * W), 0)
    expand = (lane // W == expert).astype(BF16)
    wanted = (lax.broadcasted_iota(jnp.int32, (1, E * W), 1) % W).astype(F32)
    onehot = (jnp.dot(rel, expand, preferred_element_type=F32) == wanted).astype(BF16)

    for e in range(E):
        copy(e, b).wait()
    o_ref[...] = x_ref[...] + jnp.dot(onehot, win_ref[b % 2], preferred_element_type=F32)

    col = lax.broadcasted_iota(jnp.int32, (1, W), 1)
    expert_lane = lax.broadcasted_iota(jnp.int32, (1, E), 1)

    def per_expert(e, carry):
        n_win = (starts_ref[e, b + 1] - first_row(e, b, 0) + W - 1) // W

        def extra(k, inner):
            cp = pltpu.make_async_copy(ye_ref.at[e, pl.ds(window_start(e, b, k), W)], extra_ref, extra_sem.at[0])
            cp.start()
            cp.wait()
            slot = jnp.max(jnp.where(expert_lane == e, slots, -1), axis=1, keepdims=True)
            oh = ((slot == window_start(e, b, k) + col) & (slot >= first_row(e, b, k))).astype(BF16)
            o_ref[...] += jnp.dot(oh, extra_ref[...], preferred_element_type=F32)
            return inner

        return lax.fori_loop(1, n_win, extra, carry)

    lax.fori_loop(0, E, per_expert, 0)
    if final_norm:
        o_ref[...] = _rms(o_ref[...], g_ref[...])


def _combine(x2, slot_t, starts, ye, final_g=None):
    T, D = x2.shape
    gain = jnp.ones((1, D), F32) if final_g is None else final_g.reshape(1, D).astype(F32)
    E, cap, _ = ye.shape
    tb = min(COMBINE_TB, T)
    nb = T // tb
    assert cap >= COMBINE_W and cap % 16 == 0
    first = ((starts[:, :nb] // 16) * 16).T.reshape(nb, 1, E)
    return pl.pallas_call(
        functools.partial(_combine_kernel, cap=cap, final_norm=final_g is not None),
        grid_spec=pltpu.PrefetchScalarGridSpec(
            num_scalar_prefetch=1,
            grid=(nb,),
            in_specs=[pl.BlockSpec((tb, D), lambda i, st: (i, 0)),
                      pl.BlockSpec((tb, E), lambda i, st: (i, 0)),
                      pl.BlockSpec((1, 1, E), lambda i, st: (i, 0, 0)),
                      pl.BlockSpec((1, D), lambda i, st: (0, 0)),
                      pl.BlockSpec(memory_space=pl.ANY)],
            out_specs=pl.BlockSpec((tb, D), lambda i, st: (i, 0)),
            scratch_shapes=[pltpu.VMEM((2, E * COMBINE_W, D), BF16),
                            pltpu.VMEM((COMBINE_W, D), BF16),
                            pltpu.SemaphoreType.DMA((2, E)),
                            pltpu.SemaphoreType.DMA((1,))]),
        out_shape=jax.ShapeDtypeStruct((T, D), F32),
        compiler_params=_params("arbitrary"),
        name="moe_combine",
    )(starts, x2, slot_t, first, gain, ye)


def _t5_bucket(rel):
    half = T5_BUCKETS // 2
    max_exact = half // 2
    n = np.abs(rel)
    large = max_exact + (np.log(np.maximum(n, 1) / max_exact) / np.log(T5_MAX_DISTANCE / max_exact)
                         * (half - max_exact)).astype(np.int32)
    large = np.minimum(large, half - 1)
    return (np.where(rel > 0, half, 0) + np.where(n < max_exact, n, large)).astype(np.int32)


def _dilated_tiles(L, radius):
    tq = min(LANES, L)
    kw = min(tq + 2 * radius, L)
    return tq, kw


def _dilated_bias(t5_table, g, L, window, dilation):
    radius = window // (2 * dilation)
    tq, kw = _dilated_tiles(L, radius)
    deltas = (0, -((kw - tq) // 2), tq - kw)
    period = 2 * kw - 1
    rel = np.arange(period + 1) - (kw - 1)
    tab = t5_table[:, g * A_HEADS_PER_GROUP:(g + 1) * A_HEADS_PER_GROUP].astype(F32)
    by_rel = jnp.where((np.abs(rel) <= radius)[None], tab[_t5_bucket(rel * dilation)].T, NEG_INF)
    skew = jnp.tile(by_rel, (1, tq))[:, :tq * period].reshape(A_HEADS_PER_GROUP, tq, period)
    bias = jnp.stack([skew[:, :, d + kw - 1:d + 2 * kw - 1] for d in deltas])
    return bias.reshape(3, A_HEADS_PER_GROUP // 2, 2 * tq, kw)


def _dilated_kernel(q_ref, k_ref, v_ref, bias_ref, o_ref, lse_ref, *, L, tq, kw):
    nq = L // tq
    lane = lax.broadcasted_iota(jnp.int32, (1, LANES), 1)
    lo = lane < HEAD_DIM

    def body(i, carry):
        q0 = pl.multiple_of(i * tq, tq)
        start = jnp.clip(q0 - (kw - tq) // 2, 0, L - kw)
        start = pl.multiple_of(start, 16)
        var = jnp.where(i == 0, 0, jnp.where(i == nq - 1, 2, 1))
        n_pairs = A_HEADS_PER_GROUP // 2
        cols = [slice(pair * LANES, (pair + 1) * LANES) for pair in range(n_pairs)]
        scores = []
        for pair in range(n_pairs):
            q2 = q_ref[0, 0, pl.ds(q0, tq), cols[pair]] * (HEAD_DIM ** -0.5)
            zero = jnp.zeros_like(q2)
            qs = jnp.concatenate([jnp.where(lo, q2, zero), jnp.where(lo, zero, q2)], axis=0)
            k2 = k_ref[0, 0, pl.ds(start, kw), cols[pair]]
            scores.append(lax.dot_general(qs, k2, _NT, preferred_element_type=F32) + bias_ref[var, pair])
        probs, dens, maxs = [], [], []
        for s in scores:
            m = jnp.max(s, axis=-1, keepdims=True)
            p = jnp.exp(s - m)
            maxs.append(m)
            dens.append(jnp.sum(p, axis=-1, keepdims=True))
            probs.append(p.astype(BF16))
        for pair in range(n_pairs):
            v2 = v_ref[0, 0, pl.ds(start, kw), cols[pair]]
            o = jnp.dot(probs[pair], v2, preferred_element_type=F32) * (1.0 / dens[pair])
            lse = maxs[pair] + jnp.log(dens[pair])
            o_ref[0, 0, pl.ds(q0, tq), cols[pair]] = jnp.where(lo, o[:tq], o[tq:])
            lse_ref[0, 0, pl.ds(q0, tq), cols[pair]] = jnp.where(lo, lse[:tq], lse[tq:])
        return carry

    lax.fori_loop(0, nq, body, 0)


def _dilated_attention(qkv, bias, window):
    Bn, dilation, L, _ = qkv.shape
    radius = window // (2 * dilation)
    tq, kw = _dilated_tiles(L, radius)
    assert L % tq == 0 and kw % 16 == 0 and tq % 16 == 0

    def spec(which):
        return pl.BlockSpec((1, 1, L, A_OUT), lambda b, r: (b, r, 0, which))

    o, lse = pl.pallas_call(
        functools.partial(_dilated_kernel, L=L, tq=tq, kw=kw),
        grid=(Bn, dilation),
        in_specs=[spec(0), spec(1), spec(2), pl.BlockSpec(bias.shape, lambda b, r: (0, 0, 0, 0))],
        out_specs=[spec(0), spec(0)],
        out_shape=[jax.ShapeDtypeStruct((Bn, dilation, L, A_OUT), F32)] * 2,
        compiler_params=_params("parallel", "parallel"),
        name="dilated_attention",
    )(qkv, qkv, qkv, bias)
    return o, lse


def _merge_kernel(*refs):
    n = len(A_PATTERNS)
    o_refs, l_refs, out_ref = refs[:n], refs[n:2 * n], refs[2 * n]
    scratch = refs[2 * n + 1:]
    tm = out_ref.shape[0]

    def token_order(ref, scr, dilation):
        if dilation == 1:
            return ref[0, 0]
        tiles = A_OUT // LANES
        for r in range(dilation):
            for c in range(tiles):
                scr[c, pl.ds(r, tm // dilation, stride=dilation), :] = ref[0, r, :, c * LANES:(c + 1) * LANES]
        return jnp.concatenate([scr[c] for c in range(tiles)], axis=1)

    os_ = [token_order(o_refs[g], scratch[2 * g], d) for g, (_, d) in enumerate(A_PATTERNS)]
    ls = [token_order(l_refs[g], scratch[2 * g + 1], d) for g, (_, d) in enumerate(A_PATTERNS)]
    m = jnp.maximum(jnp.maximum(ls[0], ls[1]), ls[2])
    es = [jnp.exp(l - m) for l in ls]
    den = es[0] + es[1] + es[2]
    out = (es[0] / den) * os_[0] + (es[1] / den) * os_[1] + (es[2] / den) * os_[2]
    out_ref[...] = out.astype(out_ref.dtype)


def _merge_groups(os_, ls_, S, tm=512):
    Bn = os_[0].shape[0]
    ns = S // tm

    def spec(d):
        return pl.BlockSpec((1, d, tm // d, A_OUT), lambda i: (i // ns, 0, i % ns, 0))

    specs = [spec(d) for _, d in A_PATTERNS]
    return pl.pallas_call(
        _merge_kernel, grid=(Bn * ns,), in_specs=specs * 2,
        out_specs=pl.BlockSpec((tm, A_OUT), lambda i: (i, 0)),
        out_shape=jax.ShapeDtypeStruct((Bn * S, A_OUT), BF16),
        scratch_shapes=[pltpu.VMEM((A_OUT // LANES, tm, LANES), F32)] * (2 * len(A_PATTERNS)),
        compiler_params=_params("parallel"), name="merge_groups",
    )(*os_, *ls_)


EVEN_TOKEN_COLS = 3 * A_OUT + B_KV_RANK + LANES + B_Q_RANK


def _in_proj_even_kernel(x_ref, w_ref, g_ref, u0_ref, *rest):
    n_dil = len(A_PATTERNS) - 1
    dil_refs, scr = rest[:n_dil], rest[n_dil]
    tm = x_ref.shape[0]
    acc = jnp.dot(_rms(x_ref[...], g_ref[...]).astype(BF16), w_ref[...], preferred_element_type=F32)
    u0_ref[...] = acc[:, :EVEN_TOKEN_COLS].astype(u0_ref.dtype)
    for i, (_, d) in enumerate(A_PATTERNS[1:]):
        c0 = EVEN_TOKEN_COLS + i * 3 * A_OUT
        for c in range(3 * A_OUT // LANES):
            scr[c] = acc[:, c0 + c * LANES:c0 + (c + 1) * LANES]
        for r in range(d):
            for c in range(3 * A_OUT // LANES):
                piece = scr[c, pl.ds(r, tm // d, stride=d), :]
                dil_refs[i][0, r, :, c * LANES:(c + 1) * LANES] = piece.astype(dil_refs[i].dtype)


def _in_proj_even(x2, w, gain, Bn, S, tm=512):
    T, K = x2.shape
    ns = S // tm
    dils = [d for _, d in A_PATTERNS[1:]]
    assert S % tm == 0 and all(tm % (16 * d) == 0 for d in dils)
    return pl.pallas_call(
        _in_proj_even_kernel,
        grid=(T // tm,),
        in_specs=[pl.BlockSpec((tm, K), lambda i: (i, 0)),
                  pl.BlockSpec(w.shape, lambda i: (0, 0)),
                  pl.BlockSpec((1, K), lambda i: (0, 0))],
        out_specs=[pl.BlockSpec((tm, EVEN_TOKEN_COLS), lambda i: (i, 0))]
        + [pl.BlockSpec((1, d, tm // d, 3 * A_OUT), lambda i: (i // ns, 0, i % ns, 0)) for d in dils],
        out_shape=[jax.ShapeDtypeStruct((T, EVEN_TOKEN_COLS), BF16)]
        + [jax.ShapeDtypeStruct((Bn, d, S // d, 3 * A_OUT), BF16) for d in dils],
        scratch_shapes=[pltpu.VMEM((3 * A_OUT // LANES, tm, LANES), F32)],
        compiler_params=_params("parallel"),
        name="in_proj_even",
    )(x2, w, gain.reshape(1, K).astype(F32))


def _rope_tables(S):
    inv = 1.0 / (ROPE_THETA ** (np.arange(0, B_ROPE, 2, dtype=np.float32) / B_ROPE))
    ang = np.arange(S, dtype=np.float32)[:, None] * inv[None]
    cos, sin = np.cos(ang), np.sin(ang)
    half = B_ROPE // 2
    c = np.ones((S, LANES), np.float32)
    c[:, B_NOPE:B_NOPE + half] = cos
    c[:, B_NOPE + half:B_NOPE + B_ROPE] = cos
    s_lo = np.zeros((S, LANES), np.float32)
    s_lo[:, B_NOPE:B_NOPE + half] = -sin
    s_hi = np.zeros((S, LANES), np.float32)
    s_hi[:, B_NOPE + half:B_NOPE + B_ROPE] = sin
    return jnp.asarray(c), jnp.asarray(s_lo), jnp.asarray(s_hi)


def _rope(x, c, s_lo, s_hi):
    half = B_ROPE // 2
    return x * c + pltpu.roll(x, LANES - half, 1) * s_lo + pltpu.roll(x, half, 1) * s_hi


def _mla_prep_kernel(cq_ref, ckv_ref, kr_ref, gq_ref, gkv_ref, wq_ref, wk_ref, wv_ref,
                     c_ref, slo_ref, shi_ref, q_ref, k_ref, v_ref):
    c, s_lo, s_hi = c_ref[...], slo_ref[...], shi_ref[...]
    scale = (B_NOPE + B_ROPE) ** -0.5 * LOG2_E
    xq = _rms(cq_ref[...], gq_ref[...]).astype(BF16)
    q = jnp.dot(xq, wq_ref[...], preferred_element_type=F32)
    xkv = _rms(ckv_ref[...], gkv_ref[...]).astype(BF16)
    k = jnp.dot(xkv, wk_ref[...], preferred_element_type=F32)
    vt = lax.dot_general(wv_ref[...], xkv, _NT, preferred_element_type=F32)
    ones_row = lax.broadcasted_iota(jnp.int32, vt.shape, 0) % LANES == B_V
    v_ref[0, 0] = jnp.where(ones_row, 1.0, vt).astype(v_ref.dtype)
    k_rope = _rope(kr_ref[...].astype(F32), c, s_lo, s_hi)
    for h in range(B_HEADS):
        cs = slice(h * LANES, (h + 1) * LANES)
        q_ref[:, cs] = (_rope(q[:, cs], c, s_lo, s_hi) * scale).astype(q_ref.dtype)
        k_ref[:, cs] = (k[:, cs] + k_rope).astype(k_ref.dtype)


MLA_KEY_CHUNK = 1024


def _mla_prep(u3, g_qa, g_kva, wq, wk, wv):
    Bn, S, C = u3.shape
    ts = min(MLA_KEY_CHUNK, S)
    tabs = _rope_tables(S)
    ns = S // ts
    u2 = u3.reshape(Bn * S, C)
    row = lambda b, i: b * ns + i
    a_cols = 3 * A_OUT
    off_kv, off_kr, off_q = a_cols // B_KV_RANK, (a_cols + B_KV_RANK) // LANES, (a_cols + B_KV_RANK + LANES) // B_Q_RANK
    full = lambda a: pl.BlockSpec(a.shape, lambda b, i: (0, 0))
    tab_spec = pl.BlockSpec((ts, LANES), lambda b, i: (i, 0))
    hw = B_HEADS * LANES
    return pl.pallas_call(
        _mla_prep_kernel,
        grid=(Bn, ns),
        in_specs=[pl.BlockSpec((ts, B_Q_RANK), lambda b, i: (row(b, i), off_q)),
                  pl.BlockSpec((ts, B_KV_RANK), lambda b, i: (row(b, i), off_kv)),
                  pl.BlockSpec((ts, LANES), lambda b, i: (row(b, i), off_kr)),
                  full(g_qa), full(g_kva), full(wq), full(wk), full(wv),
                  tab_spec, tab_spec, tab_spec],
        out_specs=[pl.BlockSpec((ts, hw), lambda b, i: (row(b, i), 0)),
                   pl.BlockSpec((ts, hw), lambda b, i: (row(b, i), 0)),
                   pl.BlockSpec((1, 1, hw, ts), lambda b, i: (b, i, 0, 0))],
        out_shape=[jax.ShapeDtypeStruct((Bn * S, hw), BF16),
                   jax.ShapeDtypeStruct((Bn * S, hw), BF16),
                   jax.ShapeDtypeStruct((Bn, ns, hw, ts), BF16)],
        compiler_params=_params("parallel", "parallel"),
        name="mla_prep",
    )(u2, u2, u2, g_qa, g_kva, wq, wk, wv, *tabs)


def _mla_kernel(q_ref, k_ref, v_ref, o_ref):
    nk, _, tk = v_ref.shape[1:]
    tq = q_ref.shape[1]
    lane = lax.broadcasted_iota(jnp.int32, (1, LANES), 1)
    qs = [q_ref[0, :, sub * LANES:(sub + 1) * LANES] for sub in range(2)]

    def body(j, carry):
        k0 = pl.multiple_of(j * tk, tk)
        scores = [lax.dot_general(k_ref[0, pl.ds(k0, tk), sub * LANES:(sub + 1) * LANES], qs[sub], _NT,
                                  preferred_element_type=F32) for sub in range(2)]
        new = []
        for sub in range(2):
            m, acc = carry[sub]
            s = scores[sub]
            m_new = jnp.maximum(m, jnp.max(s, axis=0, keepdims=True))
            p = jnp.exp2(s - m_new).astype(BF16)
            vt = v_ref[0, j, sub * LANES:(sub + 1) * LANES, :]
            acc = jnp.exp2(m - m_new) * acc + jnp.dot(vt, p, preferred_element_type=F32)
            new.append((m_new, acc))
        return tuple(new)

    init = tuple((jnp.full((1, tq), NEG_INF, F32), jnp.zeros((LANES, tq), F32)) for _ in range(2))
    res = lax.fori_loop(0, nk, body, init, unroll=True)
    outs = [(acc * (1.0 / acc[B_V:B_V + 1])).T for (_, acc) in res]
    o_ref[0] = jnp.where(lane < B_V, outs[0], pltpu.roll(outs[1], B_V, 1)).astype(o_ref.dtype)


def _mla_attention(q, k, vt, Bn, S, tq=1024):
    tq = min(tq, S)
    nk, _, tk = vt.shape[1:]
    hw = B_HEADS * LANES
    q3, k3 = q.reshape(Bn, S, hw), k.reshape(Bn, S, hw)
    o = pl.pallas_call(
        _mla_kernel,
        grid=(Bn, B_HEADS // 2, S // tq),
        in_specs=[pl.BlockSpec((1, tq, 2 * LANES), lambda b, h, i: (b, i, h)),
                  pl.BlockSpec((1, S, 2 * LANES), lambda b, h, i: (b, 0, h)),
                  pl.BlockSpec((1, nk, 2 * LANES, tk), lambda b, h, i: (b, 0, h, 0))],
        out_specs=pl.BlockSpec((1, tq, LANES), lambda b, h, i: (b, i, h)),
        out_shape=jax.ShapeDtypeStruct((Bn, S, B_OUT), BF16),
        compiler_params=_params("parallel", "parallel", "parallel"),
        name="mla_attention",
    )(q3, k3, vt)
    return o.reshape(Bn * S, B_OUT)


def _natten_bias(rpb):
    W = GRID_W
    c = np.arange(W)[:, None]
    kc = np.arange(W)[None, :]
    wstart = np.clip(c - C_WIN_COLS // 2, 0, W - C_WIN_COLS)
    col_valid = (kc >= wstart) & (kc < wstart + C_WIN_COLS)
    dcol = np.clip(kc - c + C_WIN_COLS - 1, 0, 2 * C_WIN_COLS - 2)
    off = np.arange(C_WIN_ROWS)[:, None]
    j = np.arange(C_WIN_ROWS)[None, :]
    drow = j - off + C_WIN_ROWS - 1
    b = rpb.astype(F32)[:, drow][:, :, :, dcol]
    b = jnp.where(col_valid[None, None, None], b, NEG_INF)
    b = jnp.transpose(b, (1, 0, 3, 2, 4))
    return b.reshape(C_WIN_ROWS, C_HEADS // 2, 2 * W, C_WIN_ROWS * W)


NATTEN_ROWS = 4


def _natten_kernel(q_ref, k_ref, v_ref, *rest, rows):
    bias_refs, o_ref = rest[:NATTEN_ROWS], rest[NATTEN_ROWS]
    nk = C_WIN_ROWS * GRID_W
    lane = lax.broadcasted_iota(jnp.int32, (1, LANES), 1)
    lo = lane < HEAD_DIM
    n_pairs = C_HEADS // 2
    cols = [slice(pair * LANES, (pair + 1) * LANES) for pair in range(n_pairs)]
    for j in range(NATTEN_ROWS):
        r = pl.program_id(1) * NATTEN_ROWS + j
        r0 = jnp.clip(r - C_WIN_ROWS // 2, 0, rows - C_WIN_ROWS)
        k0 = pl.multiple_of(r0 * GRID_W, GRID_W)
        q_rows = slice(j * GRID_W, (j + 1) * GRID_W)
        scores = []
        for pair in range(n_pairs):
            q2 = q_ref[0, q_rows, cols[pair]] * (HEAD_DIM ** -0.5)
            zero = jnp.zeros_like(q2)
            qs = jnp.concatenate([jnp.where(lo, q2, zero), jnp.where(lo, zero, q2)], axis=0)
            k2 = k_ref[0, pl.ds(k0, nk), cols[pair]]
            scores.append(lax.dot_general(qs, k2, _NT, preferred_element_type=F32) + bias_refs[j][0, pair])
        probs, dens = [], []
        for s in scores:
            p = jnp.exp(s - jnp.max(s, axis=-1, keepdims=True))
            dens.append(jnp.sum(p, axis=-1, keepdims=True))
            probs.append(p.astype(BF16))
        for pair in range(n_pairs):
            v2 = v_ref[0, pl.ds(k0, nk), cols[pair]]
            o = jnp.dot(probs[pair], v2, preferred_element_type=F32) * (1.0 / dens[pair])
            o_ref[0, q_rows, cols[pair]] = jnp.where(lo, o[:GRID_W], o[GRID_W:]).astype(o_ref.dtype)


def _natten(u3, bias):
    Bn, S, _ = u3.shape
    rows = S // GRID_W
    assert rows >= C_WIN_ROWS and rows % NATTEN_ROWS == 0

    def bias_spec(j):
        def variant(b, i):
            r = i * NATTEN_ROWS + j
            return (r - jnp.clip(r - C_WIN_ROWS // 2, 0, rows - C_WIN_ROWS), 0, 0, 0)
        return pl.BlockSpec((1, C_HEADS // 2, 2 * GRID_W, C_WIN_ROWS * GRID_W), variant)

    q_spec = pl.BlockSpec((1, NATTEN_ROWS * GRID_W, C_OUT), lambda b, i: (b, i, 0))
    o = pl.pallas_call(
        functools.partial(_natten_kernel, rows=rows),
        grid=(Bn, rows // NATTEN_ROWS),
        in_specs=[q_spec,
                  pl.BlockSpec((1, S, C_OUT), lambda b, i: (b, 0, 1), pipeline_mode=pl.Buffered(1)),
                  pl.BlockSpec((1, S, C_OUT), lambda b, i: (b, 0, 2), pipeline_mode=pl.Buffered(1))]
        + [bias_spec(j) for j in range(NATTEN_ROWS)],
        out_specs=q_spec,
        out_shape=jax.ShapeDtypeStruct((Bn, S, C_OUT), BF16),
        compiler_params=_params("parallel", "arbitrary"),
        name="natten",
    )(u3, u3, u3, *([bias] * NATTEN_ROWS))
    return o.reshape(Bn * S, C_OUT)


def _moe(x2, ln_g, router, wg, wu, wd, final_g=None):
    T, D = x2.shape
    cap = CAPACITY_FACTOR * T // N_EXPERTS
    aff_t, h = _router(x2, ln_g, router)
    slot, pos = _select(aff_t, cap)
    tb = min(COMBINE_TB, T)
    starts = jnp.concatenate([pos[:, ::tb], jnp.full((N_EXPERTS, 1), cap, jnp.int32)], axis=1)
    xe = _gather(h, slot, starts, cap)
    ye = _expert_ffn(xe, cap, wg, wu, wd)
    return _combine(x2, slot.T, starts, ye, final_g)


def _trunk(x, p):
    Bn, S, D = x.shape
    T = Bn * S
    x2 = x.reshape(T, D)
    u0, *u_dil = _in_proj_even(x2, p["w_in_even"], p["ln_mix_g"][0], Bn, S)
    u3 = u0.reshape(Bn, S, EVEN_TOKEN_COLS)
    os_, ls_ = [], []
    for g, (window, dilation) in enumerate(A_PATTERNS):
        bias = _dilated_bias(p["t5_table"], g, S // dilation, window, dilation)
        qkv = u3.reshape(Bn, 1, S, EVEN_TOKEN_COLS) if g == 0 else u_dil[g - 1]
        o, l = _dilated_attention(qkv, bias, window)
        os_.append(o)
        ls_.append(l)
    a = _merge_groups(os_, ls_, S)
    q, k, v = _mla_prep(u3, p["g_qa"], p["g_kva"], p["wq"], p["wk"], p["wv"])
    b = _mla_attention(q, k, v, Bn, S)
    x2 = _matmul([a, b], [p["w_out_even"][:A_OUT], p["w_out_even"][A_OUT:]], res=x2, out_dtype=F32)
    x2 = _moe(x2, p["ln_ffn_g"][0], p["router"][0], p["w_gate"][0], p["w_up"][0], p["w_down"][0])
    u = _matmul(x2, p["w_in_odd"], gain=p["ln_mix_g"][1], out_dtype=BF16)
    c = _natten(u.reshape(Bn, S, C_IN), p["natten_bias"])
    x2 = _matmul(c, p["w_out_odd"], res=x2, out_dtype=F32)
    x2 = _moe(x2, p["ln_ffn_g"][1], p["router"][1], p["w_gate"][1], p["w_up"][1], p["w_down"][1],
              final_g=p["final_g"])
    return x2.reshape(Bn, S, D)


def _head_tiles(w, n_heads, per_head, keep):
    K = w.shape[0]
    w = w.reshape(K, n_heads, per_head)[:, :, :keep]
    return jnp.pad(w, ((0, 0), (0, 0), (0, LANES - keep))).reshape(K, n_heads * LANES)


def _prepare(ln_mix_g, w_in_even, g_qa, w_qb, g_kva, w_kvb, t5_table, w_out_even,
             w_in_odd, rpb, w_out_odd, ln_ffn_g, router, w_gate, w_up, w_down, final_g):
    w_in = w_in_even[0]
    w_a, w_cq = w_in[:, :A_IN], w_in[:, A_IN:A_IN + B_Q_RANK]
    w_ckv = w_in[:, A_IN + B_Q_RANK:A_IN + B_Q_RANK + B_KV_RANK]
    w_kr = w_in[:, A_IN + B_Q_RANK + B_KV_RANK:]
    w_kr_tile = jnp.pad(w_kr, ((0, 0), (B_NOPE, LANES - B_NOPE - B_ROPE)))
    w_a = w_a.reshape(-1, 3, A_GROUPS, A_OUT)
    groups = [w_a[:, :, g].reshape(-1, 3 * A_OUT) for g in range(A_GROUPS)]
    w_even = jnp.concatenate([groups[0], w_ckv, w_kr_tile, w_cq] + groups[1:], axis=1)
    w_kv = w_kvb[0].reshape(B_KV_RANK, B_HEADS, B_NOPE + B_V)
    return dict(
        ln_mix_g=ln_mix_g, ln_ffn_g=ln_ffn_g, final_g=final_g, t5_table=t5_table,
        w_in_even=w_even.astype(BF16),
        g_qa=g_qa[0].reshape(1, -1).astype(F32), g_kva=g_kva[0].reshape(1, -1).astype(F32),
        wq=_head_tiles(w_qb[0], B_HEADS, B_NOPE + B_ROPE, B_NOPE + B_ROPE).astype(BF16),
        wk=_head_tiles(w_kvb[0], B_HEADS, B_NOPE + B_V, B_NOPE).astype(BF16),
        wv=_head_tiles(w_kv[:, :, B_NOPE:].reshape(B_KV_RANK, B_OUT), B_HEADS, B_V, B_V).T.astype(BF16),
        w_out_even=w_out_even[0].astype(BF16), w_in_odd=w_in_odd[0].astype(BF16),
        natten_bias=_natten_bias(rpb[0]), w_out_odd=w_out_odd[0].astype(BF16), router=router,
        w_gate=w_gate.astype(BF16), w_up=w_up.astype(BF16), w_down=w_down.astype(BF16),
    )


def kernel(x_prompt, x_sample, ln_mix_g, w_in_even, g_qa, w_qb, g_kva, w_kvb, t5_table, w_out_even,
           w_in_odd, rpb, w_out_odd, ln_ffn_g, router, w_gate, w_up, w_down, final_g):
    p = _prepare(ln_mix_g, w_in_even, g_qa, w_qb, g_kva, w_kvb, t5_table, w_out_even,
                 w_in_odd, rpb, w_out_odd, ln_ffn_g, router, w_gate, w_up, w_down, final_g)
    return _trunk(x_prompt, p), _trunk(x_sample, p)
```

```python
import functools

import jax
import jax.numpy as jnp
import numpy as np
from jax import lax
from jax.experimental import pallas as pl
from jax.experimental.pallas import tpu as pltpu

D_MODEL = 1024
RMS_EPS = 1e-6
HEAD_DIM = 64
NEG_INF = -1e30
A_PATTERNS = ((128, 1), (512, 4), (2048, 16))
A_GROUPS = 3
A_HEADS_PER_GROUP = 4
A_HEADS = A_GROUPS * A_HEADS_PER_GROUP
A_IN = 3 * A_HEADS * HEAD_DIM
A_OUT = A_HEADS_PER_GROUP * HEAD_DIM
T5_BUCKETS = 32
T5_MAX_DISTANCE = 1024
B_HEADS = 12
B_Q_RANK = 384
B_KV_RANK = 256
B_NOPE = 64
B_ROPE = 32
B_V = 64
B_OUT = B_HEADS * B_V
ROPE_THETA = 10000.0
GRID_W = 64
C_HEADS = 16
C_WIN_ROWS = 8
C_WIN_COLS = 16
C_IN = 3 * C_HEADS * HEAD_DIM
C_OUT = C_HEADS * HEAD_DIM
N_EXPERTS = 16
CAPACITY_FACTOR = 2

LANES = 128
VMEM_LIMIT_BYTES = 48 * 1024 * 1024

F32 = jnp.float32
BF16 = jnp.bfloat16
_NT = (((1,), (1,)), ((), ()))
LOG2_E = 1.4426950408889634


def _params(*sem):
    return pltpu.CompilerParams(dimension_semantics=sem, vmem_limit_bytes=VMEM_LIMIT_BYTES)


def _rms(x, g):
    xf = x.astype(F32)
    return xf * lax.rsqrt(jnp.mean(xf * xf, axis=-1, keepdims=True) + RMS_EPS) * g


def _mm_kernel(*refs, n_in, has_gain, has_res, precise):
    xs, ws = refs[:n_in], refs[n_in:2 * n_in]
    pos = 2 * n_in
    g_ref = res_ref = None
    if has_gain:
        g_ref = refs[pos]
        pos += 1
    if has_res:
        res_ref = refs[pos]
        pos += 1
    o_ref = refs[pos]
    acc = None
    for x_ref, w_ref in zip(xs, ws):
        x = x_ref[...]
        if has_gain:
            x = _rms(x, g_ref[...])
        if precise:
            part = jnp.dot(x.astype(F32), w_ref[...], preferred_element_type=F32,
                           precision=lax.Precision.HIGHEST)
        else:
            part = jnp.dot(x.astype(BF16), w_ref[...], preferred_element_type=F32)
        acc = part if acc is None else acc + part
    if has_res:
        acc = acc + res_ref[...]
    o_ref[...] = acc.astype(o_ref.dtype)


def _matmul(xs, ws, gain=None, res=None, out_dtype=F32, tm=512, precise=False):
    if not isinstance(xs, (list, tuple)):
        xs, ws = [xs], [ws]
    M = xs[0].shape[0]
    N = ws[0].shape[1]
    assert M % tm == 0
    in_specs = [pl.BlockSpec((tm, x.shape[1]), lambda i: (i, 0)) for x in xs]
    in_specs += [pl.BlockSpec(w.shape, lambda i: (0, 0)) for w in ws]
    args = list(xs) + list(ws)
    if gain is not None:
        assert len(xs) == 1
        in_specs.append(pl.BlockSpec((1, gain.shape[-1]), lambda i: (0, 0)))
        args.append(gain.reshape(1, -1).astype(F32))
    if res is not None:
        in_specs.append(pl.BlockSpec((tm, N), lambda i: (i, 0)))
        args.append(res)
    return pl.pallas_call(
        functools.partial(_mm_kernel, n_in=len(xs), has_gain=gain is not None,
                          has_res=res is not None, precise=precise),
        grid=(M // tm,),
        in_specs=in_specs,
        out_specs=pl.BlockSpec((tm, N), lambda i: (i, 0)),
        out_shape=jax.ShapeDtypeStruct((M, N), out_dtype),
        compiler_params=_params("parallel"),
        name="fused_matmul",
    )(*args)


def _ffn_kernel(x_ref, wg32_ref, wu32_ref, wd32_ref, o_ref, wg_ref, wu_ref, wd_ref):
    D = wg_ref.shape[1]

    @pl.when(pl.program_id(1) == 0)
    def _():
        wg_ref[0] = wg32_ref[0].astype(BF16)
        wu_ref[0] = wu32_ref[0].astype(BF16)
        wd_ref[0] = wd32_ref[0].astype(BF16)

    x = x_ref[0, :, :D]
    lane = lax.broadcasted_iota(jnp.int32, (1, LANES), 1)
    mine = lane // GATE_TERMS == pl.program_id(0)
    gate = jnp.sum(jnp.where(mine, x_ref[0, :, D:].astype(F32), 0.0), axis=1, keepdims=True)
    g = jnp.dot(x, wg_ref[0], preferred_element_type=F32)
    u = jnp.dot(x, wu_ref[0], preferred_element_type=F32)
    h = (g * jax.nn.sigmoid(g)) * u
    y = jnp.dot(h.astype(BF16), wd_ref[0], preferred_element_type=F32)
    o_ref[0] = (y * gate).astype(o_ref.dtype)


def _expert_ffn(xe, cap, wg, wu, wd, tm=512):
    E = xe.shape[0]
    D, FF = wg.shape[1], wg.shape[2]
    assert cap % tm == 0
    return pl.pallas_call(
        _ffn_kernel,
        grid=(E, cap // tm),
        in_specs=[pl.BlockSpec((1, tm, D + LANES), lambda e, i: (e, i, 0)),
                  pl.BlockSpec((1, D, FF), lambda e, i: (e, 0, 0)),
                  pl.BlockSpec((1, D, FF), lambda e, i: (e, 0, 0)),
                  pl.BlockSpec((1, FF, D), lambda e, i: (e, 0, 0))],
        out_specs=pl.BlockSpec((1, tm, D), lambda e, i: (e, i, 0)),
        out_shape=jax.ShapeDtypeStruct((E, cap, D), BF16),
        scratch_shapes=[pltpu.VMEM((1, D, FF), BF16), pltpu.VMEM((1, D, FF), BF16), pltpu.VMEM((1, FF, D), BF16)],
        compiler_params=_params("parallel", "arbitrary"),
        name="expert_ffn",
    )(xe, wg, wu, wd)


CHUNK = 256
COMBINE_TB = 512
COMBINE_W = 128


GATE_TERMS = 3


def _router_kernel(x_ref, g_ref, rt_ref, aff_ref, h_ref):
    D = x_ref.shape[1]
    E = rt_ref.shape[0]
    xn = _rms(x_ref[...], g_ref[...])
    logits = lax.dot_general(rt_ref[...], xn, _NT, preferred_element_type=F32,
                             precision=lax.Precision.HIGHEST)
    e = jnp.exp(logits - jnp.max(logits, axis=0, keepdims=True))
    aff = e / jnp.sum(e, axis=0, keepdims=True)
    aff_ref[...] = aff
    lane = lax.broadcasted_iota(jnp.int32, (E, LANES), 1)
    expert = lax.broadcasted_iota(jnp.int32, (E, LANES), 0)
    rest = aff
    gates = jnp.zeros((x_ref.shape[0], LANES), F32)
    for j in range(GATE_TERMS):
        term = rest.astype(BF16)
        rest = rest - term.astype(F32)
        place = (lane == GATE_TERMS * expert + j).astype(BF16)
        gates = gates + lax.dot_general(term, place, (((0,), (0,)), ((), ())), preferred_element_type=F32)
    h_ref[:, :D] = xn.astype(h_ref.dtype)
    h_ref[:, D:] = gates.astype(h_ref.dtype)


def _router(x2, ln_g, router, tm=512):
    T, D = x2.shape
    E = router.shape[1]
    assert GATE_TERMS * E <= LANES
    return pl.pallas_call(
        _router_kernel,
        grid=(T // tm,),
        in_specs=[pl.BlockSpec((tm, D), lambda i: (i, 0)),
                  pl.BlockSpec((1, D), lambda i: (0, 0)),
                  pl.BlockSpec((E, D), lambda i: (0, 0))],
        out_specs=[pl.BlockSpec((E, tm), lambda i: (0, i)),
                   pl.BlockSpec((tm, D + LANES), lambda i: (i, 0))],
        out_shape=[jax.ShapeDtypeStruct((E, T), F32), jax.ShapeDtypeStruct((T, D + LANES), BF16)],
        compiler_params=_params("parallel"),
        name="router",
    )(x2, ln_g.reshape(1, D).astype(F32), router.T.astype(F32))


def _cumsum_excl(mask, upper, lower_strict):
    incl = jnp.dot(mask.astype(BF16), upper, preferred_element_type=F32)
    before = jnp.dot(lower_strict, incl.astype(BF16), preferred_element_type=F32)
    return incl + before[:, CHUNK - 1:CHUNK] - mask


def _select_kernel(aff_ref, slot_ref, pos_ref, *, cap):
    aff = aff_ref[0]
    nc = aff.shape[0]
    bits = pltpu.bitcast(aff, jnp.int32)

    def count(cond):
        c = jnp.sum(jnp.where(cond, 1.0, 0.0), axis=0, keepdims=True)
        return jnp.sum(c, axis=1, keepdims=True)

    def bit_step(i, prefix):
        cand = prefix | (jnp.int32(1) << (30 - i))
        return jnp.where(count(bits >= cand) >= cap, cand, prefix)

    thr = lax.fori_loop(0, 31, bit_step, jnp.zeros((1, 1), jnp.int32))
    r = lax.broadcasted_iota(jnp.int32, (CHUNK, CHUNK), 0)
    c = lax.broadcasted_iota(jnp.int32, (CHUNK, CHUNK), 1)
    upper = (r <= c).astype(BF16)
    r = lax.broadcasted_iota(jnp.int32, (nc, nc), 0)
    c = lax.broadcasted_iota(jnp.int32, (nc, nc), 1)
    lower_strict = (c < r).astype(BF16)
    gt = bits > thr
    eq = jnp.where(bits == thr, 1.0, 0.0)
    need = cap - count(gt)
    sel = jnp.where(gt | ((eq > 0) & (_cumsum_excl(eq, upper, lower_strict) < need)), 1.0, 0.0)
    pos = _cumsum_excl(sel, upper, lower_strict).astype(jnp.int32)
    pos_ref[0] = pos
    slot_ref[0] = jnp.where(sel > 0, pos, -1)


def _select(aff_t, cap):
    E, T = aff_t.shape
    nc = T // CHUNK
    assert nc <= CHUNK and nc % 8 == 0
    spec = pl.BlockSpec((1, nc, CHUNK), lambda e: (e, 0, 0))
    slot, pos = pl.pallas_call(
        functools.partial(_select_kernel, cap=cap),
        grid=(E,), in_specs=[spec], out_specs=[spec, spec],
        out_shape=[jax.ShapeDtypeStruct((E, nc, CHUNK), jnp.int32)] * 2,
        compiler_params=_params("parallel"),
        name="select_topk",
    )(aff_t.reshape(E, nc, CHUNK))
    return slot.reshape(E, T), pos.reshape(E, T)


GATHER_STEP = 128
GATHER_W = GATHER_STEP + 16
GATHER_STACK = 8


def _gather_kernel(starts_ref, h_ref, slot_ref, xe_ref, stage_ref, extra_ref, tail_ref, sem, extra_sem, *, cap):
    b = pl.program_id(0)
    nb = pl.num_programs(0)
    E = slot_ref.shape[0]
    buf = b % 2
    h = h_ref[...]
    row = lax.broadcasted_iota(jnp.int32, (GATHER_W, 1), 0)

    @pl.when(b == 0)
    def _():
        tail_ref[...] = jnp.zeros_like(tail_ref)

    def first_row(e, blk):
        return pl.multiple_of((starts_ref[e, blk] // 16) * 16, 16)

    def copy(e, blk):
        return pltpu.make_async_copy(stage_ref.at[blk % 2, e], xe_ref.at[e, pl.ds(first_row(e, blk), GATHER_W)],
                                     sem.at[blk % 2, e])

    def onehot(e, k):
        return (slot_ref[pl.ds(e, 1), :] - (first_row(e, b) + k * GATHER_STEP) == row).astype(BF16)

    for e0 in range(0, E, GATHER_STACK):
        stacked = jnp.concatenate([onehot(e, 0) for e in range(e0, e0 + GATHER_STACK)], axis=0)
        rows = jnp.dot(stacked, h, preferred_element_type=F32).astype(BF16)
        for i in range(GATHER_STACK):
            stage_ref[buf, e0 + i] = rows[i * GATHER_W:(i + 1) * GATHER_W]

    def per_expert(e, carry):
        base = first_row(e, b)
        end = starts_ref[e, b + 1] - base
        group = pl.multiple_of((end // 16) * 16, 16)
        last = jnp.maximum(group - 1, 0) // GATHER_STEP

        def window(k):
            return jnp.dot(onehot(e, k), h, preferred_element_type=F32).astype(BF16)

        stage_ref[buf, e, pl.ds(0, 16), :] += tail_ref[e]

        @pl.when(last == 0)
        def _():
            tail_ref[e] = stage_ref[buf, e, pl.ds(group, 16), :]

        @pl.when(b > 0)
        def _():
            copy(e, b - 1).wait()

        copy(e, b).start()

        def extra(k, inner):
            extra_ref[...] = window(k)

            @pl.when(k == last)
            def _():
                tail_ref[e] = extra_ref[pl.ds(pl.multiple_of(group - k * GATHER_STEP, 16), 16), :]

            cp = pltpu.make_async_copy(extra_ref, xe_ref.at[e, pl.ds(base + k * GATHER_STEP, GATHER_W)],
                                       extra_sem.at[0])
            cp.start()
            cp.wait()
            return inner

        return lax.fori_loop(1, last + 1, extra, carry)

    lax.fori_loop(0, E, per_expert, 0)

    @pl.when(b == nb - 1)
    def _():
        extra_ref[...] = jnp.zeros_like(extra_ref)

        def drain(e, carry):
            copy(e, b).wait()
            cp = pltpu.make_async_copy(extra_ref, xe_ref.at[e, pl.ds(cap, GATHER_W)], extra_sem.at[0])
            cp.start()
            cp.wait()
            return carry
        lax.fori_loop(0, E, drain, 0)


def _gather(h, slot, starts, cap):
    T, D = h.shape
    E = slot.shape[0]
    tb = min(COMBINE_TB, T)
    rows = cap + GATHER_W
    return pl.pallas_call(
        functools.partial(_gather_kernel, cap=cap),
        grid_spec=pltpu.PrefetchScalarGridSpec(
            num_scalar_prefetch=1,
            grid=(T // tb,),
            in_specs=[pl.BlockSpec((tb, D), lambda i, st: (i, 0)),
                      pl.BlockSpec((E, tb), lambda i, st: (0, i))],
            out_specs=pl.BlockSpec(memory_space=pl.ANY),
            scratch_shapes=[pltpu.VMEM((2, E, GATHER_W, D), BF16),
                            pltpu.VMEM((GATHER_W, D), BF16),
                            pltpu.VMEM((E, 16, D), BF16),
                            pltpu.SemaphoreType.DMA((2, E)),
                            pltpu.SemaphoreType.DMA((1,))]),
        out_shape=jax.ShapeDtypeStruct((E, rows, D), BF16),
        compiler_params=_params("arbitrary"),
        name="moe_gather",
    )(starts, h, slot)


def _combine_kernel(starts_ref, x_ref, slot_ref, first_ref, g_ref, ye_ref, o_ref, win_ref, extra_ref, sem,
                    extra_sem, *, cap, final_norm):
    b = pl.program_id(0)
    nb = pl.num_programs(0)
    E = ye_ref.shape[0]
    W = COMBINE_W

    def first_row(e, blk, k):
        return (starts_ref[e, blk] // 16) * 16 + k * W

    def window_start(e, blk, k):
        return pl.multiple_of(jnp.minimum(first_row(e, blk, k), cap - W), 16)

    def copy(e, blk):
        buf = blk % 2
        return pltpu.make_async_copy(ye_ref.at[e, pl.ds(window_start(e, blk, 0), W)],
                                     win_ref.at[buf, pl.ds(e * W, W)], sem.at[buf, e])

    @pl.when(b == 0)
    def _():
        for e in range(E):
            copy(e, b).start()

    @pl.when(b + 1 < nb)
    def _():
        for e in range(E):
            copy(e, b + 1).start()

    slots = slot_ref[...]
    first = first_ref[0]
    rel = slots - jnp.minimum(first, cap - W)
    rel = jnp.where((slots >= first) & (rel >= 0) & (rel < W), rel, -1).astype(F32).astype(BF16)
    lane = lax.broadcasted_iota(jnp.int32, (E, E * W), 1)
    expert = lax.broadcasted_iota(jnp.int32, (E, E * W), 0)
    expand = (lane // W == expert).astype(BF16)
    wanted = (lax.broadcasted_iota(jnp.int32, (1, E * W), 1) % W).astype(F32)
    onehot = (jnp.dot(rel, expand, preferred_element_type=F32) == wanted).astype(BF16)

    for e in range(E):
        copy(e, b).wait()
    o_ref[...] = x_ref[...] + jnp.dot(onehot, win_ref[b % 2], preferred_element_type=F32)

    col = lax.broadcasted_iota(jnp.int32, (1, W), 1)
    expert_lane = lax.broadcasted_iota(jnp.int32, (1, E), 1)

    def per_expert(e, carry):
        n_win = (starts_ref[e, b + 1] - first_row(e, b, 0) + W - 1) // W

        def extra(k, inner):
            cp = pltpu.make_async_copy(ye_ref.at[e, pl.ds(window_start(e, b, k), W)], extra_ref, extra_sem.at[0])
            cp.start()
            cp.wait()
            slot = jnp.max(jnp.where(expert_lane == e, slots, -1), axis=1, keepdims=True)
            oh = ((slot == window_start(e, b, k) + col) & (slot >= first_row(e, b, k))).astype(BF16)
            o_ref[...] += jnp.dot(oh, extra_ref[...], preferred_element_type=F32)
            return inner

        return lax.fori_loop(1, n_win, extra, carry)

    lax.fori_loop(0, E, per_expert, 0)
    if final_norm:
        o_ref[...] = _rms(o_ref[...], g_ref[...])


def _combine(x2, slot_t, starts, ye, final_g=None):
    T, D = x2.shape
    gain = jnp.ones((1, D), F32) if final_g is None else final_g.reshape(1, D).astype(F32)
    E, cap, _ = ye.shape
    tb = min(COMBINE_TB, T)
    nb = T // tb
    assert cap >= COMBINE_W and cap % 16 == 0
    first = ((starts[:, :nb] // 16) * 16).T.reshape(nb, 1, E)
    return pl.pallas_call(
        functools.partial(_combine_kernel, cap=cap, final_norm=final_g is not None),
        grid_spec=pltpu.PrefetchScalarGridSpec(
            num_scalar_prefetch=1,
            grid=(nb,),
            in_specs=[pl.BlockSpec((tb, D), lambda i, st: (i, 0)),
                      pl.BlockSpec((tb, E), lambda i, st: (i, 0)),
                      pl.BlockSpec((1, 1, E), lambda i, st: (i, 0, 0)),
                      pl.BlockSpec((1, D), lambda i, st: (0, 0)),
                      pl.BlockSpec(memory_space=pl.ANY)],
            out_specs=pl.BlockSpec((tb, D), lambda i, st: (i, 0)),
            scratch_shapes=[pltpu.VMEM((2, E * COMBINE_W, D), BF16),
                            pltpu.VMEM((COMBINE_W, D), BF16),
                            pltpu.SemaphoreType.DMA((2, E)),
                            pltpu.SemaphoreType.DMA((1,))]),
        out_shape=jax.ShapeDtypeStruct((T, D), F32),
        compiler_params=_params("arbitrary"),
        name="moe_combine",
    )(starts, x2, slot_t, first, gain, ye)


def _t5_bucket(rel):
    half = T5_BUCKETS // 2
    max_exact = half // 2
    n = np.abs(rel)
    large = max_exact + (np.log(np.maximum(n, 1) / max_exact) / np.log(T5_MAX_DISTANCE / max_exact)
                         * (half - max_exact)).astype(np.int32)
    large = np.minimum(large, half - 1)
    return (np.where(rel > 0, half, 0) + np.where(n < max_exact, n, large)).astype(np.int32)


def _dilated_tiles(L, radius):
    tq = min(LANES, L)
    kw = min(tq + 2 * radius, L)
    return tq, kw


def _dilated_bias(t5_table, g, L, window, dilation):
    radius = window // (2 * dilation)
    tq, kw = _dilated_tiles(L, radius)
    deltas = (0, -((kw - tq) // 2), tq - kw)
    period = 2 * kw - 1
    rel = np.arange(period + 1) - (kw - 1)
    tab = t5_table[:, g * A_HEADS_PER_GROUP:(g + 1) * A_HEADS_PER_GROUP].astype(F32)
    by_rel = jnp.where((np.abs(rel) <= radius)[None], tab[_t5_bucket(rel * dilation)].T, NEG_INF)
    skew = jnp.tile(by_rel, (1, tq))[:, :tq * period].reshape(A_HEADS_PER_GROUP, tq, period)
    bias = jnp.stack([skew[:, :, d + kw - 1:d + 2 * kw - 1] for d in deltas])
    return bias.reshape(3, A_HEADS_PER_GROUP // 2, 2 * tq, kw)


def _dilated_kernel(q_ref, k_ref, v_ref, bias_ref, o_ref, lse_ref, *, L, tq, kw):
    nq = L // tq
    lane = lax.broadcasted_iota(jnp.int32, (1, LANES), 1)
    lo = lane < HEAD_DIM

    def body(i, carry):
        q0 = pl.multiple_of(i * tq, tq)
        start = jnp.clip(q0 - (kw - tq) // 2, 0, L - kw)
        start = pl.multiple_of(start, 16)
        var = jnp.where(i == 0, 0, jnp.where(i == nq - 1, 2, 1))
        n_pairs = A_HEADS_PER_GROUP // 2
        cols = [slice(pair * LANES, (pair + 1) * LANES) for pair in range(n_pairs)]
        scores = []
        for pair in range(n_pairs):
            q2 = q_ref[0, 0, pl.ds(q0, tq), cols[pair]] * (HEAD_DIM ** -0.5)
            zero = jnp.zeros_like(q2)
            qs = jnp.concatenate([jnp.where(lo, q2, zero), jnp.where(lo, zero, q2)], axis=0)
            k2 = k_ref[0, 0, pl.ds(start, kw), cols[pair]]
            scores.append(lax.dot_general(qs, k2, _NT, preferred_element_type=F32) + bias_ref[var, pair])
        probs, dens, maxs = [], [], []
        for s in scores:
            m = jnp.max(s, axis=-1, keepdims=True)
            p = jnp.exp(s - m)
            maxs.append(m)
            dens.append(jnp.sum(p, axis=-1, keepdims=True))
            probs.append(p.astype(BF16))
        for pair in range(n_pairs):
            v2 = v_ref[0, 0, pl.ds(start, kw), cols[pair]]
            o = jnp.dot(probs[pair], v2, preferred_element_type=F32) * (1.0 / dens[pair])
            lse = maxs[pair] + jnp.log(dens[pair])
            o_ref[0, 0, pl.ds(q0, tq), cols[pair]] = jnp.where(lo, o[:tq], o[tq:])
            lse_ref[0, 0, pl.ds(q0, tq), cols[pair]] = jnp.where(lo, lse[:tq], lse[tq:])
        return carry

    lax.fori_loop(0, nq, body, 0, unroll=next(u for u in (4, 2, 1) if nq % u == 0))


def _dilated_attention(qkv, bias, window):
    Bn, dilation, L, _ = qkv.shape
    radius = window // (2 * dilation)
    tq, kw = _dilated_tiles(L, radius)
    assert L % tq == 0 and kw % 16 == 0 and tq % 16 == 0

    def spec(which):
        return pl.BlockSpec((1, 1, L, A_OUT), lambda b, r: (b, r, 0, which))

    o, lse = pl.pallas_call(
        functools.partial(_dilated_kernel, L=L, tq=tq, kw=kw),
        grid=(Bn, dilation),
        in_specs=[spec(0), spec(1), spec(2), pl.BlockSpec(bias.shape, lambda b, r: (0, 0, 0, 0))],
        out_specs=[spec(0), spec(0)],
        out_shape=[jax.ShapeDtypeStruct((Bn, dilation, L, A_OUT), F32)] * 2,
        compiler_params=_params("parallel", "parallel"),
        name="dilated_attention",
    )(qkv, qkv, qkv, bias)
    return o, lse


def _merge_kernel(*refs):
    n = len(A_PATTERNS)
    o_refs, l_refs, out_ref = refs[:n], refs[n:2 * n], refs[2 * n]
    scratch = refs[2 * n + 1:]
    tm = out_ref.shape[0]

    def token_order(ref, scr, dilation):
        if dilation == 1:
            return ref[0, 0]
        tiles = A_OUT // LANES
        for r in range(dilation):
            for c in range(tiles):
                scr[c, pl.ds(r, tm // dilation, stride=dilation), :] = ref[0, r, :, c * LANES:(c + 1) * LANES]
        return jnp.concatenate([scr[c] for c in range(tiles)], axis=1)

    os_ = [token_order(o_refs[g], scratch[2 * g], d) for g, (_, d) in enumerate(A_PATTERNS)]
    ls = [token_order(l_refs[g], scratch[2 * g + 1], d) for g, (_, d) in enumerate(A_PATTERNS)]
    m = jnp.maximum(jnp.maximum(ls[0], ls[1]), ls[2])
    es = [jnp.exp(l - m) for l in ls]
    den = es[0] + es[1] + es[2]
    out = (es[0] / den) * os_[0] + (es[1] / den) * os_[1] + (es[2] / den) * os_[2]
    out_ref[...] = out.astype(out_ref.dtype)


def _merge_groups(os_, ls_, S, tm=512):
    Bn = os_[0].shape[0]
    ns = S // tm

    def spec(d):
        return pl.BlockSpec((1, d, tm // d, A_OUT), lambda i: (i // ns, 0, i % ns, 0))

    specs = [spec(d) for _, d in A_PATTERNS]
    return pl.pallas_call(
        _merge_kernel, grid=(Bn * ns,), in_specs=specs * 2,
        out_specs=pl.BlockSpec((tm, A_OUT), lambda i: (i, 0)),
        out_shape=jax.ShapeDtypeStruct((Bn * S, A_OUT), BF16),
        scratch_shapes=[pltpu.VMEM((A_OUT // LANES, tm, LANES), F32)] * (2 * len(A_PATTERNS)),
        compiler_params=_params("parallel"), name="merge_groups",
    )(*os_, *ls_)


EVEN_TOKEN_COLS = 3 * A_OUT + B_KV_RANK + LANES + B_Q_RANK


def _in_proj_even_kernel(x_ref, w_ref, g_ref, u0_ref, *rest):
    n_dil = len(A_PATTERNS) - 1
    dil_refs, scr = rest[:n_dil], rest[n_dil]
    tm = x_ref.shape[0]
    acc = jnp.dot(_rms(x_ref[...], g_ref[...]).astype(BF16), w_ref[...], preferred_element_type=F32)
    u0_ref[...] = acc[:, :EVEN_TOKEN_COLS].astype(u0_ref.dtype)
    for i, (_, d) in enumerate(A_PATTERNS[1:]):
        c0 = EVEN_TOKEN_COLS + i * 3 * A_OUT
        for c in range(3 * A_OUT // LANES):
            scr[c] = acc[:, c0 + c * LANES:c0 + (c + 1) * LANES]
        for r in range(d):
            for c in range(3 * A_OUT // LANES):
                piece = scr[c, pl.ds(r, tm // d, stride=d), :]
                dil_refs[i][0, r, :, c * LANES:(c + 1) * LANES] = piece.astype(dil_refs[i].dtype)


def _in_proj_even(x2, w, gain, Bn, S, tm=512):
    T, K = x2.shape
    ns = S // tm
    dils = [d for _, d in A_PATTERNS[1:]]
    assert S % tm == 0 and all(tm % (16 * d) == 0 for d in dils)
    return pl.pallas_call(
        _in_proj_even_kernel,
        grid=(T // tm,),
        in_specs=[pl.BlockSpec((tm, K), lambda i: (i, 0)),
                  pl.BlockSpec(w.shape, lambda i: (0, 0)),
                  pl.BlockSpec((1, K), lambda i: (0, 0))],
        out_specs=[pl.BlockSpec((tm, EVEN_TOKEN_COLS), lambda i: (i, 0))]
        + [pl.BlockSpec((1, d, tm // d, 3 * A_OUT), lambda i: (i // ns, 0, i % ns, 0)) for d in dils],
        out_shape=[jax.ShapeDtypeStruct((T, EVEN_TOKEN_COLS), BF16)]
        + [jax.ShapeDtypeStruct((Bn, d, S // d, 3 * A_OUT), BF16) for d in dils],
        scratch_shapes=[pltpu.VMEM((3 * A_OUT // LANES, tm, LANES), F32)],
        compiler_params=_params("parallel"),
        name="in_proj_even",
    )(x2, w, gain.reshape(1, K).astype(F32))


def _rope_tables(S):
    inv = 1.0 / (ROPE_THETA ** (np.arange(0, B_ROPE, 2, dtype=np.float32) / B_ROPE))
    ang = np.arange(S, dtype=np.float32)[:, None] * inv[None]
    cos, sin = np.cos(ang), np.sin(ang)
    half = B_ROPE // 2
    c = np.ones((S, LANES), np.float32)
    c[:, B_NOPE:B_NOPE + half] = cos
    c[:, B_NOPE + half:B_NOPE + B_ROPE] = cos
    s_lo = np.zeros((S, LANES), np.float32)
    s_lo[:, B_NOPE:B_NOPE + half] = -sin
    s_hi = np.zeros((S, LANES), np.float32)
    s_hi[:, B_NOPE + half:B_NOPE + B_ROPE] = sin
    return jnp.asarray(c), jnp.asarray(s_lo), jnp.asarray(s_hi)


def _rope(x, c, s_lo, s_hi):
    half = B_ROPE // 2
    return x * c + pltpu.roll(x, LANES - half, 1) * s_lo + pltpu.roll(x, half, 1) * s_hi


def _mla_prep_kernel(cq_ref, ckv_ref, kr_ref, gq_ref, gkv_ref, wq_ref, wk_ref, wv_ref,
                     c_ref, slo_ref, shi_ref, q_ref, k_ref, v_ref):
    c, s_lo, s_hi = c_ref[...], slo_ref[...], shi_ref[...]
    scale = (B_NOPE + B_ROPE) ** -0.5 * LOG2_E
    xq = _rms(cq_ref[...], gq_ref[...]).astype(BF16)
    q = jnp.dot(xq, wq_ref[...], preferred_element_type=F32)
    xkv = _rms(ckv_ref[...], gkv_ref[...]).astype(BF16)
    k = jnp.dot(xkv, wk_ref[...], preferred_element_type=F32)
    vt = lax.dot_general(wv_ref[...], xkv, _NT, preferred_element_type=F32)
    ones_row = lax.broadcasted_iota(jnp.int32, vt.shape, 0) % LANES == B_V
    v_ref[0, 0] = jnp.where(ones_row, 1.0, vt).astype(v_ref.dtype)
    k_rope = _rope(kr_ref[...].astype(F32), c, s_lo, s_hi)
    for h in range(B_HEADS):
        cs = slice(h * LANES, (h + 1) * LANES)
        q_ref[:, cs] = (_rope(q[:, cs], c, s_lo, s_hi) * scale).astype(q_ref.dtype)
        k_ref[:, cs] = (k[:, cs] + k_rope).astype(k_ref.dtype)


MLA_KEY_CHUNK = 1024


def _mla_prep(u3, g_qa, g_kva, wq, wk, wv):
    Bn, S, C = u3.shape
    ts = min(MLA_KEY_CHUNK, S)
    tabs = _rope_tables(S)
    ns = S // ts
    u2 = u3.reshape(Bn * S, C)
    row = lambda b, i: b * ns + i
    a_cols = 3 * A_OUT
    off_kv, off_kr, off_q = a_cols // B_KV_RANK, (a_cols + B_KV_RANK) // LANES, (a_cols + B_KV_RANK + LANES) // B_Q_RANK
    full = lambda a: pl.BlockSpec(a.shape, lambda b, i: (0, 0))
    tab_spec = pl.BlockSpec((ts, LANES), lambda b, i: (i, 0))
    hw = B_HEADS * LANES
    return pl.pallas_call(
        _mla_prep_kernel,
        grid=(Bn, ns),
        in_specs=[pl.BlockSpec((ts, B_Q_RANK), lambda b, i: (row(b, i), off_q)),
                  pl.BlockSpec((ts, B_KV_RANK), lambda b, i: (row(b, i), off_kv)),
                  pl.BlockSpec((ts, LANES), lambda b, i: (row(b, i), off_kr)),
                  full(g_qa), full(g_kva), full(wq), full(wk), full(wv),
                  tab_spec, tab_spec, tab_spec],
        out_specs=[pl.BlockSpec((ts, hw), lambda b, i: (row(b, i), 0)),
                   pl.BlockSpec((ts, hw), lambda b, i: (row(b, i), 0)),
                   pl.BlockSpec((1, 1, hw, ts), lambda b, i: (b, i, 0, 0))],
        out_shape=[jax.ShapeDtypeStruct((Bn * S, hw), BF16),
                   jax.ShapeDtypeStruct((Bn * S, hw), BF16),
                   jax.ShapeDtypeStruct((Bn, ns, hw, ts), BF16)],
        compiler_params=_params("parallel", "parallel"),
        name="mla_prep",
    )(u2, u2, u2, g_qa, g_kva, wq, wk, wv, *tabs)


def _mla_kernel(q_ref, k_ref, v_ref, o_ref):
    nk, _, tk = v_ref.shape[1:]
    tq = q_ref.shape[1]
    lane = lax.broadcasted_iota(jnp.int32, (1, LANES), 1)
    qs = [q_ref[0, :, sub * LANES:(sub + 1) * LANES] for sub in range(2)]

    def body(j, carry):
        k0 = pl.multiple_of(j * tk, tk)
        scores = [lax.dot_general(k_ref[0, pl.ds(k0, tk), sub * LANES:(sub + 1) * LANES], qs[sub], _NT,
                                  preferred_element_type=F32) for sub in range(2)]
        new = []
        for sub in range(2):
            m, acc = carry[sub]
            s = scores[sub]
            m_new = jnp.maximum(m, jnp.max(s, axis=0, keepdims=True))
            p = jnp.exp2(s - m_new).astype(BF16)
            vt = v_ref[0, j, sub * LANES:(sub + 1) * LANES, :]
            acc = jnp.exp2(m - m_new) * acc + jnp.dot(vt, p, preferred_element_type=F32)
            new.append((m_new, acc))
        return tuple(new)

    init = tuple((jnp.full((1, tq), NEG_INF, F32), jnp.zeros((LANES, tq), F32)) for _ in range(2))
    res = lax.fori_loop(0, nk, body, init, unroll=True)
    outs = [(acc * (1.0 / acc[B_V:B_V + 1])).T for (_, acc) in res]
    o_ref[0] = jnp.where(lane < B_V, outs[0], pltpu.roll(outs[1], B_V, 1)).astype(o_ref.dtype)


def _mla_attention(q, k, vt, Bn, S, tq=1024):
    tq = min(tq, S)
    nk, _, tk = vt.shape[1:]
    hw = B_HEADS * LANES
    q3, k3 = q.reshape(Bn, S, hw), k.reshape(Bn, S, hw)
    o = pl.pallas_call(
        _mla_kernel,
        grid=(Bn, B_HEADS // 2, S // tq),
        in_specs=[pl.BlockSpec((1, tq, 2 * LANES), lambda b, h, i: (b, i, h)),
                  pl.BlockSpec((1, S, 2 * LANES), lambda b, h, i: (b, 0, h)),
                  pl.BlockSpec((1, nk, 2 * LANES, tk), lambda b, h, i: (b, 0, h, 0))],
        out_specs=pl.BlockSpec((1, tq, LANES), lambda b, h, i: (b, i, h)),
        out_shape=jax.ShapeDtypeStruct((Bn, S, B_OUT), BF16),
        compiler_params=_params("parallel", "parallel", "parallel"),
        name="mla_attention",
    )(q3, k3, vt)
    return o.reshape(Bn * S, B_OUT)


def _natten_bias(rpb):
    W = GRID_W
    c = np.arange(W)[:, None]
    kc = np.arange(W)[None, :]
    wstart = np.clip(c - C_WIN_COLS // 2, 0, W - C_WIN_COLS)
    col_valid = (kc >= wstart) & (kc < wstart + C_WIN_COLS)
    dcol = np.clip(kc - c + C_WIN_COLS - 1, 0, 2 * C_WIN_COLS - 2)
    off = np.arange(C_WIN_ROWS)[:, None]
    j = np.arange(C_WIN_ROWS)[None, :]
    drow = j - off + C_WIN_ROWS - 1
    b = rpb.astype(F32)[:, drow][:, :, :, dcol]
    b = jnp.where(col_valid[None, None, None], b, NEG_INF)
    b = jnp.transpose(b, (1, 0, 3, 2, 4))
    return b.reshape(C_WIN_ROWS, C_HEADS // 2, 2 * W, C_WIN_ROWS * W)


NATTEN_ROWS = 4


def _natten_kernel(q_ref, k_ref, v_ref, *rest, rows):
    bias_refs, o_ref = rest[:NATTEN_ROWS], rest[NATTEN_ROWS]
    nk = C_WIN_ROWS * GRID_W
    lane = lax.broadcasted_iota(jnp.int32, (1, LANES), 1)
    lo = lane < HEAD_DIM
    n_pairs = C_HEADS // 2
    cols = [slice(pair * LANES, (pair + 1) * LANES) for pair in range(n_pairs)]
    for j in range(NATTEN_ROWS):
        r = pl.program_id(1) * NATTEN_ROWS + j
        r0 = jnp.clip(r - C_WIN_ROWS // 2, 0, rows - C_WIN_ROWS)
        k0 = pl.multiple_of(r0 * GRID_W, GRID_W)
        q_rows = slice(j * GRID_W, (j + 1) * GRID_W)
        scores = []
        for pair in range(n_pairs):
            q2 = q_ref[0, q_rows, cols[pair]] * (HEAD_DIM ** -0.5)
            zero = jnp.zeros_like(q2)
            qs = jnp.concatenate([jnp.where(lo, q2, zero), jnp.where(lo, zero, q2)], axis=0)
            k2 = k_ref[0, pl.ds(k0, nk), cols[pair]]
            scores.append(lax.dot_general(qs, k2, _NT, preferred_element_type=F32) + bias_refs[j][0, pair])
        probs, dens = [], []
        for s in scores:
            p = jnp.exp(s - jnp.max(s, axis=-1, keepdims=True))
            dens.append(jnp.sum(p, axis=-1, keepdims=True))
            probs.append(p.astype(BF16))
        for pair in range(n_pairs):
            v2 = v_ref[0, pl.ds(k0, nk), cols[pair]]
            o = jnp.dot(probs[pair], v2, preferred_element_type=F32) * (1.0 / dens[pair])
            o_ref[0, q_rows, cols[pair]] = jnp.where(lo, o[:GRID_W], o[GRID_W:]).astype(o_ref.dtype)


def _natten(u3, bias):
    Bn, S, _ = u3.shape
    rows = S // GRID_W
    assert rows >= C_WIN_ROWS and rows % NATTEN_ROWS == 0

    def bias_spec(j):
        def variant(b, i):
            r = i * NATTEN_ROWS + j
            return (r - jnp.clip(r - C_WIN_ROWS // 2, 0, rows - C_WIN_ROWS), 0, 0, 0)
        return pl.BlockSpec((1, C_HEADS // 2, 2 * GRID_W, C_WIN_ROWS * GRID_W), variant)

    q_spec = pl.BlockSpec((1, NATTEN_ROWS * GRID_W, C_OUT), lambda b, i: (b, i, 0))
    o = pl.pallas_call(
        functools.partial(_natten_kernel, rows=rows),
        grid=(Bn, rows // NATTEN_ROWS),
        in_specs=[q_spec,
                  pl.BlockSpec((1, S, C_OUT), lambda b, i: (b, 0, 1), pipeline_mode=pl.Buffered(1)),
                  pl.BlockSpec((1, S, C_OUT), lambda b, i: (b, 0, 2), pipeline_mode=pl.Buffered(1))]
        + [bias_spec(j) for j in range(NATTEN_ROWS)],
        out_specs=q_spec,
        out_shape=jax.ShapeDtypeStruct((Bn, S, C_OUT), BF16),
        compiler_params=_params("parallel", "arbitrary"),
        name="natten",
    )(u3, u3, u3, *([bias] * NATTEN_ROWS))
    return o.reshape(Bn * S, C_OUT)


def _moe(x2, ln_g, router, wg, wu, wd, final_g=None):
    T, D = x2.shape
    cap = CAPACITY_FACTOR * T // N_EXPERTS
    aff_t, h = _router(x2, ln_g, router)
    slot, pos = _select(aff_t, cap)
    tb = min(COMBINE_TB, T)
    starts = jnp.concatenate([pos[:, ::tb], jnp.full((N_EXPERTS, 1), cap, jnp.int32)], axis=1)
    xe = _gather(h, slot, starts, cap)
    ye = _expert_ffn(xe, cap, wg, wu, wd)
    return _combine(x2, slot.T, starts, ye, final_g)


def _trunk(x, p):
    Bn, S, D = x.shape
    T = Bn * S
    x2 = x.reshape(T, D)
    u0, *u_dil = _in_proj_even(x2, p["w_in_even"], p["ln_mix_g"][0], Bn, S)
    u3 = u0.reshape(Bn, S, EVEN_TOKEN_COLS)
    os_, ls_ = [], []
    for g, (window, dilation) in enumerate(A_PATTERNS):
        bias = _dilated_bias(p["t5_table"], g, S // dilation, window, dilation)
        qkv = u3.reshape(Bn, 1, S, EVEN_TOKEN_COLS) if g == 0 else u_dil[g - 1]
        o, l = _dilated_attention(qkv, bias, window)
        os_.append(o)
        ls_.append(l)
    a = _merge_groups(os_, ls_, S)
    q, k, v = _mla_prep(u3, p["g_qa"], p["g_kva"], p["wq"], p["wk"], p["wv"])
    b = _mla_attention(q, k, v, Bn, S)
    x2 = _matmul([a, b], [p["w_out_even"][:A_OUT], p["w_out_even"][A_OUT:]], res=x2, out_dtype=F32)
    x2 = _moe(x2, p["ln_ffn_g"][0], p["router"][0], p["w_gate"][0], p["w_up"][0], p["w_down"][0])
    u = _matmul(x2, p["w_in_odd"], gain=p["ln_mix_g"][1], out_dtype=BF16)
    c = _natten(u.reshape(Bn, S, C_IN), p["natten_bias"])
    x2 = _matmul(c, p["w_out_odd"], res=x2, out_dtype=F32)
    x2 = _moe(x2, p["ln_ffn_g"][1], p["router"][1], p["w_gate"][1], p["w_up"][1], p["w_down"][1],
              final_g=p["final_g"])
    return x2.reshape(Bn, S, D)


def _head_tiles(w, n_heads, per_head, keep):
    K = w.shape[0]
    w = w.reshape(K, n_heads, per_head)[:, :, :keep]
    return jnp.pad(w, ((0, 0), (0, 0), (0, LANES - keep))).reshape(K, n_heads * LANES)


def _prepare(ln_mix_g, w_in_even, g_qa, w_qb, g_kva, w_kvb, t5_table, w_out_even,
             w_in_odd, rpb, w_out_odd, ln_ffn_g, router, w_gate, w_up, w_down, final_g):
    w_in = w_in_even[0]
    w_a, w_cq = w_in[:, :A_IN], w_in[:, A_IN:A_IN + B_Q_RANK]
    w_ckv = w_in[:, A_IN + B_Q_RANK:A_IN + B_Q_RANK + B_KV_RANK]
    w_kr = w_in[:, A_IN + B_Q_RANK + B_KV_RANK:]
    w_kr_tile = jnp.pad(w_kr, ((0, 0), (B_NOPE, LANES - B_NOPE - B_ROPE)))
    w_a = w_a.reshape(-1, 3, A_GROUPS, A_OUT)
    groups = [w_a[:, :, g].reshape(-1, 3 * A_OUT) for g in range(A_GROUPS)]
    w_even = jnp.concatenate([groups[0], w_ckv, w_kr_tile, w_cq] + groups[1:], axis=1)
    w_kv = w_kvb[0].reshape(B_KV_RANK, B_HEADS, B_NOPE + B_V)
    return dict(
        ln_mix_g=ln_mix_g, ln_ffn_g=ln_ffn_g, final_g=final_g, t5_table=t5_table,
        w_in_even=w_even.astype(BF16),
        g_qa=g_qa[0].reshape(1, -1).astype(F32), g_kva=g_kva[0].reshape(1, -1).astype(F32),
        wq=_head_tiles(w_qb[0], B_HEADS, B_NOPE + B_ROPE, B_NOPE + B_ROPE).astype(BF16),
        wk=_head_tiles(w_kvb[0], B_HEADS, B_NOPE + B_V, B_NOPE).astype(BF16),
        wv=_head_tiles(w_kv[:, :, B_NOPE:].reshape(B_KV_RANK, B_OUT), B_HEADS, B_V, B_V).T.astype(BF16),
        w_out_even=w_out_even[0].astype(BF16), w_in_odd=w_in_odd[0].astype(BF16),
        natten_bias=_natten_bias(rpb[0]), w_out_odd=w_out_odd[0].astype(BF16), router=router,
        w_gate=w_gate, w_up=w_up, w_down=w_down,
    )


def kernel(x_prompt, x_sample, ln_mix_g, w_in_even, g_qa, w_qb, g_kva, w_kvb, t5_table, w_out_even,
           w_in_odd, rpb, w_out_odd, ln_ffn_g, router, w_gate, w_up, w_down, final_g):
    p = _prepare(ln_mix_g, w_in_even, g_qa, w_qb, g_kva, w_kvb, t5_table, w_out_even,
                 w_in_odd, rpb, w_out_odd, ln_ffn_g, router, w_gate, w_up, w_down, final_g)
    return _trunk(x_prompt, p), _trunk(x_sample, p)
```

```python
import functools

import jax
import jax.numpy as jnp
import numpy as np
from jax import lax
from jax.experimental import pallas as pl
from jax.experimental.pallas import tpu as pltpu

D_MODEL = 1024
RMS_EPS = 1e-6
HEAD_DIM = 64
NEG_INF = -1e30
A_PATTERNS = ((128, 1), (512, 4), (2048, 16))
A_GROUPS = 3
A_HEADS_PER_GROUP = 4
A_HEADS = A_GROUPS * A_HEADS_PER_GROUP
A_IN = 3 * A_HEADS * HEAD_DIM
A_OUT = A_HEADS_PER_GROUP * HEAD_DIM
T5_BUCKETS = 32
T5_MAX_DISTANCE = 1024
B_HEADS = 12
B_Q_RANK = 384
B_KV_RANK = 256
B_NOPE = 64
B_ROPE = 32
B_V = 64
B_OUT = B_HEADS * B_V
ROPE_THETA = 10000.0
GRID_W = 64
C_HEADS = 16
C_WIN_ROWS = 8
C_WIN_COLS = 16
C_IN = 3 * C_HEADS * HEAD_DIM
C_OUT = C_HEADS * HEAD_DIM
N_EXPERTS = 16
CAPACITY_FACTOR = 2

LANES = 128
VMEM_LIMIT_BYTES = 48 * 1024 * 1024

F32 = jnp.float32
BF16 = jnp.bfloat16
_NT = (((1,), (1,)), ((), ()))
LOG2_E = 1.4426950408889634


def _params(*sem):
    return pltpu.CompilerParams(dimension_semantics=sem, vmem_limit_bytes=VMEM_LIMIT_BYTES)


def _rms(x, g):
    xf = x.astype(F32)
    return xf * lax.rsqrt(jnp.mean(xf * xf, axis=-1, keepdims=True) + RMS_EPS) * g


def _mm_kernel(*refs, n_in, has_gain, has_res, precise):
    xs, ws = refs[:n_in], refs[n_in:2 * n_in]
    pos = 2 * n_in
    g_ref = res_ref = None
    if has_gain:
        g_ref = refs[pos]
        pos += 1
    if has_res:
        res_ref = refs[pos]
        pos += 1
    o_ref = refs[pos]
    acc = None
    for x_ref, w_ref in zip(xs, ws):
        x = x_ref[...]
        if has_gain:
            x = _rms(x, g_ref[...])
        if precise:
            part = jnp.dot(x.astype(F32), w_ref[...], preferred_element_type=F32,
                           precision=lax.Precision.HIGHEST)
        else:
            part = jnp.dot(x.astype(BF16), w_ref[...], preferred_element_type=F32)
        acc = part if acc is None else acc + part
    if has_res:
        acc = acc + res_ref[...]
    o_ref[...] = acc.astype(o_ref.dtype)


def _matmul(xs, ws, gain=None, res=None, out_dtype=F32, tm=512, precise=False):
    if not isinstance(xs, (list, tuple)):
        xs, ws = [xs], [ws]
    M = xs[0].shape[0]
    N = ws[0].shape[1]
    assert M % tm == 0
    in_specs = [pl.BlockSpec((tm, x.shape[1]), lambda i: (i, 0)) for x in xs]
    in_specs += [pl.BlockSpec(w.shape, lambda i: (0, 0)) for w in ws]
    args = list(xs) + list(ws)
    if gain is not None:
        assert len(xs) == 1
        in_specs.append(pl.BlockSpec((1, gain.shape[-1]), lambda i: (0, 0)))
        args.append(gain.reshape(1, -1).astype(F32))
    if res is not None:
        in_specs.append(pl.BlockSpec((tm, N), lambda i: (i, 0)))
        args.append(res)
    return pl.pallas_call(
        functools.partial(_mm_kernel, n_in=len(xs), has_gain=gain is not None,
                          has_res=res is not None, precise=precise),
        grid=(M // tm,),
        in_specs=in_specs,
        out_specs=pl.BlockSpec((tm, N), lambda i: (i, 0)),
        out_shape=jax.ShapeDtypeStruct((M, N), out_dtype),
        compiler_params=_params("parallel"),
        name="fused_matmul",
    )(*args)


def _ffn_kernel(x_ref, wg32_ref, wu32_ref, wd32_ref, o_ref, wg_ref, wu_ref, wd_ref):
    D = wg_ref.shape[1]

    @pl.when(pl.program_id(1) == 0)
    def _():
        wg_ref[0] = wg32_ref[0, 0].astype(BF16)
        wu_ref[0] = wu32_ref[0, 0].astype(BF16)
        wd_ref[0] = wd32_ref[0, 0].astype(BF16)

    x = x_ref[0, :, :D]
    lane = lax.broadcasted_iota(jnp.int32, (1, LANES), 1)
    mine = lane // GATE_TERMS == pl.program_id(0)
    gate = jnp.sum(jnp.where(mine, x_ref[0, :, D:].astype(F32), 0.0), axis=1, keepdims=True)
    g = jnp.dot(x, wg_ref[0], preferred_element_type=F32)
    u = jnp.dot(x, wu_ref[0], preferred_element_type=F32)
    h = (g * jax.nn.sigmoid(g)) * u
    y = jnp.dot(h.astype(BF16), wd_ref[0], preferred_element_type=F32)
    o_ref[0] = (y * gate).astype(o_ref.dtype)


def _expert_ffn(xe, cap, wg, wu, wd, layer, tm=512):
    E = xe.shape[0]
    D, FF = wg.shape[2], wg.shape[3]
    assert cap % tm == 0
    return pl.pallas_call(
        _ffn_kernel,
        grid=(E, cap // tm),
        in_specs=[pl.BlockSpec((1, tm, D + LANES), lambda e, i: (e, i, 0)),
                  pl.BlockSpec((1, 1, D, FF), lambda e, i: (layer, e, 0, 0)),
                  pl.BlockSpec((1, 1, D, FF), lambda e, i: (layer, e, 0, 0)),
                  pl.BlockSpec((1, 1, FF, D), lambda e, i: (layer, e, 0, 0))],
        out_specs=pl.BlockSpec((1, tm, D), lambda e, i: (e, i, 0)),
        out_shape=jax.ShapeDtypeStruct((E, cap, D), BF16),
        scratch_shapes=[pltpu.VMEM((1, D, FF), BF16), pltpu.VMEM((1, D, FF), BF16), pltpu.VMEM((1, FF, D), BF16)],
        compiler_params=_params("parallel", "arbitrary"),
        name="expert_ffn",
    )(xe, wg, wu, wd)


CHUNK = 256
COMBINE_TB = 512
COMBINE_W = 128


GATE_TERMS = 3


def _router_kernel(x_ref, g_ref, rt_ref, aff_ref, h_ref):
    D = x_ref.shape[1]
    E = rt_ref.shape[0]
    xn = _rms(x_ref[...], g_ref[...])
    logits = lax.dot_general(rt_ref[...], xn, _NT, preferred_element_type=F32,
                             precision=lax.Precision.HIGHEST)
    e = jnp.exp(logits - jnp.max(logits, axis=0, keepdims=True))
    aff = e / jnp.sum(e, axis=0, keepdims=True)
    aff_ref[...] = aff
    lane = lax.broadcasted_iota(jnp.int32, (E, LANES), 1)
    expert = lax.broadcasted_iota(jnp.int32, (E, LANES), 0)
    rest = aff
    gates = jnp.zeros((x_ref.shape[0], LANES), F32)
    for j in range(GATE_TERMS):
        term = rest.astype(BF16)
        rest = rest - term.astype(F32)
        place = (lane == GATE_TERMS * expert + j).astype(BF16)
        gates = gates + lax.dot_general(term, place, (((0,), (0,)), ((), ())), preferred_element_type=F32)
    h_ref[:, :D] = xn.astype(h_ref.dtype)
    h_ref[:, D:] = gates.astype(h_ref.dtype)


def _router(x2, ln_g, router, tm=512):
    T, D = x2.shape
    E = router.shape[1]
    assert GATE_TERMS * E <= LANES
    return pl.pallas_call(
        _router_kernel,
        grid=(T // tm,),
        in_specs=[pl.BlockSpec((tm, D), lambda i: (i, 0)),
                  pl.BlockSpec((1, D), lambda i: (0, 0)),
                  pl.BlockSpec((E, D), lambda i: (0, 0))],
        out_specs=[pl.BlockSpec((E, tm), lambda i: (0, i)),
                   pl.BlockSpec((tm, D + LANES), lambda i: (i, 0))],
        out_shape=[jax.ShapeDtypeStruct((E, T), F32), jax.ShapeDtypeStruct((T, D + LANES), BF16)],
        compiler_params=_params("parallel"),
        name="router",
    )(x2, ln_g.reshape(1, D).astype(F32), router.T.astype(F32))


def _cumsum_excl(mask, upper, lower_strict):
    incl = jnp.dot(mask.astype(BF16), upper, preferred_element_type=F32)
    before = jnp.dot(lower_strict, incl.astype(BF16), preferred_element_type=F32)
    return incl + before[:, CHUNK - 1:CHUNK] - mask


def _select_kernel(aff_ref, slot_ref, pos_ref, *, cap):
    aff = aff_ref[0]
    nc = aff.shape[0]
    bits = pltpu.bitcast(aff, jnp.int32)

    def count(cond):
        c = jnp.sum(jnp.where(cond, 1.0, 0.0), axis=0, keepdims=True)
        return jnp.sum(c, axis=1, keepdims=True)

    def bit_step(i, prefix):
        cand = prefix | (jnp.int32(1) << (30 - i))
        return jnp.where(count(bits >= cand) >= cap, cand, prefix)

    thr = lax.fori_loop(0, 31, bit_step, jnp.zeros((1, 1), jnp.int32))
    r = lax.broadcasted_iota(jnp.int32, (CHUNK, CHUNK), 0)
    c = lax.broadcasted_iota(jnp.int32, (CHUNK, CHUNK), 1)
    upper = (r <= c).astype(BF16)
    r = lax.broadcasted_iota(jnp.int32, (nc, nc), 0)
    c = lax.broadcasted_iota(jnp.int32, (nc, nc), 1)
    lower_strict = (c < r).astype(BF16)
    gt = bits > thr
    eq = jnp.where(bits == thr, 1.0, 0.0)
    need = cap - count(gt)
    sel = jnp.where(gt | ((eq > 0) & (_cumsum_excl(eq, upper, lower_strict) < need)), 1.0, 0.0)
    pos = _cumsum_excl(sel, upper, lower_strict).astype(jnp.int32)
    pos_ref[0] = pos
    slot_ref[0] = jnp.where(sel > 0, pos, -1)


def _select(aff_t, cap):
    E, T = aff_t.shape
    nc = T // CHUNK
    assert nc <= CHUNK and nc % 8 == 0
    spec = pl.BlockSpec((1, nc, CHUNK), lambda e: (e, 0, 0))
    slot, pos = pl.pallas_call(
        functools.partial(_select_kernel, cap=cap),
        grid=(E,), in_specs=[spec], out_specs=[spec, spec],
        out_shape=[jax.ShapeDtypeStruct((E, nc, CHUNK), jnp.int32)] * 2,
        compiler_params=_params("parallel"),
        name="select_topk",
    )(aff_t.reshape(E, nc, CHUNK))
    return slot.reshape(E, T), pos.reshape(E, T)


GATHER_STEP = 128
GATHER_W = GATHER_STEP + 16
GATHER_STACK = 8


def _gather_kernel(starts_ref, h_ref, slot_ref, xe_ref, stage_ref, extra_ref, tail_ref, sem, extra_sem, *, cap):
    b = pl.program_id(0)
    nb = pl.num_programs(0)
    E = slot_ref.shape[0]
    buf = b % 2
    h = h_ref[...]
    row = lax.broadcasted_iota(jnp.int32, (GATHER_W, 1), 0)

    @pl.when(b == 0)
    def _():
        tail_ref[...] = jnp.zeros_like(tail_ref)

    def first_row(e, blk):
        return pl.multiple_of((starts_ref[e, blk] // 16) * 16, 16)

    def copy(e, blk):
        return pltpu.make_async_copy(stage_ref.at[blk % 2, e], xe_ref.at[e, pl.ds(first_row(e, blk), GATHER_W)],
                                     sem.at[blk % 2, e])

    def onehot(e, k):
        return (slot_ref[pl.ds(e, 1), :] - (first_row(e, b) + k * GATHER_STEP) == row).astype(BF16)

    for e0 in range(0, E, GATHER_STACK):
        stacked = jnp.concatenate([onehot(e, 0) for e in range(e0, e0 + GATHER_STACK)], axis=0)
        rows = jnp.dot(stacked, h, preferred_element_type=F32).astype(BF16)
        for i in range(GATHER_STACK):
            stage_ref[buf, e0 + i] = rows[i * GATHER_W:(i + 1) * GATHER_W]

    def per_expert(e, carry):
        base = first_row(e, b)
        end = starts_ref[e, b + 1] - base
        group = pl.multiple_of((end // 16) * 16, 16)
        last = jnp.maximum(group - 1, 0) // GATHER_STEP

        def window(k):
            return jnp.dot(onehot(e, k), h, preferred_element_type=F32).astype(BF16)

        stage_ref[buf, e, pl.ds(0, 16), :] += tail_ref[e]

        @pl.when(last == 0)
        def _():
            tail_ref[e] = stage_ref[buf, e, pl.ds(group, 16), :]

        @pl.when(b > 0)
        def _():
            copy(e, b - 1).wait()

        copy(e, b).start()

        def extra(k, inner):
            extra_ref[...] = window(k)

            @pl.when(k == last)
            def _():
                tail_ref[e] = extra_ref[pl.ds(pl.multiple_of(group - k * GATHER_STEP, 16), 16), :]

            cp = pltpu.make_async_copy(extra_ref, xe_ref.at[e, pl.ds(base + k * GATHER_STEP, GATHER_W)],
                                       extra_sem.at[0])
            cp.start()
            cp.wait()
            return inner

        return lax.fori_loop(1, last + 1, extra, carry)

    lax.fori_loop(0, E, per_expert, 0)

    @pl.when(b == nb - 1)
    def _():
        extra_ref[...] = jnp.zeros_like(extra_ref)

        def drain(e, carry):
            copy(e, b).wait()
            cp = pltpu.make_async_copy(extra_ref, xe_ref.at[e, pl.ds(cap, GATHER_W)], extra_sem.at[0])
            cp.start()
            cp.wait()
            return carry
        lax.fori_loop(0, E, drain, 0)


def _gather(h, slot, starts, cap):
    T, D = h.shape
    E = slot.shape[0]
    tb = min(COMBINE_TB, T)
    rows = cap + GATHER_W
    return pl.pallas_call(
        functools.partial(_gather_kernel, cap=cap),
        grid_spec=pltpu.PrefetchScalarGridSpec(
            num_scalar_prefetch=1,
            grid=(T // tb,),
            in_specs=[pl.BlockSpec((tb, D), lambda i, st: (i, 0)),
                      pl.BlockSpec((E, tb), lambda i, st: (0, i))],
            out_specs=pl.BlockSpec(memory_space=pl.ANY),
            scratch_shapes=[pltpu.VMEM((2, E, GATHER_W, D), BF16),
                            pltpu.VMEM((GATHER_W, D), BF16),
                            pltpu.VMEM((E, 16, D), BF16),
                            pltpu.SemaphoreType.DMA((2, E)),
                            pltpu.SemaphoreType.DMA((1,))]),
        out_shape=jax.ShapeDtypeStruct((E, rows, D), BF16),
        compiler_params=_params("arbitrary"),
        name="moe_gather",
    )(starts, h, slot)


def _combine_kernel(starts_ref, x_ref, slot_ref, first_ref, g_ref, ye_ref, o_ref, win_ref, extra_ref, sem,
                    extra_sem, *, cap, final_norm):
    b = pl.program_id(0)
    nb = pl.num_programs(0)
    E = ye_ref.shape[0]
    W = COMBINE_W

    def first_row(e, blk, k):
        return (starts_ref[e, blk] // 16) * 16 + k * W

    def window_start(e, blk, k):
        return pl.multiple_of(jnp.minimum(first_row(e, blk, k), cap - W), 16)

    def copy(e, blk):
        buf = blk % 2
        return pltpu.make_async_copy(ye_ref.at[e, pl.ds(window_start(e, blk, 0), W)],
                                     win_ref.at[buf, pl.ds(e * W, W)], sem.at[buf, e])

    @pl.when(b == 0)
    def _():
        for e in range(E):
            copy(e, b).start()

    @pl.when(b + 1 < nb)
    def _():
        for e in range(E):
            copy(e, b + 1).start()

    slots = slot_ref[...]
    first = first_ref[0]
    rel = slots - jnp.minimum(first, cap - W)
    rel = jnp.where((slots >= first) & (rel >= 0) & (rel < W), rel, -1).astype(F32).astype(BF16)
    lane = lax.broadcasted_iota(jnp.int32, (E, E * W), 1)
    expert = lax.broadcasted_iota(jnp.int32, (E, E * W), 0)
    expand = (lane // W == expert).astype(BF16)
    wanted = (lax.broadcasted_iota(jnp.int32, (1, E * W), 1) % W).astype(F32)
    onehot = (jnp.dot(rel, expand, preferred_element_type=F32) == wanted).astype(BF16)

    for e in range(E):
        copy(e, b).wait()
    o_ref[...] = x_ref[...] + jnp.dot(onehot, win_ref[b % 2], preferred_element_type=F32)

    col = lax.broadcasted_iota(jnp.int32, (1, W), 1)
    expert_lane = lax.broadcasted_iota(jnp.int32, (1, E), 1)

    def per_expert(e, carry):
        n_win = (starts_ref[e, b + 1] - first_row(e, b, 0) + W - 1) // W

        def extra(k, inner):
            cp = pltpu.make_async_copy(ye_ref.at[e, pl.ds(window_start(e, b, k), W)], extra_ref, extra_sem.at[0])
            cp.start()
            cp.wait()
            slot = jnp.max(jnp.where(expert_lane == e, slots, -1), axis=1, keepdims=True)
            oh = ((slot == window_start(e, b, k) + col) & (slot >= first_row(e, b, k))).astype(BF16)
            o_ref[...] += jnp.dot(oh, extra_ref[...], preferred_element_type=F32)
            return inner

        return lax.fori_loop(1, n_win, extra, carry)

    lax.fori_loop(0, E, per_expert, 0)
    if final_norm:
        o_ref[...] = _rms(o_ref[...], g_ref[...])


def _combine(x2, slot_t, starts, ye, final_g=None):
    T, D = x2.shape
    gain = jnp.ones((1, D), F32) if final_g is None else final_g.reshape(1, D).astype(F32)
    E, cap, _ = ye.shape
    tb = min(COMBINE_TB, T)
    nb = T // tb
    assert cap >= COMBINE_W and cap % 16 == 0
    first = ((starts[:, :nb] // 16) * 16).T.reshape(nb, 1, E)
    return pl.pallas_call(
        functools.partial(_combine_kernel, cap=cap, final_norm=final_g is not None),
        grid_spec=pltpu.PrefetchScalarGridSpec(
            num_scalar_prefetch=1,
            grid=(nb,),
            in_specs=[pl.BlockSpec((tb, D), lambda i, st: (i, 0)),
                      pl.BlockSpec((tb, E), lambda i, st: (i, 0)),
                      pl.BlockSpec((1, 1, E), lambda i, st: (i, 0, 0)),
                      pl.BlockSpec((1, D), lambda i, st: (0, 0)),
                      pl.BlockSpec(memory_space=pl.ANY)],
            out_specs=pl.BlockSpec((tb, D), lambda i, st: (i, 0)),
            scratch_shapes=[pltpu.VMEM((2, E * COMBINE_W, D), BF16),
                            pltpu.VMEM((COMBINE_W, D), BF16),
                            pltpu.SemaphoreType.DMA((2, E)),
                            pltpu.SemaphoreType.DMA((1,))]),
        out_shape=jax.ShapeDtypeStruct((T, D), F32),
        compiler_params=_params("arbitrary"),
        name="moe_combine",
    )(starts, x2, slot_t, first, gain, ye)


def _t5_bucket(rel):
    half = T5_BUCKETS // 2
    max_exact = half // 2
    n = np.abs(rel)
    large = max_exact + (np.log(np.maximum(n, 1) / max_exact) / np.log(T5_MAX_DISTANCE / max_exact)
                         * (half - max_exact)).astype(np.int32)
    large = np.minimum(large, half - 1)
    return (np.where(rel > 0, half, 0) + np.where(n < max_exact, n, large)).astype(np.int32)


def _dilated_tiles(L, radius):
    tq = min(LANES, L)
    kw = min(tq + 2 * radius, L)
    return tq, kw


def _dilated_bias(t5_table, g, L, window, dilation):
    radius = window // (2 * dilation)
    tq, kw = _dilated_tiles(L, radius)
    deltas = (0, -((kw - tq) // 2), tq - kw)
    period = 2 * kw - 1
    rel = np.arange(period + 1) - (kw - 1)
    tab = t5_table[:, g * A_HEADS_PER_GROUP:(g + 1) * A_HEADS_PER_GROUP].astype(F32)
    by_rel = jnp.where((np.abs(rel) <= radius)[None], tab[_t5_bucket(rel * dilation)].T, NEG_INF)
    skew = jnp.tile(by_rel, (1, tq))[:, :tq * period].reshape(A_HEADS_PER_GROUP, tq, period)
    bias = jnp.stack([skew[:, :, d + kw - 1:d + 2 * kw - 1] for d in deltas])
    return bias.reshape(3, A_HEADS_PER_GROUP // 2, 2 * tq, kw)


def _dilated_kernel(q_ref, k_ref, v_ref, bias_ref, o_ref, lse_ref, *, L, tq, kw):
    nq = L // tq
    lane = lax.broadcasted_iota(jnp.int32, (1, LANES), 1)
    lo = lane < HEAD_DIM

    def body(i, carry):
        q0 = pl.multiple_of(i * tq, tq)
        start = jnp.clip(q0 - (kw - tq) // 2, 0, L - kw)
        start = pl.multiple_of(start, 16)
        var = jnp.where(i == 0, 0, jnp.where(i == nq - 1, 2, 1))
        n_pairs = A_HEADS_PER_GROUP // 2
        cols = [slice(pair * LANES, (pair + 1) * LANES) for pair in range(n_pairs)]
        scores = []
        for pair in range(n_pairs):
            q2 = q_ref[0, 0, pl.ds(q0, tq), cols[pair]] * (HEAD_DIM ** -0.5)
            zero = jnp.zeros_like(q2)
            qs = jnp.concatenate([jnp.where(lo, q2, zero), jnp.where(lo, zero, q2)], axis=0)
            k2 = k_ref[0, 0, pl.ds(start, kw), cols[pair]]
            scores.append(lax.dot_general(qs, k2, _NT, preferred_element_type=F32) + bias_ref[var, pair])
        probs, dens, maxs = [], [], []
        for s in scores:
            m = jnp.max(s, axis=-1, keepdims=True)
            p = jnp.exp(s - m)
            maxs.append(m)
            dens.append(jnp.sum(p, axis=-1, keepdims=True))
            probs.append(p.astype(BF16))
        for pair in range(n_pairs):
            v2 = v_ref[0, 0, pl.ds(start, kw), cols[pair]]
            o = jnp.dot(probs[pair], v2, preferred_element_type=F32) * (1.0 / dens[pair])
            lse = maxs[pair] + jnp.log(dens[pair])
            o_ref[0, 0, pl.ds(q0, tq), cols[pair]] = jnp.where(lo, o[:tq], o[tq:])
            lse_ref[0, 0, pl.ds(q0, tq), cols[pair]] = jnp.where(lo, lse[:tq], lse[tq:])
        return carry

    lax.fori_loop(0, nq, body, 0, unroll=next(u for u in (4, 2, 1) if nq % u == 0))


def _dilated_attention(qkv, bias, window):
    Bn, dilation, L, _ = qkv.shape
    radius = window // (2 * dilation)
    tq, kw = _dilated_tiles(L, radius)
    assert L % tq == 0 and kw % 16 == 0 and tq % 16 == 0

    def spec(which):
        return pl.BlockSpec((1, 1, L, A_OUT), lambda b, r: (b, r, 0, which))

    o, lse = pl.pallas_call(
        functools.partial(_dilated_kernel, L=L, tq=tq, kw=kw),
        grid=(Bn, dilation),
        in_specs=[spec(0), spec(1), spec(2), pl.BlockSpec(bias.shape, lambda b, r: (0, 0, 0, 0))],
        out_specs=[spec(0), spec(0)],
        out_shape=[jax.ShapeDtypeStruct((Bn, dilation, L, A_OUT), F32)] * 2,
        compiler_params=_params("parallel", "parallel"),
        name="dilated_attention",
    )(qkv, qkv, qkv, bias)
    return o, lse


def _merge_kernel(*refs):
    n = len(A_PATTERNS)
    o_refs, l_refs, out_ref = refs[:n], refs[n:2 * n], refs[2 * n]
    scratch = refs[2 * n + 1:]
    tm = out_ref.shape[0]

    def token_order(ref, scr, dilation):
        if dilation == 1:
            return ref[0, 0]
        tiles = A_OUT // LANES
        for r in range(dilation):
            for c in range(tiles):
                scr[c, pl.ds(r, tm // dilation, stride=dilation), :] = ref[0, r, :, c * LANES:(c + 1) * LANES]
        return jnp.concatenate([scr[c] for c in range(tiles)], axis=1)

    os_ = [token_order(o_refs[g], scratch[2 * g], d) for g, (_, d) in enumerate(A_PATTERNS)]
    ls = [token_order(l_refs[g], scratch[2 * g + 1], d) for g, (_, d) in enumerate(A_PATTERNS)]
    m = jnp.maximum(jnp.maximum(ls[0], ls[1]), ls[2])
    es = [jnp.exp(l - m) for l in ls]
    den = es[0] + es[1] + es[2]
    out = (es[0] / den) * os_[0] + (es[1] / den) * os_[1] + (es[2] / den) * os_[2]
    out_ref[...] = out.astype(out_ref.dtype)


def _merge_groups(os_, ls_, S, tm=512):
    Bn = os_[0].shape[0]
    ns = S // tm

    def spec(d):
        return pl.BlockSpec((1, d, tm // d, A_OUT), lambda i: (i // ns, 0, i % ns, 0))

    specs = [spec(d) for _, d in A_PATTERNS]
    return pl.pallas_call(
        _merge_kernel, grid=(Bn * ns,), in_specs=specs * 2,
        out_specs=pl.BlockSpec((tm, A_OUT), lambda i: (i, 0)),
        out_shape=jax.ShapeDtypeStruct((Bn * S, A_OUT), BF16),
        scratch_shapes=[pltpu.VMEM((A_OUT // LANES, tm, LANES), F32)] * (2 * len(A_PATTERNS)),
        compiler_params=_params("parallel"), name="merge_groups",
    )(*os_, *ls_)


EVEN_TOKEN_COLS = 3 * A_OUT + B_KV_RANK + LANES + B_Q_RANK


def _in_proj_even_kernel(x_ref, w_ref, g_ref, u0_ref, *rest):
    n_dil = len(A_PATTERNS) - 1
    dil_refs, scr = rest[:n_dil], rest[n_dil]
    tm = x_ref.shape[0]
    acc = jnp.dot(_rms(x_ref[...], g_ref[...]).astype(BF16), w_ref[...], preferred_element_type=F32)
    u0_ref[...] = acc[:, :EVEN_TOKEN_COLS].astype(u0_ref.dtype)
    for i, (_, d) in enumerate(A_PATTERNS[1:]):
        c0 = EVEN_TOKEN_COLS + i * 3 * A_OUT
        for c in range(3 * A_OUT // LANES):
            scr[c] = acc[:, c0 + c * LANES:c0 + (c + 1) * LANES]
        for r in range(d):
            for c in range(3 * A_OUT // LANES):
                piece = scr[c, pl.ds(r, tm // d, stride=d), :]
                dil_refs[i][0, r, :, c * LANES:(c + 1) * LANES] = piece.astype(dil_refs[i].dtype)


def _in_proj_even(x2, w, gain, Bn, S, tm=512):
    T, K = x2.shape
    ns = S // tm
    dils = [d for _, d in A_PATTERNS[1:]]
    assert S % tm == 0 and all(tm % (16 * d) == 0 for d in dils)
    return pl.pallas_call(
        _in_proj_even_kernel,
        grid=(T // tm,),
        in_specs=[pl.BlockSpec((tm, K), lambda i: (i, 0)),
                  pl.BlockSpec(w.shape, lambda i: (0, 0)),
                  pl.BlockSpec((1, K), lambda i: (0, 0))],
        out_specs=[pl.BlockSpec((tm, EVEN_TOKEN_COLS), lambda i: (i, 0))]
        + [pl.BlockSpec((1, d, tm // d, 3 * A_OUT), lambda i: (i // ns, 0, i % ns, 0)) for d in dils],
        out_shape=[jax.ShapeDtypeStruct((T, EVEN_TOKEN_COLS), BF16)]
        + [jax.ShapeDtypeStruct((Bn, d, S // d, 3 * A_OUT), BF16) for d in dils],
        scratch_shapes=[pltpu.VMEM((3 * A_OUT // LANES, tm, LANES), F32)],
        compiler_params=_params("parallel"),
        name="in_proj_even",
    )(x2, w, gain.reshape(1, K).astype(F32))


def _rope_tables(S):
    inv = 1.0 / (ROPE_THETA ** (np.arange(0, B_ROPE, 2, dtype=np.float32) / B_ROPE))
    ang = np.arange(S, dtype=np.float32)[:, None] * inv[None]
    cos, sin = np.cos(ang), np.sin(ang)
    half = B_ROPE // 2
    c = np.ones((S, LANES), np.float32)
    c[:, B_NOPE:B_NOPE + half] = cos
    c[:, B_NOPE + half:B_NOPE + B_ROPE] = cos
    s_lo = np.zeros((S, LANES), np.float32)
    s_lo[:, B_NOPE:B_NOPE + half] = -sin
    s_hi = np.zeros((S, LANES), np.float32)
    s_hi[:, B_NOPE + half:B_NOPE + B_ROPE] = sin
    return jnp.asarray(c), jnp.asarray(s_lo), jnp.asarray(s_hi)


def _rope(x, c, s_lo, s_hi):
    half = B_ROPE // 2
    return x * c + pltpu.roll(x, LANES - half, 1) * s_lo + pltpu.roll(x, half, 1) * s_hi


def _mla_prep_kernel(cq_ref, ckv_ref, kr_ref, gq_ref, gkv_ref, wq_ref, wk_ref, wv_ref,
                     c_ref, slo_ref, shi_ref, q_ref, k_ref, v_ref):
    c, s_lo, s_hi = c_ref[...], slo_ref[...], shi_ref[...]
    scale = (B_NOPE + B_ROPE) ** -0.5 * LOG2_E
    xq = _rms(cq_ref[...], gq_ref[...]).astype(BF16)
    q = jnp.dot(xq, wq_ref[...], preferred_element_type=F32)
    xkv = _rms(ckv_ref[...], gkv_ref[...]).astype(BF16)
    k = jnp.dot(xkv, wk_ref[...], preferred_element_type=F32)
    vt = lax.dot_general(wv_ref[...], xkv, _NT, preferred_element_type=F32)
    ones_row = lax.broadcasted_iota(jnp.int32, vt.shape, 0) % LANES == B_V
    v_ref[0, 0] = jnp.where(ones_row, 1.0, vt).astype(v_ref.dtype)
    k_rope = _rope(kr_ref[...].astype(F32), c, s_lo, s_hi)
    for h in range(B_HEADS):
        cs = slice(h * LANES, (h + 1) * LANES)
        q_ref[:, cs] = (_rope(q[:, cs], c, s_lo, s_hi) * scale).astype(q_ref.dtype)
        k_ref[:, cs] = (k[:, cs] + k_rope).astype(k_ref.dtype)


MLA_KEY_CHUNK = 1024


def _mla_prep(u3, g_qa, g_kva, wq, wk, wv):
    Bn, S, C = u3.shape
    ts = min(MLA_KEY_CHUNK, S)
    tabs = _rope_tables(S)
    ns = S // ts
    u2 = u3.reshape(Bn * S, C)
    row = lambda b, i: b * ns + i
    a_cols = 3 * A_OUT
    off_kv, off_kr, off_q = a_cols // B_KV_RANK, (a_cols + B_KV_RANK) // LANES, (a_cols + B_KV_RANK + LANES) // B_Q_RANK
    full = lambda a: pl.BlockSpec(a.shape, lambda b, i: (0, 0))
    tab_spec = pl.BlockSpec((ts, LANES), lambda b, i: (i, 0))
    hw = B_HEADS * LANES
    return pl.pallas_call(
        _mla_prep_kernel,
        grid=(Bn, ns),
        in_specs=[pl.BlockSpec((ts, B_Q_RANK), lambda b, i: (row(b, i), off_q)),
                  pl.BlockSpec((ts, B_KV_RANK), lambda b, i: (row(b, i), off_kv)),
                  pl.BlockSpec((ts, LANES), lambda b, i: (row(b, i), off_kr)),
                  full(g_qa), full(g_kva), full(wq), full(wk), full(wv),
                  tab_spec, tab_spec, tab_spec],
        out_specs=[pl.BlockSpec((ts, hw), lambda b, i: (row(b, i), 0)),
                   pl.BlockSpec((ts, hw), lambda b, i: (row(b, i), 0)),
                   pl.BlockSpec((1, 1, hw, ts), lambda b, i: (b, i, 0, 0))],
        out_shape=[jax.ShapeDtypeStruct((Bn * S, hw), BF16),
                   jax.ShapeDtypeStruct((Bn * S, hw), BF16),
                   jax.ShapeDtypeStruct((Bn, ns, hw, ts), BF16)],
        compiler_params=_params("parallel", "parallel"),
        name="mla_prep",
    )(u2, u2, u2, g_qa, g_kva, wq, wk, wv, *tabs)


def _mla_kernel(q_ref, k_ref, v_ref, o_ref):
    nk, _, tk = v_ref.shape[1:]
    tq = q_ref.shape[1]
    lane = lax.broadcasted_iota(jnp.int32, (1, LANES), 1)
    qs = [q_ref[0, :, sub * LANES:(sub + 1) * LANES] for sub in range(2)]

    def body(j, carry):
        k0 = pl.multiple_of(j * tk, tk)
        scores = [lax.dot_general(k_ref[0, pl.ds(k0, tk), sub * LANES:(sub + 1) * LANES], qs[sub], _NT,
                                  preferred_element_type=F32) for sub in range(2)]
        new = []
        for sub in range(2):
            m, acc = carry[sub]
            s = scores[sub]
            m_new = jnp.maximum(m, jnp.max(s, axis=0, keepdims=True))
            p = jnp.exp2(s - m_new).astype(BF16)
            vt = v_ref[0, j, sub * LANES:(sub + 1) * LANES, :]
            acc = jnp.exp2(m - m_new) * acc + jnp.dot(vt, p, preferred_element_type=F32)
            new.append((m_new, acc))
        return tuple(new)

    init = tuple((jnp.full((1, tq), NEG_INF, F32), jnp.zeros((LANES, tq), F32)) for _ in range(2))
    res = lax.fori_loop(0, nk, body, init, unroll=True)
    outs = [(acc * (1.0 / acc[B_V:B_V + 1])).T for (_, acc) in res]
    o_ref[0] = jnp.where(lane < B_V, outs[0], pltpu.roll(outs[1], B_V, 1)).astype(o_ref.dtype)


def _mla_attention(q, k, vt, Bn, S, tq=1024):
    tq = min(tq, S)
    nk, _, tk = vt.shape[1:]
    hw = B_HEADS * LANES
    q3, k3 = q.reshape(Bn, S, hw), k.reshape(Bn, S, hw)
    o = pl.pallas_call(
        _mla_kernel,
        grid=(Bn, B_HEADS // 2, S // tq),
        in_specs=[pl.BlockSpec((1, tq, 2 * LANES), lambda b, h, i: (b, i, h)),
                  pl.BlockSpec((1, S, 2 * LANES), lambda b, h, i: (b, 0, h)),
                  pl.BlockSpec((1, nk, 2 * LANES, tk), lambda b, h, i: (b, 0, h, 0))],
        out_specs=pl.BlockSpec((1, tq, LANES), lambda b, h, i: (b, i, h)),
        out_shape=jax.ShapeDtypeStruct((Bn, S, B_OUT), BF16),
        compiler_params=_params("parallel", "parallel", "parallel"),
        name="mla_attention",
    )(q3, k3, vt)
    return o.reshape(Bn * S, B_OUT)


def _natten_bias(rpb):
    W = GRID_W
    c = np.arange(W)[:, None]
    kc = np.arange(W)[None, :]
    wstart = np.clip(c - C_WIN_COLS // 2, 0, W - C_WIN_COLS)
    col_valid = (kc >= wstart) & (kc < wstart + C_WIN_COLS)
    dcol = np.clip(kc - c + C_WIN_COLS - 1, 0, 2 * C_WIN_COLS - 2)
    off = np.arange(C_WIN_ROWS)[:, None]
    j = np.arange(C_WIN_ROWS)[None, :]
    drow = j - off + C_WIN_ROWS - 1
    b = rpb.astype(F32)[:, drow][:, :, :, dcol]
    b = jnp.where(col_valid[None, None, None], b, NEG_INF)
    b = jnp.transpose(b, (1, 0, 3, 2, 4))
    return b.reshape(C_WIN_ROWS, C_HEADS // 2, 2 * W, C_WIN_ROWS * W)


NATTEN_ROWS = 4


def _natten_kernel(q_ref, k_ref, v_ref, *rest, rows):
    bias_refs, o_ref = rest[:NATTEN_ROWS], rest[NATTEN_ROWS]
    nk = C_WIN_ROWS * GRID_W
    lane = lax.broadcasted_iota(jnp.int32, (1, LANES), 1)
    lo = lane < HEAD_DIM
    n_pairs = C_HEADS // 2
    cols = [slice(pair * LANES, (pair + 1) * LANES) for pair in range(n_pairs)]
    for j in range(NATTEN_ROWS):
        r = pl.program_id(1) * NATTEN_ROWS + j
        r0 = jnp.clip(r - C_WIN_ROWS // 2, 0, rows - C_WIN_ROWS)
        k0 = pl.multiple_of(r0 * GRID_W, GRID_W)
        q_rows = slice(j * GRID_W, (j + 1) * GRID_W)
        scores = []
        for pair in range(n_pairs):
            q2 = q_ref[0, q_rows, cols[pair]] * (HEAD_DIM ** -0.5)
            zero = jnp.zeros_like(q2)
            qs = jnp.concatenate([jnp.where(lo, q2, zero), jnp.where(lo, zero, q2)], axis=0)
            k2 = k_ref[0, pl.ds(k0, nk), cols[pair]]
            scores.append(lax.dot_general(qs, k2, _NT, preferred_element_type=F32) + bias_refs[j][0, pair])
        probs, dens = [], []
        for s in scores:
            p = jnp.exp(s - jnp.max(s, axis=-1, keepdims=True))
            dens.append(jnp.sum(p, axis=-1, keepdims=True))
            probs.append(p.astype(BF16))
        for pair in range(n_pairs):
            v2 = v_ref[0, pl.ds(k0, nk), cols[pair]]
            o = jnp.dot(probs[pair], v2, preferred_element_type=F32) * (1.0 / dens[pair])
            o_ref[0, q_rows, cols[pair]] = jnp.where(lo, o[:GRID_W], o[GRID_W:]).astype(o_ref.dtype)


def _natten(u3, bias):
    Bn, S, _ = u3.shape
    rows = S // GRID_W
    assert rows >= C_WIN_ROWS and rows % NATTEN_ROWS == 0

    def bias_spec(j):
        def variant(b, i):
            r = i * NATTEN_ROWS + j
            return (r - jnp.clip(r - C_WIN_ROWS // 2, 0, rows - C_WIN_ROWS), 0, 0, 0)
        return pl.BlockSpec((1, C_HEADS // 2, 2 * GRID_W, C_WIN_ROWS * GRID_W), variant)

    q_spec = pl.BlockSpec((1, NATTEN_ROWS * GRID_W, C_OUT), lambda b, i: (b, i, 0))
    o = pl.pallas_call(
        functools.partial(_natten_kernel, rows=rows),
        grid=(Bn, rows // NATTEN_ROWS),
        in_specs=[q_spec,
                  pl.BlockSpec((1, S, C_OUT), lambda b, i: (b, 0, 1), pipeline_mode=pl.Buffered(1)),
                  pl.BlockSpec((1, S, C_OUT), lambda b, i: (b, 0, 2), pipeline_mode=pl.Buffered(1))]
        + [bias_spec(j) for j in range(NATTEN_ROWS)],
        out_specs=q_spec,
        out_shape=jax.ShapeDtypeStruct((Bn, S, C_OUT), BF16),
        compiler_params=_params("parallel", "arbitrary"),
        name="natten",
    )(u3, u3, u3, *([bias] * NATTEN_ROWS))
    return o.reshape(Bn * S, C_OUT)


def _moe(x2, ln_g, router, wg, wu, wd, layer, final_g=None):
    T, D = x2.shape
    cap = CAPACITY_FACTOR * T // N_EXPERTS
    aff_t, h = _router(x2, ln_g, router)
    slot, pos = _select(aff_t, cap)
    tb = min(COMBINE_TB, T)
    starts = jnp.concatenate([pos[:, ::tb], jnp.full((N_EXPERTS, 1), cap, jnp.int32)], axis=1)
    xe = _gather(h, slot, starts, cap)
    ye = _expert_ffn(xe, cap, wg, wu, wd, layer)
    return _combine(x2, slot.T, starts, ye, final_g)


def _trunk(x, p):
    Bn, S, D = x.shape
    T = Bn * S
    x2 = x.reshape(T, D)
    u0, *u_dil = _in_proj_even(x2, p["w_in_even"], p["ln_mix_g"][0], Bn, S)
    u3 = u0.reshape(Bn, S, EVEN_TOKEN_COLS)
    os_, ls_ = [], []
    for g, (window, dilation) in enumerate(A_PATTERNS):
        bias = _dilated_bias(p["t5_table"], g, S // dilation, window, dilation)
        qkv = u3.reshape(Bn, 1, S, EVEN_TOKEN_COLS) if g == 0 else u_dil[g - 1]
        o, l = _dilated_attention(qkv, bias, window)
        os_.append(o)
        ls_.append(l)
    a = _merge_groups(os_, ls_, S)
    q, k, v = _mla_prep(u3, p["g_qa"], p["g_kva"], p["wq"], p["wk"], p["wv"])
    b = _mla_attention(q, k, v, Bn, S)
    x2 = _matmul([a, b], [p["w_out_even"][:A_OUT], p["w_out_even"][A_OUT:]], res=x2, out_dtype=F32)
    x2 = _moe(x2, p["ln_ffn_g"][0], p["router"][0], p["w_gate"], p["w_up"], p["w_down"], 0)
    u = _matmul(x2, p["w_in_odd"], gain=p["ln_mix_g"][1], out_dtype=BF16)
    c = _natten(u.reshape(Bn, S, C_IN), p["natten_bias"])
    x2 = _matmul(c, p["w_out_odd"], res=x2, out_dtype=F32)
    x2 = _moe(x2, p["ln_ffn_g"][1], p["router"][1], p["w_gate"], p["w_up"], p["w_down"], 1,
              final_g=p["final_g"])
    return x2.reshape(Bn, S, D)


def _head_tiles(w, n_heads, per_head, keep):
    K = w.shape[0]
    w = w.reshape(K, n_heads, per_head)[:, :, :keep]
    return jnp.pad(w, ((0, 0), (0, 0), (0, LANES - keep))).reshape(K, n_heads * LANES)


def _prepare(ln_mix_g, w_in_even, g_qa, w_qb, g_kva, w_kvb, t5_table, w_out_even,
             w_in_odd, rpb, w_out_odd, ln_ffn_g, router, w_gate, w_up, w_down, final_g):
    w_in = w_in_even[0]
    w_a, w_cq = w_in[:, :A_IN], w_in[:, A_IN:A_IN + B_Q_RANK]
    w_ckv = w_in[:, A_IN + B_Q_RANK:A_IN + B_Q_RANK + B_KV_RANK]
    w_kr = w_in[:, A_IN + B_Q_RANK + B_KV_RANK:]
    w_kr_tile = jnp.pad(w_kr, ((0, 0), (B_NOPE, LANES - B_NOPE - B_ROPE)))
    w_a = w_a.reshape(-1, 3, A_GROUPS, A_OUT)
    groups = [w_a[:, :, g].reshape(-1, 3 * A_OUT) for g in range(A_GROUPS)]
    w_even = jnp.concatenate([groups[0], w_ckv, w_kr_tile, w_cq] + groups[1:], axis=1)
    w_kv = w_kvb[0].reshape(B_KV_RANK, B_HEADS, B_NOPE + B_V)
    return dict(
        ln_mix_g=ln_mix_g, ln_ffn_g=ln_ffn_g, final_g=final_g, t5_table=t5_table,
        w_in_even=w_even.astype(BF16),
        g_qa=g_qa[0].reshape(1, -1).astype(F32), g_kva=g_kva[0].reshape(1, -1).astype(F32),
        wq=_head_tiles(w_qb[0], B_HEADS, B_NOPE + B_ROPE, B_NOPE + B_ROPE).astype(BF16),
        wk=_head_tiles(w_kvb[0], B_HEADS, B_NOPE + B_V, B_NOPE).astype(BF16),
        wv=_head_tiles(w_kv[:, :, B_NOPE:].reshape(B_KV_RANK, B_OUT), B_HEADS, B_V, B_V).T.astype(BF16),
        w_out_even=w_out_even[0].astype(BF16), w_in_odd=w_in_odd[0].astype(BF16),
        natten_bias=_natten_bias(rpb[0]), w_out_odd=w_out_odd[0].astype(BF16), router=router,
        w_gate=w_gate, w_up=w_up, w_down=w_down,
    )


def kernel(x_prompt, x_sample, ln_mix_g, w_in_even, g_qa, w_qb, g_kva, w_kvb, t5_table, w_out_even,
           w_in_odd, rpb, w_out_odd, ln_ffn_g, router, w_gate, w_up, w_down, final_g):
    p = _prepare(ln_mix_g, w_in_even, g_qa, w_qb, g_kva, w_kvb, t5_table, w_out_even,
                 w_in_odd, rpb, w_out_odd, ln_ffn_g, router, w_gate, w_up, w_down, final_g)
    return _trunk(x_prompt, p), _trunk(x_sample, p)
```

```python
import functools

import jax
import jax.numpy as jnp
import numpy as np
from jax import lax
from jax.experimental import pallas as pl
from jax.experimental.pallas import tpu as pltpu

D_MODEL = 1024
RMS_EPS = 1e-6
HEAD_DIM = 64
NEG_INF = -1e30
A_PATTERNS = ((128, 1), (512, 4), (2048, 16))
A_GROUPS = 3
A_HEADS_PER_GROUP = 4
A_HEADS = A_GROUPS * A_HEADS_PER_GROUP
A_IN = 3 * A_HEADS * HEAD_DIM
A_OUT = A_HEADS_PER_GROUP * HEAD_DIM
T5_BUCKETS = 32
T5_MAX_DISTANCE = 1024
B_HEADS = 12
B_Q_RANK = 384
B_KV_RANK = 256
B_NOPE = 64
B_ROPE = 32
B_V = 64
B_OUT = B_HEADS * B_V
ROPE_THETA = 10000.0
GRID_W = 64
C_HEADS = 16
C_WIN_ROWS = 8
C_WIN_COLS = 16
C_IN = 3 * C_HEADS * HEAD_DIM
C_OUT = C_HEADS * HEAD_DIM
N_EXPERTS = 16
CAPACITY_FACTOR = 2

LANES = 128
VMEM_LIMIT_BYTES = 48 * 1024 * 1024

F32 = jnp.float32
BF16 = jnp.bfloat16
_NT = (((1,), (1,)), ((), ()))
LOG2_E = 1.4426950408889634


def _params(*sem):
    return pltpu.CompilerParams(dimension_semantics=sem, vmem_limit_bytes=VMEM_LIMIT_BYTES)


def _rms(x, g):
    xf = x.astype(F32)
    return xf * lax.rsqrt(jnp.mean(xf * xf, axis=-1, keepdims=True) + RMS_EPS) * g


def _mm_kernel(*refs, n_in, has_gain, has_res, precise):
    xs, ws = refs[:n_in], refs[n_in:2 * n_in]
    pos = 2 * n_in
    g_ref = res_ref = None
    if has_gain:
        g_ref = refs[pos]
        pos += 1
    if has_res:
        res_ref = refs[pos]
        pos += 1
    o_ref = refs[pos]
    acc = None
    for x_ref, w_ref in zip(xs, ws):
        x = x_ref[...]
        if has_gain:
            x = _rms(x, g_ref[...])
        if precise:
            part = jnp.dot(x.astype(F32), w_ref[...], preferred_element_type=F32,
                           precision=lax.Precision.HIGHEST)
        else:
            part = jnp.dot(x.astype(BF16), w_ref[...], preferred_element_type=F32)
        acc = part if acc is None else acc + part
    if has_res:
        acc = acc + res_ref[...]
    o_ref[...] = acc.astype(o_ref.dtype)


def _matmul(xs, ws, gain=None, res=None, out_dtype=F32, tm=512, precise=False):
    if not isinstance(xs, (list, tuple)):
        xs, ws = [xs], [ws]
    M = xs[0].shape[0]
    N = ws[0].shape[1]
    assert M % tm == 0
    in_specs = [pl.BlockSpec((tm, x.shape[1]), lambda i: (i, 0)) for x in xs]
    in_specs += [pl.BlockSpec(w.shape, lambda i: (0, 0)) for w in ws]
    args = list(xs) + list(ws)
    if gain is not None:
        assert len(xs) == 1
        in_specs.append(pl.BlockSpec((1, gain.shape[-1]), lambda i: (0, 0)))
        args.append(gain.reshape(1, -1).astype(F32))
    if res is not None:
        in_specs.append(pl.BlockSpec((tm, N), lambda i: (i, 0)))
        args.append(res)
    return pl.pallas_call(
        functools.partial(_mm_kernel, n_in=len(xs), has_gain=gain is not None,
                          has_res=res is not None, precise=precise),
        grid=(M // tm,),
        in_specs=in_specs,
        out_specs=pl.BlockSpec((tm, N), lambda i: (i, 0)),
        out_shape=jax.ShapeDtypeStruct((M, N), out_dtype),
        compiler_params=_params("parallel"),
        name="fused_matmul",
    )(*args)


def _ffn_kernel(x_ref, wg32_ref, wu32_ref, wd32_ref, o_ref, wg_ref, wu_ref, wd_ref):
    D = wg_ref.shape[1]

    @pl.when(pl.program_id(1) == 0)
    def _():
        wg_ref[0] = wg32_ref[0, 0].astype(BF16)
        wu_ref[0] = wu32_ref[0, 0].astype(BF16)
        wd_ref[0] = wd32_ref[0, 0].astype(BF16)

    x = x_ref[0, :, :D]
    lane = lax.broadcasted_iota(jnp.int32, (1, LANES), 1)
    mine = lane // GATE_TERMS == pl.program_id(0)
    gate = jnp.sum(jnp.where(mine, x_ref[0, :, D:].astype(F32), 0.0), axis=1, keepdims=True)
    g = jnp.dot(x, wg_ref[0], preferred_element_type=F32)
    u = jnp.dot(x, wu_ref[0], preferred_element_type=F32)
    h = (g * jax.nn.sigmoid(g)) * u
    y = jnp.dot(h.astype(BF16), wd_ref[0], preferred_element_type=F32)
    o_ref[0] = (y * gate).astype(o_ref.dtype)


def _expert_ffn(xe, cap, wg, wu, wd, layer, tm=512):
    E = xe.shape[0]
    D, FF = wg.shape[2], wg.shape[3]
    assert cap % tm == 0
    return pl.pallas_call(
        _ffn_kernel,
        grid=(E, cap // tm),
        in_specs=[pl.BlockSpec((1, tm, D + LANES), lambda e, i: (e, i, 0)),
                  pl.BlockSpec((1, 1, D, FF), lambda e, i: (layer, e, 0, 0)),
                  pl.BlockSpec((1, 1, D, FF), lambda e, i: (layer, e, 0, 0)),
                  pl.BlockSpec((1, 1, FF, D), lambda e, i: (layer, e, 0, 0))],
        out_specs=pl.BlockSpec((1, tm, D), lambda e, i: (e, i, 0)),
        out_shape=jax.ShapeDtypeStruct((E, cap, D), BF16),
        scratch_shapes=[pltpu.VMEM((1, D, FF), BF16), pltpu.VMEM((1, D, FF), BF16), pltpu.VMEM((1, FF, D), BF16)],
        compiler_params=_params("parallel", "arbitrary"),
        name="expert_ffn",
    )(xe, wg, wu, wd)


CHUNK = 256
COMBINE_TB = 512
COMBINE_W = 112


GATE_TERMS = 3


def _router_kernel(x_ref, g_ref, rt_ref, aff_ref, h_ref):
    D = x_ref.shape[1]
    E = rt_ref.shape[0]
    xn = _rms(x_ref[...], g_ref[...])
    logits = lax.dot_general(rt_ref[...], xn, _NT, preferred_element_type=F32,
                             precision=lax.Precision.HIGHEST)
    e = jnp.exp(logits - jnp.max(logits, axis=0, keepdims=True))
    aff = e / jnp.sum(e, axis=0, keepdims=True)
    aff_ref[...] = aff
    lane = lax.broadcasted_iota(jnp.int32, (E, LANES), 1)
    expert = lax.broadcasted_iota(jnp.int32, (E, LANES), 0)
    rest = aff
    gates = jnp.zeros((x_ref.shape[0], LANES), F32)
    for j in range(GATE_TERMS):
        term = rest.astype(BF16)
        rest = rest - term.astype(F32)
        place = (lane == GATE_TERMS * expert + j).astype(BF16)
        gates = gates + lax.dot_general(term, place, (((0,), (0,)), ((), ())), preferred_element_type=F32)
    h_ref[:, :D] = xn.astype(h_ref.dtype)
    h_ref[:, D:] = gates.astype(h_ref.dtype)


def _router(x2, ln_g, router, tm=512):
    T, D = x2.shape
    E = router.shape[1]
    assert GATE_TERMS * E <= LANES
    return pl.pallas_call(
        _router_kernel,
        grid=(T // tm,),
        in_specs=[pl.BlockSpec((tm, D), lambda i: (i, 0)),
                  pl.BlockSpec((1, D), lambda i: (0, 0)),
                  pl.BlockSpec((E, D), lambda i: (0, 0))],
        out_specs=[pl.BlockSpec((E, tm), lambda i: (0, i)),
                   pl.BlockSpec((tm, D + LANES), lambda i: (i, 0))],
        out_shape=[jax.ShapeDtypeStruct((E, T), F32), jax.ShapeDtypeStruct((T, D + LANES), BF16)],
        compiler_params=_params("parallel"),
        name="router",
    )(x2, ln_g.reshape(1, D).astype(F32), router.T.astype(F32))


def _cumsum_excl(mask, upper, lower_strict):
    incl = jnp.dot(mask.astype(BF16), upper, preferred_element_type=F32)
    before = jnp.dot(lower_strict, incl.astype(BF16), preferred_element_type=F32)
    return incl + before[:, CHUNK - 1:CHUNK] - mask


def _select_kernel(aff_ref, slot_ref, pos_ref, *, cap):
    aff = aff_ref[0]
    nc = aff.shape[0]
    bits = pltpu.bitcast(aff, jnp.int32)

    def count(cond):
        c = jnp.sum(jnp.where(cond, 1.0, 0.0), axis=0, keepdims=True)
        return jnp.sum(c, axis=1, keepdims=True)

    def bit_step(i, prefix):
        cand = prefix | (jnp.int32(1) << (30 - i))
        return jnp.where(count(bits >= cand) >= cap, cand, prefix)

    thr = lax.fori_loop(0, 31, bit_step, jnp.zeros((1, 1), jnp.int32))
    r = lax.broadcasted_iota(jnp.int32, (CHUNK, CHUNK), 0)
    c = lax.broadcasted_iota(jnp.int32, (CHUNK, CHUNK), 1)
    upper = (r <= c).astype(BF16)
    r = lax.broadcasted_iota(jnp.int32, (nc, nc), 0)
    c = lax.broadcasted_iota(jnp.int32, (nc, nc), 1)
    lower_strict = (c < r).astype(BF16)
    gt = bits > thr
    eq = jnp.where(bits == thr, 1.0, 0.0)
    need = cap - count(gt)
    sel = jnp.where(gt | ((eq > 0) & (_cumsum_excl(eq, upper, lower_strict) < need)), 1.0, 0.0)
    pos = _cumsum_excl(sel, upper, lower_strict).astype(jnp.int32)
    pos_ref[0] = pos
    slot_ref[0] = jnp.where(sel > 0, pos, -1)


def _select(aff_t, cap):
    E, T = aff_t.shape
    nc = T // CHUNK
    assert nc <= CHUNK and nc % 8 == 0
    spec = pl.BlockSpec((1, nc, CHUNK), lambda e: (e, 0, 0))
    slot, pos = pl.pallas_call(
        functools.partial(_select_kernel, cap=cap),
        grid=(E,), in_specs=[spec], out_specs=[spec, spec],
        out_shape=[jax.ShapeDtypeStruct((E, nc, CHUNK), jnp.int32)] * 2,
        compiler_params=_params("parallel"),
        name="select_topk",
    )(aff_t.reshape(E, nc, CHUNK))
    return slot.reshape(E, T), pos.reshape(E, T)


GATHER_STEP = 96
GATHER_W = GATHER_STEP + 16
GATHER_STACK = 8


def _gather_kernel(starts_ref, h_ref, slot_ref, xe_ref, stage_ref, extra_ref, tail_ref, sem, extra_sem, *, cap):
    b = pl.program_id(0)
    nb = pl.num_programs(0)
    E = slot_ref.shape[0]
    buf = b % 2
    h = h_ref[...]
    row = lax.broadcasted_iota(jnp.int32, (GATHER_W, 1), 0)

    @pl.when(b == 0)
    def _():
        tail_ref[...] = jnp.zeros_like(tail_ref)

    def first_row(e, blk):
        return pl.multiple_of((starts_ref[e, blk] // 16) * 16, 16)

    def copy(e, blk):
        return pltpu.make_async_copy(stage_ref.at[blk % 2, e], xe_ref.at[e, pl.ds(first_row(e, blk), GATHER_W)],
                                     sem.at[blk % 2, e])

    def onehot(e, k):
        return (slot_ref[pl.ds(e, 1), :] - (first_row(e, b) + k * GATHER_STEP) == row).astype(BF16)

    for e0 in range(0, E, GATHER_STACK):
        stacked = jnp.concatenate([onehot(e, 0) for e in range(e0, e0 + GATHER_STACK)], axis=0)
        rows = jnp.dot(stacked, h, preferred_element_type=F32).astype(BF16)
        for i in range(GATHER_STACK):
            stage_ref[buf, e0 + i] = rows[i * GATHER_W:(i + 1) * GATHER_W]

    def per_expert(e, carry):
        base = first_row(e, b)
        end = starts_ref[e, b + 1] - base
        group = pl.multiple_of((end // 16) * 16, 16)
        last = jnp.maximum(group - 1, 0) // GATHER_STEP

        def window(k):
            return jnp.dot(onehot(e, k), h, preferred_element_type=F32).astype(BF16)

        stage_ref[buf, e, pl.ds(0, 16), :] += tail_ref[e]

        @pl.when(last == 0)
        def _():
            tail_ref[e] = stage_ref[buf, e, pl.ds(group, 16), :]

        @pl.when(b > 0)
        def _():
            copy(e, b - 1).wait()

        copy(e, b).start()

        def extra(k, inner):
            extra_ref[...] = window(k)

            @pl.when(k == last)
            def _():
                tail_ref[e] = extra_ref[pl.ds(pl.multiple_of(group - k * GATHER_STEP, 16), 16), :]

            cp = pltpu.make_async_copy(extra_ref, xe_ref.at[e, pl.ds(base + k * GATHER_STEP, GATHER_W)],
                                       extra_sem.at[0])
            cp.start()
            cp.wait()
            return inner

        return lax.fori_loop(1, last + 1, extra, carry)

    lax.fori_loop(0, E, per_expert, 0)

    @pl.when(b == nb - 1)
    def _():
        extra_ref[...] = jnp.zeros_like(extra_ref)

        def drain(e, carry):
            copy(e, b).wait()
            cp = pltpu.make_async_copy(extra_ref, xe_ref.at[e, pl.ds(cap, GATHER_W)], extra_sem.at[0])
            cp.start()
            cp.wait()
            return carry
        lax.fori_loop(0, E, drain, 0)


def _gather(h, slot, starts, cap):
    T, D = h.shape
    E = slot.shape[0]
    tb = min(COMBINE_TB, T)
    rows = cap + GATHER_W
    return pl.pallas_call(
        functools.partial(_gather_kernel, cap=cap),
        grid_spec=pltpu.PrefetchScalarGridSpec(
            num_scalar_prefetch=1,
            grid=(T // tb,),
            in_specs=[pl.BlockSpec((tb, D), lambda i, st: (i, 0)),
                      pl.BlockSpec((E, tb), lambda i, st: (0, i))],
            out_specs=pl.BlockSpec(memory_space=pl.ANY),
            scratch_shapes=[pltpu.VMEM((2, E, GATHER_W, D), BF16),
                            pltpu.VMEM((GATHER_W, D), BF16),
                            pltpu.VMEM((E, 16, D), BF16),
                            pltpu.SemaphoreType.DMA((2, E)),
                            pltpu.SemaphoreType.DMA((1,))]),
        out_shape=jax.ShapeDtypeStruct((E, rows, D), BF16),
        compiler_params=_params("arbitrary"),
        name="moe_gather",
    )(starts, h, slot)


def _combine_kernel(starts_ref, x_ref, slot_ref, first_ref, g_ref, ye_ref, o_ref, win_ref, extra_ref, sem,
                    extra_sem, *, cap, final_norm):
    b = pl.program_id(0)
    nb = pl.num_programs(0)
    E = ye_ref.shape[0]
    W = COMBINE_W

    def first_row(e, blk, k):
        return (starts_ref[e, blk] // 16) * 16 + k * W

    def window_start(e, blk, k):
        return pl.multiple_of(jnp.minimum(first_row(e, blk, k), cap - W), 16)

    def copy(e, blk):
        buf = blk % 2
        return pltpu.make_async_copy(ye_ref.at[e, pl.ds(window_start(e, blk, 0), W)],
                                     win_ref.at[buf, pl.ds(e * W, W)], sem.at[buf, e])

    @pl.when(b == 0)
    def _():
        for e in range(E):
            copy(e, b).start()

    @pl.when(b + 1 < nb)
    def _():
        for e in range(E):
            copy(e, b + 1).start()

    slots = slot_ref[...]
    first = first_ref[0]
    rel = slots - jnp.minimum(first, cap - W)
    rel = jnp.where((slots >= first) & (rel >= 0) & (rel < W), rel, -1).astype(F32).astype(BF16)
    lane = lax.broadcasted_iota(jnp.int32, (E, E * W), 1)
    expert = lax.broadcasted_iota(jnp.int32, (E, E * W), 0)
    expand = (lane // W == expert).astype(BF16)
    wanted = (lax.broadcasted_iota(jnp.int32, (1, E * W), 1) % W).astype(F32)
    onehot = (jnp.dot(rel, expand, preferred_element_type=F32) == wanted).astype(BF16)

    for e in range(E):
        copy(e, b).wait()
    o_ref[...] = x_ref[...] + jnp.dot(onehot, win_ref[b % 2], preferred_element_type=F32)

    col = lax.broadcasted_iota(jnp.int32, (1, W), 1)
    expert_lane = lax.broadcasted_iota(jnp.int32, (1, E), 1)

    def per_expert(e, carry):
        n_win = (starts_ref[e, b + 1] - first_row(e, b, 0) + W - 1) // W

        def extra(k, inner):
            cp = pltpu.make_async_copy(ye_ref.at[e, pl.ds(window_start(e, b, k), W)], extra_ref, extra_sem.at[0])
            cp.start()
            cp.wait()
            slot = jnp.max(jnp.where(expert_lane == e, slots, -1), axis=1, keepdims=True)
            oh = ((slot == window_start(e, b, k) + col) & (slot >= first_row(e, b, k))).astype(BF16)
            o_ref[...] += jnp.dot(oh, extra_ref[...], preferred_element_type=F32)
            return inner

        return lax.fori_loop(1, n_win, extra, carry)

    lax.fori_loop(0, E, per_expert, 0)
    if final_norm:
        o_ref[...] = _rms(o_ref[...], g_ref[...])


def _combine(x2, slot_t, starts, ye, final_g=None):
    T, D = x2.shape
    gain = jnp.ones((1, D), F32) if final_g is None else final_g.reshape(1, D).astype(F32)
    E, cap, _ = ye.shape
    tb = min(COMBINE_TB, T)
    nb = T // tb
    assert cap >= COMBINE_W and cap % 16 == 0
    first = ((starts[:, :nb] // 16) * 16).T.reshape(nb, 1, E)
    return pl.pallas_call(
        functools.partial(_combine_kernel, cap=cap, final_norm=final_g is not None),
        grid_spec=pltpu.PrefetchScalarGridSpec(
            num_scalar_prefetch=1,
            grid=(nb,),
            in_specs=[pl.BlockSpec((tb, D), lambda i, st: (i, 0)),
                      pl.BlockSpec((tb, E), lambda i, st: (i, 0)),
                      pl.BlockSpec((1, 1, E), lambda i, st: (i, 0, 0)),
                      pl.BlockSpec((1, D), lambda i, st: (0, 0)),
                      pl.BlockSpec(memory_space=pl.ANY)],
            out_specs=pl.BlockSpec((tb, D), lambda i, st: (i, 0)),
            scratch_shapes=[pltpu.VMEM((2, E * COMBINE_W, D), BF16),
                            pltpu.VMEM((COMBINE_W, D), BF16),
                            pltpu.SemaphoreType.DMA((2, E)),
                            pltpu.SemaphoreType.DMA((1,))]),
        out_shape=jax.ShapeDtypeStruct((T, D), F32),
        compiler_params=_params("arbitrary"),
        name="moe_combine",
    )(starts, x2, slot_t, first, gain, ye)


def _t5_bucket(rel):
    half = T5_BUCKETS // 2
    max_exact = half // 2
    n = np.abs(rel)
    large = max_exact + (np.log(np.maximum(n, 1) / max_exact) / np.log(T5_MAX_DISTANCE / max_exact)
                         * (half - max_exact)).astype(np.int32)
    large = np.minimum(large, half - 1)
    return (np.where(rel > 0, half, 0) + np.where(n < max_exact, n, large)).astype(np.int32)


def _dilated_tiles(L, radius):
    tq = min(LANES, L)
    kw = min(tq + 2 * radius, L)
    return tq, kw


def _dilated_bias(t5_table, g, L, window, dilation):
    radius = window // (2 * dilation)
    tq, kw = _dilated_tiles(L, radius)
    deltas = (0, -((kw - tq) // 2), tq - kw)
    period = 2 * kw - 1
    rel = np.arange(period + 1) - (kw - 1)
    tab = t5_table[:, g * A_HEADS_PER_GROUP:(g + 1) * A_HEADS_PER_GROUP].astype(F32)
    by_rel = jnp.where((np.abs(rel) <= radius)[None], tab[_t5_bucket(rel * dilation)].T, NEG_INF)
    skew = jnp.tile(by_rel, (1, tq))[:, :tq * period].reshape(A_HEADS_PER_GROUP, tq, period)
    bias = jnp.stack([skew[:, :, d + kw - 1:d + 2 * kw - 1] for d in deltas])
    return bias.reshape(3, A_HEADS_PER_GROUP // 2, 2 * tq, kw)


def _dilated_kernel(q_ref, k_ref, v_ref, bias_ref, o_ref, lse_ref, *, L, tq, kw):
    nq = L // tq
    lane = lax.broadcasted_iota(jnp.int32, (1, LANES), 1)
    lo = lane < HEAD_DIM

    def body(i, carry):
        q0 = pl.multiple_of(i * tq, tq)
        start = jnp.clip(q0 - (kw - tq) // 2, 0, L - kw)
        start = pl.multiple_of(start, 16)
        var = jnp.where(i == 0, 0, jnp.where(i == nq - 1, 2, 1))
        n_pairs = A_HEADS_PER_GROUP // 2
        cols = [slice(pair * LANES, (pair + 1) * LANES) for pair in range(n_pairs)]
        scores = []
        for pair in range(n_pairs):
            q2 = q_ref[0, 0, pl.ds(q0, tq), cols[pair]] * (HEAD_DIM ** -0.5)
            zero = jnp.zeros_like(q2)
            qs = jnp.concatenate([jnp.where(lo, q2, zero), jnp.where(lo, zero, q2)], axis=0)
            k2 = k_ref[0, 0, pl.ds(start, kw), cols[pair]]
            scores.append(lax.dot_general(qs, k2, _NT, preferred_element_type=F32) + bias_ref[var, pair])
        probs, dens, maxs = [], [], []
        for s in scores:
            m = jnp.max(s, axis=-1, keepdims=True)
            p = jnp.exp(s - m)
            maxs.append(m)
            dens.append(jnp.sum(p, axis=-1, keepdims=True))
            probs.append(p.astype(BF16))
        for pair in range(n_pairs):
            v2 = v_ref[0, 0, pl.ds(start, kw), cols[pair]]
            o = jnp.dot(probs[pair], v2, preferred_element_type=F32) * (1.0 / dens[pair])
            lse = maxs[pair] + jnp.log(dens[pair])
            o_ref[0, 0, pl.ds(q0, tq), cols[pair]] = jnp.where(lo, o[:tq], o[tq:])
            lse_ref[0, 0, pl.ds(q0, tq), cols[pair]] = jnp.where(lo, lse[:tq], lse[tq:])
        return carry

    lax.fori_loop(0, nq, body, 0, unroll=next(u for u in (4, 2, 1) if nq % u == 0))


def _dilated_attention(qkv, bias, window):
    Bn, dilation, L, _ = qkv.shape
    radius = window // (2 * dilation)
    tq, kw = _dilated_tiles(L, radius)
    assert L % tq == 0 and kw % 16 == 0 and tq % 16 == 0

    def spec(which):
        return pl.BlockSpec((1, 1, L, A_OUT), lambda b, r: (b, r, 0, which))

    o, lse = pl.pallas_call(
        functools.partial(_dilated_kernel, L=L, tq=tq, kw=kw),
        grid=(Bn, dilation),
        in_specs=[spec(0), spec(1), spec(2), pl.BlockSpec(bias.shape, lambda b, r: (0, 0, 0, 0))],
        out_specs=[spec(0), spec(0)],
        out_shape=[jax.ShapeDtypeStruct((Bn, dilation, L, A_OUT), F32)] * 2,
        compiler_params=_params("parallel", "parallel"),
        name="dilated_attention",
    )(qkv, qkv, qkv, bias)
    return o, lse


def _merge_kernel(*refs):
    n = len(A_PATTERNS)
    o_refs, l_refs, out_ref = refs[:n], refs[n:2 * n], refs[2 * n]
    scratch = refs[2 * n + 1:]
    tm = out_ref.shape[0]

    def token_order(ref, scr, dilation):
        if dilation == 1:
            return ref[0, 0]
        tiles = A_OUT // LANES
        for r in range(dilation):
            for c in range(tiles):
                scr[c, pl.ds(r, tm // dilation, stride=dilation), :] = ref[0, r, :, c * LANES:(c + 1) * LANES]
        return jnp.concatenate([scr[c] for c in range(tiles)], axis=1)

    os_ = [token_order(o_refs[g], scratch[2 * g], d) for g, (_, d) in enumerate(A_PATTERNS)]
    ls = [token_order(l_refs[g], scratch[2 * g + 1], d) for g, (_, d) in enumerate(A_PATTERNS)]
    m = jnp.maximum(jnp.maximum(ls[0], ls[1]), ls[2])
    es = [jnp.exp(l - m) for l in ls]
    den = es[0] + es[1] + es[2]
    out = (es[0] / den) * os_[0] + (es[1] / den) * os_[1] + (es[2] / den) * os_[2]
    out_ref[...] = out.astype(out_ref.dtype)


def _merge_groups(os_, ls_, S, tm=512):
    Bn = os_[0].shape[0]
    ns = S // tm

    def spec(d):
        return pl.BlockSpec((1, d, tm // d, A_OUT), lambda i: (i // ns, 0, i % ns, 0))

    specs = [spec(d) for _, d in A_PATTERNS]
    return pl.pallas_call(
        _merge_kernel, grid=(Bn * ns,), in_specs=specs * 2,
        out_specs=pl.BlockSpec((tm, A_OUT), lambda i: (i, 0)),
        out_shape=jax.ShapeDtypeStruct((Bn * S, A_OUT), BF16),
        scratch_shapes=[pltpu.VMEM((A_OUT // LANES, tm, LANES), F32)] * (2 * len(A_PATTERNS)),
        compiler_params=_params("parallel"), name="merge_groups",
    )(*os_, *ls_)


EVEN_TOKEN_COLS = 3 * A_OUT + B_KV_RANK + LANES + B_Q_RANK


def _in_proj_even_kernel(x_ref, w_ref, g_ref, u0_ref, *rest):
    n_dil = len(A_PATTERNS) - 1
    dil_refs, scr = rest[:n_dil], rest[n_dil]
    tm = x_ref.shape[0]
    acc = jnp.dot(_rms(x_ref[...], g_ref[...]).astype(BF16), w_ref[...], preferred_element_type=F32)
    u0_ref[...] = acc[:, :EVEN_TOKEN_COLS].astype(u0_ref.dtype)
    for i, (_, d) in enumerate(A_PATTERNS[1:]):
        c0 = EVEN_TOKEN_COLS + i * 3 * A_OUT
        for c in range(3 * A_OUT // LANES):
            scr[c] = acc[:, c0 + c * LANES:c0 + (c + 1) * LANES]
        for r in range(d):
            for c in range(3 * A_OUT // LANES):
                piece = scr[c, pl.ds(r, tm // d, stride=d), :]
                dil_refs[i][0, r, :, c * LANES:(c + 1) * LANES] = piece.astype(dil_refs[i].dtype)


def _in_proj_even(x2, w, gain, Bn, S, tm=512):
    T, K = x2.shape
    ns = S // tm
    dils = [d for _, d in A_PATTERNS[1:]]
    assert S % tm == 0 and all(tm % (16 * d) == 0 for d in dils)
    return pl.pallas_call(
        _in_proj_even_kernel,
        grid=(T // tm,),
        in_specs=[pl.BlockSpec((tm, K), lambda i: (i, 0)),
                  pl.BlockSpec(w.shape, lambda i: (0, 0)),
                  pl.BlockSpec((1, K), lambda i: (0, 0))],
        out_specs=[pl.BlockSpec((tm, EVEN_TOKEN_COLS), lambda i: (i, 0))]
        + [pl.BlockSpec((1, d, tm // d, 3 * A_OUT), lambda i: (i // ns, 0, i % ns, 0)) for d in dils],
        out_shape=[jax.ShapeDtypeStruct((T, EVEN_TOKEN_COLS), BF16)]
        + [jax.ShapeDtypeStruct((Bn, d, S // d, 3 * A_OUT), BF16) for d in dils],
        scratch_shapes=[pltpu.VMEM((3 * A_OUT // LANES, tm, LANES), F32)],
        compiler_params=_params("parallel"),
        name="in_proj_even",
    )(x2, w, gain.reshape(1, K).astype(F32))


def _rope_tables(S):
    inv = 1.0 / (ROPE_THETA ** (np.arange(0, B_ROPE, 2, dtype=np.float32) / B_ROPE))
    ang = np.arange(S, dtype=np.float32)[:, None] * inv[None]
    cos, sin = np.cos(ang), np.sin(ang)
    half = B_ROPE // 2
    c = np.ones((S, LANES), np.float32)
    c[:, B_NOPE:B_NOPE + half] = cos
    c[:, B_NOPE + half:B_NOPE + B_ROPE] = cos
    s_lo = np.zeros((S, LANES), np.float32)
    s_lo[:, B_NOPE:B_NOPE + half] = -sin
    s_hi = np.zeros((S, LANES), np.float32)
    s_hi[:, B_NOPE + half:B_NOPE + B_ROPE] = sin
    return jnp.asarray(c), jnp.asarray(s_lo), jnp.asarray(s_hi)


def _rope(x, c, s_lo, s_hi):
    half = B_ROPE // 2
    return x * c + pltpu.roll(x, LANES - half, 1) * s_lo + pltpu.roll(x, half, 1) * s_hi


def _mla_prep_kernel(cq_ref, ckv_ref, kr_ref, gq_ref, gkv_ref, wq_ref, wk_ref, wv_ref,
                     c_ref, slo_ref, shi_ref, q_ref, k_ref, v_ref):
    c, s_lo, s_hi = c_ref[...], slo_ref[...], shi_ref[...]
    scale = (B_NOPE + B_ROPE) ** -0.5 * LOG2_E
    xq = _rms(cq_ref[...], gq_ref[...]).astype(BF16)
    q = jnp.dot(xq, wq_ref[...], preferred_element_type=F32)
    xkv = _rms(ckv_ref[...], gkv_ref[...]).astype(BF16)
    k = jnp.dot(xkv, wk_ref[...], preferred_element_type=F32)
    vt = lax.dot_general(wv_ref[...], xkv, _NT, preferred_element_type=F32)
    ones_row = lax.broadcasted_iota(jnp.int32, vt.shape, 0) % LANES == B_V
    v_ref[0, 0] = jnp.where(ones_row, 1.0, vt).astype(v_ref.dtype)
    k_rope = _rope(kr_ref[...].astype(F32), c, s_lo, s_hi)
    for h in range(B_HEADS):
        cs = slice(h * LANES, (h + 1) * LANES)
        q_ref[:, cs] = (_rope(q[:, cs], c, s_lo, s_hi) * scale).astype(q_ref.dtype)
        k_ref[:, cs] = (k[:, cs] + k_rope).astype(k_ref.dtype)


MLA_KEY_CHUNK = 1024


def _mla_prep(u3, g_qa, g_kva, wq, wk, wv):
    Bn, S, C = u3.shape
    ts = min(MLA_KEY_CHUNK, S)
    tabs = _rope_tables(S)
    ns = S // ts
    u2 = u3.reshape(Bn * S, C)
    row = lambda b, i: b * ns + i
    a_cols = 3 * A_OUT
    off_kv, off_kr, off_q = a_cols // B_KV_RANK, (a_cols + B_KV_RANK) // LANES, (a_cols + B_KV_RANK + LANES) // B_Q_RANK
    full = lambda a: pl.BlockSpec(a.shape, lambda b, i: (0, 0))
    tab_spec = pl.BlockSpec((ts, LANES), lambda b, i: (i, 0))
    hw = B_HEADS * LANES
    return pl.pallas_call(
        _mla_prep_kernel,
        grid=(Bn, ns),
        in_specs=[pl.BlockSpec((ts, B_Q_RANK), lambda b, i: (row(b, i), off_q)),
                  pl.BlockSpec((ts, B_KV_RANK), lambda b, i: (row(b, i), off_kv)),
                  pl.BlockSpec((ts, LANES), lambda b, i: (row(b, i), off_kr)),
                  full(g_qa), full(g_kva), full(wq), full(wk), full(wv),
                  tab_spec, tab_spec, tab_spec],
        out_specs=[pl.BlockSpec((ts, hw), lambda b, i: (row(b, i), 0)),
                   pl.BlockSpec((ts, hw), lambda b, i: (row(b, i), 0)),
                   pl.BlockSpec((1, 1, hw, ts), lambda b, i: (b, i, 0, 0))],
        out_shape=[jax.ShapeDtypeStruct((Bn * S, hw), BF16),
                   jax.ShapeDtypeStruct((Bn * S, hw), BF16),
                   jax.ShapeDtypeStruct((Bn, ns, hw, ts), BF16)],
        compiler_params=_params("parallel", "parallel"),
        name="mla_prep",
    )(u2, u2, u2, g_qa, g_kva, wq, wk, wv, *tabs)


def _mla_kernel(q_ref, k_ref, v_ref, o_ref):
    nk, _, tk = v_ref.shape[1:]
    tq = q_ref.shape[1]
    lane = lax.broadcasted_iota(jnp.int32, (1, LANES), 1)
    qs = [q_ref[0, :, sub * LANES:(sub + 1) * LANES] for sub in range(2)]

    def body(j, carry):
        k0 = pl.multiple_of(j * tk, tk)
        scores = [lax.dot_general(k_ref[0, pl.ds(k0, tk), sub * LANES:(sub + 1) * LANES], qs[sub], _NT,
                                  preferred_element_type=F32) for sub in range(2)]
        new = []
        for sub in range(2):
            m, acc = carry[sub]
            s = scores[sub]
            m_new = jnp.maximum(m, jnp.max(s, axis=0, keepdims=True))
            p = jnp.exp2(s - m_new).astype(BF16)
            vt = v_ref[0, j, sub * LANES:(sub + 1) * LANES, :]
            acc = jnp.exp2(m - m_new) * acc + jnp.dot(vt, p, preferred_element_type=F32)
            new.append((m_new, acc))
        return tuple(new)

    init = tuple((jnp.full((1, tq), NEG_INF, F32), jnp.zeros((LANES, tq), F32)) for _ in range(2))
    res = lax.fori_loop(0, nk, body, init, unroll=True)
    outs = [(acc * (1.0 / acc[B_V:B_V + 1])).T for (_, acc) in res]
    o_ref[0] = jnp.where(lane < B_V, outs[0], pltpu.roll(outs[1], B_V, 1)).astype(o_ref.dtype)


def _mla_attention(q, k, vt, Bn, S, tq=1024):
    tq = min(tq, S)
    nk, _, tk = vt.shape[1:]
    hw = B_HEADS * LANES
    q3, k3 = q.reshape(Bn, S, hw), k.reshape(Bn, S, hw)
    o = pl.pallas_call(
        _mla_kernel,
        grid=(Bn, B_HEADS // 2, S // tq),
        in_specs=[pl.BlockSpec((1, tq, 2 * LANES), lambda b, h, i: (b, i, h)),
                  pl.BlockSpec((1, S, 2 * LANES), lambda b, h, i: (b, 0, h)),
                  pl.BlockSpec((1, nk, 2 * LANES, tk), lambda b, h, i: (b, 0, h, 0))],
        out_specs=pl.BlockSpec((1, tq, LANES), lambda b, h, i: (b, i, h)),
        out_shape=jax.ShapeDtypeStruct((Bn, S, B_OUT), BF16),
        compiler_params=_params("parallel", "parallel", "parallel"),
        name="mla_attention",
    )(q3, k3, vt)
    return o.reshape(Bn * S, B_OUT)


def _natten_bias(rpb):
    W = GRID_W
    c = np.arange(W)[:, None]
    kc = np.arange(W)[None, :]
    wstart = np.clip(c - C_WIN_COLS // 2, 0, W - C_WIN_COLS)
    col_valid = (kc >= wstart) & (kc < wstart + C_WIN_COLS)
    n_dcol = 2 * C_WIN_COLS - 1
    left = (W - 1) - (C_WIN_COLS - 1)
    by_rel = jnp.pad(rpb.astype(F32), ((0, 0), (0, 0), (left, 2 * W - left - n_dcol)))
    skew = jnp.tile(by_rel, (1, 1, W))[:, :, :W * (2 * W - 1)].reshape(C_HEADS, -1, W, 2 * W - 1)
    cols = jnp.where(col_valid[None, None], skew[:, :, :, W - 1:2 * W - 1], NEG_INF)
    b = jnp.stack([cols[:, C_WIN_ROWS - 1 - off:2 * C_WIN_ROWS - 1 - off] for off in range(C_WIN_ROWS)])
    b = jnp.transpose(b, (0, 1, 3, 2, 4))
    return b.reshape(C_WIN_ROWS, C_HEADS // 2, 2 * W, C_WIN_ROWS * W)


NATTEN_ROWS = 4


def _natten_kernel(q_ref, k_ref, v_ref, *rest, rows):
    bias_refs, o_ref = rest[:NATTEN_ROWS], rest[NATTEN_ROWS]
    nk = C_WIN_ROWS * GRID_W
    lane = lax.broadcasted_iota(jnp.int32, (1, LANES), 1)
    lo = lane < HEAD_DIM
    n_pairs = C_HEADS // 2
    cols = [slice(pair * LANES, (pair + 1) * LANES) for pair in range(n_pairs)]
    for j in range(NATTEN_ROWS):
        r = pl.program_id(1) * NATTEN_ROWS + j
        r0 = jnp.clip(r - C_WIN_ROWS // 2, 0, rows - C_WIN_ROWS)
        k0 = pl.multiple_of(r0 * GRID_W, GRID_W)
        q_rows = slice(j * GRID_W, (j + 1) * GRID_W)
        scores = []
        for pair in range(n_pairs):
            q2 = q_ref[0, q_rows, cols[pair]] * (HEAD_DIM ** -0.5)
            zero = jnp.zeros_like(q2)
            qs = jnp.concatenate([jnp.where(lo, q2, zero), jnp.where(lo, zero, q2)], axis=0)
            k2 = k_ref[0, pl.ds(k0, nk), cols[pair]]
            scores.append(lax.dot_general(qs, k2, _NT, preferred_element_type=F32) + bias_refs[j][0, pair])
        probs, dens = [], []
        for s in scores:
            p = jnp.exp(s - jnp.max(s, axis=-1, keepdims=True))
            dens.append(jnp.sum(p, axis=-1, keepdims=True))
            probs.append(p.astype(BF16))
        for pair in range(n_pairs):
            v2 = v_ref[0, pl.ds(k0, nk), cols[pair]]
            o = jnp.dot(probs[pair], v2, preferred_element_type=F32) * (1.0 / dens[pair])
            o_ref[0, q_rows, cols[pair]] = jnp.where(lo, o[:GRID_W], o[GRID_W:]).astype(o_ref.dtype)


def _natten(u3, bias):
    Bn, S, _ = u3.shape
    rows = S // GRID_W
    assert rows >= C_WIN_ROWS and rows % NATTEN_ROWS == 0

    def bias_spec(j):
        def variant(b, i):
            r = i * NATTEN_ROWS + j
            return (r - jnp.clip(r - C_WIN_ROWS // 2, 0, rows - C_WIN_ROWS), 0, 0, 0)
        return pl.BlockSpec((1, C_HEADS // 2, 2 * GRID_W, C_WIN_ROWS * GRID_W), variant)

    q_spec = pl.BlockSpec((1, NATTEN_ROWS * GRID_W, C_OUT), lambda b, i: (b, i, 0))
    o = pl.pallas_call(
        functools.partial(_natten_kernel, rows=rows),
        grid=(Bn, rows // NATTEN_ROWS),
        in_specs=[q_spec,
                  pl.BlockSpec((1, S, C_OUT), lambda b, i: (b, 0, 1), pipeline_mode=pl.Buffered(1)),
                  pl.BlockSpec((1, S, C_OUT), lambda b, i: (b, 0, 2), pipeline_mode=pl.Buffered(1))]
        + [bias_spec(j) for j in range(NATTEN_ROWS)],
        out_specs=q_spec,
        out_shape=jax.ShapeDtypeStruct((Bn, S, C_OUT), BF16),
        compiler_params=_params("parallel", "arbitrary"),
        name="natten",
    )(u3, u3, u3, *([bias] * NATTEN_ROWS))
    return o.reshape(Bn * S, C_OUT)


def _moe(x2, ln_g, router, wg, wu, wd, layer, final_g=None):
    T, D = x2.shape
    cap = CAPACITY_FACTOR * T // N_EXPERTS
    aff_t, h = _router(x2, ln_g, router)
    slot, pos = _select(aff_t, cap)
    tb = min(COMBINE_TB, T)
    starts = jnp.concatenate([pos[:, ::tb], jnp.full((N_EXPERTS, 1), cap, jnp.int32)], axis=1)
    xe = _gather(h, slot, starts, cap)
    ye = _expert_ffn(xe, cap, wg, wu, wd, layer)
    return _combine(x2, slot.T, starts, ye, final_g)


def _trunk(x, p):
    Bn, S, D = x.shape
    T = Bn * S
    x2 = x.reshape(T, D)
    u0, *u_dil = _in_proj_even(x2, p["w_in_even"], p["ln_mix_g"][0], Bn, S)
    u3 = u0.reshape(Bn, S, EVEN_TOKEN_COLS)
    os_, ls_ = [], []
    for g, (window, dilation) in enumerate(A_PATTERNS):
        bias = _dilated_bias(p["t5_table"], g, S // dilation, window, dilation)
        qkv = u3.reshape(Bn, 1, S, EVEN_TOKEN_COLS) if g == 0 else u_dil[g - 1]
        o, l = _dilated_attention(qkv, bias, window)
        os_.append(o)
        ls_.append(l)
    a = _merge_groups(os_, ls_, S)
    q, k, v = _mla_prep(u3, p["g_qa"], p["g_kva"], p["wq"], p["wk"], p["wv"])
    b = _mla_attention(q, k, v, Bn, S)
    x2 = _matmul([a, b], [p["w_out_even"][:A_OUT], p["w_out_even"][A_OUT:]], res=x2, out_dtype=F32)
    x2 = _moe(x2, p["ln_ffn_g"][0], p["router"][0], p["w_gate"], p["w_up"], p["w_down"], 0)
    u = _matmul(x2, p["w_in_odd"], gain=p["ln_mix_g"][1], out_dtype=BF16)
    c = _natten(u.reshape(Bn, S, C_IN), p["natten_bias"])
    x2 = _matmul(c, p["w_out_odd"], res=x2, out_dtype=F32)
    x2 = _moe(x2, p["ln_ffn_g"][1], p["router"][1], p["w_gate"], p["w_up"], p["w_down"], 1,
              final_g=p["final_g"])
    return x2.reshape(Bn, S, D)


def _head_tiles(w, n_heads, per_head, keep):
    K = w.shape[0]
    w = w.reshape(K, n_heads, per_head)[:, :, :keep]
    return jnp.pad(w, ((0, 0), (0, 0), (0, LANES - keep))).reshape(K, n_heads * LANES)


def _prepare(ln_mix_g, w_in_even, g_qa, w_qb, g_kva, w_kvb, t5_table, w_out_even,
             w_in_odd, rpb, w_out_odd, ln_ffn_g, router, w_gate, w_up, w_down, final_g):
    w_in = w_in_even[0]
    w_a, w_cq = w_in[:, :A_IN], w_in[:, A_IN:A_IN + B_Q_RANK]
    w_ckv = w_in[:, A_IN + B_Q_RANK:A_IN + B_Q_RANK + B_KV_RANK]
    w_kr = w_in[:, A_IN + B_Q_RANK + B_KV_RANK:]
    w_kr_tile = jnp.pad(w_kr, ((0, 0), (B_NOPE, LANES - B_NOPE - B_ROPE)))
    w_a = w_a.reshape(-1, 3, A_GROUPS, A_OUT)
    groups = [w_a[:, :, g].reshape(-1, 3 * A_OUT) for g in range(A_GROUPS)]
    w_even = jnp.concatenate([groups[0], w_ckv, w_kr_tile, w_cq] + groups[1:], axis=1)
    w_kv = w_kvb[0].reshape(B_KV_RANK, B_HEADS, B_NOPE + B_V)
    return dict(
        ln_mix_g=ln_mix_g, ln_ffn_g=ln_ffn_g, final_g=final_g, t5_table=t5_table,
        w_in_even=w_even.astype(BF16),
        g_qa=g_qa[0].reshape(1, -1).astype(F32), g_kva=g_kva[0].reshape(1, -1).astype(F32),
        wq=_head_tiles(w_qb[0], B_HEADS, B_NOPE + B_ROPE, B_NOPE + B_ROPE).astype(BF16),
        wk=_head_tiles(w_kvb[0], B_HEADS, B_NOPE + B_V, B_NOPE).astype(BF16),
        wv=_head_tiles(w_kv[:, :, B_NOPE:].reshape(B_KV_RANK, B_OUT), B_HEADS, B_V, B_V).T.astype(BF16),
        w_out_even=w_out_even[0].astype(BF16), w_in_odd=w_in_odd[0].astype(BF16),
        natten_bias=_natten_bias(rpb[0]), w_out_odd=w_out_odd[0].astype(BF16), router=router,
        w_gate=w_gate, w_up=w_up, w_down=w_down,
    )


def kernel(x_prompt, x_sample, ln_mix_g, w_in_even, g_qa, w_qb, g_kva, w_kvb, t5_table, w_out_even,
           w_in_odd, rpb, w_out_odd, ln_ffn_g, router, w_gate, w_up, w_down, final_g):
    p = _prepare(ln_mix_g, w_in_even, g_qa, w_qb, g_kva, w_kvb, t5_table, w_out_even,
                 w_in_odd, rpb, w_out_odd, ln_ffn_g, router, w_gate, w_up, w_down, final_g)
    return _trunk(x_prompt, p), _trunk(x_sample, p)
```

```python
import functools

import jax
import jax.numpy as jnp
import numpy as np
from jax import lax
from jax.experimental import pallas as pl
from jax.experimental.pallas import tpu as pltpu

D_MODEL = 1024
RMS_EPS = 1e-6
HEAD_DIM = 64
NEG_INF = -1e30
A_PATTERNS = ((128, 1), (512, 4), (2048, 16))
A_GROUPS = 3
A_HEADS_PER_GROUP = 4
A_HEADS = A_GROUPS * A_HEADS_PER_GROUP
A_IN = 3 * A_HEADS * HEAD_DIM
A_OUT = A_HEADS_PER_GROUP * HEAD_DIM
T5_BUCKETS = 32
T5_MAX_DISTANCE = 1024
B_HEADS = 12
B_Q_RANK = 384
B_KV_RANK = 256
B_NOPE = 64
B_ROPE = 32
B_V = 64
B_OUT = B_HEADS * B_V
ROPE_THETA = 10000.0
GRID_W = 64
C_HEADS = 16
C_WIN_ROWS = 8
C_WIN_COLS = 16
C_IN = 3 * C_HEADS * HEAD_DIM
C_OUT = C_HEADS * HEAD_DIM
N_EXPERTS = 16
CAPACITY_FACTOR = 2

LANES = 128
VMEM_LIMIT_BYTES = 48 * 1024 * 1024

F32 = jnp.float32
BF16 = jnp.bfloat16
_NT = (((1,), (1,)), ((), ()))
LOG2_E = 1.4426950408889634


def _params(*sem):
    return pltpu.CompilerParams(dimension_semantics=sem, vmem_limit_bytes=VMEM_LIMIT_BYTES)


def _rms(x, g):
    xf = x.astype(F32)
    return xf * lax.rsqrt(jnp.mean(xf * xf, axis=-1, keepdims=True) + RMS_EPS) * g


def _mm_kernel(*refs, n_in, has_gain, has_res, precise):
    xs, ws = refs[:n_in], refs[n_in:2 * n_in]
    pos = 2 * n_in
    g_ref = res_ref = None
    if has_gain:
        g_ref = refs[pos]
        pos += 1
    if has_res:
        res_ref = refs[pos]
        pos += 1
    o_ref = refs[pos]
    acc = None
    for x_ref, w_ref in zip(xs, ws):
        x = x_ref[...]
        if has_gain:
            x = _rms(x, g_ref[...])
        if precise:
            part = jnp.dot(x.astype(F32), w_ref[...], preferred_element_type=F32,
                           precision=lax.Precision.HIGHEST)
        else:
            part = jnp.dot(x.astype(BF16), w_ref[...], preferred_element_type=F32)
        acc = part if acc is None else acc + part
    if has_res:
        acc = acc + res_ref[...]
    o_ref[...] = acc.astype(o_ref.dtype)


def _matmul(xs, ws, gain=None, res=None, out_dtype=F32, tm=512, precise=False):
    if not isinstance(xs, (list, tuple)):
        xs, ws = [xs], [ws]
    M = xs[0].shape[0]
    N = ws[0].shape[1]
    assert M % tm == 0
    in_specs = [pl.BlockSpec((tm, x.shape[1]), lambda i: (i, 0)) for x in xs]
    in_specs += [pl.BlockSpec(w.shape, lambda i: (0, 0)) for w in ws]
    args = list(xs) + list(ws)
    if gain is not None:
        assert len(xs) == 1
        in_specs.append(pl.BlockSpec((1, gain.shape[-1]), lambda i: (0, 0)))
        args.append(gain.reshape(1, -1).astype(F32))
    if res is not None:
        in_specs.append(pl.BlockSpec((tm, N), lambda i: (i, 0)))
        args.append(res)
    return pl.pallas_call(
        functools.partial(_mm_kernel, n_in=len(xs), has_gain=gain is not None,
                          has_res=res is not None, precise=precise),
        grid=(M // tm,),
        in_specs=in_specs,
        out_specs=pl.BlockSpec((tm, N), lambda i: (i, 0)),
        out_shape=jax.ShapeDtypeStruct((M, N), out_dtype),
        compiler_params=_params("parallel"),
        name="fused_matmul",
    )(*args)


def _ffn_kernel(x_ref, wg32_ref, wu32_ref, wd32_ref, o_ref, wg_ref, wu_ref, wd_ref):
    D = wg_ref.shape[1]

    @pl.when(pl.program_id(1) == 0)
    def _():
        wg_ref[0] = wg32_ref[0, 0].astype(BF16)
        wu_ref[0] = wu32_ref[0, 0].astype(BF16)
        wd_ref[0] = wd32_ref[0, 0].astype(BF16)

    x = x_ref[0, :, :D]
    lane = lax.broadcasted_iota(jnp.int32, (1, LANES), 1)
    mine = lane // GATE_TERMS == pl.program_id(0)
    gate = jnp.sum(jnp.where(mine, x_ref[0, :, D:].astype(F32), 0.0), axis=1, keepdims=True)
    g = jnp.dot(x, wg_ref[0], preferred_element_type=F32)
    u = jnp.dot(x, wu_ref[0], preferred_element_type=F32)
    h = (g * jax.nn.sigmoid(g)) * u
    y = jnp.dot(h.astype(BF16), wd_ref[0], preferred_element_type=F32)
    o_ref[0] = (y * gate).astype(o_ref.dtype)


def _expert_ffn(xe, cap, wg, wu, wd, layer, tm=512):
    E = xe.shape[0]
    D, FF = wg.shape[2], wg.shape[3]
    assert cap % tm == 0
    return pl.pallas_call(
        _ffn_kernel,
        grid=(E, cap // tm),
        in_specs=[pl.BlockSpec((1, tm, D + LANES), lambda e, i: (e, i, 0)),
                  pl.BlockSpec((1, 1, D, FF), lambda e, i: (layer, e, 0, 0)),
                  pl.BlockSpec((1, 1, D, FF), lambda e, i: (layer, e, 0, 0)),
                  pl.BlockSpec((1, 1, FF, D), lambda e, i: (layer, e, 0, 0))],
        out_specs=pl.BlockSpec((1, tm, D), lambda e, i: (e, i, 0)),
        out_shape=jax.ShapeDtypeStruct((E, cap, D), BF16),
        scratch_shapes=[pltpu.VMEM((1, D, FF), BF16), pltpu.VMEM((1, D, FF), BF16), pltpu.VMEM((1, FF, D), BF16)],
        compiler_params=_params("parallel", "arbitrary"),
        name="expert_ffn",
    )(xe, wg, wu, wd)


CHUNK = 256
COMBINE_TB = 512
COMBINE_W = 112


GATE_TERMS = 3


def _router_kernel(x_ref, g_ref, rt_ref, aff_ref, h_ref):
    D = x_ref.shape[1]
    E = rt_ref.shape[0]
    xn = _rms(x_ref[...], g_ref[...])
    logits = lax.dot_general(rt_ref[...], xn, _NT, preferred_element_type=F32,
                             precision=lax.Precision.HIGHEST)
    e = jnp.exp(logits - jnp.max(logits, axis=0, keepdims=True))
    aff = e / jnp.sum(e, axis=0, keepdims=True)
    aff_ref[...] = aff
    lane = lax.broadcasted_iota(jnp.int32, (E, LANES), 1)
    expert = lax.broadcasted_iota(jnp.int32, (E, LANES), 0)
    rest = aff
    gates = jnp.zeros((x_ref.shape[0], LANES), F32)
    for j in range(GATE_TERMS):
        term = rest.astype(BF16)
        rest = rest - term.astype(F32)
        place = (lane == GATE_TERMS * expert + j).astype(BF16)
        gates = gates + lax.dot_general(term, place, (((0,), (0,)), ((), ())), preferred_element_type=F32)
    h_ref[:, :D] = xn.astype(h_ref.dtype)
    h_ref[:, D:] = gates.astype(h_ref.dtype)


def _router(x2, ln_g, router, tm=512):
    T, D = x2.shape
    E = router.shape[1]
    assert GATE_TERMS * E <= LANES
    return pl.pallas_call(
        _router_kernel,
        grid=(T // tm,),
        in_specs=[pl.BlockSpec((tm, D), lambda i: (i, 0)),
                  pl.BlockSpec((1, D), lambda i: (0, 0)),
                  pl.BlockSpec((E, D), lambda i: (0, 0))],
        out_specs=[pl.BlockSpec((E, tm), lambda i: (0, i)),
                   pl.BlockSpec((tm, D + LANES), lambda i: (i, 0))],
        out_shape=[jax.ShapeDtypeStruct((E, T), F32), jax.ShapeDtypeStruct((T, D + LANES), BF16)],
        compiler_params=_params("parallel"),
        name="router",
    )(x2, ln_g.reshape(1, D).astype(F32), router.T.astype(F32))


def _cumsum_excl(mask, upper, lower_strict):
    incl = jnp.dot(mask.astype(BF16), upper, preferred_element_type=F32)
    before = jnp.dot(lower_strict, incl.astype(BF16), preferred_element_type=F32)
    return incl + before[:, CHUNK - 1:CHUNK] - mask


def _select_kernel(aff_ref, slot_ref, pos_ref, *, cap):
    aff = aff_ref[0]
    nc = aff.shape[0]
    bits = pltpu.bitcast(aff, jnp.int32)

    def count(cond):
        c = jnp.sum(jnp.where(cond, 1.0, 0.0), axis=0, keepdims=True)
        return jnp.sum(c, axis=1, keepdims=True)

    def bit_step(i, prefix):
        cand = prefix | (jnp.int32(1) << (30 - i))
        return jnp.where(count(bits >= cand) >= cap, cand, prefix)

    thr = lax.fori_loop(0, 31, bit_step, jnp.zeros((1, 1), jnp.int32))
    r = lax.broadcasted_iota(jnp.int32, (CHUNK, CHUNK), 0)
    c = lax.broadcasted_iota(jnp.int32, (CHUNK, CHUNK), 1)
    upper = (r <= c).astype(BF16)
    r = lax.broadcasted_iota(jnp.int32, (nc, nc), 0)
    c = lax.broadcasted_iota(jnp.int32, (nc, nc), 1)
    lower_strict = (c < r).astype(BF16)
    gt = bits > thr
    eq = jnp.where(bits == thr, 1.0, 0.0)
    need = cap - count(gt)
    sel = jnp.where(gt | ((eq > 0) & (_cumsum_excl(eq, upper, lower_strict) < need)), 1.0, 0.0)
    pos = _cumsum_excl(sel, upper, lower_strict).astype(jnp.int32)
    pos_ref[0] = pos
    slot_ref[0] = jnp.where(sel > 0, pos, -1)


def _select(aff_t, cap):
    E, T = aff_t.shape
    nc = T // CHUNK
    assert nc <= CHUNK and nc % 8 == 0
    spec = pl.BlockSpec((1, nc, CHUNK), lambda e: (e, 0, 0))
    slot, pos = pl.pallas_call(
        functools.partial(_select_kernel, cap=cap),
        grid=(E,), in_specs=[spec], out_specs=[spec, spec],
        out_shape=[jax.ShapeDtypeStruct((E, nc, CHUNK), jnp.int32)] * 2,
        compiler_params=_params("parallel"),
        name="select_topk",
    )(aff_t.reshape(E, nc, CHUNK))
    return slot.reshape(E, T), pos.reshape(E, T)


GATHER_STEP = 96
GATHER_W = GATHER_STEP + 16
GATHER_STACK = 8


def _gather_kernel(starts_ref, h_ref, slot_ref, xe_ref, stage_ref, extra_ref, tail_ref, sem, extra_sem, *, cap):
    b = pl.program_id(0)
    nb = pl.num_programs(0)
    E = slot_ref.shape[0]
    buf = b % 2
    h = h_ref[...]
    row = lax.broadcasted_iota(jnp.int32, (GATHER_W, 1), 0)

    @pl.when(b == 0)
    def _():
        tail_ref[...] = jnp.zeros_like(tail_ref)

    def first_row(e, blk):
        return pl.multiple_of((starts_ref[e, blk] // 16) * 16, 16)

    def copy(e, blk):
        return pltpu.make_async_copy(stage_ref.at[blk % 2, e], xe_ref.at[e, pl.ds(first_row(e, blk), GATHER_W)],
                                     sem.at[blk % 2, e])

    def onehot(e, k):
        return (slot_ref[pl.ds(e, 1), :] - (first_row(e, b) + k * GATHER_STEP) == row).astype(BF16)

    for e0 in range(0, E, GATHER_STACK):
        stacked = jnp.concatenate([onehot(e, 0) for e in range(e0, e0 + GATHER_STACK)], axis=0)
        rows = jnp.dot(stacked, h, preferred_element_type=F32).astype(BF16)
        for i in range(GATHER_STACK):
            stage_ref[buf, e0 + i] = rows[i * GATHER_W:(i + 1) * GATHER_W]

    def per_expert(e, carry):
        base = first_row(e, b)
        end = starts_ref[e, b + 1] - base
        group = pl.multiple_of((end // 16) * 16, 16)
        last = jnp.maximum(group - 1, 0) // GATHER_STEP

        def window(k):
            return jnp.dot(onehot(e, k), h, preferred_element_type=F32).astype(BF16)

        stage_ref[buf, e, pl.ds(0, 16), :] += tail_ref[e]

        @pl.when(last == 0)
        def _():
            tail_ref[e] = stage_ref[buf, e, pl.ds(group, 16), :]

        @pl.when(b > 0)
        def _():
            copy(e, b - 1).wait()

        copy(e, b).start()

        def extra(k, inner):
            extra_ref[...] = window(k)

            @pl.when(k == last)
            def _():
                tail_ref[e] = extra_ref[pl.ds(pl.multiple_of(group - k * GATHER_STEP, 16), 16), :]

            cp = pltpu.make_async_copy(extra_ref, xe_ref.at[e, pl.ds(base + k * GATHER_STEP, GATHER_W)],
                                       extra_sem.at[0])
            cp.start()
            cp.wait()
            return inner

        return lax.fori_loop(1, last + 1, extra, carry)

    lax.fori_loop(0, E, per_expert, 0)

    @pl.when(b == nb - 1)
    def _():
        extra_ref[...] = jnp.zeros_like(extra_ref)

        def drain(e, carry):
            copy(e, b).wait()
            cp = pltpu.make_async_copy(extra_ref, xe_ref.at[e, pl.ds(cap, GATHER_W)], extra_sem.at[0])
            cp.start()
            cp.wait()
            return carry
        lax.fori_loop(0, E, drain, 0)


def _gather(h, slot, starts, cap):
    T, D = h.shape
    E = slot.shape[0]
    tb = min(COMBINE_TB, T)
    rows = cap + GATHER_W
    return pl.pallas_call(
        functools.partial(_gather_kernel, cap=cap),
        grid_spec=pltpu.PrefetchScalarGridSpec(
            num_scalar_prefetch=1,
            grid=(T // tb,),
            in_specs=[pl.BlockSpec((tb, D), lambda i, st: (i, 0)),
                      pl.BlockSpec((E, tb), lambda i, st: (0, i))],
            out_specs=pl.BlockSpec(memory_space=pl.ANY),
            scratch_shapes=[pltpu.VMEM((2, E, GATHER_W, D), BF16),
                            pltpu.VMEM((GATHER_W, D), BF16),
                            pltpu.VMEM((E, 16, D), BF16),
                            pltpu.SemaphoreType.DMA((2, E)),
                            pltpu.SemaphoreType.DMA((1,))]),
        out_shape=jax.ShapeDtypeStruct((E, rows, D), BF16),
        compiler_params=_params("arbitrary"),
        name="moe_gather",
    )(starts, h, slot)


def _combine_kernel(starts_ref, x_ref, slot_ref, first_ref, g_ref, ye_ref, o_ref, win_ref, extra_ref, sem,
                    extra_sem, *, cap, final_norm):
    b = pl.program_id(0)
    nb = pl.num_programs(0)
    E = ye_ref.shape[0]
    W = COMBINE_W

    def first_row(e, blk, k):
        return (starts_ref[e, blk] // 16) * 16 + k * W

    def window_start(e, blk, k):
        return pl.multiple_of(jnp.minimum(first_row(e, blk, k), cap - W), 16)

    def copy(e, blk):
        buf = blk % 2
        return pltpu.make_async_copy(ye_ref.at[e, pl.ds(window_start(e, blk, 0), W)],
                                     win_ref.at[buf, pl.ds(e * W, W)], sem.at[buf, e])

    @pl.when(b == 0)
    def _():
        for e in range(E):
            copy(e, b).start()

    @pl.when(b + 1 < nb)
    def _():
        for e in range(E):
            copy(e, b + 1).start()

    slots = slot_ref[...]
    first = first_ref[0]
    rel = slots - jnp.minimum(first, cap - W)
    rel = jnp.where((slots >= first) & (rel >= 0) & (rel < W), rel, -1).astype(F32).astype(BF16)
    lane = lax.broadcasted_iota(jnp.int32, (E, E * W), 1)
    expert = lax.broadcasted_iota(jnp.int32, (E, E * W), 0)
    expand = (lane // W == expert).astype(BF16)
    wanted = (lax.broadcasted_iota(jnp.int32, (1, E * W), 1) % W).astype(F32)
    onehot = (jnp.dot(rel, expand, preferred_element_type=F32) == wanted).astype(BF16)

    for e in range(E):
        copy(e, b).wait()
    o_ref[...] = x_ref[...] + jnp.dot(onehot, win_ref[b % 2], preferred_element_type=F32)

    col = lax.broadcasted_iota(jnp.int32, (1, W), 1)
    expert_lane = lax.broadcasted_iota(jnp.int32, (1, E), 1)

    def per_expert(e, carry):
        n_win = (starts_ref[e, b + 1] - first_row(e, b, 0) + W - 1) // W

        def extra(k, inner):
            cp = pltpu.make_async_copy(ye_ref.at[e, pl.ds(window_start(e, b, k), W)], extra_ref, extra_sem.at[0])
            cp.start()
            cp.wait()
            slot = jnp.max(jnp.where(expert_lane == e, slots, -1), axis=1, keepdims=True)
            oh = ((slot == window_start(e, b, k) + col) & (slot >= first_row(e, b, k))).astype(BF16)
            o_ref[...] += jnp.dot(oh, extra_ref[...], preferred_element_type=F32)
            return inner

        return lax.fori_loop(1, n_win, extra, carry)

    lax.fori_loop(0, E, per_expert, 0)
    if final_norm:
        o_ref[...] = _rms(o_ref[...], g_ref[...])


def _combine(x2, slot_t, starts, ye, final_g=None):
    T, D = x2.shape
    gain = jnp.ones((1, D), F32) if final_g is None else final_g.reshape(1, D).astype(F32)
    E, cap, _ = ye.shape
    tb = min(COMBINE_TB, T)
    nb = T // tb
    assert cap >= COMBINE_W and cap % 16 == 0
    first = ((starts[:, :nb] // 16) * 16).T.reshape(nb, 1, E)
    return pl.pallas_call(
        functools.partial(_combine_kernel, cap=cap, final_norm=final_g is not None),
        grid_spec=pltpu.PrefetchScalarGridSpec(
            num_scalar_prefetch=1,
            grid=(nb,),
            in_specs=[pl.BlockSpec((tb, D), lambda i, st: (i, 0)),
                      pl.BlockSpec((tb, E), lambda i, st: (i, 0)),
                      pl.BlockSpec((1, 1, E), lambda i, st: (i, 0, 0)),
                      pl.BlockSpec((1, D), lambda i, st: (0, 0)),
                      pl.BlockSpec(memory_space=pl.ANY)],
            out_specs=pl.BlockSpec((tb, D), lambda i, st: (i, 0)),
            scratch_shapes=[pltpu.VMEM((2, E * COMBINE_W, D), BF16),
                            pltpu.VMEM((COMBINE_W, D), BF16),
                            pltpu.SemaphoreType.DMA((2, E)),
                            pltpu.SemaphoreType.DMA((1,))]),
        out_shape=jax.ShapeDtypeStruct((T, D), F32),
        compiler_params=_params("arbitrary"),
        name="moe_combine",
    )(starts, x2, slot_t, first, gain, ye)


def _t5_bucket(rel):
    half = T5_BUCKETS // 2
    max_exact = half // 2
    n = np.abs(rel)
    large = max_exact + (np.log(np.maximum(n, 1) / max_exact) / np.log(T5_MAX_DISTANCE / max_exact)
                         * (half - max_exact)).astype(np.int32)
    large = np.minimum(large, half - 1)
    return (np.where(rel > 0, half, 0) + np.where(n < max_exact, n, large)).astype(np.int32)


def _dilated_tiles(L, radius):
    tq = min(LANES, L)
    kw = min(tq + 2 * radius, L)
    return tq, kw


def _dilated_bias(t5_table, g, L, window, dilation):
    radius = window // (2 * dilation)
    tq, kw = _dilated_tiles(L, radius)
    deltas = (0, -((kw - tq) // 2), tq - kw)
    period = 2 * kw - 1
    rel = np.arange(period + 1) - (kw - 1)
    tab = t5_table[:, g * A_HEADS_PER_GROUP:(g + 1) * A_HEADS_PER_GROUP].astype(F32)
    by_rel = jnp.where((np.abs(rel) <= radius)[None], tab[_t5_bucket(rel * dilation)].T, NEG_INF)
    skew = jnp.tile(by_rel, (1, tq))[:, :tq * period].reshape(A_HEADS_PER_GROUP, tq, period)
    bias = jnp.stack([skew[:, :, d + kw - 1:d + 2 * kw - 1] for d in deltas])
    return bias.reshape(3, A_HEADS_PER_GROUP // 2, 2 * tq, kw)


def _dilated_kernel(q_ref, k_ref, v_ref, bias_ref, o_ref, lse_ref, *, L, tq, kw):
    nq = L // tq
    lane = lax.broadcasted_iota(jnp.int32, (1, LANES), 1)
    lo = lane < HEAD_DIM

    def body(i, carry):
        q0 = pl.multiple_of(i * tq, tq)
        start = jnp.clip(q0 - (kw - tq) // 2, 0, L - kw)
        start = pl.multiple_of(start, 16)
        var = jnp.where(i == 0, 0, jnp.where(i == nq - 1, 2, 1))
        n_pairs = A_HEADS_PER_GROUP // 2
        cols = [slice(pair * LANES, (pair + 1) * LANES) for pair in range(n_pairs)]
        scores = []
        for pair in range(n_pairs):
            q2 = q_ref[0, 0, pl.ds(q0, tq), cols[pair]] * (HEAD_DIM ** -0.5)
            zero = jnp.zeros_like(q2)
            qs = jnp.concatenate([jnp.where(lo, q2, zero), jnp.where(lo, zero, q2)], axis=0)
            k2 = k_ref[0, 0, pl.ds(start, kw), cols[pair]]
            scores.append(lax.dot_general(qs, k2, _NT, preferred_element_type=F32) + bias_ref[var, pair])
        probs, dens, maxs = [], [], []
        for s in scores:
            m = jnp.max(s, axis=-1, keepdims=True)
            p = jnp.exp(s - m)
            maxs.append(m)
            dens.append(jnp.sum(p, axis=-1, keepdims=True))
            probs.append(p.astype(BF16))
        for pair in range(n_pairs):
            v2 = v_ref[0, 0, pl.ds(start, kw), cols[pair]]
            o = jnp.dot(probs[pair], v2, preferred_element_type=F32) * (1.0 / dens[pair])
            lse = maxs[pair] + jnp.log(dens[pair])
            o_ref[0, 0, pl.ds(q0, tq), cols[pair]] = jnp.where(lo, o[:tq], o[tq:])
            lse_ref[0, 0, pl.ds(q0, tq), cols[pair]] = jnp.where(lo, lse[:tq], lse[tq:])
        return carry

    lax.fori_loop(0, nq, body, 0, unroll=next(u for u in (4, 2, 1) if nq % u == 0))


def _dilated_attention(qkv, bias, window):
    Bn, dilation, L, _ = qkv.shape
    radius = window // (2 * dilation)
    tq, kw = _dilated_tiles(L, radius)
    assert L % tq == 0 and kw % 16 == 0 and tq % 16 == 0

    def spec(which):
        return pl.BlockSpec((1, 1, L, A_OUT), lambda b, r: (b, r, 0, which))

    o, lse = pl.pallas_call(
        functools.partial(_dilated_kernel, L=L, tq=tq, kw=kw),
        grid=(Bn, dilation),
        in_specs=[spec(0), spec(1), spec(2), pl.BlockSpec(bias.shape, lambda b, r: (0, 0, 0, 0))],
        out_specs=[spec(0), spec(0)],
        out_shape=[jax.ShapeDtypeStruct((Bn, dilation, L, A_OUT), F32)] * 2,
        compiler_params=_params("parallel", "parallel"),
        name="dilated_attention",
    )(qkv, qkv, qkv, bias)
    return o, lse


def _merge_kernel(*refs):
    n = len(A_PATTERNS)
    o_refs, l_refs, out_ref = refs[:n], refs[n:2 * n], refs[2 * n]
    scratch = refs[2 * n + 1:]
    tm = out_ref.shape[0]

    def token_order(ref, scr, dilation):
        if dilation == 1:
            return ref[0, 0]
        tiles = A_OUT // LANES
        for r in range(dilation):
            for c in range(tiles):
                scr[c, pl.ds(r, tm // dilation, stride=dilation), :] = ref[0, r, :, c * LANES:(c + 1) * LANES]
        return jnp.concatenate([scr[c] for c in range(tiles)], axis=1)

    os_ = [token_order(o_refs[g], scratch[2 * g], d) for g, (_, d) in enumerate(A_PATTERNS)]
    ls = [token_order(l_refs[g], scratch[2 * g + 1], d) for g, (_, d) in enumerate(A_PATTERNS)]
    m = jnp.maximum(jnp.maximum(ls[0], ls[1]), ls[2])
    es = [jnp.exp(l - m) for l in ls]
    den = es[0] + es[1] + es[2]
    out = (es[0] / den) * os_[0] + (es[1] / den) * os_[1] + (es[2] / den) * os_[2]
    out_ref[...] = out.astype(out_ref.dtype)


def _merge_groups(os_, ls_, S, tm=512):
    Bn = os_[0].shape[0]
    ns = S // tm

    def spec(d):
        return pl.BlockSpec((1, d, tm // d, A_OUT), lambda i: (i // ns, 0, i % ns, 0))

    specs = [spec(d) for _, d in A_PATTERNS]
    return pl.pallas_call(
        _merge_kernel, grid=(Bn * ns,), in_specs=specs * 2,
        out_specs=pl.BlockSpec((tm, A_OUT), lambda i: (i, 0)),
        out_shape=jax.ShapeDtypeStruct((Bn * S, A_OUT), BF16),
        scratch_shapes=[pltpu.VMEM((A_OUT // LANES, tm, LANES), F32)] * (2 * len(A_PATTERNS)),
        compiler_params=_params("parallel"), name="merge_groups",
    )(*os_, *ls_)


EVEN_TOKEN_COLS = 3 * A_OUT + B_KV_RANK + LANES + B_Q_RANK


def _in_proj_even_kernel(x_ref, w_ref, g_ref, u0_ref, *rest):
    n_dil = len(A_PATTERNS) - 1
    dil_refs, scr = rest[:n_dil], rest[n_dil]
    tm = x_ref.shape[0]
    acc = jnp.dot(_rms(x_ref[...], g_ref[...]).astype(BF16), w_ref[...], preferred_element_type=F32)
    u0_ref[...] = acc[:, :EVEN_TOKEN_COLS].astype(u0_ref.dtype)
    for i, (_, d) in enumerate(A_PATTERNS[1:]):
        c0 = EVEN_TOKEN_COLS + i * 3 * A_OUT
        for c in range(3 * A_OUT // LANES):
            scr[c] = acc[:, c0 + c * LANES:c0 + (c + 1) * LANES]
        for r in range(d):
            for c in range(3 * A_OUT // LANES):
                piece = scr[c, pl.ds(r, tm // d, stride=d), :]
                dil_refs[i][0, r, :, c * LANES:(c + 1) * LANES] = piece.astype(dil_refs[i].dtype)


def _in_proj_even(x2, w, gain, Bn, S, tm=512):
    T, K = x2.shape
    ns = S // tm
    dils = [d for _, d in A_PATTERNS[1:]]
    assert S % tm == 0 and all(tm % (16 * d) == 0 for d in dils)
    return pl.pallas_call(
        _in_proj_even_kernel,
        grid=(T // tm,),
        in_specs=[pl.BlockSpec((tm, K), lambda i: (i, 0)),
                  pl.BlockSpec(w.shape, lambda i: (0, 0)),
                  pl.BlockSpec((1, K), lambda i: (0, 0))],
        out_specs=[pl.BlockSpec((tm, EVEN_TOKEN_COLS), lambda i: (i, 0))]
        + [pl.BlockSpec((1, d, tm // d, 3 * A_OUT), lambda i: (i // ns, 0, i % ns, 0)) for d in dils],
        out_shape=[jax.ShapeDtypeStruct((T, EVEN_TOKEN_COLS), BF16)]
        + [jax.ShapeDtypeStruct((Bn, d, S // d, 3 * A_OUT), BF16) for d in dils],
        scratch_shapes=[pltpu.VMEM((3 * A_OUT // LANES, tm, LANES), F32)],
        compiler_params=_params("parallel"),
        name="in_proj_even",
    )(x2, w, gain.reshape(1, K).astype(F32))


def _rope_tables(S):
    inv = 1.0 / (ROPE_THETA ** (np.arange(0, B_ROPE, 2, dtype=np.float32) / B_ROPE))
    ang = np.arange(S, dtype=np.float32)[:, None] * inv[None]
    cos, sin = np.cos(ang), np.sin(ang)
    half = B_ROPE // 2
    c = np.ones((S, LANES), np.float32)
    c[:, B_NOPE:B_NOPE + half] = cos
    c[:, B_NOPE + half:B_NOPE + B_ROPE] = cos
    s_lo = np.zeros((S, LANES), np.float32)
    s_lo[:, B_NOPE:B_NOPE + half] = -sin
    s_hi = np.zeros((S, LANES), np.float32)
    s_hi[:, B_NOPE + half:B_NOPE + B_ROPE] = sin
    return jnp.asarray(c), jnp.asarray(s_lo), jnp.asarray(s_hi)


def _rope(x, c, s_lo, s_hi):
    half = B_ROPE // 2
    return x * c + pltpu.roll(x, LANES - half, 1) * s_lo + pltpu.roll(x, half, 1) * s_hi


def _mla_prep_kernel(cq_ref, ckv_ref, kr_ref, gq_ref, gkv_ref, wq_ref, wk_ref, wv_ref,
                     c_ref, slo_ref, shi_ref, q_ref, k_ref, v_ref):
    c, s_lo, s_hi = c_ref[...], slo_ref[...], shi_ref[...]
    scale = (B_NOPE + B_ROPE) ** -0.5 * LOG2_E
    xq = _rms(cq_ref[...], gq_ref[...]).astype(BF16)
    q = jnp.dot(xq, wq_ref[...], preferred_element_type=F32)
    xkv = _rms(ckv_ref[...], gkv_ref[...]).astype(BF16)
    k = jnp.dot(xkv, wk_ref[...], preferred_element_type=F32)
    vt = lax.dot_general(wv_ref[...], xkv, _NT, preferred_element_type=F32)
    ones_row = lax.broadcasted_iota(jnp.int32, vt.shape, 0) % LANES == B_V
    v_ref[0, 0] = jnp.where(ones_row, 1.0, vt).astype(v_ref.dtype)
    k_rope = _rope(kr_ref[...].astype(F32), c, s_lo, s_hi)
    for h in range(B_HEADS):
        cs = slice(h * LANES, (h + 1) * LANES)
        q_ref[:, cs] = (_rope(q[:, cs], c, s_lo, s_hi) * scale).astype(q_ref.dtype)
        k_ref[:, cs] = (k[:, cs] + k_rope).astype(k_ref.dtype)


MLA_KEY_CHUNK = 1024


def _mla_prep(u3, g_qa, g_kva, wq, wk, wv):
    Bn, S, C = u3.shape
    ts = min(MLA_KEY_CHUNK, S)
    tabs = _rope_tables(S)
    ns = S // ts
    u2 = u3.reshape(Bn * S, C)
    row = lambda b, i: b * ns + i
    a_cols = 3 * A_OUT
    off_kv, off_kr, off_q = a_cols // B_KV_RANK, (a_cols + B_KV_RANK) // LANES, (a_cols + B_KV_RANK + LANES) // B_Q_RANK
    full = lambda a: pl.BlockSpec(a.shape, lambda b, i: (0, 0))
    tab_spec = pl.BlockSpec((ts, LANES), lambda b, i: (i, 0))
    hw = B_HEADS * LANES
    return pl.pallas_call(
        _mla_prep_kernel,
        grid=(Bn, ns),
        in_specs=[pl.BlockSpec((ts, B_Q_RANK), lambda b, i: (row(b, i), off_q)),
                  pl.BlockSpec((ts, B_KV_RANK), lambda b, i: (row(b, i), off_kv)),
                  pl.BlockSpec((ts, LANES), lambda b, i: (row(b, i), off_kr)),
                  full(g_qa), full(g_kva), full(wq), full(wk), full(wv),
                  tab_spec, tab_spec, tab_spec],
        out_specs=[pl.BlockSpec((ts, hw), lambda b, i: (row(b, i), 0)),
                   pl.BlockSpec((ts, hw), lambda b, i: (row(b, i), 0)),
                   pl.BlockSpec((1, 1, hw, ts), lambda b, i: (b, i, 0, 0))],
        out_shape=[jax.ShapeDtypeStruct((Bn * S, hw), BF16),
                   jax.ShapeDtypeStruct((Bn * S, hw), BF16),
                   jax.ShapeDtypeStruct((Bn, ns, hw, ts), BF16)],
        compiler_params=_params("parallel", "parallel"),
        name="mla_prep",
    )(u2, u2, u2, g_qa, g_kva, wq, wk, wv, *tabs)


def _mla_kernel(q_ref, k_ref, v_ref, o_ref):
    nk, _, tk = v_ref.shape[1:]
    tq = q_ref.shape[1]
    lane = lax.broadcasted_iota(jnp.int32, (1, LANES), 1)
    qs = [q_ref[0, :, sub * LANES:(sub + 1) * LANES] for sub in range(2)]

    def body(j, carry):
        k0 = pl.multiple_of(j * tk, tk)
        scores = [lax.dot_general(k_ref[0, pl.ds(k0, tk), sub * LANES:(sub + 1) * LANES], qs[sub], _NT,
                                  preferred_element_type=F32) for sub in range(2)]
        new = []
        for sub in range(2):
            m, acc = carry[sub]
            s = scores[sub]
            m_new = jnp.maximum(m, jnp.max(s, axis=0, keepdims=True))
            p = jnp.exp2(s - m_new).astype(BF16)
            vt = v_ref[0, j, sub * LANES:(sub + 1) * LANES, :]
            acc = jnp.exp2(m - m_new) * acc + jnp.dot(vt, p, preferred_element_type=F32)
            new.append((m_new, acc))
        return tuple(new)

    init = tuple((jnp.full((1, tq), NEG_INF, F32), jnp.zeros((LANES, tq), F32)) for _ in range(2))
    res = lax.fori_loop(0, nk, body, init, unroll=True)
    outs = [(acc * (1.0 / acc[B_V:B_V + 1])).T for (_, acc) in res]
    o_ref[0] = jnp.where(lane < B_V, outs[0], pltpu.roll(outs[1], B_V, 1)).astype(o_ref.dtype)


def _mla_attention(q, k, vt, Bn, S, tq=2048):
    tq = min(tq, S)
    nk, _, tk = vt.shape[1:]
    hw = B_HEADS * LANES
    q3, k3 = q.reshape(Bn, S, hw), k.reshape(Bn, S, hw)
    o = pl.pallas_call(
        _mla_kernel,
        grid=(Bn, B_HEADS // 2, S // tq),
        in_specs=[pl.BlockSpec((1, tq, 2 * LANES), lambda b, h, i: (b, i, h)),
                  pl.BlockSpec((1, S, 2 * LANES), lambda b, h, i: (b, 0, h)),
                  pl.BlockSpec((1, nk, 2 * LANES, tk), lambda b, h, i: (b, 0, h, 0))],
        out_specs=pl.BlockSpec((1, tq, LANES), lambda b, h, i: (b, i, h)),
        out_shape=jax.ShapeDtypeStruct((Bn, S, B_OUT), BF16),
        compiler_params=_params("parallel", "parallel", "parallel"),
        name="mla_attention",
    )(q3, k3, vt)
    return o.reshape(Bn * S, B_OUT)


def _natten_bias(rpb):
    W = GRID_W
    c = np.arange(W)[:, None]
    kc = np.arange(W)[None, :]
    wstart = np.clip(c - C_WIN_COLS // 2, 0, W - C_WIN_COLS)
    col_valid = (kc >= wstart) & (kc < wstart + C_WIN_COLS)
    n_dcol = 2 * C_WIN_COLS - 1
    left = (W - 1) - (C_WIN_COLS - 1)
    by_rel = jnp.pad(rpb.astype(F32), ((0, 0), (0, 0), (left, 2 * W - left - n_dcol)))
    skew = jnp.tile(by_rel, (1, 1, W))[:, :, :W * (2 * W - 1)].reshape(C_HEADS, -1, W, 2 * W - 1)
    cols = jnp.where(col_valid[None, None], skew[:, :, :, W - 1:2 * W - 1], NEG_INF)
    b = jnp.stack([cols[:, C_WIN_ROWS - 1 - off:2 * C_WIN_ROWS - 1 - off] for off in range(C_WIN_ROWS)])
    b = jnp.transpose(b, (0, 1, 3, 2, 4))
    return b.reshape(C_WIN_ROWS, C_HEADS // 2, 2 * W, C_WIN_ROWS * W)


NATTEN_ROWS = 4


def _natten_kernel(q_ref, k_ref, v_ref, *rest, rows):
    bias_refs, o_ref = rest[:NATTEN_ROWS], rest[NATTEN_ROWS]
    nk = C_WIN_ROWS * GRID_W
    lane = lax.broadcasted_iota(jnp.int32, (1, LANES), 1)
    lo = lane < HEAD_DIM
    n_pairs = C_HEADS // 2
    cols = [slice(pair * LANES, (pair + 1) * LANES) for pair in range(n_pairs)]
    for j in range(NATTEN_ROWS):
        r = pl.program_id(1) * NATTEN_ROWS + j
        r0 = jnp.clip(r - C_WIN_ROWS // 2, 0, rows - C_WIN_ROWS)
        k0 = pl.multiple_of(r0 * GRID_W, GRID_W)
        q_rows = slice(j * GRID_W, (j + 1) * GRID_W)
        scores = []
        for pair in range(n_pairs):
            q2 = q_ref[0, q_rows, cols[pair]] * (HEAD_DIM ** -0.5)
            zero = jnp.zeros_like(q2)
            qs = jnp.concatenate([jnp.where(lo, q2, zero), jnp.where(lo, zero, q2)], axis=0)
            k2 = k_ref[0, pl.ds(k0, nk), cols[pair]]
            scores.append(lax.dot_general(qs, k2, _NT, preferred_element_type=F32) + bias_refs[j][0, pair])
        probs, dens = [], []
        for s in scores:
            p = jnp.exp(s - jnp.max(s, axis=-1, keepdims=True))
            dens.append(jnp.sum(p, axis=-1, keepdims=True))
            probs.append(p.astype(BF16))
        for pair in range(n_pairs):
            v2 = v_ref[0, pl.ds(k0, nk), cols[pair]]
            o = jnp.dot(probs[pair], v2, preferred_element_type=F32) * (1.0 / dens[pair])
            o_ref[0, q_rows, cols[pair]] = jnp.where(lo, o[:GRID_W], o[GRID_W:]).astype(o_ref.dtype)


def _natten(u3, bias):
    Bn, S, _ = u3.shape
    rows = S // GRID_W
    assert rows >= C_WIN_ROWS and rows % NATTEN_ROWS == 0

    def bias_spec(j):
        def variant(b, i):
            r = i * NATTEN_ROWS + j
            return (r - jnp.clip(r - C_WIN_ROWS // 2, 0, rows - C_WIN_ROWS), 0, 0, 0)
        return pl.BlockSpec((1, C_HEADS // 2, 2 * GRID_W, C_WIN_ROWS * GRID_W), variant)

    q_spec = pl.BlockSpec((1, NATTEN_ROWS * GRID_W, C_OUT), lambda b, i: (b, i, 0))
    o = pl.pallas_call(
        functools.partial(_natten_kernel, rows=rows),
        grid=(Bn, rows // NATTEN_ROWS),
        in_specs=[q_spec,
                  pl.BlockSpec((1, S, C_OUT), lambda b, i: (b, 0, 1), pipeline_mode=pl.Buffered(1)),
                  pl.BlockSpec((1, S, C_OUT), lambda b, i: (b, 0, 2), pipeline_mode=pl.Buffered(1))]
        + [bias_spec(j) for j in range(NATTEN_ROWS)],
        out_specs=q_spec,
        out_shape=jax.ShapeDtypeStruct((Bn, S, C_OUT), BF16),
        compiler_params=_params("parallel", "arbitrary"),
        name="natten",
    )(u3, u3, u3, *([bias] * NATTEN_ROWS))
    return o.reshape(Bn * S, C_OUT)


def _moe(x2, ln_g, router, wg, wu, wd, layer, final_g=None):
    T, D = x2.shape
    cap = CAPACITY_FACTOR * T // N_EXPERTS
    aff_t, h = _router(x2, ln_g, router)
    slot, pos = _select(aff_t, cap)
    tb = min(COMBINE_TB, T)
    starts = jnp.concatenate([pos[:, ::tb], jnp.full((N_EXPERTS, 1), cap, jnp.int32)], axis=1)
    xe = _gather(h, slot, starts, cap)
    ye = _expert_ffn(xe, cap, wg, wu, wd, layer)
    return _combine(x2, slot.T, starts, ye, final_g)


def _trunk(x, p):
    Bn, S, D = x.shape
    T = Bn * S
    x2 = x.reshape(T, D)
    u0, *u_dil = _in_proj_even(x2, p["w_in_even"], p["ln_mix_g"][0], Bn, S)
    u3 = u0.reshape(Bn, S, EVEN_TOKEN_COLS)
    os_, ls_ = [], []
    for g, (window, dilation) in enumerate(A_PATTERNS):
        bias = _dilated_bias(p["t5_table"], g, S // dilation, window, dilation)
        qkv = u3.reshape(Bn, 1, S, EVEN_TOKEN_COLS) if g == 0 else u_dil[g - 1]
        o, l = _dilated_attention(qkv, bias, window)
        os_.append(o)
        ls_.append(l)
    a = _merge_groups(os_, ls_, S)
    q, k, v = _mla_prep(u3, p["g_qa"], p["g_kva"], p["wq"], p["wk"], p["wv"])
    b = _mla_attention(q, k, v, Bn, S)
    x2 = _matmul([a, b], [p["w_out_even"][:A_OUT], p["w_out_even"][A_OUT:]], res=x2, out_dtype=F32)
    x2 = _moe(x2, p["ln_ffn_g"][0], p["router"][0], p["w_gate"], p["w_up"], p["w_down"], 0)
    u = _matmul(x2, p["w_in_odd"], gain=p["ln_mix_g"][1], out_dtype=BF16)
    c = _natten(u.reshape(Bn, S, C_IN), p["natten_bias"])
    x2 = _matmul(c, p["w_out_odd"], res=x2, out_dtype=F32)
    x2 = _moe(x2, p["ln_ffn_g"][1], p["router"][1], p["w_gate"], p["w_up"], p["w_down"], 1,
              final_g=p["final_g"])
    return x2.reshape(Bn, S, D)


def _head_tiles(w, n_heads, per_head, keep):
    K = w.shape[0]
    w = w.reshape(K, n_heads, per_head)[:, :, :keep]
    return jnp.pad(w, ((0, 0), (0, 0), (0, LANES - keep))).reshape(K, n_heads * LANES)


def _prepare(ln_mix_g, w_in_even, g_qa, w_qb, g_kva, w_kvb, t5_table, w_out_even,
             w_in_odd, rpb, w_out_odd, ln_ffn_g, router, w_gate, w_up, w_down, final_g):
    w_in = w_in_even[0]
    w_a, w_cq = w_in[:, :A_IN], w_in[:, A_IN:A_IN + B_Q_RANK]
    w_ckv = w_in[:, A_IN + B_Q_RANK:A_IN + B_Q_RANK + B_KV_RANK]
    w_kr = w_in[:, A_IN + B_Q_RANK + B_KV_RANK:]
    w_kr_tile = jnp.pad(w_kr, ((0, 0), (B_NOPE, LANES - B_NOPE - B_ROPE)))
    w_a = w_a.reshape(-1, 3, A_GROUPS, A_OUT)
    groups = [w_a[:, :, g].reshape(-1, 3 * A_OUT) for g in range(A_GROUPS)]
    w_even = jnp.concatenate([groups[0], w_ckv, w_kr_tile, w_cq] + groups[1:], axis=1)
    w_kv = w_kvb[0].reshape(B_KV_RANK, B_HEADS, B_NOPE + B_V)
    return dict(
        ln_mix_g=ln_mix_g, ln_ffn_g=ln_ffn_g, final_g=final_g, t5_table=t5_table,
        w_in_even=w_even.astype(BF16),
        g_qa=g_qa[0].reshape(1, -1).astype(F32), g_kva=g_kva[0].reshape(1, -1).astype(F32),
        wq=_head_tiles(w_qb[0], B_HEADS, B_NOPE + B_ROPE, B_NOPE + B_ROPE).astype(BF16),
        wk=_head_tiles(w_kvb[0], B_HEADS, B_NOPE + B_V, B_NOPE).astype(BF16),
        wv=_head_tiles(w_kv[:, :, B_NOPE:].reshape(B_KV_RANK, B_OUT), B_HEADS, B_V, B_V).T.astype(BF16),
        w_out_even=w_out_even[0].astype(BF16), w_in_odd=w_in_odd[0].astype(BF16),
        natten_bias=_natten_bias(rpb[0]), w_out_odd=w_out_odd[0].astype(BF16), router=router,
        w_gate=w_gate, w_up=w_up, w_down=w_down,
    )


def kernel(x_prompt, x_sample, ln_mix_g, w_in_even, g_qa, w_qb, g_kva, w_kvb, t5_table, w_out_even,
           w_in_odd, rpb, w_out_odd, ln_ffn_g, router, w_gate, w_up, w_down, final_g):
    p = _prepare(ln_mix_g, w_in_even, g_qa, w_qb, g_kva, w_kvb, t5_table, w_out_even,
                 w_in_odd, rpb, w_out_odd, ln_ffn_g, router, w_gate, w_up, w_down, final_g)
    return _trunk(x_prompt, p), _trunk(x_sample, p)
```

```python
import functools

import jax
import jax.numpy as jnp
import numpy as np
from jax import lax
from jax.experimental import pallas as pl
from jax.experimental.pallas import tpu as pltpu

D_MODEL = 1024
RMS_EPS = 1e-6
HEAD_DIM = 64
NEG_INF = -1e30
A_PATTERNS = ((128, 1), (512, 4), (2048, 16))
A_GROUPS = 3
A_HEADS_PER_GROUP = 4
A_HEADS = A_GROUPS * A_HEADS_PER_GROUP
A_IN = 3 * A_HEADS * HEAD_DIM
A_OUT = A_HEADS_PER_GROUP * HEAD_DIM
T5_BUCKETS = 32
T5_MAX_DISTANCE = 1024
B_HEADS = 12
B_Q_RANK = 384
B_KV_RANK = 256
B_NOPE = 64
B_ROPE = 32
B_V = 64
B_OUT = B_HEADS * B_V
ROPE_THETA = 10000.0
GRID_W = 64
C_HEADS = 16
C_WIN_ROWS = 8
C_WIN_COLS = 16
C_IN = 3 * C_HEADS * HEAD_DIM
C_OUT = C_HEADS * HEAD_DIM
N_EXPERTS = 16
CAPACITY_FACTOR = 2

LANES = 128
VMEM_LIMIT_BYTES = 48 * 1024 * 1024

F32 = jnp.float32
BF16 = jnp.bfloat16
_NT = (((1,), (1,)), ((), ()))
LOG2_E = 1.4426950408889634


def _params(*sem):
    return pltpu.CompilerParams(dimension_semantics=sem, vmem_limit_bytes=VMEM_LIMIT_BYTES)


def _rms(x, g):
    xf = x.astype(F32)
    return xf * lax.rsqrt(jnp.mean(xf * xf, axis=-1, keepdims=True) + RMS_EPS) * g


def _mm_kernel(*refs, n_in, has_gain, has_res, precise):
    xs, ws = refs[:n_in], refs[n_in:2 * n_in]
    pos = 2 * n_in
    g_ref = res_ref = None
    if has_gain:
        g_ref = refs[pos]
        pos += 1
    if has_res:
        res_ref = refs[pos]
        pos += 1
    o_ref = refs[pos]
    acc = None
    for x_ref, w_ref in zip(xs, ws):
        x = x_ref[...]
        if has_gain:
            x = _rms(x, g_ref[...])
        if precise:
            part = jnp.dot(x.astype(F32), w_ref[...], preferred_element_type=F32,
                           precision=lax.Precision.HIGHEST)
        else:
            part = jnp.dot(x.astype(BF16), w_ref[...], preferred_element_type=F32)
        acc = part if acc is None else acc + part
    if has_res:
        acc = acc + res_ref[...]
    o_ref[...] = acc.astype(o_ref.dtype)


def _matmul(xs, ws, gain=None, res=None, out_dtype=F32, tm=512, precise=False):
    if not isinstance(xs, (list, tuple)):
        xs, ws = [xs], [ws]
    M = xs[0].shape[0]
    N = ws[0].shape[1]
    assert M % tm == 0
    in_specs = [pl.BlockSpec((tm, x.shape[1]), lambda i: (i, 0)) for x in xs]
    in_specs += [pl.BlockSpec(w.shape, lambda i: (0, 0)) for w in ws]
    args = list(xs) + list(ws)
    if gain is not None:
        assert len(xs) == 1
        in_specs.append(pl.BlockSpec((1, gain.shape[-1]), lambda i: (0, 0)))
        args.append(gain.reshape(1, -1).astype(F32))
    if res is not None:
        in_specs.append(pl.BlockSpec((tm, N), lambda i: (i, 0)))
        args.append(res)
    return pl.pallas_call(
        functools.partial(_mm_kernel, n_in=len(xs), has_gain=gain is not None,
                          has_res=res is not None, precise=precise),
        grid=(M // tm,),
        in_specs=in_specs,
        out_specs=pl.BlockSpec((tm, N), lambda i: (i, 0)),
        out_shape=jax.ShapeDtypeStruct((M, N), out_dtype),
        compiler_params=_params("parallel"),
        name="fused_matmul",
    )(*args)


def _ffn_kernel(x_ref, wg32_ref, wu32_ref, wd32_ref, o_ref, wg_ref, wu_ref, wd_ref):
    D = wg_ref.shape[1]

    @pl.when(pl.program_id(1) == 0)
    def _():
        wg_ref[0] = wg32_ref[0, 0].astype(BF16)
        wu_ref[0] = wu32_ref[0, 0].astype(BF16)
        wd_ref[0] = wd32_ref[0, 0].astype(BF16)

    x = x_ref[0, :, :D]
    lane = lax.broadcasted_iota(jnp.int32, (1, LANES), 1)
    mine = lane // GATE_TERMS == pl.program_id(0)
    gate = jnp.sum(jnp.where(mine, x_ref[0, :, D:].astype(F32), 0.0), axis=1, keepdims=True)
    g = jnp.dot(x, wg_ref[0], preferred_element_type=F32)
    u = jnp.dot(x, wu_ref[0], preferred_element_type=F32)
    h = (g * jax.nn.sigmoid(g)) * u
    y = jnp.dot(h.astype(BF16), wd_ref[0], preferred_element_type=F32)
    o_ref[0] = (y * gate).astype(o_ref.dtype)


def _expert_ffn(xe, cap, wg, wu, wd, layer, tm=1024):
    E = xe.shape[0]
    D, FF = wg.shape[2], wg.shape[3]
    tm = min(tm, cap)
    assert cap % tm == 0
    return pl.pallas_call(
        _ffn_kernel,
        grid=(E, cap // tm),
        in_specs=[pl.BlockSpec((1, tm, D + LANES), lambda e, i: (e, i, 0)),
                  pl.BlockSpec((1, 1, D, FF), lambda e, i: (layer, e, 0, 0)),
                  pl.BlockSpec((1, 1, D, FF), lambda e, i: (layer, e, 0, 0)),
                  pl.BlockSpec((1, 1, FF, D), lambda e, i: (layer, e, 0, 0))],
        out_specs=pl.BlockSpec((1, tm, D), lambda e, i: (e, i, 0)),
        out_shape=jax.ShapeDtypeStruct((E, cap, D), BF16),
        scratch_shapes=[pltpu.VMEM((1, D, FF), BF16), pltpu.VMEM((1, D, FF), BF16), pltpu.VMEM((1, FF, D), BF16)],
        compiler_params=_params("parallel", "arbitrary"),
        name="expert_ffn",
    )(xe, wg, wu, wd)


CHUNK = 256
COMBINE_TB = 512
COMBINE_W = 112


GATE_TERMS = 3


def _router_kernel(x_ref, g_ref, rt_ref, aff_ref, h_ref):
    D = x_ref.shape[1]
    E = rt_ref.shape[0]
    xn = _rms(x_ref[...], g_ref[...])
    logits = lax.dot_general(rt_ref[...], xn, _NT, preferred_element_type=F32,
                             precision=lax.Precision.HIGHEST)
    e = jnp.exp(logits - jnp.max(logits, axis=0, keepdims=True))
    aff = e / jnp.sum(e, axis=0, keepdims=True)
    aff_ref[...] = aff
    lane = lax.broadcasted_iota(jnp.int32, (E, LANES), 1)
    expert = lax.broadcasted_iota(jnp.int32, (E, LANES), 0)
    rest = aff
    gates = jnp.zeros((x_ref.shape[0], LANES), F32)
    for j in range(GATE_TERMS):
        term = rest.astype(BF16)
        rest = rest - term.astype(F32)
        place = (lane == GATE_TERMS * expert + j).astype(BF16)
        gates = gates + lax.dot_general(term, place, (((0,), (0,)), ((), ())), preferred_element_type=F32)
    h_ref[:, :D] = xn.astype(h_ref.dtype)
    h_ref[:, D:] = gates.astype(h_ref.dtype)


def _router(x2, ln_g, router, tm=512):
    T, D = x2.shape
    E = router.shape[1]
    assert GATE_TERMS * E <= LANES
    return pl.pallas_call(
        _router_kernel,
        grid=(T // tm,),
        in_specs=[pl.BlockSpec((tm, D), lambda i: (i, 0)),
                  pl.BlockSpec((1, D), lambda i: (0, 0)),
                  pl.BlockSpec((E, D), lambda i: (0, 0))],
        out_specs=[pl.BlockSpec((E, tm), lambda i: (0, i)),
                   pl.BlockSpec((tm, D + LANES), lambda i: (i, 0))],
        out_shape=[jax.ShapeDtypeStruct((E, T), F32), jax.ShapeDtypeStruct((T, D + LANES), BF16)],
        compiler_params=_params("parallel"),
        name="router",
    )(x2, ln_g.reshape(1, D).astype(F32), router.T.astype(F32))


def _cumsum_excl(mask, upper, lower_strict):
    incl = jnp.dot(mask.astype(BF16), upper, preferred_element_type=F32)
    before = jnp.dot(lower_strict, incl.astype(BF16), preferred_element_type=F32)
    return incl + before[:, CHUNK - 1:CHUNK] - mask


def _select_kernel(aff_ref, slot_ref, pos_ref, *, cap):
    aff = aff_ref[0]
    nc = aff.shape[0]
    bits = pltpu.bitcast(aff, jnp.int32)

    def count(cond):
        c = jnp.sum(jnp.where(cond, 1.0, 0.0), axis=0, keepdims=True)
        return jnp.sum(c, axis=1, keepdims=True)

    def bit_step(i, prefix):
        cand = prefix | (jnp.int32(1) << (30 - i))
        return jnp.where(count(bits >= cand) >= cap, cand, prefix)

    thr = lax.fori_loop(0, 31, bit_step, jnp.zeros((1, 1), jnp.int32))
    r = lax.broadcasted_iota(jnp.int32, (CHUNK, CHUNK), 0)
    c = lax.broadcasted_iota(jnp.int32, (CHUNK, CHUNK), 1)
    upper = (r <= c).astype(BF16)
    r = lax.broadcasted_iota(jnp.int32, (nc, nc), 0)
    c = lax.broadcasted_iota(jnp.int32, (nc, nc), 1)
    lower_strict = (c < r).astype(BF16)
    gt = bits > thr
    eq = jnp.where(bits == thr, 1.0, 0.0)
    need = cap - count(gt)
    sel = jnp.where(gt | ((eq > 0) & (_cumsum_excl(eq, upper, lower_strict) < need)), 1.0, 0.0)
    pos = _cumsum_excl(sel, upper, lower_strict).astype(jnp.int32)
    pos_ref[0] = pos
    slot_ref[0] = jnp.where(sel > 0, pos, -1)


def _select(aff_t, cap):
    E, T = aff_t.shape
    nc = T // CHUNK
    assert nc <= CHUNK and nc % 8 == 0
    spec = pl.BlockSpec((1, nc, CHUNK), lambda e: (e, 0, 0))
    slot, pos = pl.pallas_call(
        functools.partial(_select_kernel, cap=cap),
        grid=(E,), in_specs=[spec], out_specs=[spec, spec],
        out_shape=[jax.ShapeDtypeStruct((E, nc, CHUNK), jnp.int32)] * 2,
        compiler_params=_params("parallel"),
        name="select_topk",
    )(aff_t.reshape(E, nc, CHUNK))
    return slot.reshape(E, T), pos.reshape(E, T)


GATHER_STEP = 96
GATHER_W = GATHER_STEP + 16
GATHER_STACK = 8


def _gather_kernel(starts_ref, h_ref, slot_ref, xe_ref, stage_ref, extra_ref, tail_ref, sem, extra_sem, *, cap):
    b = pl.program_id(0)
    nb = pl.num_programs(0)
    E = slot_ref.shape[0]
    buf = b % 2
    h = h_ref[...]
    row = lax.broadcasted_iota(jnp.int32, (GATHER_W, 1), 0)

    @pl.when(b == 0)
    def _():
        tail_ref[...] = jnp.zeros_like(tail_ref)

    def first_row(e, blk):
        return pl.multiple_of((starts_ref[e, blk] // 16) * 16, 16)

    def copy(e, blk):
        return pltpu.make_async_copy(stage_ref.at[blk % 2, e], xe_ref.at[e, pl.ds(first_row(e, blk), GATHER_W)],
                                     sem.at[blk % 2, e])

    def onehot(e, k):
        return (slot_ref[pl.ds(e, 1), :] - (first_row(e, b) + k * GATHER_STEP) == row).astype(BF16)

    for e0 in range(0, E, GATHER_STACK):
        stacked = jnp.concatenate([onehot(e, 0) for e in range(e0, e0 + GATHER_STACK)], axis=0)
        rows = jnp.dot(stacked, h, preferred_element_type=F32).astype(BF16)
        for i in range(GATHER_STACK):
            stage_ref[buf, e0 + i] = rows[i * GATHER_W:(i + 1) * GATHER_W]

    def per_expert(e, carry):
        base = first_row(e, b)
        end = starts_ref[e, b + 1] - base
        group = pl.multiple_of((end // 16) * 16, 16)
        last = jnp.maximum(group - 1, 0) // GATHER_STEP

        def window(k):
            return jnp.dot(onehot(e, k), h, preferred_element_type=F32).astype(BF16)

        stage_ref[buf, e, pl.ds(0, 16), :] += tail_ref[e]

        @pl.when(last == 0)
        def _():
            tail_ref[e] = stage_ref[buf, e, pl.ds(group, 16), :]

        @pl.when(b > 0)
        def _():
            copy(e, b - 1).wait()

        copy(e, b).start()

        def extra(k, inner):
            extra_ref[...] = window(k)

            @pl.when(k == last)
            def _():
                tail_ref[e] = extra_ref[pl.ds(pl.multiple_of(group - k * GATHER_STEP, 16), 16), :]

            cp = pltpu.make_async_copy(extra_ref, xe_ref.at[e, pl.ds(base + k * GATHER_STEP, GATHER_W)],
                                       extra_sem.at[0])
            cp.start()
            cp.wait()
            return inner

        return lax.fori_loop(1, last + 1, extra, carry)

    lax.fori_loop(0, E, per_expert, 0)

    @pl.when(b == nb - 1)
    def _():
        extra_ref[...] = jnp.zeros_like(extra_ref)

        def drain(e, carry):
            copy(e, b).wait()
            cp = pltpu.make_async_copy(extra_ref, xe_ref.at[e, pl.ds(cap, GATHER_W)], extra_sem.at[0])
            cp.start()
            cp.wait()
            return carry
        lax.fori_loop(0, E, drain, 0)


def _gather(h, slot, starts, cap):
    T, D = h.shape
    E = slot.shape[0]
    tb = min(COMBINE_TB, T)
    rows = cap + GATHER_W
    return pl.pallas_call(
        functools.partial(_gather_kernel, cap=cap),
        grid_spec=pltpu.PrefetchScalarGridSpec(
            num_scalar_prefetch=1,
            grid=(T // tb,),
            in_specs=[pl.BlockSpec((tb, D), lambda i, st: (i, 0)),
                      pl.BlockSpec((E, tb), lambda i, st: (0, i))],
            out_specs=pl.BlockSpec(memory_space=pl.ANY),
            scratch_shapes=[pltpu.VMEM((2, E, GATHER_W, D), BF16),
                            pltpu.VMEM((GATHER_W, D), BF16),
                            pltpu.VMEM((E, 16, D), BF16),
                            pltpu.SemaphoreType.DMA((2, E)),
                            pltpu.SemaphoreType.DMA((1,))]),
        out_shape=jax.ShapeDtypeStruct((E, rows, D), BF16),
        compiler_params=_params("arbitrary"),
        name="moe_gather",
    )(starts, h, slot)


def _combine_kernel(starts_ref, x_ref, slot_ref, first_ref, g_ref, ye_ref, o_ref, win_ref, extra_ref, sem,
                    extra_sem, *, cap, final_norm):
    b = pl.program_id(0)
    nb = pl.num_programs(0)
    E = ye_ref.shape[0]
    W = COMBINE_W

    def first_row(e, blk, k):
        return (starts_ref[e, blk] // 16) * 16 + k * W

    def window_start(e, blk, k):
        return pl.multiple_of(jnp.minimum(first_row(e, blk, k), cap - W), 16)

    def copy(e, blk):
        buf = blk % 2
        return pltpu.make_async_copy(ye_ref.at[e, pl.ds(window_start(e, blk, 0), W)],
                                     win_ref.at[buf, pl.ds(e * W, W)], sem.at[buf, e])

    @pl.when(b == 0)
    def _():
        for e in range(E):
            copy(e, b).start()

    @pl.when(b + 1 < nb)
    def _():
        for e in range(E):
            copy(e, b + 1).start()

    slots = slot_ref[...]
    first = first_ref[0]
    rel = slots - jnp.minimum(first, cap - W)
    rel = jnp.where((slots >= first) & (rel >= 0) & (rel < W), rel, -1).astype(F32).astype(BF16)
    lane = lax.broadcasted_iota(jnp.int32, (E, E * W), 1)
    expert = lax.broadcasted_iota(jnp.int32, (E, E * W), 0)
    expand = (lane // W == expert).astype(BF16)
    wanted = (lax.broadcasted_iota(jnp.int32, (1, E * W), 1) % W).astype(F32)
    onehot = (jnp.dot(rel, expand, preferred_element_type=F32) == wanted).astype(BF16)

    for e in range(E):
        copy(e, b).wait()
    o_ref[...] = x_ref[...] + jnp.dot(onehot, win_ref[b % 2], preferred_element_type=F32)

    col = lax.broadcasted_iota(jnp.int32, (1, W), 1)
    expert_lane = lax.broadcasted_iota(jnp.int32, (1, E), 1)

    def per_expert(e, carry):
        n_win = (starts_ref[e, b + 1] - first_row(e, b, 0) + W - 1) // W

        def extra(k, inner):
            cp = pltpu.make_async_copy(ye_ref.at[e, pl.ds(window_start(e, b, k), W)], extra_ref, extra_sem.at[0])
            cp.start()
            cp.wait()
            slot = jnp.max(jnp.where(expert_lane == e, slots, -1), axis=1, keepdims=True)
            oh = ((slot == window_start(e, b, k) + col) & (slot >= first_row(e, b, k))).astype(BF16)
            o_ref[...] += jnp.dot(oh, extra_ref[...], preferred_element_type=F32)
            return inner

        return lax.fori_loop(1, n_win, extra, carry)

    lax.fori_loop(0, E, per_expert, 0)
    if final_norm:
        o_ref[...] = _rms(o_ref[...], g_ref[...])


def _combine(x2, slot_t, starts, ye, final_g=None):
    T, D = x2.shape
    gain = jnp.ones((1, D), F32) if final_g is None else final_g.reshape(1, D).astype(F32)
    E, cap, _ = ye.shape
    tb = min(COMBINE_TB, T)
    nb = T // tb
    assert cap >= COMBINE_W and cap % 16 == 0
    first = ((starts[:, :nb] // 16) * 16).T.reshape(nb, 1, E)
    return pl.pallas_call(
        functools.partial(_combine_kernel, cap=cap, final_norm=final_g is not None),
        grid_spec=pltpu.PrefetchScalarGridSpec(
            num_scalar_prefetch=1,
            grid=(nb,),
            in_specs=[pl.BlockSpec((tb, D), lambda i, st: (i, 0)),
                      pl.BlockSpec((tb, E), lambda i, st: (i, 0)),
                      pl.BlockSpec((1, 1, E), lambda i, st: (i, 0, 0)),
                      pl.BlockSpec((1, D), lambda i, st: (0, 0)),
                      pl.BlockSpec(memory_space=pl.ANY)],
            out_specs=pl.BlockSpec((tb, D), lambda i, st: (i, 0)),
            scratch_shapes=[pltpu.VMEM((2, E * COMBINE_W, D), BF16),
                            pltpu.VMEM((COMBINE_W, D), BF16),
                            pltpu.SemaphoreType.DMA((2, E)),
                            pltpu.SemaphoreType.DMA((1,))]),
        out_shape=jax.ShapeDtypeStruct((T, D), F32),
        compiler_params=_params("arbitrary"),
        name="moe_combine",
    )(starts, x2, slot_t, first, gain, ye)


def _t5_bucket(rel):
    half = T5_BUCKETS // 2
    max_exact = half // 2
    n = np.abs(rel)
    large = max_exact + (np.log(np.maximum(n, 1) / max_exact) / np.log(T5_MAX_DISTANCE / max_exact)
                         * (half - max_exact)).astype(np.int32)
    large = np.minimum(large, half - 1)
    return (np.where(rel > 0, half, 0) + np.where(n < max_exact, n, large)).astype(np.int32)


def _dilated_tiles(L, radius):
    tq = min(LANES, L)
    kw = min(tq + 2 * radius, L)
    return tq, kw


def _dilated_bias(t5_table, g, L, window, dilation):
    radius = window // (2 * dilation)
    tq, kw = _dilated_tiles(L, radius)
    deltas = (0, -((kw - tq) // 2), tq - kw)
    period = 2 * kw - 1
    rel = np.arange(period + 1) - (kw - 1)
    tab = t5_table[:, g * A_HEADS_PER_GROUP:(g + 1) * A_HEADS_PER_GROUP].astype(F32)
    by_rel = jnp.where((np.abs(rel) <= radius)[None], tab[_t5_bucket(rel * dilation)].T, NEG_INF)
    skew = jnp.tile(by_rel, (1, tq))[:, :tq * period].reshape(A_HEADS_PER_GROUP, tq, period)
    bias = jnp.stack([skew[:, :, d + kw - 1:d + 2 * kw - 1] for d in deltas])
    return bias.reshape(3, A_HEADS_PER_GROUP // 2, 2 * tq, kw)


def _dilated_kernel(q_ref, k_ref, v_ref, bias_ref, o_ref, lse_ref, *, L, tq, kw):
    nq = L // tq
    lane = lax.broadcasted_iota(jnp.int32, (1, LANES), 1)
    lo = lane < HEAD_DIM

    def body(i, carry):
        q0 = pl.multiple_of(i * tq, tq)
        start = jnp.clip(q0 - (kw - tq) // 2, 0, L - kw)
        start = pl.multiple_of(start, 16)
        var = jnp.where(i == 0, 0, jnp.where(i == nq - 1, 2, 1))
        n_pairs = A_HEADS_PER_GROUP // 2
        cols = [slice(pair * LANES, (pair + 1) * LANES) for pair in range(n_pairs)]
        scores = []
        for pair in range(n_pairs):
            q2 = q_ref[0, 0, pl.ds(q0, tq), cols[pair]] * (HEAD_DIM ** -0.5)
            zero = jnp.zeros_like(q2)
            qs = jnp.concatenate([jnp.where(lo, q2, zero), jnp.where(lo, zero, q2)], axis=0)
            k2 = k_ref[0, 0, pl.ds(start, kw), cols[pair]]
            scores.append(lax.dot_general(qs, k2, _NT, preferred_element_type=F32) + bias_ref[var, pair])
        probs, dens, maxs = [], [], []
        for s in scores:
            m = jnp.max(s, axis=-1, keepdims=True)
            p = jnp.exp(s - m)
            maxs.append(m)
            dens.append(jnp.sum(p, axis=-1, keepdims=True))
            probs.append(p.astype(BF16))
        for pair in range(n_pairs):
            v2 = v_ref[0, 0, pl.ds(start, kw), cols[pair]]
            o = jnp.dot(probs[pair], v2, preferred_element_type=F32) * (1.0 / dens[pair])
            lse = maxs[pair] + jnp.log(dens[pair])
            o_ref[0, 0, pl.ds(q0, tq), cols[pair]] = jnp.where(lo, o[:tq], o[tq:])
            lse_ref[0, 0, pl.ds(q0, tq), cols[pair]] = jnp.where(lo, lse[:tq], lse[tq:])
        return carry

    lax.fori_loop(0, nq, body, 0, unroll=next(u for u in (4, 2, 1) if nq % u == 0))


def _dilated_attention(qkv, bias, window):
    Bn, dilation, L, _ = qkv.shape
    radius = window // (2 * dilation)
    tq, kw = _dilated_tiles(L, radius)
    assert L % tq == 0 and kw % 16 == 0 and tq % 16 == 0

    def spec(which):
        return pl.BlockSpec((1, 1, L, A_OUT), lambda b, r: (b, r, 0, which))

    o, lse = pl.pallas_call(
        functools.partial(_dilated_kernel, L=L, tq=tq, kw=kw),
        grid=(Bn, dilation),
        in_specs=[spec(0), spec(1), spec(2), pl.BlockSpec(bias.shape, lambda b, r: (0, 0, 0, 0))],
        out_specs=[spec(0), spec(0)],
        out_shape=[jax.ShapeDtypeStruct((Bn, dilation, L, A_OUT), F32)] * 2,
        compiler_params=_params("parallel", "parallel"),
        name="dilated_attention",
    )(qkv, qkv, qkv, bias)
    return o, lse


def _merge_kernel(*refs):
    n = len(A_PATTERNS)
    o_refs, l_refs, out_ref = refs[:n], refs[n:2 * n], refs[2 * n]
    scratch = refs[2 * n + 1:]
    tm = out_ref.shape[0]

    def token_order(ref, scr, dilation):
        if dilation == 1:
            return ref[0, 0]
        tiles = A_OUT // LANES
        for r in range(dilation):
            for c in range(tiles):
                scr[c, pl.ds(r, tm // dilation, stride=dilation), :] = ref[0, r, :, c * LANES:(c + 1) * LANES]
        return jnp.concatenate([scr[c] for c in range(tiles)], axis=1)

    os_ = [token_order(o_refs[g], scratch[2 * g], d) for g, (_, d) in enumerate(A_PATTERNS)]
    ls = [token_order(l_refs[g], scratch[2 * g + 1], d) for g, (_, d) in enumerate(A_PATTERNS)]
    m = jnp.maximum(jnp.maximum(ls[0], ls[1]), ls[2])
    es = [jnp.exp(l - m) for l in ls]
    den = es[0] + es[1] + es[2]
    out = (es[0] / den) * os_[0] + (es[1] / den) * os_[1] + (es[2] / den) * os_[2]
    out_ref[...] = out.astype(out_ref.dtype)


def _merge_groups(os_, ls_, S, tm=512):
    Bn = os_[0].shape[0]
    ns = S // tm

    def spec(d):
        return pl.BlockSpec((1, d, tm // d, A_OUT), lambda i: (i // ns, 0, i % ns, 0))

    specs = [spec(d) for _, d in A_PATTERNS]
    return pl.pallas_call(
        _merge_kernel, grid=(Bn * ns,), in_specs=specs * 2,
        out_specs=pl.BlockSpec((tm, A_OUT), lambda i: (i, 0)),
        out_shape=jax.ShapeDtypeStruct((Bn * S, A_OUT), BF16),
        scratch_shapes=[pltpu.VMEM((A_OUT // LANES, tm, LANES), F32)] * (2 * len(A_PATTERNS)),
        compiler_params=_params("parallel"), name="merge_groups",
    )(*os_, *ls_)


EVEN_TOKEN_COLS = 3 * A_OUT + B_KV_RANK + LANES + B_Q_RANK


def _in_proj_even_kernel(x_ref, w_ref, g_ref, u0_ref, *rest):
    n_dil = len(A_PATTERNS) - 1
    dil_refs, scr = rest[:n_dil], rest[n_dil]
    tm = x_ref.shape[0]
    acc = jnp.dot(_rms(x_ref[...], g_ref[...]).astype(BF16), w_ref[...], preferred_element_type=F32)
    u0_ref[...] = acc[:, :EVEN_TOKEN_COLS].astype(u0_ref.dtype)
    for i, (_, d) in enumerate(A_PATTERNS[1:]):
        c0 = EVEN_TOKEN_COLS + i * 3 * A_OUT
        for c in range(3 * A_OUT // LANES):
            scr[c] = acc[:, c0 + c * LANES:c0 + (c + 1) * LANES]
        for r in range(d):
            for c in range(3 * A_OUT // LANES):
                piece = scr[c, pl.ds(r, tm // d, stride=d), :]
                dil_refs[i][0, r, :, c * LANES:(c + 1) * LANES] = piece.astype(dil_refs[i].dtype)


def _in_proj_even(x2, w, gain, Bn, S, tm=512):
    T, K = x2.shape
    ns = S // tm
    dils = [d for _, d in A_PATTERNS[1:]]
    assert S % tm == 0 and all(tm % (16 * d) == 0 for d in dils)
    return pl.pallas_call(
        _in_proj_even_kernel,
        grid=(T // tm,),
        in_specs=[pl.BlockSpec((tm, K), lambda i: (i, 0)),
                  pl.BlockSpec(w.shape, lambda i: (0, 0)),
                  pl.BlockSpec((1, K), lambda i: (0, 0))],
        out_specs=[pl.BlockSpec((tm, EVEN_TOKEN_COLS), lambda i: (i, 0))]
        + [pl.BlockSpec((1, d, tm // d, 3 * A_OUT), lambda i: (i // ns, 0, i % ns, 0)) for d in dils],
        out_shape=[jax.ShapeDtypeStruct((T, EVEN_TOKEN_COLS), BF16)]
        + [jax.ShapeDtypeStruct((Bn, d, S // d, 3 * A_OUT), BF16) for d in dils],
        scratch_shapes=[pltpu.VMEM((3 * A_OUT // LANES, tm, LANES), F32)],
        compiler_params=_params("parallel"),
        name="in_proj_even",
    )(x2, w, gain.reshape(1, K).astype(F32))


def _rope_tables(S):
    inv = 1.0 / (ROPE_THETA ** (np.arange(0, B_ROPE, 2, dtype=np.float32) / B_ROPE))
    ang = np.arange(S, dtype=np.float32)[:, None] * inv[None]
    cos, sin = np.cos(ang), np.sin(ang)
    half = B_ROPE // 2
    c = np.ones((S, LANES), np.float32)
    c[:, B_NOPE:B_NOPE + half] = cos
    c[:, B_NOPE + half:B_NOPE + B_ROPE] = cos
    s_lo = np.zeros((S, LANES), np.float32)
    s_lo[:, B_NOPE:B_NOPE + half] = -sin
    s_hi = np.zeros((S, LANES), np.float32)
    s_hi[:, B_NOPE + half:B_NOPE + B_ROPE] = sin
    return jnp.asarray(c), jnp.asarray(s_lo), jnp.asarray(s_hi)


def _rope(x, c, s_lo, s_hi):
    half = B_ROPE // 2
    return x * c + pltpu.roll(x, LANES - half, 1) * s_lo + pltpu.roll(x, half, 1) * s_hi


def _mla_prep_kernel(cq_ref, ckv_ref, kr_ref, gq_ref, gkv_ref, wq_ref, wk_ref, wv_ref,
                     c_ref, slo_ref, shi_ref, q_ref, k_ref, v_ref):
    c, s_lo, s_hi = c_ref[...], slo_ref[...], shi_ref[...]
    scale = (B_NOPE + B_ROPE) ** -0.5 * LOG2_E
    xq = _rms(cq_ref[...], gq_ref[...]).astype(BF16)
    q = jnp.dot(xq, wq_ref[...], preferred_element_type=F32)
    xkv = _rms(ckv_ref[...], gkv_ref[...]).astype(BF16)
    k = jnp.dot(xkv, wk_ref[...], preferred_element_type=F32)
    vt = lax.dot_general(wv_ref[...], xkv, _NT, preferred_element_type=F32)
    ones_row = lax.broadcasted_iota(jnp.int32, vt.shape, 0) % LANES == B_V
    v_ref[0, 0] = jnp.where(ones_row, 1.0, vt).astype(v_ref.dtype)
    k_rope = _rope(kr_ref[...].astype(F32), c, s_lo, s_hi)
    for h in range(B_HEADS):
        cs = slice(h * LANES, (h + 1) * LANES)
        q_ref[:, cs] = (_rope(q[:, cs], c, s_lo, s_hi) * scale).astype(q_ref.dtype)
        k_ref[:, cs] = (k[:, cs] + k_rope).astype(k_ref.dtype)


MLA_KEY_CHUNK = 1024


def _mla_prep(u3, g_qa, g_kva, wq, wk, wv):
    Bn, S, C = u3.shape
    ts = min(MLA_KEY_CHUNK, S)
    tabs = _rope_tables(S)
    ns = S // ts
    u2 = u3.reshape(Bn * S, C)
    row = lambda b, i: b * ns + i
    a_cols = 3 * A_OUT
    off_kv, off_kr, off_q = a_cols // B_KV_RANK, (a_cols + B_KV_RANK) // LANES, (a_cols + B_KV_RANK + LANES) // B_Q_RANK
    full = lambda a: pl.BlockSpec(a.shape, lambda b, i: (0, 0))
    tab_spec = pl.BlockSpec((ts, LANES), lambda b, i: (i, 0))
    hw = B_HEADS * LANES
    return pl.pallas_call(
        _mla_prep_kernel,
        grid=(Bn, ns),
        in_specs=[pl.BlockSpec((ts, B_Q_RANK), lambda b, i: (row(b, i), off_q)),
                  pl.BlockSpec((ts, B_KV_RANK), lambda b, i: (row(b, i), off_kv)),
                  pl.BlockSpec((ts, LANES), lambda b, i: (row(b, i), off_kr)),
                  full(g_qa), full(g_kva), full(wq), full(wk), full(wv),
                  tab_spec, tab_spec, tab_spec],
        out_specs=[pl.BlockSpec((ts, hw), lambda b, i: (row(b, i), 0)),
                   pl.BlockSpec((ts, hw), lambda b, i: (row(b, i), 0)),
                   pl.BlockSpec((1, 1, hw, ts), lambda b, i: (b, i, 0, 0))],
        out_shape=[jax.ShapeDtypeStruct((Bn * S, hw), BF16),
                   jax.ShapeDtypeStruct((Bn * S, hw), BF16),
                   jax.ShapeDtypeStruct((Bn, ns, hw, ts), BF16)],
        compiler_params=_params("parallel", "parallel"),
        name="mla_prep",
    )(u2, u2, u2, g_qa, g_kva, wq, wk, wv, *tabs)


def _mla_kernel(q_ref, k_ref, v_ref, o_ref):
    nk, _, tk = v_ref.shape[1:]
    tq = q_ref.shape[1]
    lane = lax.broadcasted_iota(jnp.int32, (1, LANES), 1)
    qs = [q_ref[0, :, sub * LANES:(sub + 1) * LANES] for sub in range(2)]

    def body(j, carry):
        k0 = pl.multiple_of(j * tk, tk)
        scores = [lax.dot_general(k_ref[0, pl.ds(k0, tk), sub * LANES:(sub + 1) * LANES], qs[sub], _NT,
                                  preferred_element_type=F32) for sub in range(2)]
        new = []
        for sub in range(2):
            m, acc = carry[sub]
            s = scores[sub]
            m_new = jnp.maximum(m, jnp.max(s, axis=0, keepdims=True))
            p = jnp.exp2(s - m_new).astype(BF16)
            vt = v_ref[0, j, sub * LANES:(sub + 1) * LANES, :]
            acc = jnp.exp2(m - m_new) * acc + jnp.dot(vt, p, preferred_element_type=F32)
            new.append((m_new, acc))
        return tuple(new)

    init = tuple((jnp.full((1, tq), NEG_INF, F32), jnp.zeros((LANES, tq), F32)) for _ in range(2))
    res = lax.fori_loop(0, nk, body, init, unroll=True)
    outs = [(acc * (1.0 / acc[B_V:B_V + 1])).T for (_, acc) in res]
    o_ref[0] = jnp.where(lane < B_V, outs[0], pltpu.roll(outs[1], B_V, 1)).astype(o_ref.dtype)


def _mla_attention(q, k, vt, Bn, S, tq=2048):
    tq = min(tq, S)
    nk, _, tk = vt.shape[1:]
    hw = B_HEADS * LANES
    q3, k3 = q.reshape(Bn, S, hw), k.reshape(Bn, S, hw)
    o = pl.pallas_call(
        _mla_kernel,
        grid=(Bn, B_HEADS // 2, S // tq),
        in_specs=[pl.BlockSpec((1, tq, 2 * LANES), lambda b, h, i: (b, i, h)),
                  pl.BlockSpec((1, S, 2 * LANES), lambda b, h, i: (b, 0, h)),
                  pl.BlockSpec((1, nk, 2 * LANES, tk), lambda b, h, i: (b, 0, h, 0))],
        out_specs=pl.BlockSpec((1, tq, LANES), lambda b, h, i: (b, i, h)),
        out_shape=jax.ShapeDtypeStruct((Bn, S, B_OUT), BF16),
        compiler_params=_params("parallel", "parallel", "parallel"),
        name="mla_attention",
    )(q3, k3, vt)
    return o.reshape(Bn * S, B_OUT)


def _natten_bias(rpb):
    W = GRID_W
    c = np.arange(W)[:, None]
    kc = np.arange(W)[None, :]
    wstart = np.clip(c - C_WIN_COLS // 2, 0, W - C_WIN_COLS)
    col_valid = (kc >= wstart) & (kc < wstart + C_WIN_COLS)
    n_dcol = 2 * C_WIN_COLS - 1
    left = (W - 1) - (C_WIN_COLS - 1)
    by_rel = jnp.pad(rpb.astype(F32), ((0, 0), (0, 0), (left, 2 * W - left - n_dcol)))
    skew = jnp.tile(by_rel, (1, 1, W))[:, :, :W * (2 * W - 1)].reshape(C_HEADS, -1, W, 2 * W - 1)
    cols = jnp.where(col_valid[None, None], skew[:, :, :, W - 1:2 * W - 1], NEG_INF)
    b = jnp.stack([cols[:, C_WIN_ROWS - 1 - off:2 * C_WIN_ROWS - 1 - off] for off in range(C_WIN_ROWS)])
    b = jnp.transpose(b, (0, 1, 3, 2, 4))
    return b.reshape(C_WIN_ROWS, C_HEADS // 2, 2 * W, C_WIN_ROWS * W)


NATTEN_ROWS = 4


def _natten_kernel(q_ref, k_ref, v_ref, *rest, rows):
    bias_refs, o_ref = rest[:NATTEN_ROWS], rest[NATTEN_ROWS]
    nk = C_WIN_ROWS * GRID_W
    lane = lax.broadcasted_iota(jnp.int32, (1, LANES), 1)
    lo = lane < HEAD_DIM
    n_pairs = C_HEADS // 2
    cols = [slice(pair * LANES, (pair + 1) * LANES) for pair in range(n_pairs)]
    for j in range(NATTEN_ROWS):
        r = pl.program_id(1) * NATTEN_ROWS + j
        r0 = jnp.clip(r - C_WIN_ROWS // 2, 0, rows - C_WIN_ROWS)
        k0 = pl.multiple_of(r0 * GRID_W, GRID_W)
        q_rows = slice(j * GRID_W, (j + 1) * GRID_W)
        scores = []
        for pair in range(n_pairs):
            q2 = q_ref[0, q_rows, cols[pair]] * (HEAD_DIM ** -0.5)
            zero = jnp.zeros_like(q2)
            qs = jnp.concatenate([jnp.where(lo, q2, zero), jnp.where(lo, zero, q2)], axis=0)
            k2 = k_ref[0, pl.ds(k0, nk), cols[pair]]
            scores.append(lax.dot_general(qs, k2, _NT, preferred_element_type=F32) + bias_refs[j][0, pair])
        probs, dens = [], []
        for s in scores:
            p = jnp.exp(s - jnp.max(s, axis=-1, keepdims=True))
            dens.append(jnp.sum(p, axis=-1, keepdims=True))
            probs.append(p.astype(BF16))
        for pair in range(n_pairs):
            v2 = v_ref[0, pl.ds(k0, nk), cols[pair]]
            o = jnp.dot(probs[pair], v2, preferred_element_type=F32) * (1.0 / dens[pair])
            o_ref[0, q_rows, cols[pair]] = jnp.where(lo, o[:GRID_W], o[GRID_W:]).astype(o_ref.dtype)


def _natten(u3, bias):
    Bn, S, _ = u3.shape
    rows = S // GRID_W
    assert rows >= C_WIN_ROWS and rows % NATTEN_ROWS == 0

    def bias_spec(j):
        def variant(b, i):
            r = i * NATTEN_ROWS + j
            return (r - jnp.clip(r - C_WIN_ROWS // 2, 0, rows - C_WIN_ROWS), 0, 0, 0)
        return pl.BlockSpec((1, C_HEADS // 2, 2 * GRID_W, C_WIN_ROWS * GRID_W), variant)

    q_spec = pl.BlockSpec((1, NATTEN_ROWS * GRID_W, C_OUT), lambda b, i: (b, i, 0))
    o = pl.pallas_call(
        functools.partial(_natten_kernel, rows=rows),
        grid=(Bn, rows // NATTEN_ROWS),
        in_specs=[q_spec,
                  pl.BlockSpec((1, S, C_OUT), lambda b, i: (b, 0, 1), pipeline_mode=pl.Buffered(1)),
                  pl.BlockSpec((1, S, C_OUT), lambda b, i: (b, 0, 2), pipeline_mode=pl.Buffered(1))]
        + [bias_spec(j) for j in range(NATTEN_ROWS)],
        out_specs=q_spec,
        out_shape=jax.ShapeDtypeStruct((Bn, S, C_OUT), BF16),
        compiler_params=_params("parallel", "arbitrary"),
        name="natten",
    )(u3, u3, u3, *([bias] * NATTEN_ROWS))
    return o.reshape(Bn * S, C_OUT)


def _moe(x2, ln_g, router, wg, wu, wd, layer, final_g=None):
    T, D = x2.shape
    cap = CAPACITY_FACTOR * T // N_EXPERTS
    aff_t, h = _router(x2, ln_g, router)
    slot, pos = _select(aff_t, cap)
    tb = min(COMBINE_TB, T)
    starts = jnp.concatenate([pos[:, ::tb], jnp.full((N_EXPERTS, 1), cap, jnp.int32)], axis=1)
    xe = _gather(h, slot, starts, cap)
    ye = _expert_ffn(xe, cap, wg, wu, wd, layer)
    return _combine(x2, slot.T, starts, ye, final_g)


def _trunk(x, p):
    Bn, S, D = x.shape
    T = Bn * S
    x2 = x.reshape(T, D)
    u0, *u_dil = _in_proj_even(x2, p["w_in_even"], p["ln_mix_g"][0], Bn, S)
    u3 = u0.reshape(Bn, S, EVEN_TOKEN_COLS)
    os_, ls_ = [], []
    for g, (window, dilation) in enumerate(A_PATTERNS):
        bias = _dilated_bias(p["t5_table"], g, S // dilation, window, dilation)
        qkv = u3.reshape(Bn, 1, S, EVEN_TOKEN_COLS) if g == 0 else u_dil[g - 1]
        o, l = _dilated_attention(qkv, bias, window)
        os_.append(o)
        ls_.append(l)
    a = _merge_groups(os_, ls_, S)
    q, k, v = _mla_prep(u3, p["g_qa"], p["g_kva"], p["wq"], p["wk"], p["wv"])
    b = _mla_attention(q, k, v, Bn, S)
    x2 = _matmul([a, b], [p["w_out_even"][:A_OUT], p["w_out_even"][A_OUT:]], res=x2, out_dtype=F32)
    x2 = _moe(x2, p["ln_ffn_g"][0], p["router"][0], p["w_gate"], p["w_up"], p["w_down"], 0)
    u = _matmul(x2, p["w_in_odd"], gain=p["ln_mix_g"][1], out_dtype=BF16)
    c = _natten(u.reshape(Bn, S, C_IN), p["natten_bias"])
    x2 = _matmul(c, p["w_out_odd"], res=x2, out_dtype=F32)
    x2 = _moe(x2, p["ln_ffn_g"][1], p["router"][1], p["w_gate"], p["w_up"], p["w_down"], 1,
              final_g=p["final_g"])
    return x2.reshape(Bn, S, D)


def _head_tiles(w, n_heads, per_head, keep):
    K = w.shape[0]
    w = w.reshape(K, n_heads, per_head)[:, :, :keep]
    return jnp.pad(w, ((0, 0), (0, 0), (0, LANES - keep))).reshape(K, n_heads * LANES)


def _prepare(ln_mix_g, w_in_even, g_qa, w_qb, g_kva, w_kvb, t5_table, w_out_even,
             w_in_odd, rpb, w_out_odd, ln_ffn_g, router, w_gate, w_up, w_down, final_g):
    w_in = w_in_even[0]
    w_a, w_cq = w_in[:, :A_IN], w_in[:, A_IN:A_IN + B_Q_RANK]
    w_ckv = w_in[:, A_IN + B_Q_RANK:A_IN + B_Q_RANK + B_KV_RANK]
    w_kr = w_in[:, A_IN + B_Q_RANK + B_KV_RANK:]
    w_kr_tile = jnp.pad(w_kr, ((0, 0), (B_NOPE, LANES - B_NOPE - B_ROPE)))
    w_a = w_a.reshape(-1, 3, A_GROUPS, A_OUT)
    groups = [w_a[:, :, g].reshape(-1, 3 * A_OUT) for g in range(A_GROUPS)]
    w_even = jnp.concatenate([groups[0], w_ckv, w_kr_tile, w_cq] + groups[1:], axis=1)
    w_kv = w_kvb[0].reshape(B_KV_RANK, B_HEADS, B_NOPE + B_V)
    return dict(
        ln_mix_g=ln_mix_g, ln_ffn_g=ln_ffn_g, final_g=final_g, t5_table=t5_table,
        w_in_even=w_even.astype(BF16),
        g_qa=g_qa[0].reshape(1, -1).astype(F32), g_kva=g_kva[0].reshape(1, -1).astype(F32),
        wq=_head_tiles(w_qb[0], B_HEADS, B_NOPE + B_ROPE, B_NOPE + B_ROPE).astype(BF16),
        wk=_head_tiles(w_kvb[0], B_HEADS, B_NOPE + B_V, B_NOPE).astype(BF16),
        wv=_head_tiles(w_kv[:, :, B_NOPE:].reshape(B_KV_RANK, B_OUT), B_HEADS, B_V, B_V).T.astype(BF16),
        w_out_even=w_out_even[0].astype(BF16), w_in_odd=w_in_odd[0].astype(BF16),
        natten_bias=_natten_bias(rpb[0]), w_out_odd=w_out_odd[0].astype(BF16), router=router,
        w_gate=w_gate, w_up=w_up, w_down=w_down,
    )


def kernel(x_prompt, x_sample, ln_mix_g, w_in_even, g_qa, w_qb, g_kva, w_kvb, t5_table, w_out_even,
           w_in_odd, rpb, w_out_odd, ln_ffn_g, router, w_gate, w_up, w_down, final_g):
    p = _prepare(ln_mix_g, w_in_even, g_qa, w_qb, g_kva, w_kvb, t5_table, w_out_even,
                 w_in_odd, rpb, w_out_odd, ln_ffn_g, router, w_gate, w_up, w_down, final_g)
    return _trunk(x_prompt, p), _trunk(x_sample, p)
```

```python
import functools

import jax
import jax.numpy as jnp
import numpy as np
from jax import lax
from jax.experimental import pallas as pl
from jax.experimental.pallas import tpu as pltpu

D_MODEL = 1024
RMS_EPS = 1e-6
HEAD_DIM = 64
NEG_INF = -1e30
A_PATTERNS = ((128, 1), (512, 4), (2048, 16))
A_GROUPS = 3
A_HEADS_PER_GROUP = 4
A_HEADS = A_GROUPS * A_HEADS_PER_GROUP
A_IN = 3 * A_HEADS * HEAD_DIM
A_OUT = A_HEADS_PER_GROUP * HEAD_DIM
T5_BUCKETS = 32
T5_MAX_DISTANCE = 1024
B_HEADS = 12
B_Q_RANK = 384
B_KV_RANK = 256
B_NOPE = 64
B_ROPE = 32
B_V = 64
B_OUT = B_HEADS * B_V
ROPE_THETA = 10000.0
GRID_W = 64
C_HEADS = 16
C_WIN_ROWS = 8
C_WIN_COLS = 16
C_IN = 3 * C_HEADS * HEAD_DIM
C_OUT = C_HEADS * HEAD_DIM
N_EXPERTS = 16
CAPACITY_FACTOR = 2

LANES = 128
ROWS16 = 16
VMEM_LIMIT_BYTES = 48 * 1024 * 1024

F32 = jnp.float32
BF16 = jnp.bfloat16
_NT = (((1,), (1,)), ((), ()))
LOG2_E = 1.4426950408889634


def _params(*sem):
    return pltpu.CompilerParams(dimension_semantics=sem, vmem_limit_bytes=VMEM_LIMIT_BYTES)


def _rms(x, g):
    xf = x.astype(F32)
    return xf * lax.rsqrt(jnp.mean(xf * xf, axis=-1, keepdims=True) + RMS_EPS) * g


def _mm_kernel(*refs, n_in, has_gain, has_res):
    xs, ws = refs[:n_in], refs[n_in:2 * n_in]
    pos = 2 * n_in
    g_ref = res_ref = None
    if has_gain:
        g_ref = refs[pos]
        pos += 1
    if has_res:
        res_ref = refs[pos]
        pos += 1
    o_ref = refs[pos]
    acc = None
    for x_ref, w_ref in zip(xs, ws):
        x = x_ref[...]
        if has_gain:
            x = _rms(x, g_ref[...])
        part = jnp.dot(x.astype(BF16), w_ref[...], preferred_element_type=F32)
        acc = part if acc is None else acc + part
    if has_res:
        acc = acc + res_ref[...]
    o_ref[...] = acc.astype(o_ref.dtype)


def _matmul(xs, ws, gain=None, res=None, out_dtype=F32, tm=512):
    if not isinstance(xs, (list, tuple)):
        xs, ws = [xs], [ws]
    M = xs[0].shape[0]
    N = ws[0].shape[1]
    assert M % tm == 0
    in_specs = [pl.BlockSpec((tm, x.shape[1]), lambda i: (i, 0)) for x in xs]
    in_specs += [pl.BlockSpec(w.shape, lambda i: (0, 0)) for w in ws]
    args = list(xs) + list(ws)
    if gain is not None:
        assert len(xs) == 1
        in_specs.append(pl.BlockSpec((1, gain.shape[-1]), lambda i: (0, 0)))
        args.append(gain.reshape(1, -1).astype(F32))
    if res is not None:
        in_specs.append(pl.BlockSpec((tm, N), lambda i: (i, 0)))
        args.append(res)
    return pl.pallas_call(
        functools.partial(_mm_kernel, n_in=len(xs), has_gain=gain is not None,
                          has_res=res is not None),
        grid=(M // tm,),
        in_specs=in_specs,
        out_specs=pl.BlockSpec((tm, N), lambda i: (i, 0)),
        out_shape=jax.ShapeDtypeStruct((M, N), out_dtype),
        compiler_params=_params("parallel"),
        name="fused_matmul",
    )(*args)


def _ffn_kernel(x_ref, wg32_ref, wu32_ref, wd32_ref, o_ref, wg_ref, wu_ref, wd_ref):
    D = wg_ref.shape[1]

    @pl.when(pl.program_id(1) == 0)
    def _():
        wg_ref[0] = wg32_ref[0, 0].astype(BF16)
        wu_ref[0] = wu32_ref[0, 0].astype(BF16)
        wd_ref[0] = wd32_ref[0, 0].astype(BF16)

    x = x_ref[0, :, :D]
    lane = lax.broadcasted_iota(jnp.int32, (1, LANES), 1)
    mine = lane // GATE_TERMS == pl.program_id(0)
    gate = jnp.sum(jnp.where(mine, x_ref[0, :, D:].astype(F32), 0.0), axis=1, keepdims=True)
    g = jnp.dot(x, wg_ref[0], preferred_element_type=F32)
    u = jnp.dot(x, wu_ref[0], preferred_element_type=F32)
    h = (g * jax.nn.sigmoid(g)) * u
    y = jnp.dot(h.astype(BF16), wd_ref[0], preferred_element_type=F32)
    o_ref[0] = (y * gate).astype(o_ref.dtype)


def _expert_ffn(xe, cap, wg, wu, wd, layer, tm=1024):
    E = xe.shape[0]
    D, FF = wg.shape[2], wg.shape[3]
    tm = min(tm, cap)
    assert cap % tm == 0
    return pl.pallas_call(
        _ffn_kernel,
        grid=(E, cap // tm),
        in_specs=[pl.BlockSpec((1, tm, D + LANES), lambda e, i: (e, i, 0)),
                  pl.BlockSpec((1, 1, D, FF), lambda e, i: (layer, e, 0, 0)),
                  pl.BlockSpec((1, 1, D, FF), lambda e, i: (layer, e, 0, 0)),
                  pl.BlockSpec((1, 1, FF, D), lambda e, i: (layer, e, 0, 0))],
        out_specs=pl.BlockSpec((1, tm, D), lambda e, i: (e, i, 0)),
        out_shape=jax.ShapeDtypeStruct((E, cap, D), BF16),
        scratch_shapes=[pltpu.VMEM((1, D, FF), BF16), pltpu.VMEM((1, D, FF), BF16), pltpu.VMEM((1, FF, D), BF16)],
        compiler_params=_params("parallel", "arbitrary"),
        name="expert_ffn",
    )(xe, wg, wu, wd)


CHUNK = 256
COMBINE_TB = 512
COMBINE_W = 112


GATE_TERMS = 3


def _router_kernel(x_ref, g_ref, rt_ref, aff_ref, h_ref):
    D = x_ref.shape[1]
    E = rt_ref.shape[0]
    xn = _rms(x_ref[...], g_ref[...])
    logits = lax.dot_general(rt_ref[...], xn, _NT, preferred_element_type=F32,
                             precision=lax.Precision.HIGHEST)
    e = jnp.exp(logits - jnp.max(logits, axis=0, keepdims=True))
    aff = e / jnp.sum(e, axis=0, keepdims=True)
    aff_ref[...] = aff
    lane = lax.broadcasted_iota(jnp.int32, (E, LANES), 1)
    expert = lax.broadcasted_iota(jnp.int32, (E, LANES), 0)
    rest = aff
    gates = jnp.zeros((x_ref.shape[0], LANES), F32)
    for j in range(GATE_TERMS):
        term = rest.astype(BF16)
        rest = rest - term.astype(F32)
        place = (lane == GATE_TERMS * expert + j).astype(BF16)
        gates = gates + lax.dot_general(term, place, (((0,), (0,)), ((), ())), preferred_element_type=F32)
    h_ref[:, :D] = xn.astype(h_ref.dtype)
    h_ref[:, D:] = gates.astype(h_ref.dtype)


def _router(x2, ln_g, router, tm=512):
    T, D = x2.shape
    E = router.shape[1]
    assert GATE_TERMS * E <= LANES
    return pl.pallas_call(
        _router_kernel,
        grid=(T // tm,),
        in_specs=[pl.BlockSpec((tm, D), lambda i: (i, 0)),
                  pl.BlockSpec((1, D), lambda i: (0, 0)),
                  pl.BlockSpec((E, D), lambda i: (0, 0))],
        out_specs=[pl.BlockSpec((E, tm), lambda i: (0, i)),
                   pl.BlockSpec((tm, D + LANES), lambda i: (i, 0))],
        out_shape=[jax.ShapeDtypeStruct((E, T), F32), jax.ShapeDtypeStruct((T, D + LANES), BF16)],
        compiler_params=_params("parallel"),
        name="router",
    )(x2, ln_g.reshape(1, D).astype(F32), router.T.astype(F32))


def _cumsum_excl(mask, upper, lower_strict):
    incl = jnp.dot(mask.astype(BF16), upper, preferred_element_type=F32)
    before = jnp.dot(lower_strict, incl.astype(BF16), preferred_element_type=F32)
    return incl + before[:, CHUNK - 1:CHUNK] - mask


def _select_kernel(aff_ref, slot_ref, pos_ref, *, cap):
    aff = aff_ref[0]
    nc = aff.shape[0]
    bits = pltpu.bitcast(aff, jnp.int32)

    def count(cond):
        c = jnp.sum(jnp.where(cond, 1.0, 0.0), axis=0, keepdims=True)
        return jnp.sum(c, axis=1, keepdims=True)

    def bit_step(i, prefix):
        cand = prefix | (jnp.int32(1) << (30 - i))
        return jnp.where(count(bits >= cand) >= cap, cand, prefix)

    thr = lax.fori_loop(0, 31, bit_step, jnp.zeros((1, 1), jnp.int32))
    r = lax.broadcasted_iota(jnp.int32, (CHUNK, CHUNK), 0)
    c = lax.broadcasted_iota(jnp.int32, (CHUNK, CHUNK), 1)
    upper = (r <= c).astype(BF16)
    r = lax.broadcasted_iota(jnp.int32, (nc, nc), 0)
    c = lax.broadcasted_iota(jnp.int32, (nc, nc), 1)
    lower_strict = (c < r).astype(BF16)
    gt = bits > thr
    eq = jnp.where(bits == thr, 1.0, 0.0)
    need = cap - count(gt)
    sel = jnp.where(gt | ((eq > 0) & (_cumsum_excl(eq, upper, lower_strict) < need)), 1.0, 0.0)
    pos = _cumsum_excl(sel, upper, lower_strict).astype(jnp.int32)
    pos_ref[0] = pos
    slot_ref[0] = jnp.where(sel > 0, pos, -1)


def _select(aff_t, cap):
    E, T = aff_t.shape
    nc = T // CHUNK
    assert nc <= CHUNK and nc % 8 == 0
    spec = pl.BlockSpec((1, nc, CHUNK), lambda e: (e, 0, 0))
    slot, pos = pl.pallas_call(
        functools.partial(_select_kernel, cap=cap),
        grid=(E,), in_specs=[spec], out_specs=[spec, spec],
        out_shape=[jax.ShapeDtypeStruct((E, nc, CHUNK), jnp.int32)] * 2,
        compiler_params=_params("parallel"),
        name="select_topk",
    )(aff_t.reshape(E, nc, CHUNK))
    return slot.reshape(E, T), pos.reshape(E, T)


GATHER_STEP = 96
GATHER_W = GATHER_STEP + ROWS16
GATHER_STACK = 8


def _gather_kernel(starts_ref, h_ref, slot_ref, xe_ref, stage_ref, extra_ref, tail_ref, sem, extra_sem, *, cap):
    b = pl.program_id(0)
    nb = pl.num_programs(0)
    E = slot_ref.shape[0]
    buf = b % 2
    h = h_ref[...]
    row = lax.broadcasted_iota(jnp.int32, (GATHER_W, 1), 0)

    @pl.when(b == 0)
    def _():
        tail_ref[...] = jnp.zeros_like(tail_ref)

    def first_row(e, blk):
        return pl.multiple_of((starts_ref[e, blk] // ROWS16) * ROWS16, ROWS16)

    def copy(e, blk):
        return pltpu.make_async_copy(stage_ref.at[blk % 2, e], xe_ref.at[e, pl.ds(first_row(e, blk), GATHER_W)],
                                     sem.at[blk % 2, e])

    def onehot(e, k):
        return (slot_ref[pl.ds(e, 1), :] - (first_row(e, b) + k * GATHER_STEP) == row).astype(BF16)

    for e0 in range(0, E, GATHER_STACK):
        stacked = jnp.concatenate([onehot(e, 0) for e in range(e0, e0 + GATHER_STACK)], axis=0)
        rows = jnp.dot(stacked, h, preferred_element_type=F32).astype(BF16)
        for i in range(GATHER_STACK):
            stage_ref[buf, e0 + i] = rows[i * GATHER_W:(i + 1) * GATHER_W]

    def per_expert(e, carry):
        base = first_row(e, b)
        end = starts_ref[e, b + 1] - base
        group = pl.multiple_of((end // ROWS16) * ROWS16, ROWS16)
        last = jnp.maximum(group - 1, 0) // GATHER_STEP

        def window(k):
            return jnp.dot(onehot(e, k), h, preferred_element_type=F32).astype(BF16)

        stage_ref[buf, e, pl.ds(0, ROWS16), :] += tail_ref[e]

        @pl.when(last == 0)
        def _():
            tail_ref[e] = stage_ref[buf, e, pl.ds(group, ROWS16), :]

        @pl.when(b > 0)
        def _():
            copy(e, b - 1).wait()

        copy(e, b).start()

        def extra(k, inner):
            extra_ref[...] = window(k)

            @pl.when(k == last)
            def _():
                tail_ref[e] = extra_ref[pl.ds(pl.multiple_of(group - k * GATHER_STEP, ROWS16), ROWS16), :]

            cp = pltpu.make_async_copy(extra_ref, xe_ref.at[e, pl.ds(base + k * GATHER_STEP, GATHER_W)],
                                       extra_sem.at[0])
            cp.start()
            cp.wait()
            return inner

        return lax.fori_loop(1, last + 1, extra, carry)

    lax.fori_loop(0, E, per_expert, 0)

    @pl.when(b == nb - 1)
    def _():
        extra_ref[...] = jnp.zeros_like(extra_ref)

        def drain(e, carry):
            copy(e, b).wait()
            cp = pltpu.make_async_copy(extra_ref, xe_ref.at[e, pl.ds(cap, GATHER_W)], extra_sem.at[0])
            cp.start()
            cp.wait()
            return carry
        lax.fori_loop(0, E, drain, 0)


def _gather(h, slot, starts, cap):
    T, D = h.shape
    E = slot.shape[0]
    tb = min(COMBINE_TB, T)
    rows = cap + GATHER_W
    return pl.pallas_call(
        functools.partial(_gather_kernel, cap=cap),
        grid_spec=pltpu.PrefetchScalarGridSpec(
            num_scalar_prefetch=1,
            grid=(T // tb,),
            in_specs=[pl.BlockSpec((tb, D), lambda i, st: (i, 0)),
                      pl.BlockSpec((E, tb), lambda i, st: (0, i))],
            out_specs=pl.BlockSpec(memory_space=pl.ANY),
            scratch_shapes=[pltpu.VMEM((2, E, GATHER_W, D), BF16),
                            pltpu.VMEM((GATHER_W, D), BF16),
                            pltpu.VMEM((E, ROWS16, D), BF16),
                            pltpu.SemaphoreType.DMA((2, E)),
                            pltpu.SemaphoreType.DMA((1,))]),
        out_shape=jax.ShapeDtypeStruct((E, rows, D), BF16),
        compiler_params=_params("arbitrary"),
        name="moe_gather",
    )(starts, h, slot)


def _combine_kernel(starts_ref, x_ref, slot_ref, first_ref, g_ref, ye_ref, o_ref, win_ref, extra_ref, sem,
                    extra_sem, *, cap, final_norm):
    b = pl.program_id(0)
    nb = pl.num_programs(0)
    E = ye_ref.shape[0]
    W = COMBINE_W

    def first_row(e, blk, k):
        return (starts_ref[e, blk] // ROWS16) * ROWS16 + k * W

    def window_start(e, blk, k):
        return pl.multiple_of(jnp.minimum(first_row(e, blk, k), cap - W), ROWS16)

    def copy(e, blk):
        buf = blk % 2
        return pltpu.make_async_copy(ye_ref.at[e, pl.ds(window_start(e, blk, 0), W)],
                                     win_ref.at[buf, pl.ds(e * W, W)], sem.at[buf, e])

    @pl.when(b == 0)
    def _():
        for e in range(E):
            copy(e, b).start()

    @pl.when(b + 1 < nb)
    def _():
        for e in range(E):
            copy(e, b + 1).start()

    slots = slot_ref[...]
    first = first_ref[0]
    rel = slots - jnp.minimum(first, cap - W)
    rel = jnp.where((slots >= first) & (rel >= 0) & (rel < W), rel, -1).astype(F32).astype(BF16)
    lane = lax.broadcasted_iota(jnp.int32, (E, E * W), 1)
    expert = lax.broadcasted_iota(jnp.int32, (E, E * W), 0)
    expand = (lane // W == expert).astype(BF16)
    wanted = (lax.broadcasted_iota(jnp.int32, (1, E * W), 1) % W).astype(F32)
    onehot = (jnp.dot(rel, expand, preferred_element_type=F32) == wanted).astype(BF16)

    for e in range(E):
        copy(e, b).wait()
    o_ref[...] = x_ref[...] + jnp.dot(onehot, win_ref[b % 2], preferred_element_type=F32)

    col = lax.broadcasted_iota(jnp.int32, (1, W), 1)
    expert_lane = lax.broadcasted_iota(jnp.int32, (1, E), 1)

    def per_expert(e, carry):
        n_win = (starts_ref[e, b + 1] - first_row(e, b, 0) + W - 1) // W

        def extra(k, inner):
            cp = pltpu.make_async_copy(ye_ref.at[e, pl.ds(window_start(e, b, k), W)], extra_ref, extra_sem.at[0])
            cp.start()
            cp.wait()
            slot = jnp.max(jnp.where(expert_lane == e, slots, -1), axis=1, keepdims=True)
            oh = ((slot == window_start(e, b, k) + col) & (slot >= first_row(e, b, k))).astype(BF16)
            o_ref[...] += jnp.dot(oh, extra_ref[...], preferred_element_type=F32)
            return inner

        return lax.fori_loop(1, n_win, extra, carry)

    lax.fori_loop(0, E, per_expert, 0)
    if final_norm:
        o_ref[...] = _rms(o_ref[...], g_ref[...])


def _combine(x2, slot_t, starts, ye, final_g=None):
    T, D = x2.shape
    gain = jnp.ones((1, D), F32) if final_g is None else final_g.reshape(1, D).astype(F32)
    E, cap, _ = ye.shape
    tb = min(COMBINE_TB, T)
    nb = T // tb
    assert cap >= COMBINE_W and cap % ROWS16 == 0 and COMBINE_W % ROWS16 == 0
    first = ((starts[:, :nb] // ROWS16) * ROWS16).T.reshape(nb, 1, E)
    return pl.pallas_call(
        functools.partial(_combine_kernel, cap=cap, final_norm=final_g is not None),
        grid_spec=pltpu.PrefetchScalarGridSpec(
            num_scalar_prefetch=1,
            grid=(nb,),
            in_specs=[pl.BlockSpec((tb, D), lambda i, st: (i, 0)),
                      pl.BlockSpec((tb, E), lambda i, st: (i, 0)),
                      pl.BlockSpec((1, 1, E), lambda i, st: (i, 0, 0)),
                      pl.BlockSpec((1, D), lambda i, st: (0, 0)),
                      pl.BlockSpec(memory_space=pl.ANY)],
            out_specs=pl.BlockSpec((tb, D), lambda i, st: (i, 0)),
            scratch_shapes=[pltpu.VMEM((2, E * COMBINE_W, D), BF16),
                            pltpu.VMEM((COMBINE_W, D), BF16),
                            pltpu.SemaphoreType.DMA((2, E)),
                            pltpu.SemaphoreType.DMA((1,))]),
        out_shape=jax.ShapeDtypeStruct((T, D), F32),
        compiler_params=_params("arbitrary"),
        name="moe_combine",
    )(starts, x2, slot_t, first, gain, ye)


def _t5_bucket(rel):
    half = T5_BUCKETS // 2
    max_exact = half // 2
    n = np.abs(rel)
    large = max_exact + (np.log(np.maximum(n, 1) / max_exact) / np.log(T5_MAX_DISTANCE / max_exact)
                         * (half - max_exact)).astype(np.int32)
    large = np.minimum(large, half - 1)
    return (np.where(rel > 0, half, 0) + np.where(n < max_exact, n, large)).astype(np.int32)


def _dilated_tiles(L, radius):
    tq = min(LANES, L)
    kw = min(tq + 2 * radius, L)
    return tq, kw


def _dilated_bias(t5_table, g, L, window, dilation):
    radius = window // (2 * dilation)
    tq, kw = _dilated_tiles(L, radius)
    deltas = (0, -((kw - tq) // 2), tq - kw)
    period = 2 * kw - 1
    rel = np.arange(period + 1) - (kw - 1)
    tab = t5_table[:, g * A_HEADS_PER_GROUP:(g + 1) * A_HEADS_PER_GROUP].astype(F32)
    by_rel = jnp.where((np.abs(rel) <= radius)[None], tab[_t5_bucket(rel * dilation)].T, NEG_INF)
    skew = jnp.tile(by_rel, (1, tq))[:, :tq * period].reshape(A_HEADS_PER_GROUP, tq, period)
    bias = jnp.stack([skew[:, :, d + kw - 1:d + 2 * kw - 1] for d in deltas])
    return bias.reshape(3, A_HEADS_PER_GROUP // 2, 2 * tq, kw)


def _dilated_kernel(q_ref, k_ref, v_ref, bias_ref, o_ref, lse_ref, *, L, tq, kw):
    nq = L // tq
    lane = lax.broadcasted_iota(jnp.int32, (1, LANES), 1)
    lo = lane < HEAD_DIM

    def body(i, carry):
        q0 = pl.multiple_of(i * tq, tq)
        start = jnp.clip(q0 - (kw - tq) // 2, 0, L - kw)
        start = pl.multiple_of(start, ROWS16)
        var = jnp.where(i == 0, 0, jnp.where(i == nq - 1, 2, 1))
        n_pairs = A_HEADS_PER_GROUP // 2
        cols = [slice(pair * LANES, (pair + 1) * LANES) for pair in range(n_pairs)]
        scores = []
        for pair in range(n_pairs):
            q2 = q_ref[0, 0, pl.ds(q0, tq), cols[pair]] * (HEAD_DIM ** -0.5)
            zero = jnp.zeros_like(q2)
            qs = jnp.concatenate([jnp.where(lo, q2, zero), jnp.where(lo, zero, q2)], axis=0)
            k2 = k_ref[0, 0, pl.ds(start, kw), cols[pair]]
            scores.append(lax.dot_general(qs, k2, _NT, preferred_element_type=F32) + bias_ref[var, pair])
        probs, dens, maxs = [], [], []
        for s in scores:
            m = jnp.max(s, axis=-1, keepdims=True)
            p = jnp.exp(s - m)
            maxs.append(m)
            dens.append(jnp.sum(p, axis=-1, keepdims=True))
            probs.append(p.astype(BF16))
        for pair in range(n_pairs):
            v2 = v_ref[0, 0, pl.ds(start, kw), cols[pair]]
            o = jnp.dot(probs[pair], v2, preferred_element_type=F32) * (1.0 / dens[pair])
            lse = maxs[pair] + jnp.log(dens[pair])
            o_ref[0, 0, pl.ds(q0, tq), cols[pair]] = jnp.where(lo, o[:tq], o[tq:])
            lse_ref[0, 0, pl.ds(q0, tq), cols[pair]] = jnp.where(lo, lse[:tq], lse[tq:])
        return carry

    lax.fori_loop(0, nq, body, 0, unroll=next(u for u in (4, 2, 1) if nq % u == 0))


def _dilated_attention(qkv, bias, window):
    Bn, dilation, L, _ = qkv.shape
    radius = window // (2 * dilation)
    tq, kw = _dilated_tiles(L, radius)
    assert L % tq == 0 and kw % ROWS16 == 0 and tq % ROWS16 == 0

    def spec(which):
        return pl.BlockSpec((1, 1, L, A_OUT), lambda b, r: (b, r, 0, which))

    o, lse = pl.pallas_call(
        functools.partial(_dilated_kernel, L=L, tq=tq, kw=kw),
        grid=(Bn, dilation),
        in_specs=[spec(0), spec(1), spec(2), pl.BlockSpec(bias.shape, lambda b, r: (0, 0, 0, 0))],
        out_specs=[spec(0), spec(0)],
        out_shape=[jax.ShapeDtypeStruct((Bn, dilation, L, A_OUT), F32)] * 2,
        compiler_params=_params("parallel", "parallel"),
        name="dilated_attention",
    )(qkv, qkv, qkv, bias)
    return o, lse


def _merge_kernel(*refs):
    n = len(A_PATTERNS)
    o_refs, l_refs, out_ref = refs[:n], refs[n:2 * n], refs[2 * n]
    scratch = refs[2 * n + 1:]
    tm = out_ref.shape[0]

    def token_order(ref, scr, dilation):
        if dilation == 1:
            return ref[0, 0]
        tiles = A_OUT // LANES
        for r in range(dilation):
            for c in range(tiles):
                scr[c, pl.ds(r, tm // dilation, stride=dilation), :] = ref[0, r, :, c * LANES:(c + 1) * LANES]
        return jnp.concatenate([scr[c] for c in range(tiles)], axis=1)

    os_ = [token_order(o_refs[g], scratch[2 * g], d) for g, (_, d) in enumerate(A_PATTERNS)]
    ls = [token_order(l_refs[g], scratch[2 * g + 1], d) for g, (_, d) in enumerate(A_PATTERNS)]
    m = jnp.maximum(jnp.maximum(ls[0], ls[1]), ls[2])
    es = [jnp.exp(l - m) for l in ls]
    den = es[0] + es[1] + es[2]
    out = (es[0] / den) * os_[0] + (es[1] / den) * os_[1] + (es[2] / den) * os_[2]
    out_ref[...] = out.astype(out_ref.dtype)


def _merge_groups(os_, ls_, S, tm=512):
    Bn = os_[0].shape[0]
    ns = S // tm

    def spec(d):
        return pl.BlockSpec((1, d, tm // d, A_OUT), lambda i: (i // ns, 0, i % ns, 0))

    specs = [spec(d) for _, d in A_PATTERNS]
    return pl.pallas_call(
        _merge_kernel, grid=(Bn * ns,), in_specs=specs * 2,
        out_specs=pl.BlockSpec((tm, A_OUT), lambda i: (i, 0)),
        out_shape=jax.ShapeDtypeStruct((Bn * S, A_OUT), BF16),
        scratch_shapes=[pltpu.VMEM((A_OUT // LANES, tm, LANES), F32)] * (2 * len(A_PATTERNS)),
        compiler_params=_params("parallel"), name="merge_groups",
    )(*os_, *ls_)


EVEN_TOKEN_COLS = 3 * A_OUT + B_KV_RANK + LANES + B_Q_RANK


def _in_proj_even_kernel(x_ref, w_ref, g_ref, u0_ref, *rest):
    n_dil = len(A_PATTERNS) - 1
    dil_refs, scr = rest[:n_dil], rest[n_dil]
    tm = x_ref.shape[0]
    acc = jnp.dot(_rms(x_ref[...], g_ref[...]).astype(BF16), w_ref[...], preferred_element_type=F32)
    u0_ref[...] = acc[:, :EVEN_TOKEN_COLS].astype(u0_ref.dtype)
    for i, (_, d) in enumerate(A_PATTERNS[1:]):
        c0 = EVEN_TOKEN_COLS + i * 3 * A_OUT
        for c in range(3 * A_OUT // LANES):
            scr[c] = acc[:, c0 + c * LANES:c0 + (c + 1) * LANES]
        for r in range(d):
            for c in range(3 * A_OUT // LANES):
                piece = scr[c, pl.ds(r, tm // d, stride=d), :]
                dil_refs[i][0, r, :, c * LANES:(c + 1) * LANES] = piece.astype(dil_refs[i].dtype)


def _in_proj_even(x2, w, gain, Bn, S, tm=512):
    T, K = x2.shape
    ns = S // tm
    dils = [d for _, d in A_PATTERNS[1:]]
    assert S % tm == 0 and all(tm % (ROWS16 * d) == 0 for d in dils)
    return pl.pallas_call(
        _in_proj_even_kernel,
        grid=(T // tm,),
        in_specs=[pl.BlockSpec((tm, K), lambda i: (i, 0)),
                  pl.BlockSpec(w.shape, lambda i: (0, 0)),
                  pl.BlockSpec((1, K), lambda i: (0, 0))],
        out_specs=[pl.BlockSpec((tm, EVEN_TOKEN_COLS), lambda i: (i, 0))]
        + [pl.BlockSpec((1, d, tm // d, 3 * A_OUT), lambda i: (i // ns, 0, i % ns, 0)) for d in dils],
        out_shape=[jax.ShapeDtypeStruct((T, EVEN_TOKEN_COLS), BF16)]
        + [jax.ShapeDtypeStruct((Bn, d, S // d, 3 * A_OUT), BF16) for d in dils],
        scratch_shapes=[pltpu.VMEM((3 * A_OUT // LANES, tm, LANES), F32)],
        compiler_params=_params("parallel"),
        name="in_proj_even",
    )(x2, w, gain.reshape(1, K).astype(F32))


def _rope_tables(S):
    inv = 1.0 / (ROPE_THETA ** (np.arange(0, B_ROPE, 2, dtype=np.float32) / B_ROPE))
    ang = np.arange(S, dtype=np.float32)[:, None] * inv[None]
    cos, sin = np.cos(ang), np.sin(ang)
    half = B_ROPE // 2
    c = np.ones((S, LANES), np.float32)
    c[:, B_NOPE:B_NOPE + half] = cos
    c[:, B_NOPE + half:B_NOPE + B_ROPE] = cos
    s_lo = np.zeros((S, LANES), np.float32)
    s_lo[:, B_NOPE:B_NOPE + half] = -sin
    s_hi = np.zeros((S, LANES), np.float32)
    s_hi[:, B_NOPE + half:B_NOPE + B_ROPE] = sin
    return jnp.asarray(c), jnp.asarray(s_lo), jnp.asarray(s_hi)


def _rope(x, c, s_lo, s_hi):
    half = B_ROPE // 2
    return x * c + pltpu.roll(x, LANES - half, 1) * s_lo + pltpu.roll(x, half, 1) * s_hi


def _mla_prep_kernel(cq_ref, ckv_ref, kr_ref, gq_ref, gkv_ref, wq_ref, wk_ref, wv_ref,
                     c_ref, slo_ref, shi_ref, q_ref, k_ref, v_ref):
    c, s_lo, s_hi = c_ref[...], slo_ref[...], shi_ref[...]
    scale = (B_NOPE + B_ROPE) ** -0.5 * LOG2_E
    xq = _rms(cq_ref[...], gq_ref[...]).astype(BF16)
    q = jnp.dot(xq, wq_ref[...], preferred_element_type=F32)
    xkv = _rms(ckv_ref[...], gkv_ref[...]).astype(BF16)
    k = jnp.dot(xkv, wk_ref[...], preferred_element_type=F32)
    vt = lax.dot_general(wv_ref[...], xkv, _NT, preferred_element_type=F32)
    ones_row = lax.broadcasted_iota(jnp.int32, vt.shape, 0) % LANES == B_V
    v_ref[0, 0] = jnp.where(ones_row, 1.0, vt).astype(v_ref.dtype)
    k_rope = _rope(kr_ref[...].astype(F32), c, s_lo, s_hi)
    for h in range(B_HEADS):
        cs = slice(h * LANES, (h + 1) * LANES)
        q_ref[:, cs] = (_rope(q[:, cs], c, s_lo, s_hi) * scale).astype(q_ref.dtype)
        k_ref[:, cs] = (k[:, cs] + k_rope).astype(k_ref.dtype)


MLA_KEY_CHUNK = 1024


def _mla_prep(u3, g_qa, g_kva, wq, wk, wv):
    Bn, S, C = u3.shape
    ts = min(MLA_KEY_CHUNK, S)
    tabs = _rope_tables(S)
    ns = S // ts
    u2 = u3.reshape(Bn * S, C)
    row = lambda b, i: b * ns + i
    a_cols = 3 * A_OUT
    off_kv, off_kr, off_q = a_cols // B_KV_RANK, (a_cols + B_KV_RANK) // LANES, (a_cols + B_KV_RANK + LANES) // B_Q_RANK
    full = lambda a: pl.BlockSpec(a.shape, lambda b, i: (0, 0))
    tab_spec = pl.BlockSpec((ts, LANES), lambda b, i: (i, 0))
    hw = B_HEADS * LANES
    return pl.pallas_call(
        _mla_prep_kernel,
        grid=(Bn, ns),
        in_specs=[pl.BlockSpec((ts, B_Q_RANK), lambda b, i: (row(b, i), off_q)),
                  pl.BlockSpec((ts, B_KV_RANK), lambda b, i: (row(b, i), off_kv)),
                  pl.BlockSpec((ts, LANES), lambda b, i: (row(b, i), off_kr)),
                  full(g_qa), full(g_kva), full(wq), full(wk), full(wv),
                  tab_spec, tab_spec, tab_spec],
        out_specs=[pl.BlockSpec((ts, hw), lambda b, i: (row(b, i), 0)),
                   pl.BlockSpec((ts, hw), lambda b, i: (row(b, i), 0)),
                   pl.BlockSpec((1, 1, hw, ts), lambda b, i: (b, i, 0, 0))],
        out_shape=[jax.ShapeDtypeStruct((Bn * S, hw), BF16),
                   jax.ShapeDtypeStruct((Bn * S, hw), BF16),
                   jax.ShapeDtypeStruct((Bn, ns, hw, ts), BF16)],
        compiler_params=_params("parallel", "parallel"),
        name="mla_prep",
    )(u2, u2, u2, g_qa, g_kva, wq, wk, wv, *tabs)


def _mla_kernel(q_ref, k_ref, v_ref, o_ref):
    nk, _, tk = v_ref.shape[1:]
    tq = q_ref.shape[1]
    lane = lax.broadcasted_iota(jnp.int32, (1, LANES), 1)
    qs = [q_ref[0, :, sub * LANES:(sub + 1) * LANES] for sub in range(2)]

    def body(j, carry):
        k0 = pl.multiple_of(j * tk, tk)
        scores = [lax.dot_general(k_ref[0, pl.ds(k0, tk), sub * LANES:(sub + 1) * LANES], qs[sub], _NT,
                                  preferred_element_type=F32) for sub in range(2)]
        new = []
        for sub in range(2):
            m, acc = carry[sub]
            s = scores[sub]
            m_new = jnp.maximum(m, jnp.max(s, axis=0, keepdims=True))
            p = jnp.exp2(s - m_new).astype(BF16)
            vt = v_ref[0, j, sub * LANES:(sub + 1) * LANES, :]
            acc = jnp.exp2(m - m_new) * acc + jnp.dot(vt, p, preferred_element_type=F32)
            new.append((m_new, acc))
        return tuple(new)

    init = tuple((jnp.full((1, tq), NEG_INF, F32), jnp.zeros((LANES, tq), F32)) for _ in range(2))
    res = lax.fori_loop(0, nk, body, init, unroll=True)
    outs = [(acc * (1.0 / acc[B_V:B_V + 1])).T for (_, acc) in res]
    o_ref[0] = jnp.where(lane < B_V, outs[0], pltpu.roll(outs[1], B_V, 1)).astype(o_ref.dtype)


def _mla_attention(q, k, vt, Bn, S, tq=2048):
    tq = min(tq, S)
    nk, _, tk = vt.shape[1:]
    hw = B_HEADS * LANES
    q3, k3 = q.reshape(Bn, S, hw), k.reshape(Bn, S, hw)
    o = pl.pallas_call(
        _mla_kernel,
        grid=(Bn, B_HEADS // 2, S // tq),
        in_specs=[pl.BlockSpec((1, tq, 2 * LANES), lambda b, h, i: (b, i, h)),
                  pl.BlockSpec((1, S, 2 * LANES), lambda b, h, i: (b, 0, h)),
                  pl.BlockSpec((1, nk, 2 * LANES, tk), lambda b, h, i: (b, 0, h, 0))],
        out_specs=pl.BlockSpec((1, tq, LANES), lambda b, h, i: (b, i, h)),
        out_shape=jax.ShapeDtypeStruct((Bn, S, B_OUT), BF16),
        compiler_params=_params("parallel", "parallel", "parallel"),
        name="mla_attention",
    )(q3, k3, vt)
    return o.reshape(Bn * S, B_OUT)


def _natten_bias(rpb):
    W = GRID_W
    c = np.arange(W)[:, None]
    kc = np.arange(W)[None, :]
    wstart = np.clip(c - C_WIN_COLS // 2, 0, W - C_WIN_COLS)
    col_valid = (kc >= wstart) & (kc < wstart + C_WIN_COLS)
    n_dcol = 2 * C_WIN_COLS - 1
    left = (W - 1) - (C_WIN_COLS - 1)
    by_rel = jnp.pad(rpb.astype(F32), ((0, 0), (0, 0), (left, 2 * W - left - n_dcol)))
    skew = jnp.tile(by_rel, (1, 1, W))[:, :, :W * (2 * W - 1)].reshape(C_HEADS, -1, W, 2 * W - 1)
    cols = jnp.where(col_valid[None, None], skew[:, :, :, W - 1:2 * W - 1], NEG_INF)
    b = jnp.stack([cols[:, C_WIN_ROWS - 1 - off:2 * C_WIN_ROWS - 1 - off] for off in range(C_WIN_ROWS)])
    b = jnp.transpose(b, (0, 1, 3, 2, 4))
    return b.reshape(C_WIN_ROWS, C_HEADS // 2, 2 * W, C_WIN_ROWS * W)


NATTEN_ROWS = 4


def _natten_kernel(q_ref, k_ref, v_ref, *rest, rows):
    bias_refs, o_ref = rest[:NATTEN_ROWS], rest[NATTEN_ROWS]
    nk = C_WIN_ROWS * GRID_W
    lane = lax.broadcasted_iota(jnp.int32, (1, LANES), 1)
    lo = lane < HEAD_DIM
    n_pairs = C_HEADS // 2
    cols = [slice(pair * LANES, (pair + 1) * LANES) for pair in range(n_pairs)]
    for j in range(NATTEN_ROWS):
        r = pl.program_id(1) * NATTEN_ROWS + j
        r0 = jnp.clip(r - C_WIN_ROWS // 2, 0, rows - C_WIN_ROWS)
        k0 = pl.multiple_of(r0 * GRID_W, GRID_W)
        q_rows = slice(j * GRID_W, (j + 1) * GRID_W)
        scores = []
        for pair in range(n_pairs):
            q2 = q_ref[0, q_rows, cols[pair]] * (HEAD_DIM ** -0.5)
            zero = jnp.zeros_like(q2)
            qs = jnp.concatenate([jnp.where(lo, q2, zero), jnp.where(lo, zero, q2)], axis=0)
            k2 = k_ref[0, pl.ds(k0, nk), cols[pair]]
            scores.append(lax.dot_general(qs, k2, _NT, preferred_element_type=F32) + bias_refs[j][0, pair])
        probs, dens = [], []
        for s in scores:
            p = jnp.exp(s - jnp.max(s, axis=-1, keepdims=True))
            dens.append(jnp.sum(p, axis=-1, keepdims=True))
            probs.append(p.astype(BF16))
        for pair in range(n_pairs):
            v2 = v_ref[0, pl.ds(k0, nk), cols[pair]]
            o = jnp.dot(probs[pair], v2, preferred_element_type=F32) * (1.0 / dens[pair])
            o_ref[0, q_rows, cols[pair]] = jnp.where(lo, o[:GRID_W], o[GRID_W:]).astype(o_ref.dtype)


def _natten(u3, bias):
    Bn, S, _ = u3.shape
    rows = S // GRID_W
    assert rows >= C_WIN_ROWS and rows % NATTEN_ROWS == 0

    def bias_spec(j):
        def variant(b, i):
            r = i * NATTEN_ROWS + j
            return (r - jnp.clip(r - C_WIN_ROWS // 2, 0, rows - C_WIN_ROWS), 0, 0, 0)
        return pl.BlockSpec((1, C_HEADS // 2, 2 * GRID_W, C_WIN_ROWS * GRID_W), variant)

    q_spec = pl.BlockSpec((1, NATTEN_ROWS * GRID_W, C_OUT), lambda b, i: (b, i, 0))
    o = pl.pallas_call(
        functools.partial(_natten_kernel, rows=rows),
        grid=(Bn, rows // NATTEN_ROWS),
        in_specs=[q_spec,
                  pl.BlockSpec((1, S, C_OUT), lambda b, i: (b, 0, 1), pipeline_mode=pl.Buffered(1)),
                  pl.BlockSpec((1, S, C_OUT), lambda b, i: (b, 0, 2), pipeline_mode=pl.Buffered(1))]
        + [bias_spec(j) for j in range(NATTEN_ROWS)],
        out_specs=q_spec,
        out_shape=jax.ShapeDtypeStruct((Bn, S, C_OUT), BF16),
        compiler_params=_params("parallel", "arbitrary"),
        name="natten",
    )(u3, u3, u3, *([bias] * NATTEN_ROWS))
    return o.reshape(Bn * S, C_OUT)


def _moe(x2, ln_g, router, wg, wu, wd, layer, final_g=None):
    T, D = x2.shape
    cap = CAPACITY_FACTOR * T // N_EXPERTS
    aff_t, h = _router(x2, ln_g, router)
    slot, pos = _select(aff_t, cap)
    tb = min(COMBINE_TB, T)
    starts = jnp.concatenate([pos[:, ::tb], jnp.full((N_EXPERTS, 1), cap, jnp.int32)], axis=1)
    xe = _gather(h, slot, starts, cap)
    ye = _expert_ffn(xe, cap, wg, wu, wd, layer)
    return _combine(x2, slot.T, starts, ye, final_g)


def _trunk(x, p):
    Bn, S, D = x.shape
    T = Bn * S
    x2 = x.reshape(T, D)
    u0, *u_dil = _in_proj_even(x2, p["w_in_even"], p["ln_mix_g"][0], Bn, S)
    u3 = u0.reshape(Bn, S, EVEN_TOKEN_COLS)
    os_, ls_ = [], []
    for g, (window, dilation) in enumerate(A_PATTERNS):
        bias = _dilated_bias(p["t5_table"], g, S // dilation, window, dilation)
        qkv = u3.reshape(Bn, 1, S, EVEN_TOKEN_COLS) if g == 0 else u_dil[g - 1]
        o, l = _dilated_attention(qkv, bias, window)
        os_.append(o)
        ls_.append(l)
    a = _merge_groups(os_, ls_, S)
    q, k, v = _mla_prep(u3, p["g_qa"], p["g_kva"], p["wq"], p["wk"], p["wv"])
    b = _mla_attention(q, k, v, Bn, S)
    x2 = _matmul([a, b], [p["w_out_even"][:A_OUT], p["w_out_even"][A_OUT:]], res=x2, out_dtype=F32, tm=1024)
    x2 = _moe(x2, p["ln_ffn_g"][0], p["router"][0], p["w_gate"], p["w_up"], p["w_down"], 0)
    u = _matmul(x2, p["w_in_odd"], gain=p["ln_mix_g"][1], out_dtype=BF16)
    c = _natten(u.reshape(Bn, S, C_IN), p["natten_bias"])
    x2 = _matmul(c, p["w_out_odd"], res=x2, out_dtype=F32, tm=1024)
    x2 = _moe(x2, p["ln_ffn_g"][1], p["router"][1], p["w_gate"], p["w_up"], p["w_down"], 1,
              final_g=p["final_g"])
    return x2.reshape(Bn, S, D)


def _head_tiles(w, n_heads, per_head, keep):
    K = w.shape[0]
    w = w.reshape(K, n_heads, per_head)[:, :, :keep]
    return jnp.pad(w, ((0, 0), (0, 0), (0, LANES - keep))).reshape(K, n_heads * LANES)


def _prepare(ln_mix_g, w_in_even, g_qa, w_qb, g_kva, w_kvb, t5_table, w_out_even,
             w_in_odd, rpb, w_out_odd, ln_ffn_g, router, w_gate, w_up, w_down, final_g):
    w_in = w_in_even[0]
    w_a, w_cq = w_in[:, :A_IN], w_in[:, A_IN:A_IN + B_Q_RANK]
    w_ckv = w_in[:, A_IN + B_Q_RANK:A_IN + B_Q_RANK + B_KV_RANK]
    w_kr = w_in[:, A_IN + B_Q_RANK + B_KV_RANK:]
    w_kr_tile = jnp.pad(w_kr, ((0, 0), (B_NOPE, LANES - B_NOPE - B_ROPE)))
    w_a = w_a.reshape(-1, 3, A_GROUPS, A_OUT)
    groups = [w_a[:, :, g].reshape(-1, 3 * A_OUT) for g in range(A_GROUPS)]
    w_even = jnp.concatenate([groups[0], w_ckv, w_kr_tile, w_cq] + groups[1:], axis=1)
    w_kv = w_kvb[0].reshape(B_KV_RANK, B_HEADS, B_NOPE + B_V)
    return dict(
        ln_mix_g=ln_mix_g, ln_ffn_g=ln_ffn_g, final_g=final_g, t5_table=t5_table,
        w_in_even=w_even.astype(BF16),
        g_qa=g_qa[0].reshape(1, -1).astype(F32), g_kva=g_kva[0].reshape(1, -1).astype(F32),
        wq=_head_tiles(w_qb[0], B_HEADS, B_NOPE + B_ROPE, B_NOPE + B_ROPE).astype(BF16),
        wk=_head_tiles(w_kvb[0], B_HEADS, B_NOPE + B_V, B_NOPE).astype(BF16),
        wv=_head_tiles(w_kv[:, :, B_NOPE:].reshape(B_KV_RANK, B_OUT), B_HEADS, B_V, B_V).T.astype(BF16),
        w_out_even=w_out_even[0].astype(BF16), w_in_odd=w_in_odd[0].astype(BF16),
        natten_bias=_natten_bias(rpb[0]), w_out_odd=w_out_odd[0].astype(BF16), router=router,
        w_gate=w_gate, w_up=w_up, w_down=w_down,
    )


def kernel(x_prompt, x_sample, ln_mix_g, w_in_even, g_qa, w_qb, g_kva, w_kvb, t5_table, w_out_even,
           w_in_odd, rpb, w_out_odd, ln_ffn_g, router, w_gate, w_up, w_down, final_g):
    p = _prepare(ln_mix_g, w_in_even, g_qa, w_qb, g_kva, w_kvb, t5_table, w_out_even,
                 w_in_odd, rpb, w_out_odd, ln_ffn_g, router, w_gate, w_up, w_down, final_g)
    return _trunk(x_prompt, p), _trunk(x_sample, p)
```

```python
import functools

import jax
import jax.numpy as jnp
import numpy as np
from jax import lax
from jax.experimental import pallas as pl
from jax.experimental.pallas import tpu as pltpu

D_MODEL = 1024
RMS_EPS = 1e-6
HEAD_DIM = 64
NEG_INF = -1e30
A_PATTERNS = ((128, 1), (512, 4), (2048, 16))
A_GROUPS = 3
A_HEADS_PER_GROUP = 4
A_HEADS = A_GROUPS * A_HEADS_PER_GROUP
A_IN = 3 * A_HEADS * HEAD_DIM
A_OUT = A_HEADS_PER_GROUP * HEAD_DIM
T5_BUCKETS = 32
T5_MAX_DISTANCE = 1024
B_HEADS = 12
B_Q_RANK = 384
B_KV_RANK = 256
B_NOPE = 64
B_ROPE = 32
B_V = 64
B_OUT = B_HEADS * B_V
ROPE_THETA = 10000.0
GRID_W = 64
C_HEADS = 16
C_WIN_ROWS = 8
C_WIN_COLS = 16
C_IN = 3 * C_HEADS * HEAD_DIM
C_OUT = C_HEADS * HEAD_DIM
N_EXPERTS = 16
CAPACITY_FACTOR = 2

LANES = 128
ROWS16 = 16
VMEM_LIMIT_BYTES = 48 * 1024 * 1024

F32 = jnp.float32
BF16 = jnp.bfloat16
_NT = (((1,), (1,)), ((), ()))
LOG2_E = 1.4426950408889634


def _params(*sem):
    return pltpu.CompilerParams(dimension_semantics=sem, vmem_limit_bytes=VMEM_LIMIT_BYTES)


def _rms(x, g):
    xf = x.astype(F32)
    return xf * lax.rsqrt(jnp.mean(xf * xf, axis=-1, keepdims=True) + RMS_EPS) * g


def _mm_kernel(*refs, n_in, has_gain, has_res):
    xs, ws = refs[:n_in], refs[n_in:2 * n_in]
    pos = 2 * n_in
    g_ref = res_ref = None
    if has_gain:
        g_ref = refs[pos]
        pos += 1
    if has_res:
        res_ref = refs[pos]
        pos += 1
    o_ref = refs[pos]
    acc = None
    for x_ref, w_ref in zip(xs, ws):
        x = x_ref[...]
        if has_gain:
            x = _rms(x, g_ref[...])
        part = jnp.dot(x.astype(BF16), w_ref[...], preferred_element_type=F32)
        acc = part if acc is None else acc + part
    if has_res:
        acc = acc + res_ref[...]
    o_ref[...] = acc.astype(o_ref.dtype)


def _matmul(xs, ws, gain=None, res=None, out_dtype=F32, tm=512):
    if not isinstance(xs, (list, tuple)):
        xs, ws = [xs], [ws]
    M = xs[0].shape[0]
    N = ws[0].shape[1]
    assert M % tm == 0
    in_specs = [pl.BlockSpec((tm, x.shape[1]), lambda i: (i, 0)) for x in xs]
    in_specs += [pl.BlockSpec(w.shape, lambda i: (0, 0)) for w in ws]
    args = list(xs) + list(ws)
    if gain is not None:
        assert len(xs) == 1
        in_specs.append(pl.BlockSpec((1, gain.shape[-1]), lambda i: (0, 0)))
        args.append(gain.reshape(1, -1).astype(F32))
    if res is not None:
        in_specs.append(pl.BlockSpec((tm, N), lambda i: (i, 0)))
        args.append(res)
    return pl.pallas_call(
        functools.partial(_mm_kernel, n_in=len(xs), has_gain=gain is not None,
                          has_res=res is not None),
        grid=(M // tm,),
        in_specs=in_specs,
        out_specs=pl.BlockSpec((tm, N), lambda i: (i, 0)),
        out_shape=jax.ShapeDtypeStruct((M, N), out_dtype),
        compiler_params=_params("parallel"),
        name="fused_matmul",
    )(*args)


def _ffn_kernel(x_ref, wg32_ref, wu32_ref, wd32_ref, o_ref, wg_ref, wu_ref, wd_ref):
    D = wg_ref.shape[1]

    @pl.when(pl.program_id(1) == 0)
    def _():
        wg_ref[0] = wg32_ref[0, 0].astype(BF16)
        wu_ref[0] = wu32_ref[0, 0].astype(BF16)
        wd_ref[0] = wd32_ref[0, 0].astype(BF16)

    x = x_ref[0, :, :D]
    lane = lax.broadcasted_iota(jnp.int32, (1, LANES), 1)
    mine = lane // GATE_TERMS == pl.program_id(0)
    gate = jnp.sum(jnp.where(mine, x_ref[0, :, D:].astype(F32), 0.0), axis=1, keepdims=True)
    g = jnp.dot(x, wg_ref[0], preferred_element_type=F32)
    u = jnp.dot(x, wu_ref[0], preferred_element_type=F32)
    h = (g * jax.nn.sigmoid(g)) * u
    y = jnp.dot(h.astype(BF16), wd_ref[0], preferred_element_type=F32)
    o_ref[0] = (y * gate).astype(o_ref.dtype)


def _expert_ffn(xe, cap, wg, wu, wd, layer, tm=1024):
    E = xe.shape[0]
    D, FF = wg.shape[2], wg.shape[3]
    tm = min(tm, cap)
    assert cap % tm == 0
    return pl.pallas_call(
        _ffn_kernel,
        grid=(E, cap // tm),
        in_specs=[pl.BlockSpec((1, tm, D + LANES), lambda e, i: (e, i, 0)),
                  pl.BlockSpec((1, 1, D, FF), lambda e, i: (layer, e, 0, 0)),
                  pl.BlockSpec((1, 1, D, FF), lambda e, i: (layer, e, 0, 0)),
                  pl.BlockSpec((1, 1, FF, D), lambda e, i: (layer, e, 0, 0))],
        out_specs=pl.BlockSpec((1, tm, D), lambda e, i: (e, i, 0)),
        out_shape=jax.ShapeDtypeStruct((E, cap, D), BF16),
        scratch_shapes=[pltpu.VMEM((1, D, FF), BF16), pltpu.VMEM((1, D, FF), BF16), pltpu.VMEM((1, FF, D), BF16)],
        compiler_params=_params("parallel", "arbitrary"),
        name="expert_ffn",
    )(xe, wg, wu, wd)


CHUNK = 256
COMBINE_TB = 512
COMBINE_W = 112


GATE_TERMS = 3


def _router_kernel(x_ref, g_ref, rt_ref, aff_ref, h_ref):
    D = x_ref.shape[1]
    E = rt_ref.shape[0]
    xn = _rms(x_ref[...], g_ref[...])
    logits = lax.dot_general(rt_ref[...], xn, _NT, preferred_element_type=F32,
                             precision=lax.Precision.HIGHEST)
    e = jnp.exp(logits - jnp.max(logits, axis=0, keepdims=True))
    aff = e / jnp.sum(e, axis=0, keepdims=True)
    aff_ref[...] = aff
    lane = lax.broadcasted_iota(jnp.int32, (E, LANES), 1)
    expert = lax.broadcasted_iota(jnp.int32, (E, LANES), 0)
    rest = aff
    gates = jnp.zeros((x_ref.shape[0], LANES), F32)
    for j in range(GATE_TERMS):
        term = rest.astype(BF16)
        rest = rest - term.astype(F32)
        place = (lane == GATE_TERMS * expert + j).astype(BF16)
        gates = gates + lax.dot_general(term, place, (((0,), (0,)), ((), ())), preferred_element_type=F32)
    h_ref[:, :D] = xn.astype(h_ref.dtype)
    h_ref[:, D:] = gates.astype(h_ref.dtype)


def _router(x2, ln_g, router, tm=1024):
    T, D = x2.shape
    E = router.shape[1]
    assert GATE_TERMS * E <= LANES
    return pl.pallas_call(
        _router_kernel,
        grid=(T // tm,),
        in_specs=[pl.BlockSpec((tm, D), lambda i: (i, 0)),
                  pl.BlockSpec((1, D), lambda i: (0, 0)),
                  pl.BlockSpec((E, D), lambda i: (0, 0))],
        out_specs=[pl.BlockSpec((E, tm), lambda i: (0, i)),
                   pl.BlockSpec((tm, D + LANES), lambda i: (i, 0))],
        out_shape=[jax.ShapeDtypeStruct((E, T), F32), jax.ShapeDtypeStruct((T, D + LANES), BF16)],
        compiler_params=_params("parallel"),
        name="router",
    )(x2, ln_g.reshape(1, D).astype(F32), router.T.astype(F32))


def _cumsum_excl(mask, upper, lower_strict):
    incl = jnp.dot(mask.astype(BF16), upper, preferred_element_type=F32)
    before = jnp.dot(lower_strict, incl.astype(BF16), preferred_element_type=F32)
    return incl + before[:, CHUNK - 1:CHUNK] - mask


def _select_kernel(aff_ref, slot_ref, pos_ref, *, cap):
    aff = aff_ref[0]
    nc = aff.shape[0]
    bits = pltpu.bitcast(aff, jnp.int32)

    def count(cond):
        c = jnp.sum(jnp.where(cond, 1.0, 0.0), axis=0, keepdims=True)
        return jnp.sum(c, axis=1, keepdims=True)

    def bit_step(i, prefix):
        cand = prefix | (jnp.int32(1) << (30 - i))
        return jnp.where(count(bits >= cand) >= cap, cand, prefix)

    thr = lax.fori_loop(0, 31, bit_step, jnp.zeros((1, 1), jnp.int32))
    r = lax.broadcasted_iota(jnp.int32, (CHUNK, CHUNK), 0)
    c = lax.broadcasted_iota(jnp.int32, (CHUNK, CHUNK), 1)
    upper = (r <= c).astype(BF16)
    r = lax.broadcasted_iota(jnp.int32, (nc, nc), 0)
    c = lax.broadcasted_iota(jnp.int32, (nc, nc), 1)
    lower_strict = (c < r).astype(BF16)
    gt = bits > thr
    eq = jnp.where(bits == thr, 1.0, 0.0)
    need = cap - count(gt)
    sel = jnp.where(gt | ((eq > 0) & (_cumsum_excl(eq, upper, lower_strict) < need)), 1.0, 0.0)
    pos = _cumsum_excl(sel, upper, lower_strict).astype(jnp.int32)
    pos_ref[0] = pos
    slot_ref[0] = jnp.where(sel > 0, pos, -1)


def _select(aff_t, cap):
    E, T = aff_t.shape
    nc = T // CHUNK
    assert nc <= CHUNK and nc % 8 == 0
    spec = pl.BlockSpec((1, nc, CHUNK), lambda e: (e, 0, 0))
    slot, pos = pl.pallas_call(
        functools.partial(_select_kernel, cap=cap),
        grid=(E,), in_specs=[spec], out_specs=[spec, spec],
        out_shape=[jax.ShapeDtypeStruct((E, nc, CHUNK), jnp.int32)] * 2,
        compiler_params=_params("parallel"),
        name="select_topk",
    )(aff_t.reshape(E, nc, CHUNK))
    return slot.reshape(E, T), pos.reshape(E, T)


GATHER_STEP = 96
GATHER_W = GATHER_STEP + ROWS16
GATHER_STACK = 8


def _gather_kernel(starts_ref, h_ref, slot_ref, xe_ref, stage_ref, extra_ref, tail_ref, sem, extra_sem, *, cap):
    b = pl.program_id(0)
    nb = pl.num_programs(0)
    E = slot_ref.shape[0]
    buf = b % 2
    h = h_ref[...]
    row = lax.broadcasted_iota(jnp.int32, (GATHER_W, 1), 0)

    @pl.when(b == 0)
    def _():
        tail_ref[...] = jnp.zeros_like(tail_ref)

    def first_row(e, blk):
        return pl.multiple_of((starts_ref[e, blk] // ROWS16) * ROWS16, ROWS16)

    def copy(e, blk):
        return pltpu.make_async_copy(stage_ref.at[blk % 2, e], xe_ref.at[e, pl.ds(first_row(e, blk), GATHER_W)],
                                     sem.at[blk % 2, e])

    def onehot(e, k):
        return (slot_ref[pl.ds(e, 1), :] - (first_row(e, b) + k * GATHER_STEP) == row).astype(BF16)

    for e0 in range(0, E, GATHER_STACK):
        stacked = jnp.concatenate([onehot(e, 0) for e in range(e0, e0 + GATHER_STACK)], axis=0)
        rows = jnp.dot(stacked, h, preferred_element_type=F32).astype(BF16)
        for i in range(GATHER_STACK):
            stage_ref[buf, e0 + i] = rows[i * GATHER_W:(i + 1) * GATHER_W]

    def per_expert(e, carry):
        base = first_row(e, b)
        end = starts_ref[e, b + 1] - base
        group = pl.multiple_of((end // ROWS16) * ROWS16, ROWS16)
        last = jnp.maximum(group - 1, 0) // GATHER_STEP

        def window(k):
            return jnp.dot(onehot(e, k), h, preferred_element_type=F32).astype(BF16)

        stage_ref[buf, e, pl.ds(0, ROWS16), :] += tail_ref[e]

        @pl.when(last == 0)
        def _():
            tail_ref[e] = stage_ref[buf, e, pl.ds(group, ROWS16), :]

        @pl.when(b > 0)
        def _():
            copy(e, b - 1).wait()

        copy(e, b).start()

        def extra(k, inner):
            extra_ref[...] = window(k)

            @pl.when(k == last)
            def _():
                tail_ref[e] = extra_ref[pl.ds(pl.multiple_of(group - k * GATHER_STEP, ROWS16), ROWS16), :]

            cp = pltpu.make_async_copy(extra_ref, xe_ref.at[e, pl.ds(base + k * GATHER_STEP, GATHER_W)],
                                       extra_sem.at[0])
            cp.start()
            cp.wait()
            return inner

        return lax.fori_loop(1, last + 1, extra, carry)

    lax.fori_loop(0, E, per_expert, 0)

    @pl.when(b == nb - 1)
    def _():
        extra_ref[...] = jnp.zeros_like(extra_ref)

        def drain(e, carry):
            copy(e, b).wait()
            cp = pltpu.make_async_copy(extra_ref, xe_ref.at[e, pl.ds(cap, GATHER_W)], extra_sem.at[0])
            cp.start()
            cp.wait()
            return carry
        lax.fori_loop(0, E, drain, 0)


def _gather(h, slot, starts, cap):
    T, D = h.shape
    E = slot.shape[0]
    tb = min(COMBINE_TB, T)
    rows = cap + GATHER_W
    return pl.pallas_call(
        functools.partial(_gather_kernel, cap=cap),
        grid_spec=pltpu.PrefetchScalarGridSpec(
            num_scalar_prefetch=1,
            grid=(T // tb,),
            in_specs=[pl.BlockSpec((tb, D), lambda i, st: (i, 0)),
                      pl.BlockSpec((E, tb), lambda i, st: (0, i))],
            out_specs=pl.BlockSpec(memory_space=pl.ANY),
            scratch_shapes=[pltpu.VMEM((2, E, GATHER_W, D), BF16),
                            pltpu.VMEM((GATHER_W, D), BF16),
                            pltpu.VMEM((E, ROWS16, D), BF16),
                            pltpu.SemaphoreType.DMA((2, E)),
                            pltpu.SemaphoreType.DMA((1,))]),
        out_shape=jax.ShapeDtypeStruct((E, rows, D), BF16),
        compiler_params=_params("arbitrary"),
        name="moe_gather",
    )(starts, h, slot)


def _combine_kernel(starts_ref, x_ref, slot_ref, first_ref, g_ref, ye_ref, o_ref, win_ref, extra_ref, sem,
                    extra_sem, *, cap, final_norm):
    b = pl.program_id(0)
    nb = pl.num_programs(0)
    E = ye_ref.shape[0]
    W = COMBINE_W

    def first_row(e, blk, k):
        return (starts_ref[e, blk] // ROWS16) * ROWS16 + k * W

    def window_start(e, blk, k):
        return pl.multiple_of(jnp.minimum(first_row(e, blk, k), cap - W), ROWS16)

    def copy(e, blk):
        buf = blk % 2
        return pltpu.make_async_copy(ye_ref.at[e, pl.ds(window_start(e, blk, 0), W)],
                                     win_ref.at[buf, pl.ds(e * W, W)], sem.at[buf, e])

    @pl.when(b == 0)
    def _():
        for e in range(E):
            copy(e, b).start()

    @pl.when(b + 1 < nb)
    def _():
        for e in range(E):
            copy(e, b + 1).start()

    slots = slot_ref[...]
    first = first_ref[0]
    rel = slots - jnp.minimum(first, cap - W)
    rel = jnp.where((slots >= first) & (rel >= 0) & (rel < W), rel, -1).astype(F32).astype(BF16)
    lane = lax.broadcasted_iota(jnp.int32, (E, E * W), 1)
    expert = lax.broadcasted_iota(jnp.int32, (E, E * W), 0)
    expand = (lane // W == expert).astype(BF16)
    wanted = (lax.broadcasted_iota(jnp.int32, (1, E * W), 1) % W).astype(F32)
    onehot = (jnp.dot(rel, expand, preferred_element_type=F32) == wanted).astype(BF16)

    for e in range(E):
        copy(e, b).wait()
    o_ref[...] = x_ref[...] + jnp.dot(onehot, win_ref[b % 2], preferred_element_type=F32)

    col = lax.broadcasted_iota(jnp.int32, (1, W), 1)
    expert_lane = lax.broadcasted_iota(jnp.int32, (1, E), 1)

    def per_expert(e, carry):
        n_win = (starts_ref[e, b + 1] - first_row(e, b, 0) + W - 1) // W

        def extra(k, inner):
            cp = pltpu.make_async_copy(ye_ref.at[e, pl.ds(window_start(e, b, k), W)], extra_ref, extra_sem.at[0])
            cp.start()
            cp.wait()
            slot = jnp.max(jnp.where(expert_lane == e, slots, -1), axis=1, keepdims=True)
            oh = ((slot == window_start(e, b, k) + col) & (slot >= first_row(e, b, k))).astype(BF16)
            o_ref[...] += jnp.dot(oh, extra_ref[...], preferred_element_type=F32)
            return inner

        return lax.fori_loop(1, n_win, extra, carry)

    lax.fori_loop(0, E, per_expert, 0)
    if final_norm:
        o_ref[...] = _rms(o_ref[...], g_ref[...])


def _combine(x2, slot_t, starts, ye, final_g=None):
    T, D = x2.shape
    gain = jnp.ones((1, D), F32) if final_g is None else final_g.reshape(1, D).astype(F32)
    E, cap, _ = ye.shape
    tb = min(COMBINE_TB, T)
    nb = T // tb
    assert cap >= COMBINE_W and cap % ROWS16 == 0 and COMBINE_W % ROWS16 == 0
    first = ((starts[:, :nb] // ROWS16) * ROWS16).T.reshape(nb, 1, E)
    return pl.pallas_call(
        functools.partial(_combine_kernel, cap=cap, final_norm=final_g is not None),
        grid_spec=pltpu.PrefetchScalarGridSpec(
            num_scalar_prefetch=1,
            grid=(nb,),
            in_specs=[pl.BlockSpec((tb, D), lambda i, st: (i, 0)),
                      pl.BlockSpec((tb, E), lambda i, st: (i, 0)),
                      pl.BlockSpec((1, 1, E), lambda i, st: (i, 0, 0)),
                      pl.BlockSpec((1, D), lambda i, st: (0, 0)),
                      pl.BlockSpec(memory_space=pl.ANY)],
            out_specs=pl.BlockSpec((tb, D), lambda i, st: (i, 0)),
            scratch_shapes=[pltpu.VMEM((2, E * COMBINE_W, D), BF16),
                            pltpu.VMEM((COMBINE_W, D), BF16),
                            pltpu.SemaphoreType.DMA((2, E)),
                            pltpu.SemaphoreType.DMA((1,))]),
        out_shape=jax.ShapeDtypeStruct((T, D), F32),
        compiler_params=_params("arbitrary"),
        name="moe_combine",
    )(starts, x2, slot_t, first, gain, ye)


def _t5_bucket(rel):
    half = T5_BUCKETS // 2
    max_exact = half // 2
    n = np.abs(rel)
    large = max_exact + (np.log(np.maximum(n, 1) / max_exact) / np.log(T5_MAX_DISTANCE / max_exact)
                         * (half - max_exact)).astype(np.int32)
    large = np.minimum(large, half - 1)
    return (np.where(rel > 0, half, 0) + np.where(n < max_exact, n, large)).astype(np.int32)


def _dilated_tiles(L, radius):
    tq = min(LANES, L)
    kw = min(tq + 2 * radius, L)
    return tq, kw


def _dilated_bias(t5_table, g, L, window, dilation):
    radius = window // (2 * dilation)
    tq, kw = _dilated_tiles(L, radius)
    deltas = (0, -((kw - tq) // 2), tq - kw)
    period = 2 * kw - 1
    rel = np.arange(period + 1) - (kw - 1)
    tab = t5_table[:, g * A_HEADS_PER_GROUP:(g + 1) * A_HEADS_PER_GROUP].astype(F32)
    by_rel = jnp.where((np.abs(rel) <= radius)[None], tab[_t5_bucket(rel * dilation)].T, NEG_INF)
    skew = jnp.tile(by_rel, (1, tq))[:, :tq * period].reshape(A_HEADS_PER_GROUP, tq, period)
    bias = jnp.stack([skew[:, :, d + kw - 1:d + 2 * kw - 1] for d in deltas])
    return bias.reshape(3, A_HEADS_PER_GROUP // 2, 2 * tq, kw)


def _dilated_kernel(q_ref, k_ref, v_ref, bias_ref, o_ref, lse_ref, *, L, tq, kw):
    nq = L // tq
    lane = lax.broadcasted_iota(jnp.int32, (1, LANES), 1)
    lo = lane < HEAD_DIM

    def body(i, carry):
        q0 = pl.multiple_of(i * tq, tq)
        start = jnp.clip(q0 - (kw - tq) // 2, 0, L - kw)
        start = pl.multiple_of(start, ROWS16)
        var = jnp.where(i == 0, 0, jnp.where(i == nq - 1, 2, 1))
        n_pairs = A_HEADS_PER_GROUP // 2
        cols = [slice(pair * LANES, (pair + 1) * LANES) for pair in range(n_pairs)]
        scores = []
        for pair in range(n_pairs):
            q2 = q_ref[0, 0, pl.ds(q0, tq), cols[pair]] * (HEAD_DIM ** -0.5)
            zero = jnp.zeros_like(q2)
            qs = jnp.concatenate([jnp.where(lo, q2, zero), jnp.where(lo, zero, q2)], axis=0)
            k2 = k_ref[0, 0, pl.ds(start, kw), cols[pair]]
            scores.append(lax.dot_general(qs, k2, _NT, preferred_element_type=F32) + bias_ref[var, pair])
        probs, dens, maxs = [], [], []
        for s in scores:
            m = jnp.max(s, axis=-1, keepdims=True)
            p = jnp.exp(s - m)
            maxs.append(m)
            dens.append(jnp.sum(p, axis=-1, keepdims=True))
            probs.append(p.astype(BF16))
        for pair in range(n_pairs):
            v2 = v_ref[0, 0, pl.ds(start, kw), cols[pair]]
            o = jnp.dot(probs[pair], v2, preferred_element_type=F32) * (1.0 / dens[pair])
            lse = maxs[pair] + jnp.log(dens[pair])
            o_ref[0, 0, pl.ds(q0, tq), cols[pair]] = jnp.where(lo, o[:tq], o[tq:])
            lse_ref[0, 0, pl.ds(q0, tq), cols[pair]] = jnp.where(lo, lse[:tq], lse[tq:])
        return carry

    lax.fori_loop(0, nq, body, 0, unroll=next(u for u in (4, 2, 1) if nq % u == 0))


def _dilated_attention(qkv, bias, window):
    Bn, dilation, L, _ = qkv.shape
    radius = window // (2 * dilation)
    tq, kw = _dilated_tiles(L, radius)
    assert L % tq == 0 and kw % ROWS16 == 0 and tq % ROWS16 == 0

    def spec(which):
        return pl.BlockSpec((1, 1, L, A_OUT), lambda b, r: (b, r, 0, which))

    o, lse = pl.pallas_call(
        functools.partial(_dilated_kernel, L=L, tq=tq, kw=kw),
        grid=(Bn, dilation),
        in_specs=[spec(0), spec(1), spec(2), pl.BlockSpec(bias.shape, lambda b, r: (0, 0, 0, 0))],
        out_specs=[spec(0), spec(0)],
        out_shape=[jax.ShapeDtypeStruct((Bn, dilation, L, A_OUT), F32)] * 2,
        compiler_params=_params("parallel", "parallel"),
        name="dilated_attention",
    )(qkv, qkv, qkv, bias)
    return o, lse


def _merge_kernel(*refs):
    n = len(A_PATTERNS)
    o_refs, l_refs, out_ref = refs[:n], refs[n:2 * n], refs[2 * n]
    scratch = refs[2 * n + 1:]
    tm = out_ref.shape[0]

    def token_order(ref, scr, dilation):
        if dilation == 1:
            return ref[0, 0]
        tiles = A_OUT // LANES
        for r in range(dilation):
            for c in range(tiles):
                scr[c, pl.ds(r, tm // dilation, stride=dilation), :] = ref[0, r, :, c * LANES:(c + 1) * LANES]
        return jnp.concatenate([scr[c] for c in range(tiles)], axis=1)

    os_ = [token_order(o_refs[g], scratch[2 * g], d) for g, (_, d) in enumerate(A_PATTERNS)]
    ls = [token_order(l_refs[g], scratch[2 * g + 1], d) for g, (_, d) in enumerate(A_PATTERNS)]
    m = jnp.maximum(jnp.maximum(ls[0], ls[1]), ls[2])
    es = [jnp.exp(l - m) for l in ls]
    den = es[0] + es[1] + es[2]
    out = (es[0] / den) * os_[0] + (es[1] / den) * os_[1] + (es[2] / den) * os_[2]
    out_ref[...] = out.astype(out_ref.dtype)


def _merge_groups(os_, ls_, S, tm=512):
    Bn = os_[0].shape[0]
    ns = S // tm

    def spec(d):
        return pl.BlockSpec((1, d, tm // d, A_OUT), lambda i: (i // ns, 0, i % ns, 0))

    specs = [spec(d) for _, d in A_PATTERNS]
    return pl.pallas_call(
        _merge_kernel, grid=(Bn * ns,), in_specs=specs * 2,
        out_specs=pl.BlockSpec((tm, A_OUT), lambda i: (i, 0)),
        out_shape=jax.ShapeDtypeStruct((Bn * S, A_OUT), BF16),
        scratch_shapes=[pltpu.VMEM((A_OUT // LANES, tm, LANES), F32)] * (2 * len(A_PATTERNS)),
        compiler_params=_params("parallel"), name="merge_groups",
    )(*os_, *ls_)


EVEN_TOKEN_COLS = 3 * A_OUT + B_KV_RANK + LANES + B_Q_RANK


def _in_proj_even_kernel(x_ref, w_ref, g_ref, u0_ref, *rest):
    n_dil = len(A_PATTERNS) - 1
    dil_refs, scr = rest[:n_dil], rest[n_dil]
    tm = x_ref.shape[0]
    acc = jnp.dot(_rms(x_ref[...], g_ref[...]).astype(BF16), w_ref[...], preferred_element_type=F32)
    u0_ref[...] = acc[:, :EVEN_TOKEN_COLS].astype(u0_ref.dtype)
    for i, (_, d) in enumerate(A_PATTERNS[1:]):
        c0 = EVEN_TOKEN_COLS + i * 3 * A_OUT
        for c in range(3 * A_OUT // LANES):
            scr[c] = acc[:, c0 + c * LANES:c0 + (c + 1) * LANES]
        for r in range(d):
            for c in range(3 * A_OUT // LANES):
                piece = scr[c, pl.ds(r, tm // d, stride=d), :]
                dil_refs[i][0, r, :, c * LANES:(c + 1) * LANES] = piece.astype(dil_refs[i].dtype)


def _in_proj_even(x2, w, gain, Bn, S, tm=512):
    T, K = x2.shape
    ns = S // tm
    dils = [d for _, d in A_PATTERNS[1:]]
    assert S % tm == 0 and all(tm % (ROWS16 * d) == 0 for d in dils)
    return pl.pallas_call(
        _in_proj_even_kernel,
        grid=(T // tm,),
        in_specs=[pl.BlockSpec((tm, K), lambda i: (i, 0)),
                  pl.BlockSpec(w.shape, lambda i: (0, 0)),
                  pl.BlockSpec((1, K), lambda i: (0, 0))],
        out_specs=[pl.BlockSpec((tm, EVEN_TOKEN_COLS), lambda i: (i, 0))]
        + [pl.BlockSpec((1, d, tm // d, 3 * A_OUT), lambda i: (i // ns, 0, i % ns, 0)) for d in dils],
        out_shape=[jax.ShapeDtypeStruct((T, EVEN_TOKEN_COLS), BF16)]
        + [jax.ShapeDtypeStruct((Bn, d, S // d, 3 * A_OUT), BF16) for d in dils],
        scratch_shapes=[pltpu.VMEM((3 * A_OUT // LANES, tm, LANES), F32)],
        compiler_params=_params("parallel"),
        name="in_proj_even",
    )(x2, w, gain.reshape(1, K).astype(F32))


def _rope_tables(S):
    inv = 1.0 / (ROPE_THETA ** (np.arange(0, B_ROPE, 2, dtype=np.float32) / B_ROPE))
    ang = np.arange(S, dtype=np.float32)[:, None] * inv[None]
    cos, sin = np.cos(ang), np.sin(ang)
    half = B_ROPE // 2
    c = np.ones((S, LANES), np.float32)
    c[:, B_NOPE:B_NOPE + half] = cos
    c[:, B_NOPE + half:B_NOPE + B_ROPE] = cos
    s_lo = np.zeros((S, LANES), np.float32)
    s_lo[:, B_NOPE:B_NOPE + half] = -sin
    s_hi = np.zeros((S, LANES), np.float32)
    s_hi[:, B_NOPE + half:B_NOPE + B_ROPE] = sin
    return jnp.asarray(c), jnp.asarray(s_lo), jnp.asarray(s_hi)


def _rope(x, c, s_lo, s_hi):
    half = B_ROPE // 2
    return x * c + pltpu.roll(x, LANES - half, 1) * s_lo + pltpu.roll(x, half, 1) * s_hi


def _mla_prep_kernel(cq_ref, ckv_ref, kr_ref, gq_ref, gkv_ref, wq_ref, wk_ref, wv_ref,
                     c_ref, slo_ref, shi_ref, q_ref, k_ref, v_ref):
    c, s_lo, s_hi = c_ref[...], slo_ref[...], shi_ref[...]
    scale = (B_NOPE + B_ROPE) ** -0.5 * LOG2_E
    xq = _rms(cq_ref[...], gq_ref[...]).astype(BF16)
    q = jnp.dot(xq, wq_ref[...], preferred_element_type=F32)
    xkv = _rms(ckv_ref[...], gkv_ref[...]).astype(BF16)
    k = jnp.dot(xkv, wk_ref[...], preferred_element_type=F32)
    vt = lax.dot_general(wv_ref[...], xkv, _NT, preferred_element_type=F32)
    ones_row = lax.broadcasted_iota(jnp.int32, vt.shape, 0) % LANES == B_V
    v_ref[0, 0] = jnp.where(ones_row, 1.0, vt).astype(v_ref.dtype)
    k_rope = _rope(kr_ref[...].astype(F32), c, s_lo, s_hi)
    for h in range(B_HEADS):
        cs = slice(h * LANES, (h + 1) * LANES)
        q_ref[:, cs] = (_rope(q[:, cs], c, s_lo, s_hi) * scale).astype(q_ref.dtype)
        k_ref[:, cs] = (k[:, cs] + k_rope).astype(k_ref.dtype)


MLA_KEY_CHUNK = 1024


def _mla_prep(u3, g_qa, g_kva, wq, wk, wv):
    Bn, S, C = u3.shape
    ts = min(MLA_KEY_CHUNK, S)
    tabs = _rope_tables(S)
    ns = S // ts
    u2 = u3.reshape(Bn * S, C)
    row = lambda b, i: b * ns + i
    a_cols = 3 * A_OUT
    off_kv, off_kr, off_q = a_cols // B_KV_RANK, (a_cols + B_KV_RANK) // LANES, (a_cols + B_KV_RANK + LANES) // B_Q_RANK
    full = lambda a: pl.BlockSpec(a.shape, lambda b, i: (0, 0))
    tab_spec = pl.BlockSpec((ts, LANES), lambda b, i: (i, 0))
    hw = B_HEADS * LANES
    return pl.pallas_call(
        _mla_prep_kernel,
        grid=(Bn, ns),
        in_specs=[pl.BlockSpec((ts, B_Q_RANK), lambda b, i: (row(b, i), off_q)),
                  pl.BlockSpec((ts, B_KV_RANK), lambda b, i: (row(b, i), off_kv)),
                  pl.BlockSpec((ts, LANES), lambda b, i: (row(b, i), off_kr)),
                  full(g_qa), full(g_kva), full(wq), full(wk), full(wv),
                  tab_spec, tab_spec, tab_spec],
        out_specs=[pl.BlockSpec((ts, hw), lambda b, i: (row(b, i), 0)),
                   pl.BlockSpec((ts, hw), lambda b, i: (row(b, i), 0)),
                   pl.BlockSpec((1, 1, hw, ts), lambda b, i: (b, i, 0, 0))],
        out_shape=[jax.ShapeDtypeStruct((Bn * S, hw), BF16),
                   jax.ShapeDtypeStruct((Bn * S, hw), BF16),
                   jax.ShapeDtypeStruct((Bn, ns, hw, ts), BF16)],
        compiler_params=_params("parallel", "parallel"),
        name="mla_prep",
    )(u2, u2, u2, g_qa, g_kva, wq, wk, wv, *tabs)


def _mla_kernel(q_ref, k_ref, v_ref, o_ref):
    nk, _, tk = v_ref.shape[1:]
    tq = q_ref.shape[1]
    lane = lax.broadcasted_iota(jnp.int32, (1, LANES), 1)
    qs = [q_ref[0, :, sub * LANES:(sub + 1) * LANES] for sub in range(2)]

    def body(j, carry):
        k0 = pl.multiple_of(j * tk, tk)
        scores = [lax.dot_general(k_ref[0, pl.ds(k0, tk), sub * LANES:(sub + 1) * LANES], qs[sub], _NT,
                                  preferred_element_type=F32) for sub in range(2)]
        new = []
        for sub in range(2):
            m, acc = carry[sub]
            s = scores[sub]
            m_new = jnp.maximum(m, jnp.max(s, axis=0, keepdims=True))
            p = jnp.exp2(s - m_new).astype(BF16)
            vt = v_ref[0, j, sub * LANES:(sub + 1) * LANES, :]
            acc = jnp.exp2(m - m_new) * acc + jnp.dot(vt, p, preferred_element_type=F32)
            new.append((m_new, acc))
        return tuple(new)

    init = tuple((jnp.full((1, tq), NEG_INF, F32), jnp.zeros((LANES, tq), F32)) for _ in range(2))
    res = lax.fori_loop(0, nk, body, init, unroll=True)
    outs = [(acc * (1.0 / acc[B_V:B_V + 1])).T for (_, acc) in res]
    o_ref[0] = jnp.where(lane < B_V, outs[0], pltpu.roll(outs[1], B_V, 1)).astype(o_ref.dtype)


def _mla_attention(q, k, vt, Bn, S, tq=2048):
    tq = min(tq, S)
    nk, _, tk = vt.shape[1:]
    hw = B_HEADS * LANES
    q3, k3 = q.reshape(Bn, S, hw), k.reshape(Bn, S, hw)
    o = pl.pallas_call(
        _mla_kernel,
        grid=(Bn, B_HEADS // 2, S // tq),
        in_specs=[pl.BlockSpec((1, tq, 2 * LANES), lambda b, h, i: (b, i, h)),
                  pl.BlockSpec((1, S, 2 * LANES), lambda b, h, i: (b, 0, h)),
                  pl.BlockSpec((1, nk, 2 * LANES, tk), lambda b, h, i: (b, 0, h, 0))],
        out_specs=pl.BlockSpec((1, tq, LANES), lambda b, h, i: (b, i, h)),
        out_shape=jax.ShapeDtypeStruct((Bn, S, B_OUT), BF16),
        compiler_params=_params("parallel", "parallel", "parallel"),
        name="mla_attention",
    )(q3, k3, vt)
    return o.reshape(Bn * S, B_OUT)


def _natten_bias(rpb):
    W = GRID_W
    c = np.arange(W)[:, None]
    kc = np.arange(W)[None, :]
    wstart = np.clip(c - C_WIN_COLS // 2, 0, W - C_WIN_COLS)
    col_valid = (kc >= wstart) & (kc < wstart + C_WIN_COLS)
    n_dcol = 2 * C_WIN_COLS - 1
    left = (W - 1) - (C_WIN_COLS - 1)
    by_rel = jnp.pad(rpb.astype(F32), ((0, 0), (0, 0), (left, 2 * W - left - n_dcol)))
    skew = jnp.tile(by_rel, (1, 1, W))[:, :, :W * (2 * W - 1)].reshape(C_HEADS, -1, W, 2 * W - 1)
    cols = jnp.where(col_valid[None, None], skew[:, :, :, W - 1:2 * W - 1], NEG_INF)
    b = jnp.stack([cols[:, C_WIN_ROWS - 1 - off:2 * C_WIN_ROWS - 1 - off] for off in range(C_WIN_ROWS)])
    b = jnp.transpose(b, (0, 1, 3, 2, 4))
    return b.reshape(C_WIN_ROWS, C_HEADS // 2, 2 * W, C_WIN_ROWS * W)


NATTEN_ROWS = 4


def _natten_kernel(q_ref, k_ref, v_ref, *rest, rows):
    bias_refs, o_ref = rest[:NATTEN_ROWS], rest[NATTEN_ROWS]
    nk = C_WIN_ROWS * GRID_W
    lane = lax.broadcasted_iota(jnp.int32, (1, LANES), 1)
    lo = lane < HEAD_DIM
    n_pairs = C_HEADS // 2
    cols = [slice(pair * LANES, (pair + 1) * LANES) for pair in range(n_pairs)]
    for j in range(NATTEN_ROWS):
        r = pl.program_id(1) * NATTEN_ROWS + j
        r0 = jnp.clip(r - C_WIN_ROWS // 2, 0, rows - C_WIN_ROWS)
        k0 = pl.multiple_of(r0 * GRID_W, GRID_W)
        q_rows = slice(j * GRID_W, (j + 1) * GRID_W)
        scores = []
        for pair in range(n_pairs):
            q2 = q_ref[0, q_rows, cols[pair]] * (HEAD_DIM ** -0.5)
            zero = jnp.zeros_like(q2)
            qs = jnp.concatenate([jnp.where(lo, q2, zero), jnp.where(lo, zero, q2)], axis=0)
            k2 = k_ref[0, pl.ds(k0, nk), cols[pair]]
            scores.append(lax.dot_general(qs, k2, _NT, preferred_element_type=F32) + bias_refs[j][0, pair])
        probs, dens = [], []
        for s in scores:
            p = jnp.exp(s - jnp.max(s, axis=-1, keepdims=True))
            dens.append(jnp.sum(p, axis=-1, keepdims=True))
            probs.append(p.astype(BF16))
        for pair in range(n_pairs):
            v2 = v_ref[0, pl.ds(k0, nk), cols[pair]]
            o = jnp.dot(probs[pair], v2, preferred_element_type=F32) * (1.0 / dens[pair])
            o_ref[0, q_rows, cols[pair]] = jnp.where(lo, o[:GRID_W], o[GRID_W:]).astype(o_ref.dtype)


def _natten(u3, bias):
    Bn, S, _ = u3.shape
    rows = S // GRID_W
    assert rows >= C_WIN_ROWS and rows % NATTEN_ROWS == 0

    def bias_spec(j):
        def variant(b, i):
            r = i * NATTEN_ROWS + j
            return (r - jnp.clip(r - C_WIN_ROWS // 2, 0, rows - C_WIN_ROWS), 0, 0, 0)
        return pl.BlockSpec((1, C_HEADS // 2, 2 * GRID_W, C_WIN_ROWS * GRID_W), variant)

    q_spec = pl.BlockSpec((1, NATTEN_ROWS * GRID_W, C_OUT), lambda b, i: (b, i, 0))
    o = pl.pallas_call(
        functools.partial(_natten_kernel, rows=rows),
        grid=(Bn, rows // NATTEN_ROWS),
        in_specs=[q_spec,
                  pl.BlockSpec((1, S, C_OUT), lambda b, i: (b, 0, 1), pipeline_mode=pl.Buffered(1)),
                  pl.BlockSpec((1, S, C_OUT), lambda b, i: (b, 0, 2), pipeline_mode=pl.Buffered(1))]
        + [bias_spec(j) for j in range(NATTEN_ROWS)],
        out_specs=q_spec,
        out_shape=jax.ShapeDtypeStruct((Bn, S, C_OUT), BF16),
        compiler_params=_params("parallel", "arbitrary"),
        name="natten",
    )(u3, u3, u3, *([bias] * NATTEN_ROWS))
    return o.reshape(Bn * S, C_OUT)


def _moe(x2, ln_g, router, wg, wu, wd, layer, final_g=None):
    T, D = x2.shape
    cap = CAPACITY_FACTOR * T // N_EXPERTS
    aff_t, h = _router(x2, ln_g, router)
    slot, pos = _select(aff_t, cap)
    tb = min(COMBINE_TB, T)
    starts = jnp.concatenate([pos[:, ::tb], jnp.full((N_EXPERTS, 1), cap, jnp.int32)], axis=1)
    xe = _gather(h, slot, starts, cap)
    ye = _expert_ffn(xe, cap, wg, wu, wd, layer)
    return _combine(x2, slot.T, starts, ye, final_g)


def _trunk(x, p):
    Bn, S, D = x.shape
    T = Bn * S
    x2 = x.reshape(T, D)
    u0, *u_dil = _in_proj_even(x2, p["w_in_even"], p["ln_mix_g"][0], Bn, S)
    u3 = u0.reshape(Bn, S, EVEN_TOKEN_COLS)
    os_, ls_ = [], []
    for g, (window, dilation) in enumerate(A_PATTERNS):
        bias = _dilated_bias(p["t5_table"], g, S // dilation, window, dilation)
        qkv = u3.reshape(Bn, 1, S, EVEN_TOKEN_COLS) if g == 0 else u_dil[g - 1]
        o, l = _dilated_attention(qkv, bias, window)
        os_.append(o)
        ls_.append(l)
    a = _merge_groups(os_, ls_, S)
    q, k, v = _mla_prep(u3, p["g_qa"], p["g_kva"], p["wq"], p["wk"], p["wv"])
    b = _mla_attention(q, k, v, Bn, S)
    x2 = _matmul([a, b], [p["w_out_even"][:A_OUT], p["w_out_even"][A_OUT:]], res=x2, out_dtype=F32, tm=1024)
    x2 = _moe(x2, p["ln_ffn_g"][0], p["router"][0], p["w_gate"], p["w_up"], p["w_down"], 0)
    u = _matmul(x2, p["w_in_odd"], gain=p["ln_mix_g"][1], out_dtype=BF16)
    c = _natten(u.reshape(Bn, S, C_IN), p["natten_bias"])
    x2 = _matmul(c, p["w_out_odd"], res=x2, out_dtype=F32, tm=1024)
    x2 = _moe(x2, p["ln_ffn_g"][1], p["router"][1], p["w_gate"], p["w_up"], p["w_down"], 1,
              final_g=p["final_g"])
    return x2.reshape(Bn, S, D)


def _head_tiles(w, n_heads, per_head, keep):
    K = w.shape[0]
    w = w.reshape(K, n_heads, per_head)[:, :, :keep]
    return jnp.pad(w, ((0, 0), (0, 0), (0, LANES - keep))).reshape(K, n_heads * LANES)


def _prepare(ln_mix_g, w_in_even, g_qa, w_qb, g_kva, w_kvb, t5_table, w_out_even,
             w_in_odd, rpb, w_out_odd, ln_ffn_g, router, w_gate, w_up, w_down, final_g):
    w_in = w_in_even[0]
    w_a, w_cq = w_in[:, :A_IN], w_in[:, A_IN:A_IN + B_Q_RANK]
    w_ckv = w_in[:, A_IN + B_Q_RANK:A_IN + B_Q_RANK + B_KV_RANK]
    w_kr = w_in[:, A_IN + B_Q_RANK + B_KV_RANK:]
    w_kr_tile = jnp.pad(w_kr, ((0, 0), (B_NOPE, LANES - B_NOPE - B_ROPE)))
    w_a = w_a.reshape(-1, 3, A_GROUPS, A_OUT)
    groups = [w_a[:, :, g].reshape(-1, 3 * A_OUT) for g in range(A_GROUPS)]
    w_even = jnp.concatenate([groups[0], w_ckv, w_kr_tile, w_cq] + groups[1:], axis=1)
    w_kv = w_kvb[0].reshape(B_KV_RANK, B_HEADS, B_NOPE + B_V)
    return dict(
        ln_mix_g=ln_mix_g, ln_ffn_g=ln_ffn_g, final_g=final_g, t5_table=t5_table,
        w_in_even=w_even.astype(BF16),
        g_qa=g_qa[0].reshape(1, -1).astype(F32), g_kva=g_kva[0].reshape(1, -1).astype(F32),
        wq=_head_tiles(w_qb[0], B_HEADS, B_NOPE + B_ROPE, B_NOPE + B_ROPE).astype(BF16),
        wk=_head_tiles(w_kvb[0], B_HEADS, B_NOPE + B_V, B_NOPE).astype(BF16),
        wv=_head_tiles(w_kv[:, :, B_NOPE:].reshape(B_KV_RANK, B_OUT), B_HEADS, B_V, B_V).T.astype(BF16),
        w_out_even=w_out_even[0].astype(BF16), w_in_odd=w_in_odd[0].astype(BF16),
        natten_bias=_natten_bias(rpb[0]), w_out_odd=w_out_odd[0].astype(BF16), router=router,
        w_gate=w_gate, w_up=w_up, w_down=w_down,
    )


def kernel(x_prompt, x_sample, ln_mix_g, w_in_even, g_qa, w_qb, g_kva, w_kvb, t5_table, w_out_even,
           w_in_odd, rpb, w_out_odd, ln_ffn_g, router, w_gate, w_up, w_down, final_g):
    p = _prepare(ln_mix_g, w_in_even, g_qa, w_qb, g_kva, w_kvb, t5_table, w_out_even,
                 w_in_odd, rpb, w_out_odd, ln_ffn_g, router, w_gate, w_up, w_down, final_g)
    return _trunk(x_prompt, p), _trunk(x_sample, p)
```

```python
import functools

import jax
import jax.numpy as jnp
import numpy as np
from jax import lax
from jax.experimental import pallas as pl
from jax.experimental.pallas import tpu as pltpu

D_MODEL = 1024
RMS_EPS = 1e-6
HEAD_DIM = 64
NEG_INF = -1e30
A_PATTERNS = ((128, 1), (512, 4), (2048, 16))
A_GROUPS = 3
A_HEADS_PER_GROUP = 4
A_HEADS = A_GROUPS * A_HEADS_PER_GROUP
A_IN = 3 * A_HEADS * HEAD_DIM
A_OUT = A_HEADS_PER_GROUP * HEAD_DIM
T5_BUCKETS = 32
T5_MAX_DISTANCE = 1024
B_HEADS = 12
B_Q_RANK = 384
B_KV_RANK = 256
B_NOPE = 64
B_ROPE = 32
B_V = 64
B_OUT = B_HEADS * B_V
ROPE_THETA = 10000.0
GRID_W = 64
C_HEADS = 16
C_WIN_ROWS = 8
C_WIN_COLS = 16
C_IN = 3 * C_HEADS * HEAD_DIM
C_OUT = C_HEADS * HEAD_DIM
N_EXPERTS = 16
CAPACITY_FACTOR = 2

LANES = 128
ROWS16 = 16
VMEM_LIMIT_BYTES = 48 * 1024 * 1024

F32 = jnp.float32
BF16 = jnp.bfloat16
_NT = (((1,), (1,)), ((), ()))
LOG2_E = 1.4426950408889634


def _params(*sem):
    return pltpu.CompilerParams(dimension_semantics=sem, vmem_limit_bytes=VMEM_LIMIT_BYTES)


def _rms(x, g):
    xf = x.astype(F32)
    return xf * lax.rsqrt(jnp.mean(xf * xf, axis=-1, keepdims=True) + RMS_EPS) * g


def _mm_kernel(*refs, n_in, has_gain, has_res):
    xs, ws = refs[:n_in], refs[n_in:2 * n_in]
    pos = 2 * n_in
    g_ref = res_ref = None
    if has_gain:
        g_ref = refs[pos]
        pos += 1
    if has_res:
        res_ref = refs[pos]
        pos += 1
    o_ref = refs[pos]
    acc = None
    for x_ref, w_ref in zip(xs, ws):
        x = x_ref[...]
        if has_gain:
            x = _rms(x, g_ref[...])
        part = jnp.dot(x.astype(BF16), w_ref[...], preferred_element_type=F32)
        acc = part if acc is None else acc + part
    if has_res:
        acc = acc + res_ref[...]
    o_ref[...] = acc.astype(o_ref.dtype)


def _matmul(xs, ws, gain=None, res=None, out_dtype=F32, tm=512):
    if not isinstance(xs, (list, tuple)):
        xs, ws = [xs], [ws]
    M = xs[0].shape[0]
    N = ws[0].shape[1]
    assert M % tm == 0
    in_specs = [pl.BlockSpec((tm, x.shape[1]), lambda i: (i, 0)) for x in xs]
    in_specs += [pl.BlockSpec(w.shape, lambda i: (0, 0)) for w in ws]
    args = list(xs) + list(ws)
    if gain is not None:
        assert len(xs) == 1
        in_specs.append(pl.BlockSpec((1, gain.shape[-1]), lambda i: (0, 0)))
        args.append(gain.reshape(1, -1).astype(F32))
    if res is not None:
        in_specs.append(pl.BlockSpec((tm, N), lambda i: (i, 0)))
        args.append(res)
    return pl.pallas_call(
        functools.partial(_mm_kernel, n_in=len(xs), has_gain=gain is not None,
                          has_res=res is not None),
        grid=(M // tm,),
        in_specs=in_specs,
        out_specs=pl.BlockSpec((tm, N), lambda i: (i, 0)),
        out_shape=jax.ShapeDtypeStruct((M, N), out_dtype),
        compiler_params=_params("parallel"),
        name="fused_matmul",
    )(*args)


def _ffn_kernel(x_ref, wg32_ref, wu32_ref, wd32_ref, o_ref, wg_ref, wu_ref, wd_ref):
    D = wg_ref.shape[1]

    @pl.when(pl.program_id(1) == 0)
    def _():
        wg_ref[0] = wg32_ref[0, 0].astype(BF16)
        wu_ref[0] = wu32_ref[0, 0].astype(BF16)
        wd_ref[0] = wd32_ref[0, 0].astype(BF16)

    x = x_ref[0, :, :D]
    lane = lax.broadcasted_iota(jnp.int32, (1, LANES), 1)
    mine = lane // GATE_TERMS == pl.program_id(0)
    gate = jnp.sum(jnp.where(mine, x_ref[0, :, D:].astype(F32), 0.0), axis=1, keepdims=True)
    g = jnp.dot(x, wg_ref[0], preferred_element_type=F32)
    u = jnp.dot(x, wu_ref[0], preferred_element_type=F32)
    h = (g * jax.nn.sigmoid(g)) * u
    y = jnp.dot(h.astype(BF16), wd_ref[0], preferred_element_type=F32)
    o_ref[0] = (y * gate).astype(o_ref.dtype)


def _expert_ffn(xe, cap, wg, wu, wd, layer, tm=1024):
    E = xe.shape[0]
    D, FF = wg.shape[2], wg.shape[3]
    tm = min(tm, cap)
    assert cap % tm == 0
    return pl.pallas_call(
        _ffn_kernel,
        grid=(E, cap // tm),
        in_specs=[pl.BlockSpec((1, tm, D + LANES), lambda e, i: (e, i, 0)),
                  pl.BlockSpec((1, 1, D, FF), lambda e, i: (layer, e, 0, 0)),
                  pl.BlockSpec((1, 1, D, FF), lambda e, i: (layer, e, 0, 0)),
                  pl.BlockSpec((1, 1, FF, D), lambda e, i: (layer, e, 0, 0))],
        out_specs=pl.BlockSpec((1, tm, D), lambda e, i: (e, i, 0)),
        out_shape=jax.ShapeDtypeStruct((E, cap, D), BF16),
        scratch_shapes=[pltpu.VMEM((1, D, FF), BF16), pltpu.VMEM((1, D, FF), BF16), pltpu.VMEM((1, FF, D), BF16)],
        compiler_params=_params("parallel", "arbitrary"),
        name="expert_ffn",
    )(xe, wg, wu, wd)


CHUNK = 256
COMBINE_TB = 512
COMBINE_W = 112


GATE_TERMS = 3


def _router_kernel(x_ref, g_ref, rt_ref, aff_ref, h_ref):
    D = x_ref.shape[1]
    E = rt_ref.shape[0]
    xn = _rms(x_ref[...], g_ref[...])
    logits = lax.dot_general(rt_ref[...], xn, _NT, preferred_element_type=F32,
                             precision=lax.Precision.HIGHEST)
    e = jnp.exp(logits - jnp.max(logits, axis=0, keepdims=True))
    aff = e / jnp.sum(e, axis=0, keepdims=True)
    aff_ref[...] = aff
    lane = lax.broadcasted_iota(jnp.int32, (E, LANES), 1)
    expert = lax.broadcasted_iota(jnp.int32, (E, LANES), 0)
    rest = aff
    gates = jnp.zeros((x_ref.shape[0], LANES), F32)
    for j in range(GATE_TERMS):
        term = rest.astype(BF16)
        rest = rest - term.astype(F32)
        place = (lane == GATE_TERMS * expert + j).astype(BF16)
        gates = gates + lax.dot_general(term, place, (((0,), (0,)), ((), ())), preferred_element_type=F32)
    h_ref[:, :D] = xn.astype(h_ref.dtype)
    h_ref[:, D:] = gates.astype(h_ref.dtype)


def _router(x2, ln_g, router, tm=1024):
    T, D = x2.shape
    E = router.shape[1]
    assert GATE_TERMS * E <= LANES
    return pl.pallas_call(
        _router_kernel,
        grid=(T // tm,),
        in_specs=[pl.BlockSpec((tm, D), lambda i: (i, 0)),
                  pl.BlockSpec((1, D), lambda i: (0, 0)),
                  pl.BlockSpec((E, D), lambda i: (0, 0))],
        out_specs=[pl.BlockSpec((E, tm), lambda i: (0, i)),
                   pl.BlockSpec((tm, D + LANES), lambda i: (i, 0))],
        out_shape=[jax.ShapeDtypeStruct((E, T), F32), jax.ShapeDtypeStruct((T, D + LANES), BF16)],
        compiler_params=_params("parallel"),
        name="router",
    )(x2, ln_g.reshape(1, D).astype(F32), router.T.astype(F32))


def _cumsum_excl(mask, upper, lower_strict):
    incl = jnp.dot(mask.astype(BF16), upper, preferred_element_type=F32)
    before = jnp.dot(lower_strict, incl.astype(BF16), preferred_element_type=F32)
    return incl + before[:, CHUNK - 1:CHUNK] - mask


def _select_kernel(aff_ref, slot_ref, pos_ref, *, cap):
    aff = aff_ref[0]
    nc = aff.shape[0]
    bits = pltpu.bitcast(aff, jnp.int32)

    def count(cond):
        c = jnp.sum(jnp.where(cond, 1.0, 0.0), axis=0, keepdims=True)
        return jnp.sum(c, axis=1, keepdims=True)

    def has_cap_above(cand):
        return jnp.where(count(bits >= cand) >= cap, 1, 0).astype(jnp.int32)

    def digit_step(i, prefix):
        shift = 28 - 2 * i
        digit = sum(has_cap_above(prefix | (jnp.int32(d) << shift)) for d in (1, 2, 3))
        return prefix | (digit << shift)

    top = has_cap_above(jnp.full((1, 1), 1 << 30, jnp.int32)) << 30
    thr = lax.fori_loop(0, 15, digit_step, top)
    r = lax.broadcasted_iota(jnp.int32, (CHUNK, CHUNK), 0)
    c = lax.broadcasted_iota(jnp.int32, (CHUNK, CHUNK), 1)
    upper = (r <= c).astype(BF16)
    r = lax.broadcasted_iota(jnp.int32, (nc, nc), 0)
    c = lax.broadcasted_iota(jnp.int32, (nc, nc), 1)
    lower_strict = (c < r).astype(BF16)
    gt = bits > thr
    eq = jnp.where(bits == thr, 1.0, 0.0)
    need = cap - count(gt)
    sel = jnp.where(gt | ((eq > 0) & (_cumsum_excl(eq, upper, lower_strict) < need)), 1.0, 0.0)
    pos = _cumsum_excl(sel, upper, lower_strict).astype(jnp.int32)
    pos_ref[0] = pos
    slot_ref[0] = jnp.where(sel > 0, pos, -1)


def _select(aff_t, cap):
    E, T = aff_t.shape
    nc = T // CHUNK
    assert nc <= CHUNK and nc % 8 == 0
    spec = pl.BlockSpec((1, nc, CHUNK), lambda e: (e, 0, 0))
    slot, pos = pl.pallas_call(
        functools.partial(_select_kernel, cap=cap),
        grid=(E,), in_specs=[spec], out_specs=[spec, spec],
        out_shape=[jax.ShapeDtypeStruct((E, nc, CHUNK), jnp.int32)] * 2,
        compiler_params=_params("parallel"),
        name="select_topk",
    )(aff_t.reshape(E, nc, CHUNK))
    return slot.reshape(E, T), pos.reshape(E, T)


GATHER_STEP = 96
GATHER_W = GATHER_STEP + ROWS16
GATHER_STACK = 8


def _gather_kernel(starts_ref, h_ref, slot_ref, xe_ref, stage_ref, extra_ref, tail_ref, sem, extra_sem, *, cap):
    b = pl.program_id(0)
    nb = pl.num_programs(0)
    E = slot_ref.shape[0]
    buf = b % 2
    h = h_ref[...]
    row = lax.broadcasted_iota(jnp.int32, (GATHER_W, 1), 0)

    @pl.when(b == 0)
    def _():
        tail_ref[...] = jnp.zeros_like(tail_ref)

    def first_row(e, blk):
        return pl.multiple_of((starts_ref[e, blk] // ROWS16) * ROWS16, ROWS16)

    def copy(e, blk):
        return pltpu.make_async_copy(stage_ref.at[blk % 2, e], xe_ref.at[e, pl.ds(first_row(e, blk), GATHER_W)],
                                     sem.at[blk % 2, e])

    def onehot(e, k):
        return (slot_ref[pl.ds(e, 1), :] - (first_row(e, b) + k * GATHER_STEP) == row).astype(BF16)

    for e0 in range(0, E, GATHER_STACK):
        stacked = jnp.concatenate([onehot(e, 0) for e in range(e0, e0 + GATHER_STACK)], axis=0)
        rows = jnp.dot(stacked, h, preferred_element_type=F32).astype(BF16)
        for i in range(GATHER_STACK):
            stage_ref[buf, e0 + i] = rows[i * GATHER_W:(i + 1) * GATHER_W]

    def per_expert(e, carry):
        base = first_row(e, b)
        end = starts_ref[e, b + 1] - base
        group = pl.multiple_of((end // ROWS16) * ROWS16, ROWS16)
        last = jnp.maximum(group - 1, 0) // GATHER_STEP

        def window(k):
            return jnp.dot(onehot(e, k), h, preferred_element_type=F32).astype(BF16)

        stage_ref[buf, e, pl.ds(0, ROWS16), :] += tail_ref[e]

        @pl.when(last == 0)
        def _():
            tail_ref[e] = stage_ref[buf, e, pl.ds(group, ROWS16), :]

        @pl.when(b > 0)
        def _():
            copy(e, b - 1).wait()

        copy(e, b).start()

        def extra(k, inner):
            extra_ref[...] = window(k)

            @pl.when(k == last)
            def _():
                tail_ref[e] = extra_ref[pl.ds(pl.multiple_of(group - k * GATHER_STEP, ROWS16), ROWS16), :]

            cp = pltpu.make_async_copy(extra_ref, xe_ref.at[e, pl.ds(base + k * GATHER_STEP, GATHER_W)],
                                       extra_sem.at[0])
            cp.start()
            cp.wait()
            return inner

        return lax.fori_loop(1, last + 1, extra, carry)

    lax.fori_loop(0, E, per_expert, 0)

    @pl.when(b == nb - 1)
    def _():
        extra_ref[...] = jnp.zeros_like(extra_ref)

        def drain(e, carry):
            copy(e, b).wait()
            cp = pltpu.make_async_copy(extra_ref, xe_ref.at[e, pl.ds(cap, GATHER_W)], extra_sem.at[0])
            cp.start()
            cp.wait()
            return carry
        lax.fori_loop(0, E, drain, 0)


def _gather(h, slot, starts, cap):
    T, D = h.shape
    E = slot.shape[0]
    tb = min(COMBINE_TB, T)
    rows = cap + GATHER_W
    return pl.pallas_call(
        functools.partial(_gather_kernel, cap=cap),
        grid_spec=pltpu.PrefetchScalarGridSpec(
            num_scalar_prefetch=1,
            grid=(T // tb,),
            in_specs=[pl.BlockSpec((tb, D), lambda i, st: (i, 0)),
                      pl.BlockSpec((E, tb), lambda i, st: (0, i))],
            out_specs=pl.BlockSpec(memory_space=pl.ANY),
            scratch_shapes=[pltpu.VMEM((2, E, GATHER_W, D), BF16),
                            pltpu.VMEM((GATHER_W, D), BF16),
                            pltpu.VMEM((E, ROWS16, D), BF16),
                            pltpu.SemaphoreType.DMA((2, E)),
                            pltpu.SemaphoreType.DMA((1,))]),
        out_shape=jax.ShapeDtypeStruct((E, rows, D), BF16),
        compiler_params=_params("arbitrary"),
        name="moe_gather",
    )(starts, h, slot)


def _combine_kernel(starts_ref, x_ref, slot_ref, first_ref, g_ref, ye_ref, o_ref, win_ref, extra_ref, sem,
                    extra_sem, *, cap, final_norm):
    b = pl.program_id(0)
    nb = pl.num_programs(0)
    E = ye_ref.shape[0]
    W = COMBINE_W

    def first_row(e, blk, k):
        return (starts_ref[e, blk] // ROWS16) * ROWS16 + k * W

    def window_start(e, blk, k):
        return pl.multiple_of(jnp.minimum(first_row(e, blk, k), cap - W), ROWS16)

    def copy(e, blk):
        buf = blk % 2
        return pltpu.make_async_copy(ye_ref.at[e, pl.ds(window_start(e, blk, 0), W)],
                                     win_ref.at[buf, pl.ds(e * W, W)], sem.at[buf, e])

    @pl.when(b == 0)
    def _():
        for e in range(E):
            copy(e, b).start()

    @pl.when(b + 1 < nb)
    def _():
        for e in range(E):
            copy(e, b + 1).start()

    slots = slot_ref[...]
    first = first_ref[0]
    rel = slots - jnp.minimum(first, cap - W)
    rel = jnp.where((slots >= first) & (rel >= 0) & (rel < W), rel, -1).astype(F32).astype(BF16)
    lane = lax.broadcasted_iota(jnp.int32, (E, E * W), 1)
    expert = lax.broadcasted_iota(jnp.int32, (E, E * W), 0)
    expand = (lane // W == expert).astype(BF16)
    wanted = (lax.broadcasted_iota(jnp.int32, (1, E * W), 1) % W).astype(F32)
    onehot = (jnp.dot(rel, expand, preferred_element_type=F32) == wanted).astype(BF16)

    for e in range(E):
        copy(e, b).wait()
    o_ref[...] = x_ref[...] + jnp.dot(onehot, win_ref[b % 2], preferred_element_type=F32)

    col = lax.broadcasted_iota(jnp.int32, (1, W), 1)
    expert_lane = lax.broadcasted_iota(jnp.int32, (1, E), 1)

    def per_expert(e, carry):
        n_win = (starts_ref[e, b + 1] - first_row(e, b, 0) + W - 1) // W

        def extra(k, inner):
            cp = pltpu.make_async_copy(ye_ref.at[e, pl.ds(window_start(e, b, k), W)], extra_ref, extra_sem.at[0])
            cp.start()
            cp.wait()
            slot = jnp.max(jnp.where(expert_lane == e, slots, -1), axis=1, keepdims=True)
            oh = ((slot == window_start(e, b, k) + col) & (slot >= first_row(e, b, k))).astype(BF16)
            o_ref[...] += jnp.dot(oh, extra_ref[...], preferred_element_type=F32)
            return inner

        return lax.fori_loop(1, n_win, extra, carry)

    lax.fori_loop(0, E, per_expert, 0)
    if final_norm:
        o_ref[...] = _rms(o_ref[...], g_ref[...])


def _combine(x2, slot_t, starts, ye, final_g=None):
    T, D = x2.shape
    gain = jnp.ones((1, D), F32) if final_g is None else final_g.reshape(1, D).astype(F32)
    E, cap, _ = ye.shape
    tb = min(COMBINE_TB, T)
    nb = T // tb
    assert cap >= COMBINE_W and cap % ROWS16 == 0 and COMBINE_W % ROWS16 == 0
    first = ((starts[:, :nb] // ROWS16) * ROWS16).T.reshape(nb, 1, E)
    return pl.pallas_call(
        functools.partial(_combine_kernel, cap=cap, final_norm=final_g is not None),
        grid_spec=pltpu.PrefetchScalarGridSpec(
            num_scalar_prefetch=1,
            grid=(nb,),
            in_specs=[pl.BlockSpec((tb, D), lambda i, st: (i, 0)),
                      pl.BlockSpec((tb, E), lambda i, st: (i, 0)),
                      pl.BlockSpec((1, 1, E), lambda i, st: (i, 0, 0)),
                      pl.BlockSpec((1, D), lambda i, st: (0, 0)),
                      pl.BlockSpec(memory_space=pl.ANY)],
            out_specs=pl.BlockSpec((tb, D), lambda i, st: (i, 0)),
            scratch_shapes=[pltpu.VMEM((2, E * COMBINE_W, D), BF16),
                            pltpu.VMEM((COMBINE_W, D), BF16),
                            pltpu.SemaphoreType.DMA((2, E)),
                            pltpu.SemaphoreType.DMA((1,))]),
        out_shape=jax.ShapeDtypeStruct((T, D), F32),
        compiler_params=_params("arbitrary"),
        name="moe_combine",
    )(starts, x2, slot_t, first, gain, ye)


def _t5_bucket(rel):
    half = T5_BUCKETS // 2
    max_exact = half // 2
    n = np.abs(rel)
    large = max_exact + (np.log(np.maximum(n, 1) / max_exact) / np.log(T5_MAX_DISTANCE / max_exact)
                         * (half - max_exact)).astype(np.int32)
    large = np.minimum(large, half - 1)
    return (np.where(rel > 0, half, 0) + np.where(n < max_exact, n, large)).astype(np.int32)


def _dilated_tiles(L, radius):
    tq = min(LANES, L)
    kw = min(tq + 2 * radius, L)
    return tq, kw


def _dilated_bias(t5_table, g, L, window, dilation):
    radius = window // (2 * dilation)
    tq, kw = _dilated_tiles(L, radius)
    deltas = (0, -((kw - tq) // 2), tq - kw)
    period = 2 * kw - 1
    rel = np.arange(period + 1) - (kw - 1)
    tab = t5_table[:, g * A_HEADS_PER_GROUP:(g + 1) * A_HEADS_PER_GROUP].astype(F32)
    by_rel = jnp.where((np.abs(rel) <= radius)[None], tab[_t5_bucket(rel * dilation)].T, NEG_INF)
    skew = jnp.tile(by_rel, (1, tq))[:, :tq * period].reshape(A_HEADS_PER_GROUP, tq, period)
    bias = jnp.stack([skew[:, :, d + kw - 1:d + 2 * kw - 1] for d in deltas])
    return bias.reshape(3, A_HEADS_PER_GROUP // 2, 2 * tq, kw)


def _dilated_kernel(q_ref, k_ref, v_ref, bias_ref, o_ref, lse_ref, *, L, tq, kw):
    nq = L // tq
    lane = lax.broadcasted_iota(jnp.int32, (1, LANES), 1)
    lo = lane < HEAD_DIM

    def body(i, carry):
        q0 = pl.multiple_of(i * tq, tq)
        start = jnp.clip(q0 - (kw - tq) // 2, 0, L - kw)
        start = pl.multiple_of(start, ROWS16)
        var = jnp.where(i == 0, 0, jnp.where(i == nq - 1, 2, 1))
        n_pairs = A_HEADS_PER_GROUP // 2
        cols = [slice(pair * LANES, (pair + 1) * LANES) for pair in range(n_pairs)]
        scores = []
        for pair in range(n_pairs):
            q2 = q_ref[0, 0, pl.ds(q0, tq), cols[pair]] * (HEAD_DIM ** -0.5)
            zero = jnp.zeros_like(q2)
            qs = jnp.concatenate([jnp.where(lo, q2, zero), jnp.where(lo, zero, q2)], axis=0)
            k2 = k_ref[0, 0, pl.ds(start, kw), cols[pair]]
            scores.append(lax.dot_general(qs, k2, _NT, preferred_element_type=F32) + bias_ref[var, pair])
        probs, dens, maxs = [], [], []
        for s in scores:
            m = jnp.max(s, axis=-1, keepdims=True)
            p = jnp.exp(s - m)
            maxs.append(m)
            dens.append(jnp.sum(p, axis=-1, keepdims=True))
            probs.append(p.astype(BF16))
        for pair in range(n_pairs):
            v2 = v_ref[0, 0, pl.ds(start, kw), cols[pair]]
            o = jnp.dot(probs[pair], v2, preferred_element_type=F32) * (1.0 / dens[pair])
            lse = maxs[pair] + jnp.log(dens[pair])
            o_ref[0, 0, pl.ds(q0, tq), cols[pair]] = jnp.where(lo, o[:tq], o[tq:])
            lse_ref[0, 0, pl.ds(q0, tq), cols[pair]] = jnp.where(lo, lse[:tq], lse[tq:])
        return carry

    lax.fori_loop(0, nq, body, 0, unroll=next(u for u in (4, 2, 1) if nq % u == 0))


def _dilated_attention(qkv, bias, window):
    Bn, dilation, L, _ = qkv.shape
    radius = window // (2 * dilation)
    tq, kw = _dilated_tiles(L, radius)
    assert L % tq == 0 and kw % ROWS16 == 0 and tq % ROWS16 == 0

    def spec(which):
        return pl.BlockSpec((1, 1, L, A_OUT), lambda b, r: (b, r, 0, which))

    o, lse = pl.pallas_call(
        functools.partial(_dilated_kernel, L=L, tq=tq, kw=kw),
        grid=(Bn, dilation),
        in_specs=[spec(0), spec(1), spec(2), pl.BlockSpec(bias.shape, lambda b, r: (0, 0, 0, 0))],
        out_specs=[spec(0), spec(0)],
        out_shape=[jax.ShapeDtypeStruct((Bn, dilation, L, A_OUT), F32)] * 2,
        compiler_params=_params("parallel", "parallel"),
        name="dilated_attention",
    )(qkv, qkv, qkv, bias)
    return o, lse


def _merge_kernel(*refs):
    n = len(A_PATTERNS)
    o_refs, l_refs, out_ref = refs[:n], refs[n:2 * n], refs[2 * n]
    scratch = refs[2 * n + 1:]
    tm = out_ref.shape[0]

    def token_order(ref, scr, dilation):
        if dilation == 1:
            return ref[0, 0]
        tiles = A_OUT // LANES
        for r in range(dilation):
            for c in range(tiles):
                scr[c, pl.ds(r, tm // dilation, stride=dilation), :] = ref[0, r, :, c * LANES:(c + 1) * LANES]
        return jnp.concatenate([scr[c] for c in range(tiles)], axis=1)

    os_ = [token_order(o_refs[g], scratch[2 * g], d) for g, (_, d) in enumerate(A_PATTERNS)]
    ls = [token_order(l_refs[g], scratch[2 * g + 1], d) for g, (_, d) in enumerate(A_PATTERNS)]
    m = jnp.maximum(jnp.maximum(ls[0], ls[1]), ls[2])
    es = [jnp.exp(l - m) for l in ls]
    den = es[0] + es[1] + es[2]
    out = (es[0] / den) * os_[0] + (es[1] / den) * os_[1] + (es[2] / den) * os_[2]
    out_ref[...] = out.astype(out_ref.dtype)


def _merge_groups(os_, ls_, S, tm=512):
    Bn = os_[0].shape[0]
    ns = S // tm

    def spec(d):
        return pl.BlockSpec((1, d, tm // d, A_OUT), lambda i: (i // ns, 0, i % ns, 0))

    specs = [spec(d) for _, d in A_PATTERNS]
    return pl.pallas_call(
        _merge_kernel, grid=(Bn * ns,), in_specs=specs * 2,
        out_specs=pl.BlockSpec((tm, A_OUT), lambda i: (i, 0)),
        out_shape=jax.ShapeDtypeStruct((Bn * S, A_OUT), BF16),
        scratch_shapes=[pltpu.VMEM((A_OUT // LANES, tm, LANES), F32)] * (2 * len(A_PATTERNS)),
        compiler_params=_params("parallel"), name="merge_groups",
    )(*os_, *ls_)


EVEN_TOKEN_COLS = 3 * A_OUT + B_KV_RANK + LANES + B_Q_RANK


def _in_proj_even_kernel(x_ref, w_ref, g_ref, u0_ref, *rest):
    n_dil = len(A_PATTERNS) - 1
    dil_refs, scr = rest[:n_dil], rest[n_dil]
    tm = x_ref.shape[0]
    acc = jnp.dot(_rms(x_ref[...], g_ref[...]).astype(BF16), w_ref[...], preferred_element_type=F32)
    u0_ref[...] = acc[:, :EVEN_TOKEN_COLS].astype(u0_ref.dtype)
    for i, (_, d) in enumerate(A_PATTERNS[1:]):
        c0 = EVEN_TOKEN_COLS + i * 3 * A_OUT
        for c in range(3 * A_OUT // LANES):
            scr[c] = acc[:, c0 + c * LANES:c0 + (c + 1) * LANES]
        for r in range(d):
            for c in range(3 * A_OUT // LANES):
                piece = scr[c, pl.ds(r, tm // d, stride=d), :]
                dil_refs[i][0, r, :, c * LANES:(c + 1) * LANES] = piece.astype(dil_refs[i].dtype)


def _in_proj_even(x2, w, gain, Bn, S, tm=512):
    T, K = x2.shape
    ns = S // tm
    dils = [d for _, d in A_PATTERNS[1:]]
    assert S % tm == 0 and all(tm % (ROWS16 * d) == 0 for d in dils)
    return pl.pallas_call(
        _in_proj_even_kernel,
        grid=(T // tm,),
        in_specs=[pl.BlockSpec((tm, K), lambda i: (i, 0)),
                  pl.BlockSpec(w.shape, lambda i: (0, 0)),
                  pl.BlockSpec((1, K), lambda i: (0, 0))],
        out_specs=[pl.BlockSpec((tm, EVEN_TOKEN_COLS), lambda i: (i, 0))]
        + [pl.BlockSpec((1, d, tm // d, 3 * A_OUT), lambda i: (i // ns, 0, i % ns, 0)) for d in dils],
        out_shape=[jax.ShapeDtypeStruct((T, EVEN_TOKEN_COLS), BF16)]
        + [jax.ShapeDtypeStruct((Bn, d, S // d, 3 * A_OUT), BF16) for d in dils],
        scratch_shapes=[pltpu.VMEM((3 * A_OUT // LANES, tm, LANES), F32)],
        compiler_params=_params("parallel"),
        name="in_proj_even",
    )(x2, w, gain.reshape(1, K).astype(F32))


def _rope_tables(S):
    inv = 1.0 / (ROPE_THETA ** (np.arange(0, B_ROPE, 2, dtype=np.float32) / B_ROPE))
    ang = np.arange(S, dtype=np.float32)[:, None] * inv[None]
    cos, sin = np.cos(ang), np.sin(ang)
    half = B_ROPE // 2
    c = np.ones((S, LANES), np.float32)
    c[:, B_NOPE:B_NOPE + half] = cos
    c[:, B_NOPE + half:B_NOPE + B_ROPE] = cos
    s_lo = np.zeros((S, LANES), np.float32)
    s_lo[:, B_NOPE:B_NOPE + half] = -sin
    s_hi = np.zeros((S, LANES), np.float32)
    s_hi[:, B_NOPE + half:B_NOPE + B_ROPE] = sin
    return jnp.asarray(c), jnp.asarray(s_lo), jnp.asarray(s_hi)


def _rope(x, c, s_lo, s_hi):
    half = B_ROPE // 2
    return x * c + pltpu.roll(x, LANES - half, 1) * s_lo + pltpu.roll(x, half, 1) * s_hi


def _mla_prep_kernel(cq_ref, ckv_ref, kr_ref, gq_ref, gkv_ref, wq_ref, wk_ref, wv_ref,
                     c_ref, slo_ref, shi_ref, q_ref, k_ref, v_ref):
    c, s_lo, s_hi = c_ref[...], slo_ref[...], shi_ref[...]
    scale = (B_NOPE + B_ROPE) ** -0.5 * LOG2_E
    xq = _rms(cq_ref[...], gq_ref[...]).astype(BF16)
    q = jnp.dot(xq, wq_ref[...], preferred_element_type=F32)
    xkv = _rms(ckv_ref[...], gkv_ref[...]).astype(BF16)
    k = jnp.dot(xkv, wk_ref[...], preferred_element_type=F32)
    vt = lax.dot_general(wv_ref[...], xkv, _NT, preferred_element_type=F32)
    ones_row = lax.broadcasted_iota(jnp.int32, vt.shape, 0) % LANES == B_V
    v_ref[0, 0] = jnp.where(ones_row, 1.0, vt).astype(v_ref.dtype)
    k_rope = _rope(kr_ref[...].astype(F32), c, s_lo, s_hi)
    for h in range(B_HEADS):
        cs = slice(h * LANES, (h + 1) * LANES)
        q_ref[:, cs] = (_rope(q[:, cs], c, s_lo, s_hi) * scale).astype(q_ref.dtype)
        k_ref[:, cs] = (k[:, cs] + k_rope).astype(k_ref.dtype)


MLA_KEY_CHUNK = 1024


def _mla_prep(u3, g_qa, g_kva, wq, wk, wv):
    Bn, S, C = u3.shape
    ts = min(MLA_KEY_CHUNK, S)
    tabs = _rope_tables(S)
    ns = S // ts
    u2 = u3.reshape(Bn * S, C)
    row = lambda b, i: b * ns + i
    a_cols = 3 * A_OUT
    off_kv, off_kr, off_q = a_cols // B_KV_RANK, (a_cols + B_KV_RANK) // LANES, (a_cols + B_KV_RANK + LANES) // B_Q_RANK
    full = lambda a: pl.BlockSpec(a.shape, lambda b, i: (0, 0))
    tab_spec = pl.BlockSpec((ts, LANES), lambda b, i: (i, 0))
    hw = B_HEADS * LANES
    return pl.pallas_call(
        _mla_prep_kernel,
        grid=(Bn, ns),
        in_specs=[pl.BlockSpec((ts, B_Q_RANK), lambda b, i: (row(b, i), off_q)),
                  pl.BlockSpec((ts, B_KV_RANK), lambda b, i: (row(b, i), off_kv)),
                  pl.BlockSpec((ts, LANES), lambda b, i: (row(b, i), off_kr)),
                  full(g_qa), full(g_kva), full(wq), full(wk), full(wv),
                  tab_spec, tab_spec, tab_spec],
        out_specs=[pl.BlockSpec((ts, hw), lambda b, i: (row(b, i), 0)),
                   pl.BlockSpec((ts, hw), lambda b, i: (row(b, i), 0)),
                   pl.BlockSpec((1, 1, hw, ts), lambda b, i: (b, i, 0, 0))],
        out_shape=[jax.ShapeDtypeStruct((Bn * S, hw), BF16),
                   jax.ShapeDtypeStruct((Bn * S, hw), BF16),
                   jax.ShapeDtypeStruct((Bn, ns, hw, ts), BF16)],
        compiler_params=_params("parallel", "parallel"),
        name="mla_prep",
    )(u2, u2, u2, g_qa, g_kva, wq, wk, wv, *tabs)


def _mla_kernel(q_ref, k_ref, v_ref, o_ref):
    nk, _, tk = v_ref.shape[1:]
    tq = q_ref.shape[1]
    lane = lax.broadcasted_iota(jnp.int32, (1, LANES), 1)
    qs = [q_ref[0, :, sub * LANES:(sub + 1) * LANES] for sub in range(2)]

    def body(j, carry):
        k0 = pl.multiple_of(j * tk, tk)
        scores = [lax.dot_general(k_ref[0, pl.ds(k0, tk), sub * LANES:(sub + 1) * LANES], qs[sub], _NT,
                                  preferred_element_type=F32) for sub in range(2)]
        new = []
        for sub in range(2):
            m, acc = carry[sub]
            s = scores[sub]
            m_new = jnp.maximum(m, jnp.max(s, axis=0, keepdims=True))
            p = jnp.exp2(s - m_new).astype(BF16)
            vt = v_ref[0, j, sub * LANES:(sub + 1) * LANES, :]
            acc = jnp.exp2(m - m_new) * acc + jnp.dot(vt, p, preferred_element_type=F32)
            new.append((m_new, acc))
        return tuple(new)

    init = tuple((jnp.full((1, tq), NEG_INF, F32), jnp.zeros((LANES, tq), F32)) for _ in range(2))
    res = lax.fori_loop(0, nk, body, init, unroll=True)
    outs = [(acc * (1.0 / acc[B_V:B_V + 1])).T for (_, acc) in res]
    o_ref[0] = jnp.where(lane < B_V, outs[0], pltpu.roll(outs[1], B_V, 1)).astype(o_ref.dtype)


def _mla_attention(q, k, vt, Bn, S, tq=2048):
    tq = min(tq, S)
    nk, _, tk = vt.shape[1:]
    hw = B_HEADS * LANES
    q3, k3 = q.reshape(Bn, S, hw), k.reshape(Bn, S, hw)
    o = pl.pallas_call(
        _mla_kernel,
        grid=(Bn, B_HEADS // 2, S // tq),
        in_specs=[pl.BlockSpec((1, tq, 2 * LANES), lambda b, h, i: (b, i, h)),
                  pl.BlockSpec((1, S, 2 * LANES), lambda b, h, i: (b, 0, h)),
                  pl.BlockSpec((1, nk, 2 * LANES, tk), lambda b, h, i: (b, 0, h, 0))],
        out_specs=pl.BlockSpec((1, tq, LANES), lambda b, h, i: (b, i, h)),
        out_shape=jax.ShapeDtypeStruct((Bn, S, B_OUT), BF16),
        compiler_params=_params("parallel", "parallel", "parallel"),
        name="mla_attention",
    )(q3, k3, vt)
    return o.reshape(Bn * S, B_OUT)


def _natten_bias(rpb):
    W = GRID_W
    c = np.arange(W)[:, None]
    kc = np.arange(W)[None, :]
    wstart = np.clip(c - C_WIN_COLS // 2, 0, W - C_WIN_COLS)
    col_valid = (kc >= wstart) & (kc < wstart + C_WIN_COLS)
    n_dcol = 2 * C_WIN_COLS - 1
    left = (W - 1) - (C_WIN_COLS - 1)
    by_rel = jnp.pad(rpb.astype(F32), ((0, 0), (0, 0), (left, 2 * W - left - n_dcol)))
    skew = jnp.tile(by_rel, (1, 1, W))[:, :, :W * (2 * W - 1)].reshape(C_HEADS, -1, W, 2 * W - 1)
    cols = jnp.where(col_valid[None, None], skew[:, :, :, W - 1:2 * W - 1], NEG_INF)
    b = jnp.stack([cols[:, C_WIN_ROWS - 1 - off:2 * C_WIN_ROWS - 1 - off] for off in range(C_WIN_ROWS)])
    b = jnp.transpose(b, (0, 1, 3, 2, 4))
    return b.reshape(C_WIN_ROWS, C_HEADS // 2, 2 * W, C_WIN_ROWS * W)


NATTEN_ROWS = 4


def _natten_kernel(q_ref, k_ref, v_ref, *rest, rows):
    bias_refs, o_ref = rest[:NATTEN_ROWS], rest[NATTEN_ROWS]
    nk = C_WIN_ROWS * GRID_W
    lane = lax.broadcasted_iota(jnp.int32, (1, LANES), 1)
    lo = lane < HEAD_DIM
    n_pairs = C_HEADS // 2
    cols = [slice(pair * LANES, (pair + 1) * LANES) for pair in range(n_pairs)]
    for j in range(NATTEN_ROWS):
        r = pl.program_id(1) * NATTEN_ROWS + j
        r0 = jnp.clip(r - C_WIN_ROWS // 2, 0, rows - C_WIN_ROWS)
        k0 = pl.multiple_of(r0 * GRID_W, GRID_W)
        q_rows = slice(j * GRID_W, (j + 1) * GRID_W)
        scores = []
        for pair in range(n_pairs):
            q2 = q_ref[0, q_rows, cols[pair]] * (HEAD_DIM ** -0.5)
            zero = jnp.zeros_like(q2)
            qs = jnp.concatenate([jnp.where(lo, q2, zero), jnp.where(lo, zero, q2)], axis=0)
            k2 = k_ref[0, pl.ds(k0, nk), cols[pair]]
            scores.append(lax.dot_general(qs, k2, _NT, preferred_element_type=F32) + bias_refs[j][0, pair])
        probs, dens = [], []
        for s in scores:
            p = jnp.exp(s - jnp.max(s, axis=-1, keepdims=True))
            dens.append(jnp.sum(p, axis=-1, keepdims=True))
            probs.append(p.astype(BF16))
        for pair in range(n_pairs):
            v2 = v_ref[0, pl.ds(k0, nk), cols[pair]]
            o = jnp.dot(probs[pair], v2, preferred_element_type=F32) * (1.0 / dens[pair])
            o_ref[0, q_rows, cols[pair]] = jnp.where(lo, o[:GRID_W], o[GRID_W:]).astype(o_ref.dtype)


def _natten(u3, bias):
    Bn, S, _ = u3.shape
    rows = S // GRID_W
    assert rows >= C_WIN_ROWS and rows % NATTEN_ROWS == 0

    def bias_spec(j):
        def variant(b, i):
            r = i * NATTEN_ROWS + j
            return (r - jnp.clip(r - C_WIN_ROWS // 2, 0, rows - C_WIN_ROWS), 0, 0, 0)
        return pl.BlockSpec((1, C_HEADS // 2, 2 * GRID_W, C_WIN_ROWS * GRID_W), variant)

    q_spec = pl.BlockSpec((1, NATTEN_ROWS * GRID_W, C_OUT), lambda b, i: (b, i, 0))
    o = pl.pallas_call(
        functools.partial(_natten_kernel, rows=rows),
        grid=(Bn, rows // NATTEN_ROWS),
        in_specs=[q_spec,
                  pl.BlockSpec((1, S, C_OUT), lambda b, i: (b, 0, 1), pipeline_mode=pl.Buffered(1)),
                  pl.BlockSpec((1, S, C_OUT), lambda b, i: (b, 0, 2), pipeline_mode=pl.Buffered(1))]
        + [bias_spec(j) for j in range(NATTEN_ROWS)],
        out_specs=q_spec,
        out_shape=jax.ShapeDtypeStruct((Bn, S, C_OUT), BF16),
        compiler_params=_params("parallel", "arbitrary"),
        name="natten",
    )(u3, u3, u3, *([bias] * NATTEN_ROWS))
    return o.reshape(Bn * S, C_OUT)


def _moe(x2, ln_g, router, wg, wu, wd, layer, final_g=None):
    T, D = x2.shape
    cap = CAPACITY_FACTOR * T // N_EXPERTS
    aff_t, h = _router(x2, ln_g, router)
    slot, pos = _select(aff_t, cap)
    tb = min(COMBINE_TB, T)
    starts = jnp.concatenate([pos[:, ::tb], jnp.full((N_EXPERTS, 1), cap, jnp.int32)], axis=1)
    xe = _gather(h, slot, starts, cap)
    ye = _expert_ffn(xe, cap, wg, wu, wd, layer)
    return _combine(x2, slot.T, starts, ye, final_g)


def _trunk(x, p):
    Bn, S, D = x.shape
    T = Bn * S
    x2 = x.reshape(T, D)
    u0, *u_dil = _in_proj_even(x2, p["w_in_even"], p["ln_mix_g"][0], Bn, S)
    u3 = u0.reshape(Bn, S, EVEN_TOKEN_COLS)
    os_, ls_ = [], []
    for g, (window, dilation) in enumerate(A_PATTERNS):
        bias = _dilated_bias(p["t5_table"], g, S // dilation, window, dilation)
        qkv = u3.reshape(Bn, 1, S, EVEN_TOKEN_COLS) if g == 0 else u_dil[g - 1]
        o, l = _dilated_attention(qkv, bias, window)
        os_.append(o)
        ls_.append(l)
    a = _merge_groups(os_, ls_, S)
    q, k, v = _mla_prep(u3, p["g_qa"], p["g_kva"], p["wq"], p["wk"], p["wv"])
    b = _mla_attention(q, k, v, Bn, S)
    x2 = _matmul([a, b], [p["w_out_even"][:A_OUT], p["w_out_even"][A_OUT:]], res=x2, out_dtype=F32, tm=1024)
    x2 = _moe(x2, p["ln_ffn_g"][0], p["router"][0], p["w_gate"], p["w_up"], p["w_down"], 0)
    u = _matmul(x2, p["w_in_odd"], gain=p["ln_mix_g"][1], out_dtype=BF16)
    c = _natten(u.reshape(Bn, S, C_IN), p["natten_bias"])
    x2 = _matmul(c, p["w_out_odd"], res=x2, out_dtype=F32, tm=1024)
    x2 = _moe(x2, p["ln_ffn_g"][1], p["router"][1], p["w_gate"], p["w_up"], p["w_down"], 1,
              final_g=p["final_g"])
    return x2.reshape(Bn, S, D)


def _head_tiles(w, n_heads, per_head, keep):
    K = w.shape[0]
    w = w.reshape(K, n_heads, per_head)[:, :, :keep]
    return jnp.pad(w, ((0, 0), (0, 0), (0, LANES - keep))).reshape(K, n_heads * LANES)


def _prepare(ln_mix_g, w_in_even, g_qa, w_qb, g_kva, w_kvb, t5_table, w_out_even,
             w_in_odd, rpb, w_out_odd, ln_ffn_g, router, w_gate, w_up, w_down, final_g):
    w_in = w_in_even[0]
    w_a, w_cq = w_in[:, :A_IN], w_in[:, A_IN:A_IN + B_Q_RANK]
    w_ckv = w_in[:, A_IN + B_Q_RANK:A_IN + B_Q_RANK + B_KV_RANK]
    w_kr = w_in[:, A_IN + B_Q_RANK + B_KV_RANK:]
    w_kr_tile = jnp.pad(w_kr, ((0, 0), (B_NOPE, LANES - B_NOPE - B_ROPE)))
    w_a = w_a.reshape(-1, 3, A_GROUPS, A_OUT)
    groups = [w_a[:, :, g].reshape(-1, 3 * A_OUT) for g in range(A_GROUPS)]
    w_even = jnp.concatenate([groups[0], w_ckv, w_kr_tile, w_cq] + groups[1:], axis=1)
    w_kv = w_kvb[0].reshape(B_KV_RANK, B_HEADS, B_NOPE + B_V)
    return dict(
        ln_mix_g=ln_mix_g, ln_ffn_g=ln_ffn_g, final_g=final_g, t5_table=t5_table,
        w_in_even=w_even.astype(BF16),
        g_qa=g_qa[0].reshape(1, -1).astype(F32), g_kva=g_kva[0].reshape(1, -1).astype(F32),
        wq=_head_tiles(w_qb[0], B_HEADS, B_NOPE + B_ROPE, B_NOPE + B_ROPE).astype(BF16),
        wk=_head_tiles(w_kvb[0], B_HEADS, B_NOPE + B_V, B_NOPE).astype(BF16),
        wv=_head_tiles(w_kv[:, :, B_NOPE:].reshape(B_KV_RANK, B_OUT), B_HEADS, B_V, B_V).T.astype(BF16),
        w_out_even=w_out_even[0].astype(BF16), w_in_odd=w_in_odd[0].astype(BF16),
        natten_bias=_natten_bias(rpb[0]), w_out_odd=w_out_odd[0].astype(BF16), router=router,
        w_gate=w_gate, w_up=w_up, w_down=w_down,
    )


def kernel(x_prompt, x_sample, ln_mix_g, w_in_even, g_qa, w_qb, g_kva, w_kvb, t5_table, w_out_even,
           w_in_odd, rpb, w_out_odd, ln_ffn_g, router, w_gate, w_up, w_down, final_g):
    p = _prepare(ln_mix_g, w_in_even, g_qa, w_qb, g_kva, w_kvb, t5_table, w_out_even,
                 w_in_odd, rpb, w_out_odd, ln_ffn_g, router, w_gate, w_up, w_down, final_g)
    return _trunk(x_prompt, p), _trunk(x_sample, p)
```

```python
import functools

import jax
import jax.numpy as jnp
import numpy as np
from jax import lax
from jax.experimental import pallas as pl
from jax.experimental.pallas import tpu as pltpu

D_MODEL = 1024
RMS_EPS = 1e-6
HEAD_DIM = 64
NEG_INF = -1e30
A_PATTERNS = ((128, 1), (512, 4), (2048, 16))
A_GROUPS = 3
A_HEADS_PER_GROUP = 4
A_HEADS = A_GROUPS * A_HEADS_PER_GROUP
A_IN = 3 * A_HEADS * HEAD_DIM
A_OUT = A_HEADS_PER_GROUP * HEAD_DIM
T5_BUCKETS = 32
T5_MAX_DISTANCE = 1024
B_HEADS = 12
B_Q_RANK = 384
B_KV_RANK = 256
B_NOPE = 64
B_ROPE = 32
B_V = 64
B_OUT = B_HEADS * B_V
ROPE_THETA = 10000.0
GRID_W = 64
C_HEADS = 16
C_WIN_ROWS = 8
C_WIN_COLS = 16
C_IN = 3 * C_HEADS * HEAD_DIM
C_OUT = C_HEADS * HEAD_DIM
N_EXPERTS = 16
CAPACITY_FACTOR = 2

LANES = 128
ROWS16 = 16
VMEM_LIMIT_BYTES = 48 * 1024 * 1024

F32 = jnp.float32
BF16 = jnp.bfloat16
_NT = (((1,), (1,)), ((), ()))
LOG2_E = 1.4426950408889634


def _params(*sem):
    return pltpu.CompilerParams(dimension_semantics=sem, vmem_limit_bytes=VMEM_LIMIT_BYTES)


def _rms(x, g):
    xf = x.astype(F32)
    return xf * lax.rsqrt(jnp.mean(xf * xf, axis=-1, keepdims=True) + RMS_EPS) * g


def _mm_kernel(*refs, n_in, has_gain, has_res):
    xs, ws = refs[:n_in], refs[n_in:2 * n_in]
    pos = 2 * n_in
    g_ref = res_ref = None
    if has_gain:
        g_ref = refs[pos]
        pos += 1
    if has_res:
        res_ref = refs[pos]
        pos += 1
    o_ref = refs[pos]
    acc = None
    for x_ref, w_ref in zip(xs, ws):
        x = x_ref[...]
        if has_gain:
            x = _rms(x, g_ref[...])
        part = jnp.dot(x.astype(BF16), w_ref[...], preferred_element_type=F32)
        acc = part if acc is None else acc + part
    if has_res:
        acc = acc + res_ref[...]
    o_ref[...] = acc.astype(o_ref.dtype)


def _matmul(xs, ws, gain=None, res=None, out_dtype=F32, tm=512):
    if not isinstance(xs, (list, tuple)):
        xs, ws = [xs], [ws]
    M = xs[0].shape[0]
    N = ws[0].shape[1]
    assert M % tm == 0
    in_specs = [pl.BlockSpec((tm, x.shape[1]), lambda i: (i, 0)) for x in xs]
    in_specs += [pl.BlockSpec(w.shape, lambda i: (0, 0)) for w in ws]
    args = list(xs) + list(ws)
    if gain is not None:
        assert len(xs) == 1
        in_specs.append(pl.BlockSpec((1, gain.shape[-1]), lambda i: (0, 0)))
        args.append(gain.reshape(1, -1).astype(F32))
    if res is not None:
        in_specs.append(pl.BlockSpec((tm, N), lambda i: (i, 0)))
        args.append(res)
    return pl.pallas_call(
        functools.partial(_mm_kernel, n_in=len(xs), has_gain=gain is not None,
                          has_res=res is not None),
        grid=(M // tm,),
        in_specs=in_specs,
        out_specs=pl.BlockSpec((tm, N), lambda i: (i, 0)),
        out_shape=jax.ShapeDtypeStruct((M, N), out_dtype),
        compiler_params=_params("parallel"),
        name="fused_matmul",
    )(*args)


def _ffn_kernel(x_ref, wg32_ref, wu32_ref, wd32_ref, o_ref, wg_ref, wu_ref, wd_ref):
    D = wg_ref.shape[1]

    @pl.when(pl.program_id(1) == 0)
    def _():
        wg_ref[0] = wg32_ref[0, 0].astype(BF16)
        wu_ref[0] = wu32_ref[0, 0].astype(BF16)
        wd_ref[0] = wd32_ref[0, 0].astype(BF16)

    x = x_ref[0, :, :D]
    lane = lax.broadcasted_iota(jnp.int32, (1, LANES), 1)
    mine = lane // GATE_TERMS == pl.program_id(0)
    gate = jnp.sum(jnp.where(mine, x_ref[0, :, D:].astype(F32), 0.0), axis=1, keepdims=True)
    g = jnp.dot(x, wg_ref[0], preferred_element_type=F32)
    u = jnp.dot(x, wu_ref[0], preferred_element_type=F32)
    h = (g * jax.nn.sigmoid(g)) * u
    y = jnp.dot(h.astype(BF16), wd_ref[0], preferred_element_type=F32)
    o_ref[0] = (y * gate).astype(o_ref.dtype)


def _expert_ffn(xe, cap, wg, wu, wd, layer, tm=1024):
    E = xe.shape[0]
    D, FF = wg.shape[2], wg.shape[3]
    tm = min(tm, cap)
    assert cap % tm == 0
    return pl.pallas_call(
        _ffn_kernel,
        grid=(E, cap // tm),
        in_specs=[pl.BlockSpec((1, tm, D + LANES), lambda e, i: (e, i, 0)),
                  pl.BlockSpec((1, 1, D, FF), lambda e, i: (layer, e, 0, 0)),
                  pl.BlockSpec((1, 1, D, FF), lambda e, i: (layer, e, 0, 0)),
                  pl.BlockSpec((1, 1, FF, D), lambda e, i: (layer, e, 0, 0))],
        out_specs=pl.BlockSpec((1, tm, D), lambda e, i: (e, i, 0)),
        out_shape=jax.ShapeDtypeStruct((E, cap, D), BF16),
        scratch_shapes=[pltpu.VMEM((1, D, FF), BF16), pltpu.VMEM((1, D, FF), BF16), pltpu.VMEM((1, FF, D), BF16)],
        compiler_params=_params("parallel", "arbitrary"),
        name="expert_ffn",
    )(xe, wg, wu, wd)


CHUNK = 256
COMBINE_TB = 512
COMBINE_W = 112


GATE_TERMS = 3


def _router_kernel(x_ref, g_ref, rt_ref, aff_ref, h_ref):
    D = x_ref.shape[1]
    E = rt_ref.shape[0]
    xn = _rms(x_ref[...], g_ref[...])
    logits = lax.dot_general(rt_ref[...], xn, _NT, preferred_element_type=F32,
                             precision=lax.Precision.HIGHEST)
    e = jnp.exp(logits - jnp.max(logits, axis=0, keepdims=True))
    aff = e / jnp.sum(e, axis=0, keepdims=True)
    aff_ref[...] = aff
    lane = lax.broadcasted_iota(jnp.int32, (E, LANES), 1)
    expert = lax.broadcasted_iota(jnp.int32, (E, LANES), 0)
    rest = aff
    gates = jnp.zeros((x_ref.shape[0], LANES), F32)
    for j in range(GATE_TERMS):
        term = rest.astype(BF16)
        rest = rest - term.astype(F32)
        place = (lane == GATE_TERMS * expert + j).astype(BF16)
        gates = gates + lax.dot_general(term, place, (((0,), (0,)), ((), ())), preferred_element_type=F32)
    h_ref[:, :D] = xn.astype(h_ref.dtype)
    h_ref[:, D:] = gates.astype(h_ref.dtype)


def _router(x2, ln_g, router, tm=1024):
    T, D = x2.shape
    E = router.shape[1]
    assert GATE_TERMS * E <= LANES
    return pl.pallas_call(
        _router_kernel,
        grid=(T // tm,),
        in_specs=[pl.BlockSpec((tm, D), lambda i: (i, 0)),
                  pl.BlockSpec((1, D), lambda i: (0, 0)),
                  pl.BlockSpec((E, D), lambda i: (0, 0))],
        out_specs=[pl.BlockSpec((E, tm), lambda i: (0, i)),
                   pl.BlockSpec((tm, D + LANES), lambda i: (i, 0))],
        out_shape=[jax.ShapeDtypeStruct((E, T), F32), jax.ShapeDtypeStruct((T, D + LANES), BF16)],
        compiler_params=_params("parallel"),
        name="router",
    )(x2, ln_g.reshape(1, D).astype(F32), router.T.astype(F32))


def _cumsum_excl(mask, upper, lower_strict):
    incl = jnp.dot(mask.astype(BF16), upper, preferred_element_type=F32)
    before = jnp.dot(lower_strict, incl.astype(BF16), preferred_element_type=F32)
    return incl + before[:, CHUNK - 1:CHUNK] - mask


def _select_kernel(aff_ref, slot_ref, pos_ref, *, cap):
    aff = aff_ref[0]
    nc = aff.shape[0]
    bits = pltpu.bitcast(aff, jnp.int32)

    def count(cond):
        c = jnp.sum(jnp.where(cond, 1.0, 0.0), axis=0, keepdims=True)
        return jnp.sum(c, axis=1, keepdims=True)

    def has_cap_above(cand):
        return jnp.where(count(bits >= cand) >= cap, 1, 0).astype(jnp.int32)

    def digit_step(i, prefix):
        shift = 28 - 2 * i
        digit = sum(has_cap_above(prefix | (jnp.int32(d) << shift)) for d in (1, 2, 3))
        return prefix | (digit << shift)

    top = has_cap_above(jnp.full((1, 1), 1 << 30, jnp.int32)) << 30
    thr = lax.fori_loop(0, 15, digit_step, top)
    r = lax.broadcasted_iota(jnp.int32, (CHUNK, CHUNK), 0)
    c = lax.broadcasted_iota(jnp.int32, (CHUNK, CHUNK), 1)
    upper = (r <= c).astype(BF16)
    r = lax.broadcasted_iota(jnp.int32, (nc, nc), 0)
    c = lax.broadcasted_iota(jnp.int32, (nc, nc), 1)
    lower_strict = (c < r).astype(BF16)
    gt = bits > thr
    eq = jnp.where(bits == thr, 1.0, 0.0)
    need = cap - count(gt)
    sel = jnp.where(gt | ((eq > 0) & (_cumsum_excl(eq, upper, lower_strict) < need)), 1.0, 0.0)
    pos = _cumsum_excl(sel, upper, lower_strict).astype(jnp.int32)
    pos_ref[0] = pos
    slot_ref[0] = jnp.where(sel > 0, pos, -1)


def _select(aff_t, cap):
    E, T = aff_t.shape
    nc = T // CHUNK
    assert nc <= CHUNK and nc % 8 == 0
    spec = pl.BlockSpec((1, nc, CHUNK), lambda e: (e, 0, 0))
    slot, pos = pl.pallas_call(
        functools.partial(_select_kernel, cap=cap),
        grid=(E,), in_specs=[spec], out_specs=[spec, spec],
        out_shape=[jax.ShapeDtypeStruct((E, nc, CHUNK), jnp.int32)] * 2,
        compiler_params=_params("parallel"),
        name="select_topk",
    )(aff_t.reshape(E, nc, CHUNK))
    return slot.reshape(E, T), pos.reshape(E, T)


GATHER_STEP = 96
GATHER_W = GATHER_STEP + ROWS16
GATHER_STACK = 8


def _gather_kernel(starts_ref, h_ref, slot_ref, xe_ref, stage_ref, extra_ref, tail_ref, sem, extra_sem, *, cap):
    b = pl.program_id(0)
    nb = pl.num_programs(0)
    E = slot_ref.shape[0]
    buf = b % 2
    h = h_ref[...]
    row = lax.broadcasted_iota(jnp.int32, (GATHER_W, 1), 0)

    @pl.when(b == 0)
    def _():
        tail_ref[...] = jnp.zeros_like(tail_ref)

    def first_row(e, blk):
        return pl.multiple_of((starts_ref[e, blk] // ROWS16) * ROWS16, ROWS16)

    def copy(e, blk):
        return pltpu.make_async_copy(stage_ref.at[blk % 2, e], xe_ref.at[e, pl.ds(first_row(e, blk), GATHER_W)],
                                     sem.at[blk % 2, e])

    def onehot(e, k):
        return (slot_ref[pl.ds(e, 1), :] - (first_row(e, b) + k * GATHER_STEP) == row).astype(BF16)

    for e0 in range(0, E, GATHER_STACK):
        stacked = jnp.concatenate([onehot(e, 0) for e in range(e0, e0 + GATHER_STACK)], axis=0)
        rows = jnp.dot(stacked, h, preferred_element_type=F32).astype(BF16)
        for i in range(GATHER_STACK):
            stage_ref[buf, e0 + i] = rows[i * GATHER_W:(i + 1) * GATHER_W]

    def per_expert(e, carry):
        base = first_row(e, b)
        end = starts_ref[e, b + 1] - base
        group = pl.multiple_of((end // ROWS16) * ROWS16, ROWS16)
        last = jnp.maximum(group - 1, 0) // GATHER_STEP

        def window(k):
            return jnp.dot(onehot(e, k), h, preferred_element_type=F32).astype(BF16)

        stage_ref[buf, e, pl.ds(0, ROWS16), :] += tail_ref[e]

        @pl.when(last == 0)
        def _():
            tail_ref[e] = stage_ref[buf, e, pl.ds(group, ROWS16), :]

        @pl.when(b > 0)
        def _():
            copy(e, b - 1).wait()

        copy(e, b).start()

        def extra(k, inner):
            extra_ref[...] = window(k)

            @pl.when(k == last)
            def _():
                tail_ref[e] = extra_ref[pl.ds(pl.multiple_of(group - k * GATHER_STEP, ROWS16), ROWS16), :]

            cp = pltpu.make_async_copy(extra_ref, xe_ref.at[e, pl.ds(base + k * GATHER_STEP, GATHER_W)],
                                       extra_sem.at[0])
            cp.start()
            cp.wait()
            return inner

        return lax.fori_loop(1, last + 1, extra, carry)

    lax.fori_loop(0, E, per_expert, 0)

    @pl.when(b == nb - 1)
    def _():
        extra_ref[...] = jnp.zeros_like(extra_ref)

        def drain(e, carry):
            copy(e, b).wait()
            cp = pltpu.make_async_copy(extra_ref, xe_ref.at[e, pl.ds(cap, GATHER_W)], extra_sem.at[0])
            cp.start()
            cp.wait()
            return carry
        lax.fori_loop(0, E, drain, 0)


def _gather(h, slot, starts, cap):
    T, D = h.shape
    E = slot.shape[0]
    tb = min(COMBINE_TB, T)
    rows = cap + GATHER_W
    return pl.pallas_call(
        functools.partial(_gather_kernel, cap=cap),
        grid_spec=pltpu.PrefetchScalarGridSpec(
            num_scalar_prefetch=1,
            grid=(T // tb,),
            in_specs=[pl.BlockSpec((tb, D), lambda i, st: (i, 0)),
                      pl.BlockSpec((E, tb), lambda i, st: (0, i))],
            out_specs=pl.BlockSpec(memory_space=pl.ANY),
            scratch_shapes=[pltpu.VMEM((2, E, GATHER_W, D), BF16),
                            pltpu.VMEM((GATHER_W, D), BF16),
                            pltpu.VMEM((E, ROWS16, D), BF16),
                            pltpu.SemaphoreType.DMA((2, E)),
                            pltpu.SemaphoreType.DMA((1,))]),
        out_shape=jax.ShapeDtypeStruct((E, rows, D), BF16),
        compiler_params=_params("arbitrary"),
        name="moe_gather",
    )(starts, h, slot)


def _combine_kernel(starts_ref, x_ref, slot_ref, first_ref, g_ref, ye_ref, o_ref, win_ref, extra_ref, sem,
                    extra_sem, *, cap, final_norm):
    b = pl.program_id(0)
    nb = pl.num_programs(0)
    E = ye_ref.shape[0]
    W = COMBINE_W

    def first_row(e, blk, k):
        return (starts_ref[e, blk] // ROWS16) * ROWS16 + k * W

    def window_start(e, blk, k):
        return pl.multiple_of(jnp.minimum(first_row(e, blk, k), cap - W), ROWS16)

    def copy(e, blk):
        buf = blk % 2
        return pltpu.make_async_copy(ye_ref.at[e, pl.ds(window_start(e, blk, 0), W)],
                                     win_ref.at[buf, pl.ds(e * W, W)], sem.at[buf, e])

    @pl.when(b == 0)
    def _():
        for e in range(E):
            copy(e, b).start()

    @pl.when(b + 1 < nb)
    def _():
        for e in range(E):
            copy(e, b + 1).start()

    slots = slot_ref[...]
    first = first_ref[0]
    rel = slots - jnp.minimum(first, cap - W)
    rel = jnp.where((slots >= first) & (rel >= 0) & (rel < W), rel, -1).astype(F32).astype(BF16)
    lane = lax.broadcasted_iota(jnp.int32, (E, E * W), 1)
    expert = lax.broadcasted_iota(jnp.int32, (E, E * W), 0)
    expand = (lane // W == expert).astype(BF16)
    wanted = (lax.broadcasted_iota(jnp.int32, (1, E * W), 1) % W).astype(F32)
    onehot = (jnp.dot(rel, expand, preferred_element_type=F32) == wanted).astype(BF16)

    for e in range(E):
        copy(e, b).wait()
    o_ref[...] = x_ref[...] + jnp.dot(onehot, win_ref[b % 2], preferred_element_type=F32)

    col = lax.broadcasted_iota(jnp.int32, (1, W), 1)
    expert_lane = lax.broadcasted_iota(jnp.int32, (1, E), 1)

    def per_expert(e, carry):
        n_win = (starts_ref[e, b + 1] - first_row(e, b, 0) + W - 1) // W

        def extra(k, inner):
            cp = pltpu.make_async_copy(ye_ref.at[e, pl.ds(window_start(e, b, k), W)], extra_ref, extra_sem.at[0])
            cp.start()
            cp.wait()
            slot = jnp.max(jnp.where(expert_lane == e, slots, -1), axis=1, keepdims=True)
            oh = ((slot == window_start(e, b, k) + col) & (slot >= first_row(e, b, k))).astype(BF16)
            o_ref[...] += jnp.dot(oh, extra_ref[...], preferred_element_type=F32)
            return inner

        return lax.fori_loop(1, n_win, extra, carry)

    lax.fori_loop(0, E, per_expert, 0)
    if final_norm:
        o_ref[...] = _rms(o_ref[...], g_ref[...])


def _combine(x2, slot_t, starts, ye, final_g=None):
    T, D = x2.shape
    gain = jnp.ones((1, D), F32) if final_g is None else final_g.reshape(1, D).astype(F32)
    E, cap, _ = ye.shape
    tb = min(COMBINE_TB, T)
    nb = T // tb
    assert cap >= COMBINE_W and cap % ROWS16 == 0 and COMBINE_W % ROWS16 == 0
    first = ((starts[:, :nb] // ROWS16) * ROWS16).T.reshape(nb, 1, E)
    return pl.pallas_call(
        functools.partial(_combine_kernel, cap=cap, final_norm=final_g is not None),
        grid_spec=pltpu.PrefetchScalarGridSpec(
            num_scalar_prefetch=1,
            grid=(nb,),
            in_specs=[pl.BlockSpec((tb, D), lambda i, st: (i, 0)),
                      pl.BlockSpec((tb, E), lambda i, st: (i, 0)),
                      pl.BlockSpec((1, 1, E), lambda i, st: (i, 0, 0)),
                      pl.BlockSpec((1, D), lambda i, st: (0, 0)),
                      pl.BlockSpec(memory_space=pl.ANY)],
            out_specs=pl.BlockSpec((tb, D), lambda i, st: (i, 0)),
            scratch_shapes=[pltpu.VMEM((2, E * COMBINE_W, D), BF16),
                            pltpu.VMEM((COMBINE_W, D), BF16),
                            pltpu.SemaphoreType.DMA((2, E)),
                            pltpu.SemaphoreType.DMA((1,))]),
        out_shape=jax.ShapeDtypeStruct((T, D), F32),
        compiler_params=_params("arbitrary"),
        name="moe_combine",
    )(starts, x2, slot_t, first, gain, ye)


def _t5_bucket(rel):
    half = T5_BUCKETS // 2
    max_exact = half // 2
    n = np.abs(rel)
    large = max_exact + (np.log(np.maximum(n, 1) / max_exact) / np.log(T5_MAX_DISTANCE / max_exact)
                         * (half - max_exact)).astype(np.int32)
    large = np.minimum(large, half - 1)
    return (np.where(rel > 0, half, 0) + np.where(n < max_exact, n, large)).astype(np.int32)


def _dilated_tiles(L, radius):
    tq = min(LANES, L)
    kw = min(tq + 2 * radius, L)
    return tq, kw


def _dilated_bias(t5_table, g, L, window, dilation):
    radius = window // (2 * dilation)
    tq, kw = _dilated_tiles(L, radius)
    deltas = (0, -((kw - tq) // 2), tq - kw)
    period = 2 * kw - 1
    rel = np.arange(period + 1) - (kw - 1)
    tab = t5_table[:, g * A_HEADS_PER_GROUP:(g + 1) * A_HEADS_PER_GROUP].astype(F32)
    by_rel = jnp.where((np.abs(rel) <= radius)[None], tab[_t5_bucket(rel * dilation)].T, NEG_INF)
    skew = jnp.tile(by_rel, (1, tq))[:, :tq * period].reshape(A_HEADS_PER_GROUP, tq, period)
    bias = jnp.stack([skew[:, :, d + kw - 1:d + 2 * kw - 1] for d in deltas])
    return bias.reshape(3, A_HEADS_PER_GROUP // 2, 2 * tq, kw)


def _dilated_kernel(q_ref, k_ref, v_ref, bias_ref, o_ref, lse_ref, *, L, tq, kw):
    nq = L // tq
    n_res = q_ref.shape[1]
    lane = lax.broadcasted_iota(jnp.int32, (1, LANES), 1)
    lo = lane < HEAD_DIM

    def body(it, carry):
        rr, i = it // nq, it % nq
        q0 = pl.multiple_of(i * tq, tq)
        start = jnp.clip(q0 - (kw - tq) // 2, 0, L - kw)
        start = pl.multiple_of(start, ROWS16)
        var = jnp.where(i == 0, 0, jnp.where(i == nq - 1, 2, 1))
        n_pairs = A_HEADS_PER_GROUP // 2
        cols = [slice(pair * LANES, (pair + 1) * LANES) for pair in range(n_pairs)]
        scores = []
        for pair in range(n_pairs):
            q2 = q_ref[0, rr, pl.ds(q0, tq), cols[pair]] * (HEAD_DIM ** -0.5)
            zero = jnp.zeros_like(q2)
            qs = jnp.concatenate([jnp.where(lo, q2, zero), jnp.where(lo, zero, q2)], axis=0)
            k2 = k_ref[0, rr, pl.ds(start, kw), cols[pair]]
            scores.append(lax.dot_general(qs, k2, _NT, preferred_element_type=F32) + bias_ref[var, pair])
        probs, dens, maxs = [], [], []
        for s in scores:
            m = jnp.max(s, axis=-1, keepdims=True)
            p = jnp.exp(s - m)
            maxs.append(m)
            dens.append(jnp.sum(p, axis=-1, keepdims=True))
            probs.append(p.astype(BF16))
        for pair in range(n_pairs):
            v2 = v_ref[0, rr, pl.ds(start, kw), cols[pair]]
            o = jnp.dot(probs[pair], v2, preferred_element_type=F32) * (1.0 / dens[pair])
            lse = maxs[pair] + jnp.log(dens[pair])
            o_ref[0, rr, pl.ds(q0, tq), cols[pair]] = jnp.where(lo, o[:tq], o[tq:])
            lse_ref[0, rr, pl.ds(q0, tq), cols[pair]] = jnp.where(lo, lse[:tq], lse[tq:])
        return carry

    n_it = n_res * nq
    lax.fori_loop(0, n_it, body, 0, unroll=next(u for u in (4, 2, 1) if n_it % u == 0))


DILATED_TILES_PER_STEP = 8


def _dilated_attention(qkv, bias, window):
    Bn, dilation, L, _ = qkv.shape
    radius = window // (2 * dilation)
    tq, kw = _dilated_tiles(L, radius)
    assert L % tq == 0 and kw % ROWS16 == 0 and tq % ROWS16 == 0
    n_res = max(1, min(dilation, DILATED_TILES_PER_STEP // (L // tq)))
    assert dilation % n_res == 0

    def spec(which):
        return pl.BlockSpec((1, n_res, L, A_OUT), lambda b, r: (b, r, 0, which))

    o, lse = pl.pallas_call(
        functools.partial(_dilated_kernel, L=L, tq=tq, kw=kw),
        grid=(Bn, dilation // n_res),
        in_specs=[spec(0), spec(1), spec(2), pl.BlockSpec(bias.shape, lambda b, r: (0, 0, 0, 0))],
        out_specs=[spec(0), spec(0)],
        out_shape=[jax.ShapeDtypeStruct((Bn, dilation, L, A_OUT), F32)] * 2,
        compiler_params=_params("parallel", "parallel"),
        name="dilated_attention",
    )(qkv, qkv, qkv, bias)
    return o, lse


def _merge_kernel(*refs):
    n = len(A_PATTERNS)
    o_refs, l_refs, out_ref = refs[:n], refs[n:2 * n], refs[2 * n]
    scratch = refs[2 * n + 1:]
    tm = out_ref.shape[0]

    def token_order(ref, scr, dilation):
        if dilation == 1:
            return ref[0, 0]
        tiles = A_OUT // LANES
        for r in range(dilation):
            for c in range(tiles):
                scr[c, pl.ds(r, tm // dilation, stride=dilation), :] = ref[0, r, :, c * LANES:(c + 1) * LANES]
        return jnp.concatenate([scr[c] for c in range(tiles)], axis=1)

    os_ = [token_order(o_refs[g], scratch[2 * g], d) for g, (_, d) in enumerate(A_PATTERNS)]
    ls = [token_order(l_refs[g], scratch[2 * g + 1], d) for g, (_, d) in enumerate(A_PATTERNS)]
    m = jnp.maximum(jnp.maximum(ls[0], ls[1]), ls[2])
    es = [jnp.exp(l - m) for l in ls]
    den = es[0] + es[1] + es[2]
    out = (es[0] / den) * os_[0] + (es[1] / den) * os_[1] + (es[2] / den) * os_[2]
    out_ref[...] = out.astype(out_ref.dtype)


def _merge_groups(os_, ls_, S, tm=512):
    Bn = os_[0].shape[0]
    ns = S // tm

    def spec(d):
        return pl.BlockSpec((1, d, tm // d, A_OUT), lambda i: (i // ns, 0, i % ns, 0))

    specs = [spec(d) for _, d in A_PATTERNS]
    return pl.pallas_call(
        _merge_kernel, grid=(Bn * ns,), in_specs=specs * 2,
        out_specs=pl.BlockSpec((tm, A_OUT), lambda i: (i, 0)),
        out_shape=jax.ShapeDtypeStruct((Bn * S, A_OUT), BF16),
        scratch_shapes=[pltpu.VMEM((A_OUT // LANES, tm, LANES), F32)] * (2 * len(A_PATTERNS)),
        compiler_params=_params("parallel"), name="merge_groups",
    )(*os_, *ls_)


EVEN_TOKEN_COLS = 3 * A_OUT + B_KV_RANK + LANES + B_Q_RANK


def _in_proj_even_kernel(x_ref, w_ref, g_ref, u0_ref, *rest):
    n_dil = len(A_PATTERNS) - 1
    dil_refs, scr = rest[:n_dil], rest[n_dil]
    tm = x_ref.shape[0]
    acc = jnp.dot(_rms(x_ref[...], g_ref[...]).astype(BF16), w_ref[...], preferred_element_type=F32)
    u0_ref[...] = acc[:, :EVEN_TOKEN_COLS].astype(u0_ref.dtype)
    for i, (_, d) in enumerate(A_PATTERNS[1:]):
        c0 = EVEN_TOKEN_COLS + i * 3 * A_OUT
        for c in range(3 * A_OUT // LANES):
            scr[c] = acc[:, c0 + c * LANES:c0 + (c + 1) * LANES]
        for r in range(d):
            for c in range(3 * A_OUT // LANES):
                piece = scr[c, pl.ds(r, tm // d, stride=d), :]
                dil_refs[i][0, r, :, c * LANES:(c + 1) * LANES] = piece.astype(dil_refs[i].dtype)


def _in_proj_even(x2, w, gain, Bn, S, tm=512):
    T, K = x2.shape
    ns = S // tm
    dils = [d for _, d in A_PATTERNS[1:]]
    assert S % tm == 0 and all(tm % (ROWS16 * d) == 0 for d in dils)
    return pl.pallas_call(
        _in_proj_even_kernel,
        grid=(T // tm,),
        in_specs=[pl.BlockSpec((tm, K), lambda i: (i, 0)),
                  pl.BlockSpec(w.shape, lambda i: (0, 0)),
                  pl.BlockSpec((1, K), lambda i: (0, 0))],
        out_specs=[pl.BlockSpec((tm, EVEN_TOKEN_COLS), lambda i: (i, 0))]
        + [pl.BlockSpec((1, d, tm // d, 3 * A_OUT), lambda i: (i // ns, 0, i % ns, 0)) for d in dils],
        out_shape=[jax.ShapeDtypeStruct((T, EVEN_TOKEN_COLS), BF16)]
        + [jax.ShapeDtypeStruct((Bn, d, S // d, 3 * A_OUT), BF16) for d in dils],
        scratch_shapes=[pltpu.VMEM((3 * A_OUT // LANES, tm, LANES), F32)],
        compiler_params=_params("parallel"),
        name="in_proj_even",
    )(x2, w, gain.reshape(1, K).astype(F32))


def _rope_tables(S):
    inv = 1.0 / (ROPE_THETA ** (np.arange(0, B_ROPE, 2, dtype=np.float32) / B_ROPE))
    ang = np.arange(S, dtype=np.float32)[:, None] * inv[None]
    cos, sin = np.cos(ang), np.sin(ang)
    half = B_ROPE // 2
    c = np.ones((S, LANES), np.float32)
    c[:, B_NOPE:B_NOPE + half] = cos
    c[:, B_NOPE + half:B_NOPE + B_ROPE] = cos
    s_lo = np.zeros((S, LANES), np.float32)
    s_lo[:, B_NOPE:B_NOPE + half] = -sin
    s_hi = np.zeros((S, LANES), np.float32)
    s_hi[:, B_NOPE + half:B_NOPE + B_ROPE] = sin
    return jnp.asarray(c), jnp.asarray(s_lo), jnp.asarray(s_hi)


def _rope(x, c, s_lo, s_hi):
    half = B_ROPE // 2
    return x * c + pltpu.roll(x, LANES - half, 1) * s_lo + pltpu.roll(x, half, 1) * s_hi


def _mla_prep_kernel(cq_ref, ckv_ref, kr_ref, gq_ref, gkv_ref, wq_ref, wk_ref, wv_ref,
                     c_ref, slo_ref, shi_ref, q_ref, k_ref, v_ref):
    c, s_lo, s_hi = c_ref[...], slo_ref[...], shi_ref[...]
    scale = (B_NOPE + B_ROPE) ** -0.5 * LOG2_E
    xq = _rms(cq_ref[...], gq_ref[...]).astype(BF16)
    q = jnp.dot(xq, wq_ref[...], preferred_element_type=F32)
    xkv = _rms(ckv_ref[...], gkv_ref[...]).astype(BF16)
    k = jnp.dot(xkv, wk_ref[...], preferred_element_type=F32)
    vt = lax.dot_general(wv_ref[...], xkv, _NT, preferred_element_type=F32)
    ones_row = lax.broadcasted_iota(jnp.int32, vt.shape, 0) % LANES == B_V
    v_ref[0, 0] = jnp.where(ones_row, 1.0, vt).astype(v_ref.dtype)
    k_rope = _rope(kr_ref[...].astype(F32), c, s_lo, s_hi)
    for h in range(B_HEADS):
        cs = slice(h * LANES, (h + 1) * LANES)
        q_ref[:, cs] = (_rope(q[:, cs], c, s_lo, s_hi) * scale).astype(q_ref.dtype)
        k_ref[:, cs] = (k[:, cs] + k_rope).astype(k_ref.dtype)


MLA_KEY_CHUNK = 1024


def _mla_prep(u3, g_qa, g_kva, wq, wk, wv):
    Bn, S, C = u3.shape
    ts = min(MLA_KEY_CHUNK, S)
    tabs = _rope_tables(S)
    ns = S // ts
    u2 = u3.reshape(Bn * S, C)
    row = lambda b, i: b * ns + i
    a_cols = 3 * A_OUT
    off_kv, off_kr, off_q = a_cols // B_KV_RANK, (a_cols + B_KV_RANK) // LANES, (a_cols + B_KV_RANK + LANES) // B_Q_RANK
    full = lambda a: pl.BlockSpec(a.shape, lambda b, i: (0, 0))
    tab_spec = pl.BlockSpec((ts, LANES), lambda b, i: (i, 0))
    hw = B_HEADS * LANES
    return pl.pallas_call(
        _mla_prep_kernel,
        grid=(Bn, ns),
        in_specs=[pl.BlockSpec((ts, B_Q_RANK), lambda b, i: (row(b, i), off_q)),
                  pl.BlockSpec((ts, B_KV_RANK), lambda b, i: (row(b, i), off_kv)),
                  pl.BlockSpec((ts, LANES), lambda b, i: (row(b, i), off_kr)),
                  full(g_qa), full(g_kva), full(wq), full(wk), full(wv),
                  tab_spec, tab_spec, tab_spec],
        out_specs=[pl.BlockSpec((ts, hw), lambda b, i: (row(b, i), 0)),
                   pl.BlockSpec((ts, hw), lambda b, i: (row(b, i), 0)),
                   pl.BlockSpec((1, 1, hw, ts), lambda b, i: (b, i, 0, 0))],
        out_shape=[jax.ShapeDtypeStruct((Bn * S, hw), BF16),
                   jax.ShapeDtypeStruct((Bn * S, hw), BF16),
                   jax.ShapeDtypeStruct((Bn, ns, hw, ts), BF16)],
        compiler_params=_params("parallel", "parallel"),
        name="mla_prep",
    )(u2, u2, u2, g_qa, g_kva, wq, wk, wv, *tabs)


def _mla_kernel(q_ref, k_ref, v_ref, o_ref):
    nk, _, tk = v_ref.shape[1:]
    tq = q_ref.shape[1]
    lane = lax.broadcasted_iota(jnp.int32, (1, LANES), 1)
    qs = [q_ref[0, :, sub * LANES:(sub + 1) * LANES] for sub in range(2)]

    def body(j, carry):
        k0 = pl.multiple_of(j * tk, tk)
        scores = [lax.dot_general(k_ref[0, pl.ds(k0, tk), sub * LANES:(sub + 1) * LANES], qs[sub], _NT,
                                  preferred_element_type=F32) for sub in range(2)]
        new = []
        for sub in range(2):
            m, acc = carry[sub]
            s = scores[sub]
            m_new = jnp.maximum(m, jnp.max(s, axis=0, keepdims=True))
            p = jnp.exp2(s - m_new).astype(BF16)
            vt = v_ref[0, j, sub * LANES:(sub + 1) * LANES, :]
            acc = jnp.exp2(m - m_new) * acc + jnp.dot(vt, p, preferred_element_type=F32)
            new.append((m_new, acc))
        return tuple(new)

    init = tuple((jnp.full((1, tq), NEG_INF, F32), jnp.zeros((LANES, tq), F32)) for _ in range(2))
    res = lax.fori_loop(0, nk, body, init, unroll=True)
    outs = [(acc * (1.0 / acc[B_V:B_V + 1])).T for (_, acc) in res]
    o_ref[0] = jnp.where(lane < B_V, outs[0], pltpu.roll(outs[1], B_V, 1)).astype(o_ref.dtype)


def _mla_attention(q, k, vt, Bn, S, tq=2048):
    tq = min(tq, S)
    nk, _, tk = vt.shape[1:]
    hw = B_HEADS * LANES
    q3, k3 = q.reshape(Bn, S, hw), k.reshape(Bn, S, hw)
    o = pl.pallas_call(
        _mla_kernel,
        grid=(Bn, B_HEADS // 2, S // tq),
        in_specs=[pl.BlockSpec((1, tq, 2 * LANES), lambda b, h, i: (b, i, h)),
                  pl.BlockSpec((1, S, 2 * LANES), lambda b, h, i: (b, 0, h)),
                  pl.BlockSpec((1, nk, 2 * LANES, tk), lambda b, h, i: (b, 0, h, 0))],
        out_specs=pl.BlockSpec((1, tq, LANES), lambda b, h, i: (b, i, h)),
        out_shape=jax.ShapeDtypeStruct((Bn, S, B_OUT), BF16),
        compiler_params=_params("parallel", "parallel", "parallel"),
        name="mla_attention",
    )(q3, k3, vt)
    return o.reshape(Bn * S, B_OUT)


def _natten_bias(rpb):
    W = GRID_W
    c = np.arange(W)[:, None]
    kc = np.arange(W)[None, :]
    wstart = np.clip(c - C_WIN_COLS // 2, 0, W - C_WIN_COLS)
    col_valid = (kc >= wstart) & (kc < wstart + C_WIN_COLS)
    n_dcol = 2 * C_WIN_COLS - 1
    left = (W - 1) - (C_WIN_COLS - 1)
    by_rel = jnp.pad(rpb.astype(F32), ((0, 0), (0, 0), (left, 2 * W - left - n_dcol)))
    skew = jnp.tile(by_rel, (1, 1, W))[:, :, :W * (2 * W - 1)].reshape(C_HEADS, -1, W, 2 * W - 1)
    cols = jnp.where(col_valid[None, None], skew[:, :, :, W - 1:2 * W - 1], NEG_INF)
    b = jnp.stack([cols[:, C_WIN_ROWS - 1 - off:2 * C_WIN_ROWS - 1 - off] for off in range(C_WIN_ROWS)])
    b = jnp.transpose(b, (0, 1, 3, 2, 4))
    return b.reshape(C_WIN_ROWS, C_HEADS // 2, 2 * W, C_WIN_ROWS * W)


NATTEN_ROWS = 4


def _natten_kernel(q_ref, k_ref, v_ref, *rest, rows):
    bias_refs, o_ref = rest[:NATTEN_ROWS], rest[NATTEN_ROWS]
    nk = C_WIN_ROWS * GRID_W
    lane = lax.broadcasted_iota(jnp.int32, (1, LANES), 1)
    lo = lane < HEAD_DIM
    n_pairs = C_HEADS // 2
    cols = [slice(pair * LANES, (pair + 1) * LANES) for pair in range(n_pairs)]
    for j in range(NATTEN_ROWS):
        r = pl.program_id(1) * NATTEN_ROWS + j
        r0 = jnp.clip(r - C_WIN_ROWS // 2, 0, rows - C_WIN_ROWS)
        k0 = pl.multiple_of(r0 * GRID_W, GRID_W)
        q_rows = slice(j * GRID_W, (j + 1) * GRID_W)
        scores = []
        for pair in range(n_pairs):
            q2 = q_ref[0, q_rows, cols[pair]] * (HEAD_DIM ** -0.5)
            zero = jnp.zeros_like(q2)
            qs = jnp.concatenate([jnp.where(lo, q2, zero), jnp.where(lo, zero, q2)], axis=0)
            k2 = k_ref[0, pl.ds(k0, nk), cols[pair]]
            scores.append(lax.dot_general(qs, k2, _NT, preferred_element_type=F32) + bias_refs[j][0, pair])
        probs, dens = [], []
        for s in scores:
            p = jnp.exp(s - jnp.max(s, axis=-1, keepdims=True))
            dens.append(jnp.sum(p, axis=-1, keepdims=True))
            probs.append(p.astype(BF16))
        for pair in range(n_pairs):
            v2 = v_ref[0, pl.ds(k0, nk), cols[pair]]
            o = jnp.dot(probs[pair], v2, preferred_element_type=F32) * (1.0 / dens[pair])
            o_ref[0, q_rows, cols[pair]] = jnp.where(lo, o[:GRID_W], o[GRID_W:]).astype(o_ref.dtype)


def _natten(u3, bias):
    Bn, S, _ = u3.shape
    rows = S // GRID_W
    assert rows >= C_WIN_ROWS and rows % NATTEN_ROWS == 0

    def bias_spec(j):
        def variant(b, i):
            r = i * NATTEN_ROWS + j
            return (r - jnp.clip(r - C_WIN_ROWS // 2, 0, rows - C_WIN_ROWS), 0, 0, 0)
        return pl.BlockSpec((1, C_HEADS // 2, 2 * GRID_W, C_WIN_ROWS * GRID_W), variant)

    q_spec = pl.BlockSpec((1, NATTEN_ROWS * GRID_W, C_OUT), lambda b, i: (b, i, 0))
    o = pl.pallas_call(
        functools.partial(_natten_kernel, rows=rows),
        grid=(Bn, rows // NATTEN_ROWS),
        in_specs=[q_spec,
                  pl.BlockSpec((1, S, C_OUT), lambda b, i: (b, 0, 1), pipeline_mode=pl.Buffered(1)),
                  pl.BlockSpec((1, S, C_OUT), lambda b, i: (b, 0, 2), pipeline_mode=pl.Buffered(1))]
        + [bias_spec(j) for j in range(NATTEN_ROWS)],
        out_specs=q_spec,
        out_shape=jax.ShapeDtypeStruct((Bn, S, C_OUT), BF16),
        compiler_params=_params("parallel", "arbitrary"),
        name="natten",
    )(u3, u3, u3, *([bias] * NATTEN_ROWS))
    return o.reshape(Bn * S, C_OUT)


def _moe(x2, ln_g, router, wg, wu, wd, layer, final_g=None):
    T, D = x2.shape
    cap = CAPACITY_FACTOR * T // N_EXPERTS
    aff_t, h = _router(x2, ln_g, router)
    slot, pos = _select(aff_t, cap)
    tb = min(COMBINE_TB, T)
    starts = jnp.concatenate([pos[:, ::tb], jnp.full((N_EXPERTS, 1), cap, jnp.int32)], axis=1)
    xe = _gather(h, slot, starts, cap)
    ye = _expert_ffn(xe, cap, wg, wu, wd, layer)
    return _combine(x2, slot.T, starts, ye, final_g)


def _trunk(x, p):
    Bn, S, D = x.shape
    T = Bn * S
    x2 = x.reshape(T, D)
    u0, *u_dil = _in_proj_even(x2, p["w_in_even"], p["ln_mix_g"][0], Bn, S)
    u3 = u0.reshape(Bn, S, EVEN_TOKEN_COLS)
    os_, ls_ = [], []
    for g, (window, dilation) in enumerate(A_PATTERNS):
        bias = _dilated_bias(p["t5_table"], g, S // dilation, window, dilation)
        qkv = u3.reshape(Bn, 1, S, EVEN_TOKEN_COLS) if g == 0 else u_dil[g - 1]
        o, l = _dilated_attention(qkv, bias, window)
        os_.append(o)
        ls_.append(l)
    a = _merge_groups(os_, ls_, S)
    q, k, v = _mla_prep(u3, p["g_qa"], p["g_kva"], p["wq"], p["wk"], p["wv"])
    b = _mla_attention(q, k, v, Bn, S)
    x2 = _matmul([a, b], [p["w_out_even"][:A_OUT], p["w_out_even"][A_OUT:]], res=x2, out_dtype=F32, tm=1024)
    x2 = _moe(x2, p["ln_ffn_g"][0], p["router"][0], p["w_gate"], p["w_up"], p["w_down"], 0)
    u = _matmul(x2, p["w_in_odd"], gain=p["ln_mix_g"][1], out_dtype=BF16)
    c = _natten(u.reshape(Bn, S, C_IN), p["natten_bias"])
    x2 = _matmul(c, p["w_out_odd"], res=x2, out_dtype=F32, tm=1024)
    x2 = _moe(x2, p["ln_ffn_g"][1], p["router"][1], p["w_gate"], p["w_up"], p["w_down"], 1,
              final_g=p["final_g"])
    return x2.reshape(Bn, S, D)


def _head_tiles(w, n_heads, per_head, keep):
    K = w.shape[0]
    w = w.reshape(K, n_heads, per_head)[:, :, :keep]
    return jnp.pad(w, ((0, 0), (0, 0), (0, LANES - keep))).reshape(K, n_heads * LANES)


def _prepare(ln_mix_g, w_in_even, g_qa, w_qb, g_kva, w_kvb, t5_table, w_out_even,
             w_in_odd, rpb, w_out_odd, ln_ffn_g, router, w_gate, w_up, w_down, final_g):
    w_in = w_in_even[0]
    w_a, w_cq = w_in[:, :A_IN], w_in[:, A_IN:A_IN + B_Q_RANK]
    w_ckv = w_in[:, A_IN + B_Q_RANK:A_IN + B_Q_RANK + B_KV_RANK]
    w_kr = w_in[:, A_IN + B_Q_RANK + B_KV_RANK:]
    w_kr_tile = jnp.pad(w_kr, ((0, 0), (B_NOPE, LANES - B_NOPE - B_ROPE)))
    w_a = w_a.reshape(-1, 3, A_GROUPS, A_OUT)
    groups = [w_a[:, :, g].reshape(-1, 3 * A_OUT) for g in range(A_GROUPS)]
    w_even = jnp.concatenate([groups[0], w_ckv, w_kr_tile, w_cq] + groups[1:], axis=1)
    w_kv = w_kvb[0].reshape(B_KV_RANK, B_HEADS, B_NOPE + B_V)
    return dict(
        ln_mix_g=ln_mix_g, ln_ffn_g=ln_ffn_g, final_g=final_g, t5_table=t5_table,
        w_in_even=w_even.astype(BF16),
        g_qa=g_qa[0].reshape(1, -1).astype(F32), g_kva=g_kva[0].reshape(1, -1).astype(F32),
        wq=_head_tiles(w_qb[0], B_HEADS, B_NOPE + B_ROPE, B_NOPE + B_ROPE).astype(BF16),
        wk=_head_tiles(w_kvb[0], B_HEADS, B_NOPE + B_V, B_NOPE).astype(BF16),
        wv=_head_tiles(w_kv[:, :, B_NOPE:].reshape(B_KV_RANK, B_OUT), B_HEADS, B_V, B_V).T.astype(BF16),
        w_out_even=w_out_even[0].astype(BF16), w_in_odd=w_in_odd[0].astype(BF16),
        natten_bias=_natten_bias(rpb[0]), w_out_odd=w_out_odd[0].astype(BF16), router=router,
        w_gate=w_gate, w_up=w_up, w_down=w_down,
    )


def kernel(x_prompt, x_sample, ln_mix_g, w_in_even, g_qa, w_qb, g_kva, w_kvb, t5_table, w_out_even,
           w_in_odd, rpb, w_out_odd, ln_ffn_g, router, w_gate, w_up, w_down, final_g):
    p = _prepare(ln_mix_g, w_in_even, g_qa, w_qb, g_kva, w_kvb, t5_table, w_out_even,
                 w_in_odd, rpb, w_out_odd, ln_ffn_g, router, w_gate, w_up, w_down, final_g)
    return _trunk(x_prompt, p), _trunk(x_sample, p)
```

```python
import functools

import jax
import jax.numpy as jnp
import numpy as np
from jax import lax
from jax.experimental import pallas as pl
from jax.experimental.pallas import tpu as pltpu

D_MODEL = 1024
RMS_EPS = 1e-6
HEAD_DIM = 64
NEG_INF = -1e30
A_PATTERNS = ((128, 1), (512, 4), (2048, 16))
A_GROUPS = 3
A_HEADS_PER_GROUP = 4
A_HEADS = A_GROUPS * A_HEADS_PER_GROUP
A_IN = 3 * A_HEADS * HEAD_DIM
A_OUT = A_HEADS_PER_GROUP * HEAD_DIM
T5_BUCKETS = 32
T5_MAX_DISTANCE = 1024
B_HEADS = 12
B_Q_RANK = 384
B_KV_RANK = 256
B_NOPE = 64
B_ROPE = 32
B_V = 64
B_OUT = B_HEADS * B_V
ROPE_THETA = 10000.0
GRID_W = 64
C_HEADS = 16
C_WIN_ROWS = 8
C_WIN_COLS = 16
C_IN = 3 * C_HEADS * HEAD_DIM
C_OUT = C_HEADS * HEAD_DIM
N_EXPERTS = 16
CAPACITY_FACTOR = 2

LANES = 128
ROWS16 = 16
VMEM_LIMIT_BYTES = 48 * 1024 * 1024

F32 = jnp.float32
BF16 = jnp.bfloat16
_NT = (((1,), (1,)), ((), ()))
LOG2_E = 1.4426950408889634


def _params(*sem):
    return pltpu.CompilerParams(dimension_semantics=sem, vmem_limit_bytes=VMEM_LIMIT_BYTES)


def _rms(x, g):
    xf = x.astype(F32)
    return xf * lax.rsqrt(jnp.mean(xf * xf, axis=-1, keepdims=True) + RMS_EPS) * g


def _mm_kernel(*refs, n_in, has_gain, has_res):
    xs, ws = refs[:n_in], refs[n_in:2 * n_in]
    pos = 2 * n_in
    g_ref = res_ref = None
    if has_gain:
        g_ref = refs[pos]
        pos += 1
    if has_res:
        res_ref = refs[pos]
        pos += 1
    o_ref = refs[pos]
    acc = None
    for x_ref, w_ref in zip(xs, ws):
        x = x_ref[...]
        if has_gain:
            x = _rms(x, g_ref[...])
        part = jnp.dot(x.astype(BF16), w_ref[...], preferred_element_type=F32)
        acc = part if acc is None else acc + part
    if has_res:
        acc = acc + res_ref[...]
    o_ref[...] = acc.astype(o_ref.dtype)


def _matmul(xs, ws, gain=None, res=None, out_dtype=F32, tm=512):
    if not isinstance(xs, (list, tuple)):
        xs, ws = [xs], [ws]
    M = xs[0].shape[0]
    N = ws[0].shape[1]
    assert M % tm == 0
    in_specs = [pl.BlockSpec((tm, x.shape[1]), lambda i: (i, 0)) for x in xs]
    in_specs += [pl.BlockSpec(w.shape, lambda i: (0, 0)) for w in ws]
    args = list(xs) + list(ws)
    if gain is not None:
        assert len(xs) == 1
        in_specs.append(pl.BlockSpec((1, gain.shape[-1]), lambda i: (0, 0)))
        args.append(gain.reshape(1, -1).astype(F32))
    if res is not None:
        in_specs.append(pl.BlockSpec((tm, N), lambda i: (i, 0)))
        args.append(res)
    return pl.pallas_call(
        functools.partial(_mm_kernel, n_in=len(xs), has_gain=gain is not None,
                          has_res=res is not None),
        grid=(M // tm,),
        in_specs=in_specs,
        out_specs=pl.BlockSpec((tm, N), lambda i: (i, 0)),
        out_shape=jax.ShapeDtypeStruct((M, N), out_dtype),
        compiler_params=_params("parallel"),
        name="fused_matmul",
    )(*args)


def _ffn_kernel(x_ref, wg32_ref, wu32_ref, wd32_ref, o_ref, wg_ref, wu_ref, wd_ref):
    D = wg_ref.shape[1]

    @pl.when(pl.program_id(1) == 0)
    def _():
        wg_ref[0] = wg32_ref[0, 0].astype(BF16)
        wu_ref[0] = wu32_ref[0, 0].astype(BF16)
        wd_ref[0] = wd32_ref[0, 0].astype(BF16)

    x = x_ref[0, :, :D]
    lane = lax.broadcasted_iota(jnp.int32, (1, LANES), 1)
    mine = lane // GATE_TERMS == pl.program_id(0)
    gate = jnp.sum(jnp.where(mine, x_ref[0, :, D:].astype(F32), 0.0), axis=1, keepdims=True)
    g = jnp.dot(x, wg_ref[0], preferred_element_type=F32)
    u = jnp.dot(x, wu_ref[0], preferred_element_type=F32)
    h = (g * jax.nn.sigmoid(g)) * u
    y = jnp.dot(h.astype(BF16), wd_ref[0], preferred_element_type=F32)
    o_ref[0] = (y * gate).astype(o_ref.dtype)


def _expert_ffn(xe, cap, wg, wu, wd, layer, tm=1024):
    E = xe.shape[0]
    D, FF = wg.shape[2], wg.shape[3]
    tm = min(tm, cap)
    assert cap % tm == 0
    return pl.pallas_call(
        _ffn_kernel,
        grid=(E, cap // tm),
        in_specs=[pl.BlockSpec((1, tm, D + LANES), lambda e, i: (e, i, 0)),
                  pl.BlockSpec((1, 1, D, FF), lambda e, i: (layer, e, 0, 0)),
                  pl.BlockSpec((1, 1, D, FF), lambda e, i: (layer, e, 0, 0)),
                  pl.BlockSpec((1, 1, FF, D), lambda e, i: (layer, e, 0, 0))],
        out_specs=pl.BlockSpec((1, tm, D), lambda e, i: (e, i, 0)),
        out_shape=jax.ShapeDtypeStruct((E, cap, D), BF16),
        scratch_shapes=[pltpu.VMEM((1, D, FF), BF16), pltpu.VMEM((1, D, FF), BF16), pltpu.VMEM((1, FF, D), BF16)],
        compiler_params=_params("parallel", "arbitrary"),
        name="expert_ffn",
    )(xe, wg, wu, wd)


CHUNK = 256
COMBINE_TB = 512
COMBINE_W = 112


GATE_TERMS = 3


def _router_kernel(x_ref, g_ref, rt_ref, aff_ref, h_ref):
    D = x_ref.shape[1]
    E = rt_ref.shape[0]
    xn = _rms(x_ref[...], g_ref[...])
    logits = lax.dot_general(rt_ref[...], xn, _NT, preferred_element_type=F32,
                             precision=lax.Precision.HIGHEST)
    e = jnp.exp(logits - jnp.max(logits, axis=0, keepdims=True))
    aff = e / jnp.sum(e, axis=0, keepdims=True)
    aff_ref[...] = aff
    lane = lax.broadcasted_iota(jnp.int32, (E, LANES), 1)
    expert = lax.broadcasted_iota(jnp.int32, (E, LANES), 0)
    rest = aff
    gates = jnp.zeros((x_ref.shape[0], LANES), F32)
    for j in range(GATE_TERMS):
        term = rest.astype(BF16)
        rest = rest - term.astype(F32)
        place = (lane == GATE_TERMS * expert + j).astype(BF16)
        gates = gates + lax.dot_general(term, place, (((0,), (0,)), ((), ())), preferred_element_type=F32)
    h_ref[:, :D] = xn.astype(h_ref.dtype)
    h_ref[:, D:] = gates.astype(h_ref.dtype)


def _router(x2, ln_g, router, tm=1024):
    T, D = x2.shape
    E = router.shape[1]
    assert GATE_TERMS * E <= LANES
    return pl.pallas_call(
        _router_kernel,
        grid=(T // tm,),
        in_specs=[pl.BlockSpec((tm, D), lambda i: (i, 0)),
                  pl.BlockSpec((1, D), lambda i: (0, 0)),
                  pl.BlockSpec((E, D), lambda i: (0, 0))],
        out_specs=[pl.BlockSpec((E, tm), lambda i: (0, i)),
                   pl.BlockSpec((tm, D + LANES), lambda i: (i, 0))],
        out_shape=[jax.ShapeDtypeStruct((E, T), F32), jax.ShapeDtypeStruct((T, D + LANES), BF16)],
        compiler_params=_params("parallel"),
        name="router",
    )(x2, ln_g.reshape(1, D).astype(F32), router.T.astype(F32))


def _cumsum_excl(mask, upper, lower_strict):
    incl = jnp.dot(mask.astype(BF16), upper, preferred_element_type=F32)
    before = jnp.dot(lower_strict, incl.astype(BF16), preferred_element_type=F32)
    return incl + before[:, CHUNK - 1:CHUNK] - mask


def _select_kernel(aff_ref, slot_ref, pos_ref, *, cap):
    aff = aff_ref[0]
    nc = aff.shape[0]
    bits = pltpu.bitcast(aff, jnp.int32)

    def count(cond):
        c = jnp.sum(jnp.where(cond, 1.0, 0.0), axis=0, keepdims=True)
        return jnp.sum(c, axis=1, keepdims=True)

    def has_cap_above(cand):
        return jnp.where(count(bits >= cand) >= cap, 1, 0).astype(jnp.int32)

    def digit_step(i, prefix):
        shift = 28 - 2 * i
        digit = sum(has_cap_above(prefix | (jnp.int32(d) << shift)) for d in (1, 2, 3))
        return prefix | (digit << shift)

    top = has_cap_above(jnp.full((1, 1), 1 << 30, jnp.int32)) << 30
    thr = lax.fori_loop(0, 15, digit_step, top)
    r = lax.broadcasted_iota(jnp.int32, (CHUNK, CHUNK), 0)
    c = lax.broadcasted_iota(jnp.int32, (CHUNK, CHUNK), 1)
    upper = (r <= c).astype(BF16)
    r = lax.broadcasted_iota(jnp.int32, (nc, nc), 0)
    c = lax.broadcasted_iota(jnp.int32, (nc, nc), 1)
    lower_strict = (c < r).astype(BF16)
    gt = bits > thr
    eq = jnp.where(bits == thr, 1.0, 0.0)
    need = cap - count(gt)
    sel = jnp.where(gt | ((eq > 0) & (_cumsum_excl(eq, upper, lower_strict) < need)), 1.0, 0.0)
    pos = _cumsum_excl(sel, upper, lower_strict).astype(jnp.int32)
    pos_ref[0] = pos
    slot_ref[0] = jnp.where(sel > 0, pos, -1)


def _select(aff_t, cap):
    E, T = aff_t.shape
    nc = T // CHUNK
    assert nc <= CHUNK and nc % 8 == 0
    spec = pl.BlockSpec((1, nc, CHUNK), lambda e: (e, 0, 0))
    slot, pos = pl.pallas_call(
        functools.partial(_select_kernel, cap=cap),
        grid=(E,), in_specs=[spec], out_specs=[spec, spec],
        out_shape=[jax.ShapeDtypeStruct((E, nc, CHUNK), jnp.int32)] * 2,
        compiler_params=_params("parallel"),
        name="select_topk",
    )(aff_t.reshape(E, nc, CHUNK))
    return slot.reshape(E, T), pos.reshape(E, T)


GATHER_STEP = 96
GATHER_W = GATHER_STEP + ROWS16
GATHER_STACK = 8


def _gather_kernel(starts_ref, h_ref, slot_ref, xe_ref, stage_ref, extra_ref, tail_ref, sem, extra_sem, *, cap):
    b = pl.program_id(0)
    nb = pl.num_programs(0)
    E = slot_ref.shape[0]
    buf = b % 2
    h = h_ref[...]
    row = lax.broadcasted_iota(jnp.int32, (GATHER_W, 1), 0)

    @pl.when(b == 0)
    def _():
        tail_ref[...] = jnp.zeros_like(tail_ref)

    def first_row(e, blk):
        return pl.multiple_of((starts_ref[e, blk] // ROWS16) * ROWS16, ROWS16)

    def copy(e, blk):
        return pltpu.make_async_copy(stage_ref.at[blk % 2, e], xe_ref.at[e, pl.ds(first_row(e, blk), GATHER_W)],
                                     sem.at[blk % 2, e])

    def onehot(e, k):
        return (slot_ref[pl.ds(e, 1), :] - (first_row(e, b) + k * GATHER_STEP) == row).astype(BF16)

    for e0 in range(0, E, GATHER_STACK):
        stacked = jnp.concatenate([onehot(e, 0) for e in range(e0, e0 + GATHER_STACK)], axis=0)
        rows = jnp.dot(stacked, h, preferred_element_type=F32).astype(BF16)
        for i in range(GATHER_STACK):
            stage_ref[buf, e0 + i] = rows[i * GATHER_W:(i + 1) * GATHER_W]

    def per_expert(e, carry):
        base = first_row(e, b)
        end = starts_ref[e, b + 1] - base
        group = pl.multiple_of((end // ROWS16) * ROWS16, ROWS16)
        last = jnp.maximum(group - 1, 0) // GATHER_STEP

        def window(k):
            return jnp.dot(onehot(e, k), h, preferred_element_type=F32).astype(BF16)

        stage_ref[buf, e, pl.ds(0, ROWS16), :] += tail_ref[e]

        @pl.when(last == 0)
        def _():
            tail_ref[e] = stage_ref[buf, e, pl.ds(group, ROWS16), :]

        @pl.when(b > 0)
        def _():
            copy(e, b - 1).wait()

        copy(e, b).start()

        def extra(k, inner):
            extra_ref[...] = window(k)

            @pl.when(k == last)
            def _():
                tail_ref[e] = extra_ref[pl.ds(pl.multiple_of(group - k * GATHER_STEP, ROWS16), ROWS16), :]

            cp = pltpu.make_async_copy(extra_ref, xe_ref.at[e, pl.ds(base + k * GATHER_STEP, GATHER_W)],
                                       extra_sem.at[0])
            cp.start()
            cp.wait()
            return inner

        return lax.fori_loop(1, last + 1, extra, carry)

    lax.fori_loop(0, E, per_expert, 0)

    @pl.when(b == nb - 1)
    def _():
        extra_ref[...] = jnp.zeros_like(extra_ref)

        def drain(e, carry):
            copy(e, b).wait()
            cp = pltpu.make_async_copy(extra_ref, xe_ref.at[e, pl.ds(cap, GATHER_W)], extra_sem.at[0])
            cp.start()
            cp.wait()
            return carry
        lax.fori_loop(0, E, drain, 0)


def _gather(h, slot, starts, cap):
    T, D = h.shape
    E = slot.shape[0]
    tb = min(COMBINE_TB, T)
    rows = cap + GATHER_W
    return pl.pallas_call(
        functools.partial(_gather_kernel, cap=cap),
        grid_spec=pltpu.PrefetchScalarGridSpec(
            num_scalar_prefetch=1,
            grid=(T // tb,),
            in_specs=[pl.BlockSpec((tb, D), lambda i, st: (i, 0)),
                      pl.BlockSpec((E, tb), lambda i, st: (0, i))],
            out_specs=pl.BlockSpec(memory_space=pl.ANY),
            scratch_shapes=[pltpu.VMEM((2, E, GATHER_W, D), BF16),
                            pltpu.VMEM((GATHER_W, D), BF16),
                            pltpu.VMEM((E, ROWS16, D), BF16),
                            pltpu.SemaphoreType.DMA((2, E)),
                            pltpu.SemaphoreType.DMA((1,))]),
        out_shape=jax.ShapeDtypeStruct((E, rows, D), BF16),
        compiler_params=_params("arbitrary"),
        name="moe_gather",
    )(starts, h, slot)


def _combine_kernel(starts_ref, x_ref, slot_ref, first_ref, g_ref, ye_ref, o_ref, win_ref, extra_ref, sem,
                    extra_sem, *, cap, final_norm):
    b = pl.program_id(0)
    nb = pl.num_programs(0)
    E = ye_ref.shape[0]
    W = COMBINE_W

    def first_row(e, blk, k):
        return (starts_ref[e, blk] // ROWS16) * ROWS16 + k * W

    def window_start(e, blk, k):
        return pl.multiple_of(jnp.minimum(first_row(e, blk, k), cap - W), ROWS16)

    def copy(e, blk):
        buf = blk % 2
        return pltpu.make_async_copy(ye_ref.at[e, pl.ds(window_start(e, blk, 0), W)],
                                     win_ref.at[buf, pl.ds(e * W, W)], sem.at[buf, e])

    @pl.when(b == 0)
    def _():
        for e in range(E):
            copy(e, b).start()

    @pl.when(b + 1 < nb)
    def _():
        for e in range(E):
            copy(e, b + 1).start()

    slots = slot_ref[...]
    first = first_ref[0]
    rel = slots - jnp.minimum(first, cap - W)
    rel = jnp.where((slots >= first) & (rel >= 0) & (rel < W), rel, -1).astype(F32).astype(BF16)
    lane = lax.broadcasted_iota(jnp.int32, (E, E * W), 1)
    expert = lax.broadcasted_iota(jnp.int32, (E, E * W), 0)
    expand = (lane // W == expert).astype(BF16)
    wanted = (lax.broadcasted_iota(jnp.int32, (1, E * W), 1) % W).astype(F32)
    onehot = (jnp.dot(rel, expand, preferred_element_type=F32) == wanted).astype(BF16)

    for e in range(E):
        copy(e, b).wait()
    o_ref[...] = x_ref[...] + jnp.dot(onehot, win_ref[b % 2], preferred_element_type=F32)

    col = lax.broadcasted_iota(jnp.int32, (1, W), 1)
    expert_lane = lax.broadcasted_iota(jnp.int32, (1, E), 1)

    def per_expert(e, carry):
        n_win = (starts_ref[e, b + 1] - first_row(e, b, 0) + W - 1) // W

        def extra(k, inner):
            cp = pltpu.make_async_copy(ye_ref.at[e, pl.ds(window_start(e, b, k), W)], extra_ref, extra_sem.at[0])
            cp.start()
            cp.wait()
            slot = jnp.max(jnp.where(expert_lane == e, slots, -1), axis=1, keepdims=True)
            oh = ((slot == window_start(e, b, k) + col) & (slot >= first_row(e, b, k))).astype(BF16)
            o_ref[...] += jnp.dot(oh, extra_ref[...], preferred_element_type=F32)
            return inner

        return lax.fori_loop(1, n_win, extra, carry)

    lax.fori_loop(0, E, per_expert, 0)
    if final_norm:
        o_ref[...] = _rms(o_ref[...], g_ref[...])


def _combine(x2, slot_t, starts, ye, final_g=None):
    T, D = x2.shape
    gain = jnp.ones((1, D), F32) if final_g is None else final_g.reshape(1, D).astype(F32)
    E, cap, _ = ye.shape
    tb = min(COMBINE_TB, T)
    nb = T // tb
    assert cap >= COMBINE_W and cap % ROWS16 == 0 and COMBINE_W % ROWS16 == 0
    first = ((starts[:, :nb] // ROWS16) * ROWS16).T.reshape(nb, 1, E)
    return pl.pallas_call(
        functools.partial(_combine_kernel, cap=cap, final_norm=final_g is not None),
        grid_spec=pltpu.PrefetchScalarGridSpec(
            num_scalar_prefetch=1,
            grid=(nb,),
            in_specs=[pl.BlockSpec((tb, D), lambda i, st: (i, 0)),
                      pl.BlockSpec((tb, E), lambda i, st: (i, 0)),
                      pl.BlockSpec((1, 1, E), lambda i, st: (i, 0, 0)),
                      pl.BlockSpec((1, D), lambda i, st: (0, 0)),
                      pl.BlockSpec(memory_space=pl.ANY)],
            out_specs=pl.BlockSpec((tb, D), lambda i, st: (i, 0)),
            scratch_shapes=[pltpu.VMEM((2, E * COMBINE_W, D), BF16),
                            pltpu.VMEM((COMBINE_W, D), BF16),
                            pltpu.SemaphoreType.DMA((2, E)),
                            pltpu.SemaphoreType.DMA((1,))]),
        out_shape=jax.ShapeDtypeStruct((T, D), F32),
        compiler_params=_params("arbitrary"),
        name="moe_combine",
    )(starts, x2, slot_t, first, gain, ye)


def _t5_bucket(rel):
    half = T5_BUCKETS // 2
    max_exact = half // 2
    n = np.abs(rel)
    large = max_exact + (np.log(np.maximum(n, 1) / max_exact) / np.log(T5_MAX_DISTANCE / max_exact)
                         * (half - max_exact)).astype(np.int32)
    large = np.minimum(large, half - 1)
    return (np.where(rel > 0, half, 0) + np.where(n < max_exact, n, large)).astype(np.int32)


def _dilated_tiles(L, radius):
    tq = min(LANES, L)
    kw = min(tq + 2 * radius, L)
    return tq, kw


def _dilated_bias(t5_table, g, L, window, dilation):
    radius = window // (2 * dilation)
    tq, kw = _dilated_tiles(L, radius)
    deltas = (0, -((kw - tq) // 2), tq - kw)
    period = 2 * kw - 1
    rel = np.arange(period + 1) - (kw - 1)
    tab = t5_table[:, g * A_HEADS_PER_GROUP:(g + 1) * A_HEADS_PER_GROUP].astype(F32)
    by_rel = jnp.where((np.abs(rel) <= radius)[None], tab[_t5_bucket(rel * dilation)].T, NEG_INF)
    skew = jnp.tile(by_rel, (1, tq))[:, :tq * period].reshape(A_HEADS_PER_GROUP, tq, period)
    bias = jnp.stack([skew[:, :, d + kw - 1:d + 2 * kw - 1] for d in deltas])
    return bias.reshape(3, A_HEADS_PER_GROUP // 2, 2 * tq, kw)


def _dilated_kernel(q_ref, k_ref, v_ref, bias_ref, o_ref, lse_ref, *, L, tq, kw):
    nq = L // tq
    n_res = q_ref.shape[1]
    lane = lax.broadcasted_iota(jnp.int32, (1, LANES), 1)
    lo = lane < HEAD_DIM

    def body(it, carry):
        rr, i = it // nq, it % nq
        q0 = pl.multiple_of(i * tq, tq)
        start = jnp.clip(q0 - (kw - tq) // 2, 0, L - kw)
        start = pl.multiple_of(start, ROWS16)
        var = jnp.where(i == 0, 0, jnp.where(i == nq - 1, 2, 1))
        n_pairs = A_HEADS_PER_GROUP // 2
        cols = [slice(pair * LANES, (pair + 1) * LANES) for pair in range(n_pairs)]
        scores = []
        for pair in range(n_pairs):
            q2 = q_ref[0, rr, pl.ds(q0, tq), cols[pair]] * (HEAD_DIM ** -0.5)
            zero = jnp.zeros_like(q2)
            qs = jnp.concatenate([jnp.where(lo, q2, zero), jnp.where(lo, zero, q2)], axis=0)
            k2 = k_ref[0, rr, pl.ds(start, kw), cols[pair]]
            scores.append(lax.dot_general(qs, k2, _NT, preferred_element_type=F32) + bias_ref[var, pair])
        probs, dens, maxs = [], [], []
        for s in scores:
            m = jnp.max(s, axis=-1, keepdims=True)
            p = jnp.exp(s - m)
            maxs.append(m)
            dens.append(jnp.sum(p, axis=-1, keepdims=True))
            probs.append(p.astype(BF16))
        for pair in range(n_pairs):
            v2 = v_ref[0, rr, pl.ds(start, kw), cols[pair]]
            o = jnp.dot(probs[pair], v2, preferred_element_type=F32) * (1.0 / dens[pair])
            lse = maxs[pair] + jnp.log(dens[pair])
            o_ref[0, rr, pl.ds(q0, tq), cols[pair]] = jnp.where(lo, o[:tq], o[tq:])
            lse_ref[0, rr, pl.ds(q0, tq), cols[pair]] = jnp.where(lo, lse[:tq], lse[tq:])
        return carry

    n_it = n_res * nq
    lax.fori_loop(0, n_it, body, 0, unroll=next(u for u in (4, 2, 1) if n_it % u == 0))


DILATED_TILES_PER_STEP = 32


def _dilated_attention(qkv, bias, window):
    Bn, dilation, L, _ = qkv.shape
    radius = window // (2 * dilation)
    tq, kw = _dilated_tiles(L, radius)
    assert L % tq == 0 and kw % ROWS16 == 0 and tq % ROWS16 == 0
    n_res = max(1, min(dilation, DILATED_TILES_PER_STEP // (L // tq)))
    assert dilation % n_res == 0

    def spec(which):
        return pl.BlockSpec((1, n_res, L, A_OUT), lambda b, r: (b, r, 0, which))

    o, lse = pl.pallas_call(
        functools.partial(_dilated_kernel, L=L, tq=tq, kw=kw),
        grid=(Bn, dilation // n_res),
        in_specs=[spec(0), spec(1), spec(2), pl.BlockSpec(bias.shape, lambda b, r: (0, 0, 0, 0))],
        out_specs=[spec(0), spec(0)],
        out_shape=[jax.ShapeDtypeStruct((Bn, dilation, L, A_OUT), F32)] * 2,
        compiler_params=_params("parallel", "parallel"),
        name="dilated_attention",
    )(qkv, qkv, qkv, bias)
    return o, lse


def _merge_kernel(*refs):
    n = len(A_PATTERNS)
    o_refs, l_refs, out_ref = refs[:n], refs[n:2 * n], refs[2 * n]
    scratch = refs[2 * n + 1:]
    tm = out_ref.shape[0]

    def token_order(ref, scr, dilation):
        if dilation == 1:
            return ref[0, 0]
        tiles = A_OUT // LANES
        for r in range(dilation):
            for c in range(tiles):
                scr[c, pl.ds(r, tm // dilation, stride=dilation), :] = ref[0, r, :, c * LANES:(c + 1) * LANES]
        return jnp.concatenate([scr[c] for c in range(tiles)], axis=1)

    os_ = [token_order(o_refs[g], scratch[2 * g], d) for g, (_, d) in enumerate(A_PATTERNS)]
    ls = [token_order(l_refs[g], scratch[2 * g + 1], d) for g, (_, d) in enumerate(A_PATTERNS)]
    m = jnp.maximum(jnp.maximum(ls[0], ls[1]), ls[2])
    es = [jnp.exp(l - m) for l in ls]
    den = es[0] + es[1] + es[2]
    out = (es[0] / den) * os_[0] + (es[1] / den) * os_[1] + (es[2] / den) * os_[2]
    out_ref[...] = out.astype(out_ref.dtype)


def _merge_groups(os_, ls_, S, tm=1024):
    Bn = os_[0].shape[0]
    ns = S // tm

    def spec(d):
        return pl.BlockSpec((1, d, tm // d, A_OUT), lambda i: (i // ns, 0, i % ns, 0))

    specs = [spec(d) for _, d in A_PATTERNS]
    return pl.pallas_call(
        _merge_kernel, grid=(Bn * ns,), in_specs=specs * 2,
        out_specs=pl.BlockSpec((tm, A_OUT), lambda i: (i, 0)),
        out_shape=jax.ShapeDtypeStruct((Bn * S, A_OUT), BF16),
        scratch_shapes=[pltpu.VMEM((A_OUT // LANES, tm, LANES), F32)] * (2 * len(A_PATTERNS)),
        compiler_params=_params("parallel"), name="merge_groups",
    )(*os_, *ls_)


EVEN_TOKEN_COLS = 3 * A_OUT + B_KV_RANK + LANES + B_Q_RANK


def _in_proj_even_kernel(x_ref, w_ref, g_ref, u0_ref, *rest):
    n_dil = len(A_PATTERNS) - 1
    dil_refs, scr = rest[:n_dil], rest[n_dil]
    tm = x_ref.shape[0]
    acc = jnp.dot(_rms(x_ref[...], g_ref[...]).astype(BF16), w_ref[...], preferred_element_type=F32)
    u0_ref[...] = acc[:, :EVEN_TOKEN_COLS].astype(u0_ref.dtype)
    for i, (_, d) in enumerate(A_PATTERNS[1:]):
        c0 = EVEN_TOKEN_COLS + i * 3 * A_OUT
        for c in range(3 * A_OUT // LANES):
            scr[c] = acc[:, c0 + c * LANES:c0 + (c + 1) * LANES]
        for r in range(d):
            for c in range(3 * A_OUT // LANES):
                piece = scr[c, pl.ds(r, tm // d, stride=d), :]
                dil_refs[i][0, r, :, c * LANES:(c + 1) * LANES] = piece.astype(dil_refs[i].dtype)


def _in_proj_even(x2, w, gain, Bn, S, tm=1024):
    T, K = x2.shape
    ns = S // tm
    dils = [d for _, d in A_PATTERNS[1:]]
    assert S % tm == 0 and all(tm % (ROWS16 * d) == 0 for d in dils)
    return pl.pallas_call(
        _in_proj_even_kernel,
        grid=(T // tm,),
        in_specs=[pl.BlockSpec((tm, K), lambda i: (i, 0)),
                  pl.BlockSpec(w.shape, lambda i: (0, 0)),
                  pl.BlockSpec((1, K), lambda i: (0, 0))],
        out_specs=[pl.BlockSpec((tm, EVEN_TOKEN_COLS), lambda i: (i, 0))]
        + [pl.BlockSpec((1, d, tm // d, 3 * A_OUT), lambda i: (i // ns, 0, i % ns, 0)) for d in dils],
        out_shape=[jax.ShapeDtypeStruct((T, EVEN_TOKEN_COLS), BF16)]
        + [jax.ShapeDtypeStruct((Bn, d, S // d, 3 * A_OUT), BF16) for d in dils],
        scratch_shapes=[pltpu.VMEM((3 * A_OUT // LANES, tm, LANES), F32)],
        compiler_params=_params("parallel"),
        name="in_proj_even",
    )(x2, w, gain.reshape(1, K).astype(F32))


def _rope_tables(S):
    inv = 1.0 / (ROPE_THETA ** (np.arange(0, B_ROPE, 2, dtype=np.float32) / B_ROPE))
    ang = np.arange(S, dtype=np.float32)[:, None] * inv[None]
    cos, sin = np.cos(ang), np.sin(ang)
    half = B_ROPE // 2
    c = np.ones((S, LANES), np.float32)
    c[:, B_NOPE:B_NOPE + half] = cos
    c[:, B_NOPE + half:B_NOPE + B_ROPE] = cos
    s_lo = np.zeros((S, LANES), np.float32)
    s_lo[:, B_NOPE:B_NOPE + half] = -sin
    s_hi = np.zeros((S, LANES), np.float32)
    s_hi[:, B_NOPE + half:B_NOPE + B_ROPE] = sin
    return jnp.asarray(c), jnp.asarray(s_lo), jnp.asarray(s_hi)


def _rope(x, c, s_lo, s_hi):
    half = B_ROPE // 2
    return x * c + pltpu.roll(x, LANES - half, 1) * s_lo + pltpu.roll(x, half, 1) * s_hi


def _mla_prep_kernel(cq_ref, ckv_ref, kr_ref, gq_ref, gkv_ref, wq_ref, wk_ref, wv_ref,
                     c_ref, slo_ref, shi_ref, q_ref, k_ref, v_ref):
    c, s_lo, s_hi = c_ref[...], slo_ref[...], shi_ref[...]
    scale = (B_NOPE + B_ROPE) ** -0.5 * LOG2_E
    xq = _rms(cq_ref[...], gq_ref[...]).astype(BF16)
    q = jnp.dot(xq, wq_ref[...], preferred_element_type=F32)
    xkv = _rms(ckv_ref[...], gkv_ref[...]).astype(BF16)
    k = jnp.dot(xkv, wk_ref[...], preferred_element_type=F32)
    vt = lax.dot_general(wv_ref[...], xkv, _NT, preferred_element_type=F32)
    ones_row = lax.broadcasted_iota(jnp.int32, vt.shape, 0) % LANES == B_V
    v_ref[0, 0] = jnp.where(ones_row, 1.0, vt).astype(v_ref.dtype)
    k_rope = _rope(kr_ref[...].astype(F32), c, s_lo, s_hi)
    for h in range(B_HEADS):
        cs = slice(h * LANES, (h + 1) * LANES)
        q_ref[:, cs] = (_rope(q[:, cs], c, s_lo, s_hi) * scale).astype(q_ref.dtype)
        k_ref[:, cs] = (k[:, cs] + k_rope).astype(k_ref.dtype)


MLA_KEY_CHUNK = 1024


def _mla_prep(u3, g_qa, g_kva, wq, wk, wv):
    Bn, S, C = u3.shape
    ts = min(MLA_KEY_CHUNK, S)
    tabs = _rope_tables(S)
    ns = S // ts
    u2 = u3.reshape(Bn * S, C)
    row = lambda b, i: b * ns + i
    a_cols = 3 * A_OUT
    off_kv, off_kr, off_q = a_cols // B_KV_RANK, (a_cols + B_KV_RANK) // LANES, (a_cols + B_KV_RANK + LANES) // B_Q_RANK
    full = lambda a: pl.BlockSpec(a.shape, lambda b, i: (0, 0))
    tab_spec = pl.BlockSpec((ts, LANES), lambda b, i: (i, 0))
    hw = B_HEADS * LANES
    return pl.pallas_call(
        _mla_prep_kernel,
        grid=(Bn, ns),
        in_specs=[pl.BlockSpec((ts, B_Q_RANK), lambda b, i: (row(b, i), off_q)),
                  pl.BlockSpec((ts, B_KV_RANK), lambda b, i: (row(b, i), off_kv)),
                  pl.BlockSpec((ts, LANES), lambda b, i: (row(b, i), off_kr)),
                  full(g_qa), full(g_kva), full(wq), full(wk), full(wv),
                  tab_spec, tab_spec, tab_spec],
        out_specs=[pl.BlockSpec((ts, hw), lambda b, i: (row(b, i), 0)),
                   pl.BlockSpec((ts, hw), lambda b, i: (row(b, i), 0)),
                   pl.BlockSpec((1, 1, hw, ts), lambda b, i: (b, i, 0, 0))],
        out_shape=[jax.ShapeDtypeStruct((Bn * S, hw), BF16),
                   jax.ShapeDtypeStruct((Bn * S, hw), BF16),
                   jax.ShapeDtypeStruct((Bn, ns, hw, ts), BF16)],
        compiler_params=_params("parallel", "parallel"),
        name="mla_prep",
    )(u2, u2, u2, g_qa, g_kva, wq, wk, wv, *tabs)


def _mla_kernel(q_ref, k_ref, v_ref, o_ref):
    nk, _, tk = v_ref.shape[1:]
    tq = q_ref.shape[1]
    lane = lax.broadcasted_iota(jnp.int32, (1, LANES), 1)
    qs = [q_ref[0, :, sub * LANES:(sub + 1) * LANES] for sub in range(2)]

    def body(j, carry):
        k0 = pl.multiple_of(j * tk, tk)
        scores = [lax.dot_general(k_ref[0, pl.ds(k0, tk), sub * LANES:(sub + 1) * LANES], qs[sub], _NT,
                                  preferred_element_type=F32) for sub in range(2)]
        new = []
        for sub in range(2):
            m, acc = carry[sub]
            s = scores[sub]
            m_new = jnp.maximum(m, jnp.max(s, axis=0, keepdims=True))
            p = jnp.exp2(s - m_new).astype(BF16)
            vt = v_ref[0, j, sub * LANES:(sub + 1) * LANES, :]
            acc = jnp.exp2(m - m_new) * acc + jnp.dot(vt, p, preferred_element_type=F32)
            new.append((m_new, acc))
        return tuple(new)

    init = tuple((jnp.full((1, tq), NEG_INF, F32), jnp.zeros((LANES, tq), F32)) for _ in range(2))
    res = lax.fori_loop(0, nk, body, init, unroll=True)
    outs = [(acc * (1.0 / acc[B_V:B_V + 1])).T for (_, acc) in res]
    o_ref[0] = jnp.where(lane < B_V, outs[0], pltpu.roll(outs[1], B_V, 1)).astype(o_ref.dtype)


def _mla_attention(q, k, vt, Bn, S, tq=2048):
    tq = min(tq, S)
    nk, _, tk = vt.shape[1:]
    hw = B_HEADS * LANES
    q3, k3 = q.reshape(Bn, S, hw), k.reshape(Bn, S, hw)
    o = pl.pallas_call(
        _mla_kernel,
        grid=(Bn, B_HEADS // 2, S // tq),
        in_specs=[pl.BlockSpec((1, tq, 2 * LANES), lambda b, h, i: (b, i, h)),
                  pl.BlockSpec((1, S, 2 * LANES), lambda b, h, i: (b, 0, h)),
                  pl.BlockSpec((1, nk, 2 * LANES, tk), lambda b, h, i: (b, 0, h, 0))],
        out_specs=pl.BlockSpec((1, tq, LANES), lambda b, h, i: (b, i, h)),
        out_shape=jax.ShapeDtypeStruct((Bn, S, B_OUT), BF16),
        compiler_params=_params("parallel", "parallel", "parallel"),
        name="mla_attention",
    )(q3, k3, vt)
    return o.reshape(Bn * S, B_OUT)


def _natten_bias(rpb):
    W = GRID_W
    c = np.arange(W)[:, None]
    kc = np.arange(W)[None, :]
    wstart = np.clip(c - C_WIN_COLS // 2, 0, W - C_WIN_COLS)
    col_valid = (kc >= wstart) & (kc < wstart + C_WIN_COLS)
    n_dcol = 2 * C_WIN_COLS - 1
    left = (W - 1) - (C_WIN_COLS - 1)
    by_rel = jnp.pad(rpb.astype(F32), ((0, 0), (0, 0), (left, 2 * W - left - n_dcol)))
    skew = jnp.tile(by_rel, (1, 1, W))[:, :, :W * (2 * W - 1)].reshape(C_HEADS, -1, W, 2 * W - 1)
    cols = jnp.where(col_valid[None, None], skew[:, :, :, W - 1:2 * W - 1], NEG_INF)
    b = jnp.stack([cols[:, C_WIN_ROWS - 1 - off:2 * C_WIN_ROWS - 1 - off] for off in range(C_WIN_ROWS)])
    b = jnp.transpose(b, (0, 1, 3, 2, 4))
    return b.reshape(C_WIN_ROWS, C_HEADS // 2, 2 * W, C_WIN_ROWS * W)


NATTEN_ROWS = 4


def _natten_kernel(q_ref, k_ref, v_ref, *rest, rows):
    bias_refs, o_ref = rest[:NATTEN_ROWS], rest[NATTEN_ROWS]
    nk = C_WIN_ROWS * GRID_W
    lane = lax.broadcasted_iota(jnp.int32, (1, LANES), 1)
    lo = lane < HEAD_DIM
    n_pairs = C_HEADS // 2
    cols = [slice(pair * LANES, (pair + 1) * LANES) for pair in range(n_pairs)]
    for j in range(NATTEN_ROWS):
        r = pl.program_id(1) * NATTEN_ROWS + j
        r0 = jnp.clip(r - C_WIN_ROWS // 2, 0, rows - C_WIN_ROWS)
        k0 = pl.multiple_of(r0 * GRID_W, GRID_W)
        q_rows = slice(j * GRID_W, (j + 1) * GRID_W)
        scores = []
        for pair in range(n_pairs):
            q2 = q_ref[0, q_rows, cols[pair]] * (HEAD_DIM ** -0.5)
            zero = jnp.zeros_like(q2)
            qs = jnp.concatenate([jnp.where(lo, q2, zero), jnp.where(lo, zero, q2)], axis=0)
            k2 = k_ref[0, pl.ds(k0, nk), cols[pair]]
            scores.append(lax.dot_general(qs, k2, _NT, preferred_element_type=F32) + bias_refs[j][0, pair])
        probs, dens = [], []
        for s in scores:
            p = jnp.exp(s - jnp.max(s, axis=-1, keepdims=True))
            dens.append(jnp.sum(p, axis=-1, keepdims=True))
            probs.append(p.astype(BF16))
        for pair in range(n_pairs):
            v2 = v_ref[0, pl.ds(k0, nk), cols[pair]]
            o = jnp.dot(probs[pair], v2, preferred_element_type=F32) * (1.0 / dens[pair])
            o_ref[0, q_rows, cols[pair]] = jnp.where(lo, o[:GRID_W], o[GRID_W:]).astype(o_ref.dtype)


def _natten(u3, bias):
    Bn, S, _ = u3.shape
    rows = S // GRID_W
    assert rows >= C_WIN_ROWS and rows % NATTEN_ROWS == 0

    def bias_spec(j):
        def variant(b, i):
            r = i * NATTEN_ROWS + j
            return (r - jnp.clip(r - C_WIN_ROWS // 2, 0, rows - C_WIN_ROWS), 0, 0, 0)
        return pl.BlockSpec((1, C_HEADS // 2, 2 * GRID_W, C_WIN_ROWS * GRID_W), variant)

    q_spec = pl.BlockSpec((1, NATTEN_ROWS * GRID_W, C_OUT), lambda b, i: (b, i, 0))
    o = pl.pallas_call(
        functools.partial(_natten_kernel, rows=rows),
        grid=(Bn, rows // NATTEN_ROWS),
        in_specs=[q_spec,
                  pl.BlockSpec((1, S, C_OUT), lambda b, i: (b, 0, 1), pipeline_mode=pl.Buffered(1)),
                  pl.BlockSpec((1, S, C_OUT), lambda b, i: (b, 0, 2), pipeline_mode=pl.Buffered(1))]
        + [bias_spec(j) for j in range(NATTEN_ROWS)],
        out_specs=q_spec,
        out_shape=jax.ShapeDtypeStruct((Bn, S, C_OUT), BF16),
        compiler_params=_params("parallel", "arbitrary"),
        name="natten",
    )(u3, u3, u3, *([bias] * NATTEN_ROWS))
    return o.reshape(Bn * S, C_OUT)


def _moe(x2, ln_g, router, wg, wu, wd, layer, final_g=None):
    T, D = x2.shape
    cap = CAPACITY_FACTOR * T // N_EXPERTS
    aff_t, h = _router(x2, ln_g, router)
    slot, pos = _select(aff_t, cap)
    tb = min(COMBINE_TB, T)
    starts = jnp.concatenate([pos[:, ::tb], jnp.full((N_EXPERTS, 1), cap, jnp.int32)], axis=1)
    xe = _gather(h, slot, starts, cap)
    ye = _expert_ffn(xe, cap, wg, wu, wd, layer)
    return _combine(x2, slot.T, starts, ye, final_g)


def _trunk(x, p):
    Bn, S, D = x.shape
    T = Bn * S
    x2 = x.reshape(T, D)
    u0, *u_dil = _in_proj_even(x2, p["w_in_even"], p["ln_mix_g"][0], Bn, S)
    u3 = u0.reshape(Bn, S, EVEN_TOKEN_COLS)
    os_, ls_ = [], []
    for g, (window, dilation) in enumerate(A_PATTERNS):
        bias = _dilated_bias(p["t5_table"], g, S // dilation, window, dilation)
        qkv = u3.reshape(Bn, 1, S, EVEN_TOKEN_COLS) if g == 0 else u_dil[g - 1]
        o, l = _dilated_attention(qkv, bias, window)
        os_.append(o)
        ls_.append(l)
    a = _merge_groups(os_, ls_, S)
    q, k, v = _mla_prep(u3, p["g_qa"], p["g_kva"], p["wq"], p["wk"], p["wv"])
    b = _mla_attention(q, k, v, Bn, S)
    x2 = _matmul([a, b], [p["w_out_even"][:A_OUT], p["w_out_even"][A_OUT:]], res=x2, out_dtype=F32, tm=1024)
    x2 = _moe(x2, p["ln_ffn_g"][0], p["router"][0], p["w_gate"], p["w_up"], p["w_down"], 0)
    u = _matmul(x2, p["w_in_odd"], gain=p["ln_mix_g"][1], out_dtype=BF16, tm=1024)
    c = _natten(u.reshape(Bn, S, C_IN), p["natten_bias"])
    x2 = _matmul(c, p["w_out_odd"], res=x2, out_dtype=F32, tm=1024)
    x2 = _moe(x2, p["ln_ffn_g"][1], p["router"][1], p["w_gate"], p["w_up"], p["w_down"], 1,
              final_g=p["final_g"])
    return x2.reshape(Bn, S, D)


def _head_tiles(w, n_heads, per_head, keep):
    K = w.shape[0]
    w = w.reshape(K, n_heads, per_head)[:, :, :keep]
    return jnp.pad(w, ((0, 0), (0, 0), (0, LANES - keep))).reshape(K, n_heads * LANES)


def _prepare(ln_mix_g, w_in_even, g_qa, w_qb, g_kva, w_kvb, t5_table, w_out_even,
             w_in_odd, rpb, w_out_odd, ln_ffn_g, router, w_gate, w_up, w_down, final_g):
    w_in = w_in_even[0]
    w_a, w_cq = w_in[:, :A_IN], w_in[:, A_IN:A_IN + B_Q_RANK]
    w_ckv = w_in[:, A_IN + B_Q_RANK:A_IN + B_Q_RANK + B_KV_RANK]
    w_kr = w_in[:, A_IN + B_Q_RANK + B_KV_RANK:]
    w_kr_tile = jnp.pad(w_kr, ((0, 0), (B_NOPE, LANES - B_NOPE - B_ROPE)))
    w_a = w_a.reshape(-1, 3, A_GROUPS, A_OUT)
    groups = [w_a[:, :, g].reshape(-1, 3 * A_OUT) for g in range(A_GROUPS)]
    w_even = jnp.concatenate([groups[0], w_ckv, w_kr_tile, w_cq] + groups[1:], axis=1)
    w_kv = w_kvb[0].reshape(B_KV_RANK, B_HEADS, B_NOPE + B_V)
    return dict(
        ln_mix_g=ln_mix_g, ln_ffn_g=ln_ffn_g, final_g=final_g, t5_table=t5_table,
        w_in_even=w_even.astype(BF16),
        g_qa=g_qa[0].reshape(1, -1).astype(F32), g_kva=g_kva[0].reshape(1, -1).astype(F32),
        wq=_head_tiles(w_qb[0], B_HEADS, B_NOPE + B_ROPE, B_NOPE + B_ROPE).astype(BF16),
        wk=_head_tiles(w_kvb[0], B_HEADS, B_NOPE + B_V, B_NOPE).astype(BF16),
        wv=_head_tiles(w_kv[:, :, B_NOPE:].reshape(B_KV_RANK, B_OUT), B_HEADS, B_V, B_V).T.astype(BF16),
        w_out_even=w_out_even[0].astype(BF16), w_in_odd=w_in_odd[0].astype(BF16),
        natten_bias=_natten_bias(rpb[0]), w_out_odd=w_out_odd[0].astype(BF16), router=router,
        w_gate=w_gate, w_up=w_up, w_down=w_down,
    )


def kernel(x_prompt, x_sample, ln_mix_g, w_in_even, g_qa, w_qb, g_kva, w_kvb, t5_table, w_out_even,
           w_in_odd, rpb, w_out_odd, ln_ffn_g, router, w_gate, w_up, w_down, final_g):
    p = _prepare(ln_mix_g, w_in_even, g_qa, w_qb, g_kva, w_kvb, t5_table, w_out_even,
                 w_in_odd, rpb, w_out_odd, ln_ffn_g, router, w_gate, w_up, w_down, final_g)
    return _trunk(x_prompt, p), _trunk(x_sample, p)
```

```python
import functools

import jax
import jax.numpy as jnp
import numpy as np
from jax import lax
from jax.experimental import pallas as pl
from jax.experimental.pallas import tpu as pltpu

D_MODEL = 1024
RMS_EPS = 1e-6
HEAD_DIM = 64
NEG_INF = -1e30
A_PATTERNS = ((128, 1), (512, 4), (2048, 16))
A_GROUPS = 3
A_HEADS_PER_GROUP = 4
A_HEADS = A_GROUPS * A_HEADS_PER_GROUP
A_IN = 3 * A_HEADS * HEAD_DIM
A_OUT = A_HEADS_PER_GROUP * HEAD_DIM
T5_BUCKETS = 32
T5_MAX_DISTANCE = 1024
B_HEADS = 12
B_Q_RANK = 384
B_KV_RANK = 256
B_NOPE = 64
B_ROPE = 32
B_V = 64
B_OUT = B_HEADS * B_V
ROPE_THETA = 10000.0
GRID_W = 64
C_HEADS = 16
C_WIN_ROWS = 8
C_WIN_COLS = 16
C_IN = 3 * C_HEADS * HEAD_DIM
C_OUT = C_HEADS * HEAD_DIM
N_EXPERTS = 16
CAPACITY_FACTOR = 2

LANES = 128
ROWS16 = 16
VMEM_LIMIT_BYTES = 48 * 1024 * 1024

F32 = jnp.float32
BF16 = jnp.bfloat16
_NT = (((1,), (1,)), ((), ()))
LOG2_E = 1.4426950408889634


def _params(*sem):
    return pltpu.CompilerParams(dimension_semantics=sem, vmem_limit_bytes=VMEM_LIMIT_BYTES)


def _rms(x, g):
    xf = x.astype(F32)
    return xf * lax.rsqrt(jnp.mean(xf * xf, axis=-1, keepdims=True) + RMS_EPS) * g


def _mm_kernel(*refs, n_in, has_gain, has_res):
    xs, ws = refs[:n_in], refs[n_in:2 * n_in]
    pos = 2 * n_in
    g_ref = res_ref = None
    if has_gain:
        g_ref = refs[pos]
        pos += 1
    if has_res:
        res_ref = refs[pos]
        pos += 1
    o_ref = refs[pos]
    acc = None
    for x_ref, w_ref in zip(xs, ws):
        x = x_ref[...]
        if has_gain:
            x = _rms(x, g_ref[...])
        part = jnp.dot(x.astype(BF16), w_ref[...], preferred_element_type=F32)
        acc = part if acc is None else acc + part
    if has_res:
        acc = acc + res_ref[...]
    o_ref[...] = acc.astype(o_ref.dtype)


def _matmul(xs, ws, gain=None, res=None, out_dtype=F32, tm=512):
    if not isinstance(xs, (list, tuple)):
        xs, ws = [xs], [ws]
    M = xs[0].shape[0]
    N = ws[0].shape[1]
    assert M % tm == 0
    in_specs = [pl.BlockSpec((tm, x.shape[1]), lambda i: (i, 0)) for x in xs]
    in_specs += [pl.BlockSpec(w.shape, lambda i: (0, 0)) for w in ws]
    args = list(xs) + list(ws)
    if gain is not None:
        assert len(xs) == 1
        in_specs.append(pl.BlockSpec((1, gain.shape[-1]), lambda i: (0, 0)))
        args.append(gain.reshape(1, -1).astype(F32))
    if res is not None:
        in_specs.append(pl.BlockSpec((tm, N), lambda i: (i, 0)))
        args.append(res)
    return pl.pallas_call(
        functools.partial(_mm_kernel, n_in=len(xs), has_gain=gain is not None,
                          has_res=res is not None),
        grid=(M // tm,),
        in_specs=in_specs,
        out_specs=pl.BlockSpec((tm, N), lambda i: (i, 0)),
        out_shape=jax.ShapeDtypeStruct((M, N), out_dtype),
        compiler_params=_params("parallel"),
        name="fused_matmul",
    )(*args)


def _ffn_kernel(x_ref, wg32_ref, wu32_ref, wd32_ref, o_ref, wg_ref, wu_ref, wd_ref):
    D = wg_ref.shape[1]

    @pl.when(pl.program_id(1) == 0)
    def _():
        wg_ref[0] = wg32_ref[0, 0].astype(BF16)
        wu_ref[0] = wu32_ref[0, 0].astype(BF16)
        wd_ref[0] = wd32_ref[0, 0].astype(BF16)

    x = x_ref[0, :, :D]
    lane = lax.broadcasted_iota(jnp.int32, (1, LANES), 1)
    mine = lane // GATE_TERMS == pl.program_id(0)
    gate = jnp.sum(jnp.where(mine, x_ref[0, :, D:].astype(F32), 0.0), axis=1, keepdims=True)
    g = jnp.dot(x, wg_ref[0], preferred_element_type=F32)
    u = jnp.dot(x, wu_ref[0], preferred_element_type=F32)
    h = (g * jax.nn.sigmoid(g)) * u
    y = jnp.dot(h.astype(BF16), wd_ref[0], preferred_element_type=F32)
    o_ref[0] = (y * gate).astype(o_ref.dtype)


def _expert_ffn(xe, cap, wg, wu, wd, layer, tm=1024):
    E = xe.shape[0]
    D, FF = wg.shape[2], wg.shape[3]
    tm = min(tm, cap)
    assert cap % tm == 0
    return pl.pallas_call(
        _ffn_kernel,
        grid=(E, cap // tm),
        in_specs=[pl.BlockSpec((1, tm, D + LANES), lambda e, i: (e, i, 0)),
                  pl.BlockSpec((1, 1, D, FF), lambda e, i: (layer, e, 0, 0)),
                  pl.BlockSpec((1, 1, D, FF), lambda e, i: (layer, e, 0, 0)),
                  pl.BlockSpec((1, 1, FF, D), lambda e, i: (layer, e, 0, 0))],
        out_specs=pl.BlockSpec((1, tm, D), lambda e, i: (e, i, 0)),
        out_shape=jax.ShapeDtypeStruct((E, cap, D), BF16),
        scratch_shapes=[pltpu.VMEM((1, D, FF), BF16), pltpu.VMEM((1, D, FF), BF16), pltpu.VMEM((1, FF, D), BF16)],
        compiler_params=_params("parallel", "arbitrary"),
        name="expert_ffn",
    )(xe, wg, wu, wd)


CHUNK = 256
COMBINE_TB = 512
COMBINE_W = 112


GATE_TERMS = 3


def _router_kernel(x_ref, g_ref, rt_ref, aff_ref, h_ref):
    D = x_ref.shape[1]
    E = rt_ref.shape[0]
    xn = _rms(x_ref[...], g_ref[...])
    logits = lax.dot_general(rt_ref[...], xn, _NT, preferred_element_type=F32,
                             precision=lax.Precision.HIGHEST)
    e = jnp.exp(logits - jnp.max(logits, axis=0, keepdims=True))
    aff = e / jnp.sum(e, axis=0, keepdims=True)
    aff_ref[...] = aff
    lane = lax.broadcasted_iota(jnp.int32, (E, LANES), 1)
    expert = lax.broadcasted_iota(jnp.int32, (E, LANES), 0)
    rest = aff
    gates = jnp.zeros((x_ref.shape[0], LANES), F32)
    for j in range(GATE_TERMS):
        term = rest.astype(BF16)
        rest = rest - term.astype(F32)
        place = (lane == GATE_TERMS * expert + j).astype(BF16)
        gates = gates + lax.dot_general(term, place, (((0,), (0,)), ((), ())), preferred_element_type=F32)
    h_ref[:, :D] = xn.astype(h_ref.dtype)
    h_ref[:, D:] = gates.astype(h_ref.dtype)


def _router(x2, ln_g, router, tm=2048):
    T, D = x2.shape
    E = router.shape[1]
    assert GATE_TERMS * E <= LANES
    return pl.pallas_call(
        _router_kernel,
        grid=(T // tm,),
        in_specs=[pl.BlockSpec((tm, D), lambda i: (i, 0)),
                  pl.BlockSpec((1, D), lambda i: (0, 0)),
                  pl.BlockSpec((E, D), lambda i: (0, 0))],
        out_specs=[pl.BlockSpec((E, tm), lambda i: (0, i)),
                   pl.BlockSpec((tm, D + LANES), lambda i: (i, 0))],
        out_shape=[jax.ShapeDtypeStruct((E, T), F32), jax.ShapeDtypeStruct((T, D + LANES), BF16)],
        compiler_params=_params("parallel"),
        name="router",
    )(x2, ln_g.reshape(1, D).astype(F32), router.T.astype(F32))


def _cumsum_excl(mask, upper, lower_strict):
    incl = jnp.dot(mask.astype(BF16), upper, preferred_element_type=F32)
    before = jnp.dot(lower_strict, incl.astype(BF16), preferred_element_type=F32)
    return incl + before[:, CHUNK - 1:CHUNK] - mask


def _select_kernel(aff_ref, slot_ref, pos_ref, *, cap):
    aff = aff_ref[0]
    nc = aff.shape[0]
    bits = pltpu.bitcast(aff, jnp.int32)

    def count(cond):
        c = jnp.sum(jnp.where(cond, 1.0, 0.0), axis=0, keepdims=True)
        return jnp.sum(c, axis=1, keepdims=True)

    def has_cap_above(cand):
        return jnp.where(count(bits >= cand) >= cap, 1, 0).astype(jnp.int32)

    def digit_step(i, prefix):
        shift = 28 - 2 * i
        digit = sum(has_cap_above(prefix | (jnp.int32(d) << shift)) for d in (1, 2, 3))
        return prefix | (digit << shift)

    top = has_cap_above(jnp.full((1, 1), 1 << 30, jnp.int32)) << 30
    thr = lax.fori_loop(0, 15, digit_step, top)
    r = lax.broadcasted_iota(jnp.int32, (CHUNK, CHUNK), 0)
    c = lax.broadcasted_iota(jnp.int32, (CHUNK, CHUNK), 1)
    upper = (r <= c).astype(BF16)
    r = lax.broadcasted_iota(jnp.int32, (nc, nc), 0)
    c = lax.broadcasted_iota(jnp.int32, (nc, nc), 1)
    lower_strict = (c < r).astype(BF16)
    gt = bits > thr
    eq = jnp.where(bits == thr, 1.0, 0.0)
    need = cap - count(gt)
    sel = jnp.where(gt | ((eq > 0) & (_cumsum_excl(eq, upper, lower_strict) < need)), 1.0, 0.0)
    pos = _cumsum_excl(sel, upper, lower_strict).astype(jnp.int32)
    pos_ref[0] = pos
    slot_ref[0] = jnp.where(sel > 0, pos, -1)


def _select(aff_t, cap):
    E, T = aff_t.shape
    nc = T // CHUNK
    assert nc <= CHUNK and nc % 8 == 0
    spec = pl.BlockSpec((1, nc, CHUNK), lambda e: (e, 0, 0))
    slot, pos = pl.pallas_call(
        functools.partial(_select_kernel, cap=cap),
        grid=(E,), in_specs=[spec], out_specs=[spec, spec],
        out_shape=[jax.ShapeDtypeStruct((E, nc, CHUNK), jnp.int32)] * 2,
        compiler_params=_params("parallel"),
        name="select_topk",
    )(aff_t.reshape(E, nc, CHUNK))
    return slot.reshape(E, T), pos.reshape(E, T)


GATHER_STEP = 96
GATHER_W = GATHER_STEP + ROWS16
GATHER_STACK = 8


def _gather_kernel(starts_ref, h_ref, slot_ref, xe_ref, stage_ref, extra_ref, tail_ref, sem, extra_sem, *, cap):
    b = pl.program_id(0)
    nb = pl.num_programs(0)
    E = slot_ref.shape[0]
    buf = b % 2
    h = h_ref[...]
    row = lax.broadcasted_iota(jnp.int32, (GATHER_W, 1), 0)

    @pl.when(b == 0)
    def _():
        tail_ref[...] = jnp.zeros_like(tail_ref)

    def first_row(e, blk):
        return pl.multiple_of((starts_ref[e, blk] // ROWS16) * ROWS16, ROWS16)

    def copy(e, blk):
        return pltpu.make_async_copy(stage_ref.at[blk % 2, e], xe_ref.at[e, pl.ds(first_row(e, blk), GATHER_W)],
                                     sem.at[blk % 2, e])

    def onehot(e, k):
        return (slot_ref[pl.ds(e, 1), :] - (first_row(e, b) + k * GATHER_STEP) == row).astype(BF16)

    for e0 in range(0, E, GATHER_STACK):
        stacked = jnp.concatenate([onehot(e, 0) for e in range(e0, e0 + GATHER_STACK)], axis=0)
        rows = jnp.dot(stacked, h, preferred_element_type=F32).astype(BF16)
        for i in range(GATHER_STACK):
            stage_ref[buf, e0 + i] = rows[i * GATHER_W:(i + 1) * GATHER_W]

    def per_expert(e, carry):
        base = first_row(e, b)
        end = starts_ref[e, b + 1] - base
        group = pl.multiple_of((end // ROWS16) * ROWS16, ROWS16)
        last = jnp.maximum(group - 1, 0) // GATHER_STEP

        def window(k):
            return jnp.dot(onehot(e, k), h, preferred_element_type=F32).astype(BF16)

        stage_ref[buf, e, pl.ds(0, ROWS16), :] += tail_ref[e]

        @pl.when(last == 0)
        def _():
            tail_ref[e] = stage_ref[buf, e, pl.ds(group, ROWS16), :]

        @pl.when(b > 0)
        def _():
            copy(e, b - 1).wait()

        copy(e, b).start()

        def extra(k, inner):
            extra_ref[...] = window(k)

            @pl.when(k == last)
            def _():
                tail_ref[e] = extra_ref[pl.ds(pl.multiple_of(group - k * GATHER_STEP, ROWS16), ROWS16), :]

            cp = pltpu.make_async_copy(extra_ref, xe_ref.at[e, pl.ds(base + k * GATHER_STEP, GATHER_W)],
                                       extra_sem.at[0])
            cp.start()
            cp.wait()
            return inner

        return lax.fori_loop(1, last + 1, extra, carry)

    lax.fori_loop(0, E, per_expert, 0)

    @pl.when(b == nb - 1)
    def _():
        extra_ref[...] = jnp.zeros_like(extra_ref)

        def drain(e, carry):
            copy(e, b).wait()
            cp = pltpu.make_async_copy(extra_ref, xe_ref.at[e, pl.ds(cap, GATHER_W)], extra_sem.at[0])
            cp.start()
            cp.wait()
            return carry
        lax.fori_loop(0, E, drain, 0)


def _gather(h, slot, starts, cap):
    T, D = h.shape
    E = slot.shape[0]
    tb = min(COMBINE_TB, T)
    rows = cap + GATHER_W
    return pl.pallas_call(
        functools.partial(_gather_kernel, cap=cap),
        grid_spec=pltpu.PrefetchScalarGridSpec(
            num_scalar_prefetch=1,
            grid=(T // tb,),
            in_specs=[pl.BlockSpec((tb, D), lambda i, st: (i, 0)),
                      pl.BlockSpec((E, tb), lambda i, st: (0, i))],
            out_specs=pl.BlockSpec(memory_space=pl.ANY),
            scratch_shapes=[pltpu.VMEM((2, E, GATHER_W, D), BF16),
                            pltpu.VMEM((GATHER_W, D), BF16),
                            pltpu.VMEM((E, ROWS16, D), BF16),
                            pltpu.SemaphoreType.DMA((2, E)),
                            pltpu.SemaphoreType.DMA((1,))]),
        out_shape=jax.ShapeDtypeStruct((E, rows, D), BF16),
        compiler_params=_params("arbitrary"),
        name="moe_gather",
    )(starts, h, slot)


def _combine_kernel(starts_ref, x_ref, slot_ref, first_ref, g_ref, ye_ref, o_ref, win_ref, extra_ref, sem,
                    extra_sem, *, cap, final_norm):
    b = pl.program_id(0)
    nb = pl.num_programs(0)
    E = ye_ref.shape[0]
    W = COMBINE_W

    def first_row(e, blk, k):
        return (starts_ref[e, blk] // ROWS16) * ROWS16 + k * W

    def window_start(e, blk, k):
        return pl.multiple_of(jnp.minimum(first_row(e, blk, k), cap - W), ROWS16)

    def copy(e, blk):
        buf = blk % 2
        return pltpu.make_async_copy(ye_ref.at[e, pl.ds(window_start(e, blk, 0), W)],
                                     win_ref.at[buf, pl.ds(e * W, W)], sem.at[buf, e])

    @pl.when(b == 0)
    def _():
        for e in range(E):
            copy(e, b).start()

    @pl.when(b + 1 < nb)
    def _():
        for e in range(E):
            copy(e, b + 1).start()

    slots = slot_ref[...]
    first = first_ref[0]
    rel = slots - jnp.minimum(first, cap - W)
    rel = jnp.where((slots >= first) & (rel >= 0) & (rel < W), rel, -1).astype(F32).astype(BF16)
    lane = lax.broadcasted_iota(jnp.int32, (E, E * W), 1)
    expert = lax.broadcasted_iota(jnp.int32, (E, E * W), 0)
    expand = (lane // W == expert).astype(BF16)
    wanted = (lax.broadcasted_iota(jnp.int32, (1, E * W), 1) % W).astype(F32)
    onehot = (jnp.dot(rel, expand, preferred_element_type=F32) == wanted).astype(BF16)

    for e in range(E):
        copy(e, b).wait()
    o_ref[...] = x_ref[...] + jnp.dot(onehot, win_ref[b % 2], preferred_element_type=F32)

    col = lax.broadcasted_iota(jnp.int32, (1, W), 1)
    expert_lane = lax.broadcasted_iota(jnp.int32, (1, E), 1)

    def per_expert(e, carry):
        n_win = (starts_ref[e, b + 1] - first_row(e, b, 0) + W - 1) // W

        def extra(k, inner):
            cp = pltpu.make_async_copy(ye_ref.at[e, pl.ds(window_start(e, b, k), W)], extra_ref, extra_sem.at[0])
            cp.start()
            cp.wait()
            slot = jnp.max(jnp.where(expert_lane == e, slots, -1), axis=1, keepdims=True)
            oh = ((slot == window_start(e, b, k) + col) & (slot >= first_row(e, b, k))).astype(BF16)
            o_ref[...] += jnp.dot(oh, extra_ref[...], preferred_element_type=F32)
            return inner

        return lax.fori_loop(1, n_win, extra, carry)

    lax.fori_loop(0, E, per_expert, 0)
    if final_norm:
        o_ref[...] = _rms(o_ref[...], g_ref[...])


def _combine(x2, slot_t, starts, ye, final_g=None):
    T, D = x2.shape
    gain = jnp.ones((1, D), F32) if final_g is None else final_g.reshape(1, D).astype(F32)
    E, cap, _ = ye.shape
    tb = min(COMBINE_TB, T)
    nb = T // tb
    assert cap >= COMBINE_W and cap % ROWS16 == 0 and COMBINE_W % ROWS16 == 0
    first = ((starts[:, :nb] // ROWS16) * ROWS16).T.reshape(nb, 1, E)
    return pl.pallas_call(
        functools.partial(_combine_kernel, cap=cap, final_norm=final_g is not None),
        grid_spec=pltpu.PrefetchScalarGridSpec(
            num_scalar_prefetch=1,
            grid=(nb,),
            in_specs=[pl.BlockSpec((tb, D), lambda i, st: (i, 0)),
                      pl.BlockSpec((tb, E), lambda i, st: (i, 0)),
                      pl.BlockSpec((1, 1, E), lambda i, st: (i, 0, 0)),
                      pl.BlockSpec((1, D), lambda i, st: (0, 0)),
                      pl.BlockSpec(memory_space=pl.ANY)],
            out_specs=pl.BlockSpec((tb, D), lambda i, st: (i, 0)),
            scratch_shapes=[pltpu.VMEM((2, E * COMBINE_W, D), BF16),
                            pltpu.VMEM((COMBINE_W, D), BF16),
                            pltpu.SemaphoreType.DMA((2, E)),
                            pltpu.SemaphoreType.DMA((1,))]),
        out_shape=jax.ShapeDtypeStruct((T, D), F32),
        compiler_params=_params("arbitrary"),
        name="moe_combine",
    )(starts, x2, slot_t, first, gain, ye)


def _t5_bucket(rel):
    half = T5_BUCKETS // 2
    max_exact = half // 2
    n = np.abs(rel)
    large = max_exact + (np.log(np.maximum(n, 1) / max_exact) / np.log(T5_MAX_DISTANCE / max_exact)
                         * (half - max_exact)).astype(np.int32)
    large = np.minimum(large, half - 1)
    return (np.where(rel > 0, half, 0) + np.where(n < max_exact, n, large)).astype(np.int32)


def _dilated_tiles(L, radius):
    tq = min(LANES, L)
    kw = min(tq + 2 * radius, L)
    return tq, kw


def _dilated_bias(t5_table, g, L, window, dilation):
    radius = window // (2 * dilation)
    tq, kw = _dilated_tiles(L, radius)
    deltas = (0, -((kw - tq) // 2), tq - kw)
    period = 2 * kw - 1
    rel = np.arange(period + 1) - (kw - 1)
    tab = t5_table[:, g * A_HEADS_PER_GROUP:(g + 1) * A_HEADS_PER_GROUP].astype(F32)
    by_rel = jnp.where((np.abs(rel) <= radius)[None], tab[_t5_bucket(rel * dilation)].T * LOG2_E, NEG_INF)
    skew = jnp.tile(by_rel, (1, tq))[:, :tq * period].reshape(A_HEADS_PER_GROUP, tq, period)
    bias = jnp.stack([skew[:, :, d + kw - 1:d + 2 * kw - 1] for d in deltas])
    return bias.reshape(3, A_HEADS_PER_GROUP // 2, 2 * tq, kw)


def _dilated_kernel(q_ref, k_ref, v_ref, bias_ref, o_ref, lse_ref, *, L, tq, kw):
    nq = L // tq
    n_res = q_ref.shape[1]
    lane = lax.broadcasted_iota(jnp.int32, (1, LANES), 1)
    lo = lane < HEAD_DIM

    def body(it, carry):
        rr, i = it // nq, it % nq
        q0 = pl.multiple_of(i * tq, tq)
        start = jnp.clip(q0 - (kw - tq) // 2, 0, L - kw)
        start = pl.multiple_of(start, ROWS16)
        var = jnp.where(i == 0, 0, jnp.where(i == nq - 1, 2, 1))
        n_pairs = A_HEADS_PER_GROUP // 2
        cols = [slice(pair * LANES, (pair + 1) * LANES) for pair in range(n_pairs)]
        scores = []
        for pair in range(n_pairs):
            q2 = q_ref[0, rr, pl.ds(q0, tq), cols[pair]]
            zero = jnp.zeros_like(q2)
            qs = jnp.concatenate([jnp.where(lo, q2, zero), jnp.where(lo, zero, q2)], axis=0)
            k2 = k_ref[0, rr, pl.ds(start, kw), cols[pair]]
            scores.append(lax.dot_general(qs, k2, _NT, preferred_element_type=F32) + bias_ref[var, pair])
        probs, dens, maxs = [], [], []
        for s in scores:
            m = jnp.max(s, axis=-1, keepdims=True)
            p = jnp.exp2(s - m)
            maxs.append(m)
            dens.append(jnp.sum(p, axis=-1, keepdims=True))
            probs.append(p.astype(BF16))
        for pair in range(n_pairs):
            v2 = v_ref[0, rr, pl.ds(start, kw), cols[pair]]
            o = jnp.dot(probs[pair], v2, preferred_element_type=F32) * (1.0 / dens[pair])
            lse = maxs[pair] * (1.0 / LOG2_E) + jnp.log(dens[pair])
            o_ref[0, rr, pl.ds(q0, tq), cols[pair]] = jnp.where(lo, o[:tq], o[tq:])
            lse_ref[0, rr, pl.ds(q0, tq), cols[pair]] = jnp.where(lo, lse[:tq], lse[tq:])
        return carry

    n_it = n_res * nq
    lax.fori_loop(0, n_it, body, 0, unroll=next(u for u in (4, 2, 1) if n_it % u == 0))


DILATED_TILES_PER_STEP = 32


def _dilated_attention(qkv, bias, window):
    Bn, dilation, L, _ = qkv.shape
    radius = window // (2 * dilation)
    tq, kw = _dilated_tiles(L, radius)
    assert L % tq == 0 and kw % ROWS16 == 0 and tq % ROWS16 == 0
    n_res = max(1, min(dilation, DILATED_TILES_PER_STEP // (L // tq)))
    assert dilation % n_res == 0

    def spec(which):
        return pl.BlockSpec((1, n_res, L, A_OUT), lambda b, r: (b, r, 0, which))

    o, lse = pl.pallas_call(
        functools.partial(_dilated_kernel, L=L, tq=tq, kw=kw),
        grid=(Bn, dilation // n_res),
        in_specs=[spec(0), spec(1), spec(2), pl.BlockSpec(bias.shape, lambda b, r: (0, 0, 0, 0))],
        out_specs=[spec(0), spec(0)],
        out_shape=[jax.ShapeDtypeStruct((Bn, dilation, L, A_OUT), F32)] * 2,
        compiler_params=_params("parallel", "parallel"),
        name="dilated_attention",
    )(qkv, qkv, qkv, bias)
    return o, lse


def _merge_kernel(*refs):
    n = len(A_PATTERNS)
    o_refs, l_refs, out_ref = refs[:n], refs[n:2 * n], refs[2 * n]
    scratch = refs[2 * n + 1:]
    tm = out_ref.shape[0]

    def token_order(ref, scr, dilation):
        if dilation == 1:
            return ref[0, 0]
        tiles = A_OUT // LANES
        for r in range(dilation):
            for c in range(tiles):
                scr[c, pl.ds(r, tm // dilation, stride=dilation), :] = ref[0, r, :, c * LANES:(c + 1) * LANES]
        return jnp.concatenate([scr[c] for c in range(tiles)], axis=1)

    os_ = [token_order(o_refs[g], scratch[2 * g], d) for g, (_, d) in enumerate(A_PATTERNS)]
    ls = [token_order(l_refs[g], scratch[2 * g + 1], d) for g, (_, d) in enumerate(A_PATTERNS)]
    m = jnp.maximum(jnp.maximum(ls[0], ls[1]), ls[2])
    es = [jnp.exp(l - m) for l in ls]
    den = es[0] + es[1] + es[2]
    out = (es[0] / den) * os_[0] + (es[1] / den) * os_[1] + (es[2] / den) * os_[2]
    out_ref[...] = out.astype(out_ref.dtype)


def _merge_groups(os_, ls_, S, tm=1024):
    Bn = os_[0].shape[0]
    ns = S // tm

    def spec(d):
        return pl.BlockSpec((1, d, tm // d, A_OUT), lambda i: (i // ns, 0, i % ns, 0))

    specs = [spec(d) for _, d in A_PATTERNS]
    return pl.pallas_call(
        _merge_kernel, grid=(Bn * ns,), in_specs=specs * 2,
        out_specs=pl.BlockSpec((tm, A_OUT), lambda i: (i, 0)),
        out_shape=jax.ShapeDtypeStruct((Bn * S, A_OUT), BF16),
        scratch_shapes=[pltpu.VMEM((A_OUT // LANES, tm, LANES), F32)] * (2 * len(A_PATTERNS)),
        compiler_params=_params("parallel"), name="merge_groups",
    )(*os_, *ls_)


EVEN_TOKEN_COLS = 3 * A_OUT + B_KV_RANK + LANES + B_Q_RANK


def _in_proj_even_kernel(x_ref, w_ref, g_ref, u0_ref, *rest):
    n_dil = len(A_PATTERNS) - 1
    dil_refs, scr = rest[:n_dil], rest[n_dil]
    tm = x_ref.shape[0]
    acc = jnp.dot(_rms(x_ref[...], g_ref[...]).astype(BF16), w_ref[...], preferred_element_type=F32)
    u0_ref[...] = acc[:, :EVEN_TOKEN_COLS].astype(u0_ref.dtype)
    for i, (_, d) in enumerate(A_PATTERNS[1:]):
        c0 = EVEN_TOKEN_COLS + i * 3 * A_OUT
        for c in range(3 * A_OUT // LANES):
            scr[c] = acc[:, c0 + c * LANES:c0 + (c + 1) * LANES]
        for r in range(d):
            for c in range(3 * A_OUT // LANES):
                piece = scr[c, pl.ds(r, tm // d, stride=d), :]
                dil_refs[i][0, r, :, c * LANES:(c + 1) * LANES] = piece.astype(dil_refs[i].dtype)


def _in_proj_even(x2, w, gain, Bn, S, tm=1024):
    T, K = x2.shape
    ns = S // tm
    dils = [d for _, d in A_PATTERNS[1:]]
    assert S % tm == 0 and all(tm % (ROWS16 * d) == 0 for d in dils)
    return pl.pallas_call(
        _in_proj_even_kernel,
        grid=(T // tm,),
        in_specs=[pl.BlockSpec((tm, K), lambda i: (i, 0)),
                  pl.BlockSpec(w.shape, lambda i: (0, 0)),
                  pl.BlockSpec((1, K), lambda i: (0, 0))],
        out_specs=[pl.BlockSpec((tm, EVEN_TOKEN_COLS), lambda i: (i, 0))]
        + [pl.BlockSpec((1, d, tm // d, 3 * A_OUT), lambda i: (i // ns, 0, i % ns, 0)) for d in dils],
        out_shape=[jax.ShapeDtypeStruct((T, EVEN_TOKEN_COLS), BF16)]
        + [jax.ShapeDtypeStruct((Bn, d, S // d, 3 * A_OUT), BF16) for d in dils],
        scratch_shapes=[pltpu.VMEM((3 * A_OUT // LANES, tm, LANES), F32)],
        compiler_params=_params("parallel"),
        name="in_proj_even",
    )(x2, w, gain.reshape(1, K).astype(F32))


def _rope_tables(S):
    inv = 1.0 / (ROPE_THETA ** (np.arange(0, B_ROPE, 2, dtype=np.float32) / B_ROPE))
    ang = np.arange(S, dtype=np.float32)[:, None] * inv[None]
    cos, sin = np.cos(ang), np.sin(ang)
    half = B_ROPE // 2
    c = np.ones((S, LANES), np.float32)
    c[:, B_NOPE:B_NOPE + half] = cos
    c[:, B_NOPE + half:B_NOPE + B_ROPE] = cos
    s_lo = np.zeros((S, LANES), np.float32)
    s_lo[:, B_NOPE:B_NOPE + half] = -sin
    s_hi = np.zeros((S, LANES), np.float32)
    s_hi[:, B_NOPE + half:B_NOPE + B_ROPE] = sin
    return jnp.asarray(c), jnp.asarray(s_lo), jnp.asarray(s_hi)


def _rope(x, c, s_lo, s_hi):
    half = B_ROPE // 2
    return x * c + pltpu.roll(x, LANES - half, 1) * s_lo + pltpu.roll(x, half, 1) * s_hi


def _mla_prep_kernel(cq_ref, ckv_ref, kr_ref, gq_ref, gkv_ref, wq_ref, wk_ref, wv_ref,
                     c_ref, slo_ref, shi_ref, q_ref, k_ref, v_ref):
    c, s_lo, s_hi = c_ref[...], slo_ref[...], shi_ref[...]
    scale = (B_NOPE + B_ROPE) ** -0.5 * LOG2_E
    xq = _rms(cq_ref[...], gq_ref[...]).astype(BF16)
    q = jnp.dot(xq, wq_ref[...], preferred_element_type=F32)
    xkv = _rms(ckv_ref[...], gkv_ref[...]).astype(BF16)
    k = jnp.dot(xkv, wk_ref[...], preferred_element_type=F32)
    vt = lax.dot_general(wv_ref[...], xkv, _NT, preferred_element_type=F32)
    ones_row = lax.broadcasted_iota(jnp.int32, vt.shape, 0) % LANES == B_V
    v_ref[0, 0] = jnp.where(ones_row, 1.0, vt).astype(v_ref.dtype)
    k_rope = _rope(kr_ref[...].astype(F32), c, s_lo, s_hi)
    for h in range(B_HEADS):
        cs = slice(h * LANES, (h + 1) * LANES)
        q_ref[:, cs] = (_rope(q[:, cs], c, s_lo, s_hi) * scale).astype(q_ref.dtype)
        k_ref[:, cs] = (k[:, cs] + k_rope).astype(k_ref.dtype)


MLA_KEY_CHUNK = 1024


def _mla_prep(u3, g_qa, g_kva, wq, wk, wv):
    Bn, S, C = u3.shape
    ts = min(MLA_KEY_CHUNK, S)
    tabs = _rope_tables(S)
    ns = S // ts
    u2 = u3.reshape(Bn * S, C)
    row = lambda b, i: b * ns + i
    a_cols = 3 * A_OUT
    off_kv, off_kr, off_q = a_cols // B_KV_RANK, (a_cols + B_KV_RANK) // LANES, (a_cols + B_KV_RANK + LANES) // B_Q_RANK
    full = lambda a: pl.BlockSpec(a.shape, lambda b, i: (0, 0))
    tab_spec = pl.BlockSpec((ts, LANES), lambda b, i: (i, 0))
    hw = B_HEADS * LANES
    return pl.pallas_call(
        _mla_prep_kernel,
        grid=(Bn, ns),
        in_specs=[pl.BlockSpec((ts, B_Q_RANK), lambda b, i: (row(b, i), off_q)),
                  pl.BlockSpec((ts, B_KV_RANK), lambda b, i: (row(b, i), off_kv)),
                  pl.BlockSpec((ts, LANES), lambda b, i: (row(b, i), off_kr)),
                  full(g_qa), full(g_kva), full(wq), full(wk), full(wv),
                  tab_spec, tab_spec, tab_spec],
        out_specs=[pl.BlockSpec((ts, hw), lambda b, i: (row(b, i), 0)),
                   pl.BlockSpec((ts, hw), lambda b, i: (row(b, i), 0)),
                   pl.BlockSpec((1, 1, hw, ts), lambda b, i: (b, i, 0, 0))],
        out_shape=[jax.ShapeDtypeStruct((Bn * S, hw), BF16),
                   jax.ShapeDtypeStruct((Bn * S, hw), BF16),
                   jax.ShapeDtypeStruct((Bn, ns, hw, ts), BF16)],
        compiler_params=_params("parallel", "parallel"),
        name="mla_prep",
    )(u2, u2, u2, g_qa, g_kva, wq, wk, wv, *tabs)


def _mla_kernel(q_ref, k_ref, v_ref, o_ref):
    nk, _, tk = v_ref.shape[1:]
    tq = q_ref.shape[1]
    lane = lax.broadcasted_iota(jnp.int32, (1, LANES), 1)
    qs = [q_ref[0, :, sub * LANES:(sub + 1) * LANES] for sub in range(2)]

    def body(j, carry):
        k0 = pl.multiple_of(j * tk, tk)
        scores = [lax.dot_general(k_ref[0, pl.ds(k0, tk), sub * LANES:(sub + 1) * LANES], qs[sub], _NT,
                                  preferred_element_type=F32) for sub in range(2)]
        new = []
        for sub in range(2):
            m, acc = carry[sub]
            s = scores[sub]
            m_new = jnp.maximum(m, jnp.max(s, axis=0, keepdims=True))
            p = jnp.exp2(s - m_new).astype(BF16)
            vt = v_ref[0, j, sub * LANES:(sub + 1) * LANES, :]
            acc = jnp.exp2(m - m_new) * acc + jnp.dot(vt, p, preferred_element_type=F32)
            new.append((m_new, acc))
        return tuple(new)

    init = tuple((jnp.full((1, tq), NEG_INF, F32), jnp.zeros((LANES, tq), F32)) for _ in range(2))
    res = lax.fori_loop(0, nk, body, init, unroll=True)
    outs = [(acc * (1.0 / acc[B_V:B_V + 1])).T for (_, acc) in res]
    o_ref[0] = jnp.where(lane < B_V, outs[0], pltpu.roll(outs[1], B_V, 1)).astype(o_ref.dtype)


def _mla_attention(q, k, vt, Bn, S, tq=2048):
    tq = min(tq, S)
    nk, _, tk = vt.shape[1:]
    hw = B_HEADS * LANES
    q3, k3 = q.reshape(Bn, S, hw), k.reshape(Bn, S, hw)
    o = pl.pallas_call(
        _mla_kernel,
        grid=(Bn, B_HEADS // 2, S // tq),
        in_specs=[pl.BlockSpec((1, tq, 2 * LANES), lambda b, h, i: (b, i, h)),
                  pl.BlockSpec((1, S, 2 * LANES), lambda b, h, i: (b, 0, h)),
                  pl.BlockSpec((1, nk, 2 * LANES, tk), lambda b, h, i: (b, 0, h, 0))],
        out_specs=pl.BlockSpec((1, tq, LANES), lambda b, h, i: (b, i, h)),
        out_shape=jax.ShapeDtypeStruct((Bn, S, B_OUT), BF16),
        compiler_params=_params("parallel", "parallel", "parallel"),
        name="mla_attention",
    )(q3, k3, vt)
    return o.reshape(Bn * S, B_OUT)


def _natten_bias(rpb):
    W = GRID_W
    c = np.arange(W)[:, None]
    kc = np.arange(W)[None, :]
    wstart = np.clip(c - C_WIN_COLS // 2, 0, W - C_WIN_COLS)
    col_valid = (kc >= wstart) & (kc < wstart + C_WIN_COLS)
    n_dcol = 2 * C_WIN_COLS - 1
    left = (W - 1) - (C_WIN_COLS - 1)
    by_rel = jnp.pad(rpb.astype(F32) * LOG2_E, ((0, 0), (0, 0), (left, 2 * W - left - n_dcol)))
    skew = jnp.tile(by_rel, (1, 1, W))[:, :, :W * (2 * W - 1)].reshape(C_HEADS, -1, W, 2 * W - 1)
    cols = jnp.where(col_valid[None, None], skew[:, :, :, W - 1:2 * W - 1], NEG_INF)
    b = jnp.stack([cols[:, C_WIN_ROWS - 1 - off:2 * C_WIN_ROWS - 1 - off] for off in range(C_WIN_ROWS)])
    b = jnp.transpose(b, (0, 1, 3, 2, 4))
    return b.reshape(C_WIN_ROWS, C_HEADS // 2, 2 * W, C_WIN_ROWS * W)


NATTEN_ROWS = 4


def _natten_kernel(q_ref, k_ref, v_ref, *rest, rows):
    bias_refs, o_ref = rest[:NATTEN_ROWS], rest[NATTEN_ROWS]
    nk = C_WIN_ROWS * GRID_W
    lane = lax.broadcasted_iota(jnp.int32, (1, LANES), 1)
    lo = lane < HEAD_DIM
    n_pairs = C_HEADS // 2
    cols = [slice(pair * LANES, (pair + 1) * LANES) for pair in range(n_pairs)]
    for j in range(NATTEN_ROWS):
        r = pl.program_id(1) * NATTEN_ROWS + j
        r0 = jnp.clip(r - C_WIN_ROWS // 2, 0, rows - C_WIN_ROWS)
        k0 = pl.multiple_of(r0 * GRID_W, GRID_W)
        q_rows = slice(j * GRID_W, (j + 1) * GRID_W)
        scores = []
        for pair in range(n_pairs):
            q2 = q_ref[0, q_rows, cols[pair]]
            zero = jnp.zeros_like(q2)
            qs = jnp.concatenate([jnp.where(lo, q2, zero), jnp.where(lo, zero, q2)], axis=0)
            k2 = k_ref[0, pl.ds(k0, nk), cols[pair]]
            scores.append(lax.dot_general(qs, k2, _NT, preferred_element_type=F32) + bias_refs[j][0, pair])
        probs, dens = [], []
        for s in scores:
            p = jnp.exp2(s - jnp.max(s, axis=-1, keepdims=True))
            dens.append(jnp.sum(p, axis=-1, keepdims=True))
            probs.append(p.astype(BF16))
        for pair in range(n_pairs):
            v2 = v_ref[0, pl.ds(k0, nk), cols[pair]]
            o = jnp.dot(probs[pair], v2, preferred_element_type=F32) * (1.0 / dens[pair])
            o_ref[0, q_rows, cols[pair]] = jnp.where(lo, o[:GRID_W], o[GRID_W:]).astype(o_ref.dtype)


def _natten(u3, bias):
    Bn, S, _ = u3.shape
    rows = S // GRID_W
    assert rows >= C_WIN_ROWS and rows % NATTEN_ROWS == 0

    def bias_spec(j):
        def variant(b, i):
            r = i * NATTEN_ROWS + j
            return (r - jnp.clip(r - C_WIN_ROWS // 2, 0, rows - C_WIN_ROWS), 0, 0, 0)
        return pl.BlockSpec((1, C_HEADS // 2, 2 * GRID_W, C_WIN_ROWS * GRID_W), variant)

    q_spec = pl.BlockSpec((1, NATTEN_ROWS * GRID_W, C_OUT), lambda b, i: (b, i, 0))
    o = pl.pallas_call(
        functools.partial(_natten_kernel, rows=rows),
        grid=(Bn, rows // NATTEN_ROWS),
        in_specs=[q_spec,
                  pl.BlockSpec((1, S, C_OUT), lambda b, i: (b, 0, 1), pipeline_mode=pl.Buffered(1)),
                  pl.BlockSpec((1, S, C_OUT), lambda b, i: (b, 0, 2), pipeline_mode=pl.Buffered(1))]
        + [bias_spec(j) for j in range(NATTEN_ROWS)],
        out_specs=q_spec,
        out_shape=jax.ShapeDtypeStruct((Bn, S, C_OUT), BF16),
        compiler_params=_params("parallel", "arbitrary"),
        name="natten",
    )(u3, u3, u3, *([bias] * NATTEN_ROWS))
    return o.reshape(Bn * S, C_OUT)


def _moe(x2, ln_g, router, wg, wu, wd, layer, final_g=None):
    T, D = x2.shape
    cap = CAPACITY_FACTOR * T // N_EXPERTS
    aff_t, h = _router(x2, ln_g, router)
    slot, pos = _select(aff_t, cap)
    tb = min(COMBINE_TB, T)
    starts = jnp.concatenate([pos[:, ::tb], jnp.full((N_EXPERTS, 1), cap, jnp.int32)], axis=1)
    xe = _gather(h, slot, starts, cap)
    ye = _expert_ffn(xe, cap, wg, wu, wd, layer)
    return _combine(x2, slot.T, starts, ye, final_g)


def _trunk(x, p):
    Bn, S, D = x.shape
    T = Bn * S
    x2 = x.reshape(T, D)
    u0, *u_dil = _in_proj_even(x2, p["w_in_even"], p["ln_mix_g"][0], Bn, S)
    u3 = u0.reshape(Bn, S, EVEN_TOKEN_COLS)
    os_, ls_ = [], []
    for g, (window, dilation) in enumerate(A_PATTERNS):
        bias = _dilated_bias(p["t5_table"], g, S // dilation, window, dilation)
        qkv = u3.reshape(Bn, 1, S, EVEN_TOKEN_COLS) if g == 0 else u_dil[g - 1]
        o, l = _dilated_attention(qkv, bias, window)
        os_.append(o)
        ls_.append(l)
    a = _merge_groups(os_, ls_, S)
    q, k, v = _mla_prep(u3, p["g_qa"], p["g_kva"], p["wq"], p["wk"], p["wv"])
    b = _mla_attention(q, k, v, Bn, S)
    x2 = _matmul([a, b], [p["w_out_even"][:A_OUT], p["w_out_even"][A_OUT:]], res=x2, out_dtype=F32, tm=1024)
    x2 = _moe(x2, p["ln_ffn_g"][0], p["router"][0], p["w_gate"], p["w_up"], p["w_down"], 0)
    u = _matmul(x2, p["w_in_odd"], gain=p["ln_mix_g"][1], out_dtype=BF16, tm=1024)
    c = _natten(u.reshape(Bn, S, C_IN), p["natten_bias"])
    x2 = _matmul(c, p["w_out_odd"], res=x2, out_dtype=F32, tm=1024)
    x2 = _moe(x2, p["ln_ffn_g"][1], p["router"][1], p["w_gate"], p["w_up"], p["w_down"], 1,
              final_g=p["final_g"])
    return x2.reshape(Bn, S, D)


def _head_tiles(w, n_heads, per_head, keep):
    K = w.shape[0]
    w = w.reshape(K, n_heads, per_head)[:, :, :keep]
    return jnp.pad(w, ((0, 0), (0, 0), (0, LANES - keep))).reshape(K, n_heads * LANES)


def _prepare(ln_mix_g, w_in_even, g_qa, w_qb, g_kva, w_kvb, t5_table, w_out_even,
             w_in_odd, rpb, w_out_odd, ln_ffn_g, router, w_gate, w_up, w_down, final_g):
    w_in = w_in_even[0]
    w_a, w_cq = w_in[:, :A_IN], w_in[:, A_IN:A_IN + B_Q_RANK]
    w_ckv = w_in[:, A_IN + B_Q_RANK:A_IN + B_Q_RANK + B_KV_RANK]
    w_kr = w_in[:, A_IN + B_Q_RANK + B_KV_RANK:]
    w_kr_tile = jnp.pad(w_kr, ((0, 0), (B_NOPE, LANES - B_NOPE - B_ROPE)))
    q_scale = HEAD_DIM ** -0.5 * LOG2_E
    qkv_scale = jnp.array([q_scale, 1.0, 1.0], F32)
    w_a = w_a.reshape(-1, 3, A_GROUPS, A_OUT) * qkv_scale[None, :, None, None]
    groups = [w_a[:, :, g].reshape(-1, 3 * A_OUT) for g in range(A_GROUPS)]
    w_odd = (w_in_odd[0].reshape(-1, 3, C_OUT) * qkv_scale[None, :, None]).reshape(-1, C_IN)
    w_even = jnp.concatenate([groups[0], w_ckv, w_kr_tile, w_cq] + groups[1:], axis=1)
    w_kv = w_kvb[0].reshape(B_KV_RANK, B_HEADS, B_NOPE + B_V)
    return dict(
        ln_mix_g=ln_mix_g, ln_ffn_g=ln_ffn_g, final_g=final_g, t5_table=t5_table,
        w_in_even=w_even.astype(BF16),
        g_qa=g_qa[0].reshape(1, -1).astype(F32), g_kva=g_kva[0].reshape(1, -1).astype(F32),
        wq=_head_tiles(w_qb[0], B_HEADS, B_NOPE + B_ROPE, B_NOPE + B_ROPE).astype(BF16),
        wk=_head_tiles(w_kvb[0], B_HEADS, B_NOPE + B_V, B_NOPE).astype(BF16),
        wv=_head_tiles(w_kv[:, :, B_NOPE:].reshape(B_KV_RANK, B_OUT), B_HEADS, B_V, B_V).T.astype(BF16),
        w_out_even=w_out_even[0].astype(BF16), w_in_odd=w_odd.astype(BF16),
        natten_bias=_natten_bias(rpb[0]), w_out_odd=w_out_odd[0].astype(BF16), router=router,
        w_gate=w_gate, w_up=w_up, w_down=w_down,
    )


def kernel(x_prompt, x_sample, ln_mix_g, w_in_even, g_qa, w_qb, g_kva, w_kvb, t5_table, w_out_even,
           w_in_odd, rpb, w_out_odd, ln_ffn_g, router, w_gate, w_up, w_down, final_g):
    p = _prepare(ln_mix_g, w_in_even, g_qa, w_qb, g_kva, w_kvb, t5_table, w_out_even,
                 w_in_odd, rpb, w_out_odd, ln_ffn_g, router, w_gate, w_up, w_down, final_g)
    return _trunk(x_prompt, p), _trunk(x_sample, p)
```
